```python
import jax, jax.numpy as jnp
from jax import lax
import numpy as np

D_MODEL = 1024
BATCH = 4
SEQ = 8192
DEPTH = 2

GRID_W = 64
CTX_LEN = 256
HEAD_DIM = 64
EPS = 1e-6
A_HEADS = 4
A_WIDTH = A_HEADS * HEAD_DIM
CHUNK = 128
B_GROUPS = 4
B_WIDTH = B_GROUPS * HEAD_DIM
C_Q_HEADS = 8
C_KV_HEADS = 2
C_GROUP = C_Q_HEADS // C_KV_HEADS
C_WIDTH = C_Q_HEADS * HEAD_DIM
KV_WIDTH = C_KV_HEADS * HEAD_DIM
WINDOW = 128
BLOCK = 128
ROPE_BASE = 10000.0
D_MIX = A_WIDTH + B_WIDTH + C_WIDTH
OFF_B = 2 * A_WIDTH
OFF_Q = OFF_B + B_WIDTH
OFF_K = OFF_Q + C_WIDTH
OFF_V = OFF_K + KV_WIDTH
N_IN = OFF_V + KV_WIDTH
D_FF = 3584
N_EXPERTS = 8
TOP_K = 2
N_DENSE = (DEPTH + 1) // 2
N_MOE = DEPTH // 2

kernel_name = "hybrid_parallel_mixer_diffusion_block"


def rmsnorm(x, g):
    xf = x.astype(jnp.float32)
    y = xf * lax.rsqrt(jnp.mean(xf * xf, axis=-1, keepdims=True) + EPS)
    return (y * g.astype(jnp.float32)).astype(x.dtype)


def adaln(cond, w, b):
    m = jax.nn.silu(cond) @ w + b
    return jnp.split(m[..., None, :], 6, axis=-1)


def modulate(y, shift, scale):
    return y * (1 + scale) + shift


def axial_rope(n_tok):
    rows = n_tok // GRID_W
    row = jnp.broadcast_to(jnp.arange(rows)[:, None], (rows, GRID_W)).reshape(-1)
    col = jnp.broadcast_to(jnp.arange(GRID_W)[None, :], (rows, GRID_W)).reshape(-1)
    half = HEAD_DIM // 2
    inv = ROPE_BASE ** (-jnp.arange(0, half, 2, dtype=jnp.float32) / half)
    ang = jnp.stack([row.astype(jnp.float32)[:, None] * inv,
                     col.astype(jnp.float32)[:, None] * inv], axis=1)
    return jnp.cos(ang), jnp.sin(ang)


def apply_rope(x, cos, sin):
    xf = x.astype(jnp.float32).reshape(x.shape[:-1] + (2, 2, HEAD_DIM // 4))
    x1, x2 = xf[..., 0, :], xf[..., 1, :]
    cs, sn = cos[:, None], sin[:, None]
    out = jnp.stack([x1 * cs - x2 * sn, x2 * cs + x1 * sn], axis=-2)
    return out.reshape(x.shape).astype(x.dtype)


def chunk_sgu(uv, w_s, b_s, g_v):
    bsz, n, _ = uv.shape
    u, v = jnp.split(uv, 2, axis=-1)
    v = rmsnorm(v.reshape(bsz, n, A_HEADS, HEAD_DIM), g_v)
    v = v.reshape(bsz, n // CHUNK, CHUNK, A_HEADS, HEAD_DIM)
    sv = jnp.einsum('hpq,bcqhd->bcphd', w_s, v) + b_s.T[None, None, :, :, None]
    return u * sv.reshape(bsz, n, A_WIDTH)


def fourier_mix(xb, w_f):
    bsz, n, _ = xb.shape
    xg = xb.astype(jnp.float32).reshape(bsz, n, B_GROUPS, HEAD_DIM)
    y = jnp.fft.fft2(xg, axes=(1, 3), norm='ortho').real.astype(xb.dtype)
    y = jnp.einsum('bngd,gde->bnge', y, w_f)
    return y.reshape(bsz, n, B_WIDTH)


def window_attention(q, k, v, kc, vc, sink):
    bsz, n = q.shape[:2]
    nb = n // BLOCK
    scale = HEAD_DIM ** -0.5
    qb = q.reshape(bsz, nb, BLOCK, C_KV_HEADS, C_GROUP, HEAD_DIM)

    def band(t):
        tb = jnp.pad(t, ((0, 0), (BLOCK, BLOCK), (0, 0), (0, 0)))
        tb = tb.reshape(bsz, nb + 2, BLOCK, C_KV_HEADS, HEAD_DIM)
        return jnp.concatenate([tb[:, :-2], tb[:, 1:-1], tb[:, 2:]], axis=2)

    kb, vb = band(k), band(v)
    s_loc = jnp.einsum('bnqkgd,bnskd->bnkgqs', qb, kb,
                       preferred_element_type=jnp.float32) * scale
    s_ctx = jnp.einsum('bnqkgd,bckd->bnkgqc', qb, kc,
                       preferred_element_type=jnp.float32) * scale
    kpos = jnp.arange(3 * BLOCK) - BLOCK
    rel = kpos[None, :] - jnp.arange(BLOCK)[:, None]
    gpos = jnp.arange(nb)[:, None, None] * BLOCK + kpos[None, None, :]
    valid = (jnp.abs(rel) <= WINDOW)[None] & (gpos >= 0) & (gpos < n)
    s_loc = jnp.where(valid[None, :, None, None], s_loc, -jnp.inf)
    sink_col = jnp.broadcast_to(
        sink.astype(jnp.float32).reshape(C_KV_HEADS, C_GROUP, 1, 1), s_loc.shape[:-1] + (1,))
    p = jax.nn.softmax(jnp.concatenate([s_loc, s_ctx, sink_col], axis=-1), axis=-1)
    n_loc = 3 * BLOCK
    n_ctx = kc.shape[1]
    p_loc = p[..., :n_loc].astype(v.dtype)
    p_ctx = p[..., n_loc:n_loc + n_ctx].astype(v.dtype)
    o = (jnp.einsum('bnkgqs,bnskd->bnqkgd', p_loc, vb)
         + jnp.einsum('bnkgqc,bckd->bnqkgd', p_ctx, vc))
    return o.reshape(bsz, n, C_WIDTH)


def context_attention(qc, kc, vc, sink):
    bsz, m = qc.shape[:2]
    qg = qc.reshape(bsz, m, C_KV_HEADS, C_GROUP, HEAD_DIM)
    s = jnp.einsum('bqkgd,bckd->bkgqc', qg, kc,
                   preferred_element_type=jnp.float32) * HEAD_DIM ** -0.5
    sink_col = jnp.broadcast_to(
        sink.astype(jnp.float32).reshape(C_KV_HEADS, C_GROUP, 1, 1), s.shape[:-1] + (1,))
    p = jax.nn.softmax(jnp.concatenate([s, sink_col], axis=-1), axis=-1)
    o = jnp.einsum('bkgqc,bckd->bqkgd', p[..., :-1].astype(vc.dtype), vc)
    return o.reshape(bsz, m, C_WIDTH)


def swiglu(y, wg, wu, wd):
    return (jax.nn.silu(y @ wg) * (y @ wu)) @ wd


def moe_swiglu(y, w_r, b_r, wg, wu, wd):
    logits = (y @ w_r).astype(jnp.float32) + b_r.astype(jnp.float32)
    top_val, top_idx = lax.top_k(logits, TOP_K)
    gates = jax.nn.softmax(top_val, axis=-1)
    combine = jnp.sum(jax.nn.one_hot(top_idx, N_EXPERTS, dtype=jnp.float32)
                      * gates[..., None], axis=-2).astype(y.dtype)
    out = jnp.zeros_like(y)
    for e in range(N_EXPERTS):
        out = out + combine[..., e:e + 1] * swiglu(y, wg[e], wu[e], wd[e])
    return out


def setup_inputs(seed: int = 0) -> dict:
    key = jax.random.key(seed)
    ks = jax.random.split(key, 25)

    def nrm(k, shape, s):
        return jax.random.normal(k, shape, jnp.float32) * s

    return {
        "x": nrm(ks[0], (BATCH, SEQ, D_MODEL), 1.0),
        "c": nrm(ks[1], (BATCH, D_MODEL), 1.0),
        "ctx": nrm(ks[2], (BATCH, CTX_LEN, D_MODEL), 1.0),
        "c_ctx": nrm(ks[3], (D_MODEL,), 1.0),
        "w_ada": nrm(ks[4], (DEPTH, D_MODEL, 6 * D_MODEL), 0.5 * D_MODEL ** -0.5),
        "b_ada": nrm(ks[5], (DEPTH, 6 * D_MODEL), 0.02),
        "g_mix_pre": 1.0 + nrm(ks[6], (DEPTH, D_MODEL), 0.1),
        "g_mix_post": 1.0 + nrm(ks[7], (DEPTH, D_MODEL), 0.1),
        "g_ffn_pre": 1.0 + nrm(ks[8], (DEPTH, D_MODEL), 0.1),
        "g_ffn_post": 1.0 + nrm(ks[9], (DEPTH, D_MODEL), 0.1),
        "w_in": nrm(ks[10], (DEPTH, D_MODEL, N_IN), D_MODEL ** -0.5),
        "w_s": nrm(ks[11], (DEPTH, A_HEADS, CHUNK, CHUNK), CHUNK ** -0.5),
        "b_s": 1.0 + nrm(ks[12], (DEPTH, A_HEADS, CHUNK), 0.1),
        "g_v": 1.0 + nrm(ks[13], (DEPTH, A_HEADS, HEAD_DIM), 0.1),
        "w_f": nrm(ks[14], (DEPTH, B_GROUPS, HEAD_DIM, HEAD_DIM), HEAD_DIM ** -0.5),
        "sink": nrm(ks[15], (DEPTH, C_Q_HEADS), 0.5),
        "w_out": nrm(ks[16], (DEPTH, D_MIX, D_MODEL), D_MIX ** -0.5),
        "w_gate_d": nrm(ks[17], (N_DENSE, D_MODEL, D_FF), D_MODEL ** -0.5),
        "w_up_d": nrm(ks[18], (N_DENSE, D_MODEL, D_FF), D_MODEL ** -0.5),
        "w_down_d": nrm(ks[19], (N_DENSE, D_FF, D_MODEL), D_FF ** -0.5),
        "w_router": nrm(ks[20], (N_MOE, D_MODEL, N_EXPERTS), D_MODEL ** -0.5),
        "b_router": nrm(ks[21], (N_MOE, N_EXPERTS), 0.01),
        "w_gate_e": nrm(ks[22], (N_MOE, N_EXPERTS, D_MODEL, D_FF), D_MODEL ** -0.5),
        "w_up_e": nrm(ks[23], (N_MOE, N_EXPERTS, D_MODEL, D_FF), D_MODEL ** -0.5),
        "w_down_e": nrm(ks[24], (N_MOE, N_EXPERTS, D_FF, D_MODEL), D_FF ** -0.5),
    }


def reference(x, c, ctx, c_ctx, w_ada, b_ada, g_mix_pre, g_mix_post, g_ffn_pre, g_ffn_post,
              w_in, w_s, b_s, g_v, w_f, sink, w_out, w_gate_d, w_up_d, w_down_d,
              w_router, b_router, w_gate_e, w_up_e, w_down_e):
    bsz, n_lat = x.shape[:2]
    n_ctx = ctx.shape[1]
    cos, sin = axial_rope(n_lat)
    h, hc = x, ctx
    for i in range(DEPTH):
        last = i == DEPTH - 1
        sh_m, sc_m, gt_m, sh_f, sc_f, gt_f = adaln(c, w_ada[i], b_ada[i])
        csh_m, csc_m, cgt_m, csh_f, csc_f, cgt_f = adaln(c_ctx[None, :], w_ada[i], b_ada[i])

        z = modulate(rmsnorm(h, g_mix_pre[i]), sh_m, sc_m) @ w_in[i]
        a, fb, q, k, v = jnp.split(z, [OFF_B, OFF_Q, OFF_K, OFF_V], axis=-1)
        q = apply_rope(q.reshape(bsz, n_lat, C_Q_HEADS, HEAD_DIM), cos, sin)
        k = apply_rope(k.reshape(bsz, n_lat, C_KV_HEADS, HEAD_DIM), cos, sin)
        v = v.reshape(bsz, n_lat, C_KV_HEADS, HEAD_DIM)
        zc_in = modulate(rmsnorm(hc, g_mix_pre[i]), csh_m, csc_m)
        kc, vc = jnp.split(zc_in @ w_in[i][:, OFF_K:], 2, axis=-1)
        kc = kc.reshape(bsz, n_ctx, C_KV_HEADS, HEAD_DIM)
        vc = vc.reshape(bsz, n_ctx, C_KV_HEADS, HEAD_DIM)
        o = jnp.concatenate([
            chunk_sgu(jax.nn.gelu(a), w_s[i], b_s[i], g_v[i]),
            fourier_mix(fb, w_f[i]),
            window_attention(q, k, v, kc, vc, sink[i]),
        ], axis=-1) @ w_out[i]
        h = h + gt_m * rmsnorm(o, g_mix_post[i])
        if not last:
            ac, fbc, qc = jnp.split(zc_in @ w_in[i][:, :OFF_K], [OFF_B, OFF_Q], axis=-1)
            oc = jnp.concatenate([
                chunk_sgu(jax.nn.gelu(ac), w_s[i], b_s[i], g_v[i]),
                fourier_mix(fbc, w_f[i]),
                context_attention(qc.reshape(bsz, n_ctx, C_Q_HEADS, HEAD_DIM), kc, vc, sink[i]),
            ], axis=-1) @ w_out[i]
            hc = hc + cgt_m * rmsnorm(oc, g_mix_post[i])

        j = i // 2
        if i % 2 == 0:
            def ffn(y):
                return swiglu(y, w_gate_d[j], w_up_d[j], w_down_d[j])
        else:
            def ffn(y):
                return moe_swiglu(y, w_router[j], b_router[j], w_gate_e[j], w_up_e[j], w_down_e[j])
        f = ffn(modulate(rmsnorm(h, g_ffn_pre[i]), sh_f, sc_f))
        h = h + gt_f * rmsnorm(f, g_ffn_post[i])
        if not last:
            fc = ffn(modulate(rmsnorm(hc, g_ffn_pre[i]), csh_f, csc_f))
            hc = hc + cgt_f * rmsnorm(fc, g_ffn_post[i])
    return h
```

```python
import functools

import numpy as np
import jax
import jax.numpy as jnp
from jax import lax
from jax.experimental import pallas as pl
from jax.experimental.pallas import tpu as pltpu

D_MODEL = 1024
DEPTH = 2
GRID_W = 64
HEAD_DIM = 64
EPS = 1e-6
A_HEADS = 4
A_WIDTH = 256
CHUNK = 128
B_WIDTH = 256
C_Q_HEADS = 8
C_WIDTH = 512
KV_WIDTH = 128
WINDOW = 128
BLOCK = 128
ROPE_BASE = 10000.0
OFF_B = 512
OFF_Q = 768
OFF_K = 1280
OFF_V = 1408
N_IN = 1536
D_FF = 3584
N_EXPERTS = 8

LANES = 128
VMEM_LIMIT = 56 * 1024 * 1024
NEG = -1e30

F32 = jnp.float32
BF16 = jnp.bfloat16


def _dot(a, b):
    return jnp.dot(a, b, preferred_element_type=F32)


def _cparams(sem):
    return pltpu.CompilerParams(dimension_semantics=sem, vmem_limit_bytes=VMEM_LIMIT)


def _adaln_kernel(c_ref, w_ref, b_ref, o_ref):
    c = c_ref[...]
    s = (c * jax.nn.sigmoid(c)).astype(BF16)
    o_ref[...] = _dot(s, w_ref[...].astype(BF16)) + b_ref[...]


def _adaln(cond8, w, b):
    tn = 1536
    return pl.pallas_call(
        _adaln_kernel,
        grid=(6 * D_MODEL // tn,),
        in_specs=[pl.BlockSpec((8, D_MODEL), lambda j: (0, 0)),
                  pl.BlockSpec((D_MODEL, tn), lambda j: (0, j)),
                  pl.BlockSpec((1, tn), lambda j: (0, j))],
        out_specs=pl.BlockSpec((8, tn), lambda j: (0, j)),
        out_shape=jax.ShapeDtypeStruct((8, 6 * D_MODEL), F32),
        compiler_params=_cparams(("arbitrary",)),
    )(cond8, w, b)


def _gelu_tanh(x):
    return 0.5 * x * (1.0 + jnp.tanh(0.7978845608028654 * (x + 0.044715 * x * x * x)))


def _premix_kernel(*refs, rope):
    if rope:
        (h_ref, sh_ref, sc_ref, g_ref, win_ref, ws_ref, bs_ref, gv_ref, hm_ref, wdft_ref,
         cos_ref, s1_ref, s2_ref, sgu_ref, re_ref, im_ref, q_ref, k_ref, v_ref) = refs
    else:
        (h_ref, sh_ref, sc_ref, g_ref, win_ref, ws_ref, bs_ref, gv_ref, hm_ref, wdft_ref,
         sgu_ref, re_ref, im_ref, q_ref, k_ref, v_ref) = refs
    x = h_ref[...]
    tm = x.shape[0]
    ms = jnp.mean(x * x, axis=-1, keepdims=True)
    xn = x * lax.rsqrt(ms + EPS) * g_ref[...]
    xm = (xn * (1.0 + sc_ref[...]) + sh_ref[...]).astype(BF16)
    z = _dot(xm, win_ref[...])

    a = _gelu_tanh(z[:, :OFF_B])
    u = a[:, :A_WIDTH]
    v = a[:, A_WIDTH:]
    msv = _dot((v * v).astype(BF16), hm_ref[...])
    vn = (v * lax.rsqrt(msv + EPS) * gv_ref[...]).astype(BF16)
    head = lax.broadcasted_iota(jnp.int32, (CHUNK, A_WIDTH), 1) // HEAD_DIM
    for ck in range(tm // CHUNK):
        rows = slice(ck * CHUNK, (ck + 1) * CHUNK)
        vc = vn[rows]
        sv = bs_ref[...]
        for hh in range(A_HEADS):
            sv = sv + jnp.where(head == hh, _dot(ws_ref[hh], vc), 0.0)
        sgu_ref[rows, :] = (u[rows] * sv).astype(BF16)

    f = _dot(z[:, OFF_B:OFF_Q].astype(BF16), wdft_ref[...])
    re_ref[...] = f[:, :B_WIDTH].astype(BF16)
    im_ref[...] = f[:, B_WIDTH:].astype(BF16)

    def rot(t):
        if not rope:
            return t
        return (t * cos_ref[...] + pltpu.roll(t, LANES - 16, 1) * s1_ref[...]
                + pltpu.roll(t, 16, 1) * s2_ref[...])

    for j in range(C_WIDTH // LANES):
        cols = slice(OFF_Q + j * LANES, OFF_Q + (j + 1) * LANES)
        q_ref[:, j * LANES:(j + 1) * LANES] = (rot(z[:, cols]) * 0.125).astype(BF16)

    lo = lax.broadcasted_iota(jnp.int32, (tm, LANES), 1) < HEAD_DIM
    kk = rot(z[:, OFF_K:OFF_V])
    kr = pltpu.roll(kk, HEAD_DIM, 1)
    k_ref[:, :LANES] = jnp.where(lo, kk, kr).astype(BF16)
    k_ref[:, LANES:] = jnp.where(lo, kr, kk).astype(BF16)
    vv = z[:, OFF_V:]
    vr = pltpu.roll(vv, HEAD_DIM, 1)
    v_ref[:, :LANES] = jnp.where(lo, vv, vr).astype(BF16)
    v_ref[:, LANES:] = jnp.where(lo, vr, vv).astype(BF16)


def _premix(h, mods, g, win, ws, bsx, gv, hm, wdft, rope_tabs, *, tm, tiles_per_batch, n_pos):
    T = h.shape[0]
    rope = rope_tabs is not None
    if tiles_per_batch is None:
        mrow = lambda i: 4
    else:
        mrow = lambda i: i // tiles_per_batch
    const2 = lambda i: (0, 0)
    in_specs = [
        pl.BlockSpec((tm, D_MODEL), lambda i: (i, 0)),
        pl.BlockSpec((None, None, 1, D_MODEL), lambda i: (mrow(i), 0, 0, 0)),
        pl.BlockSpec((None, None, 1, D_MODEL), lambda i: (mrow(i), 1, 0, 0)),
        pl.BlockSpec((1, D_MODEL), const2),
        pl.BlockSpec((D_MODEL, N_IN), const2),
        pl.BlockSpec((A_HEADS, CHUNK, CHUNK), lambda i: (0, 0, 0)),
        pl.BlockSpec((CHUNK, A_WIDTH), const2),
        pl.BlockSpec((1, A_WIDTH), const2),
        pl.BlockSpec((A_WIDTH, A_WIDTH), const2),
        pl.BlockSpec((B_WIDTH, 2 * B_WIDTH), const2),
    ]
    args = [h, mods, mods, g, win, ws, bsx, gv, hm, wdft]
    if rope:
        nt = n_pos // tm
        for t in rope_tabs:
            in_specs.append(pl.BlockSpec((tm, LANES), lambda i: (i % nt, 0)))
            args.append(t)
    widths = (A_WIDTH, B_WIDTH, B_WIDTH, C_WIDTH, 2 * KV_WIDTH, 2 * KV_WIDTH)
    return pl.pallas_call(
        functools.partial(_premix_kernel, rope=rope),
        grid=(T // tm,),
        in_specs=in_specs,
        out_specs=[pl.BlockSpec((tm, w), lambda i: (i, 0)) for w in widths],
        out_shape=[jax.ShapeDtypeStruct((T, w), BF16) for w in widths],
        compiler_params=_cparams(("parallel",)),
    )(*args)


def _dft1_kernel(m1_ref, twc_ref, tws_ref, re_ref, im_ref, o_ref):
    x = jnp.concatenate([re_ref[0], im_ref[0]], axis=0)
    a = _dot(m1_ref[...], x)
    nr = a.shape[0] // 2
    are, aim = a[:nr], a[nr:]
    c, s = twc_ref[...], tws_ref[...]
    bre = (are * c + aim * s).astype(BF16)
    bim = (aim * c - are * s).astype(BF16)
    for cl in range(o_ref.shape[2]):
        o_ref[0, 0, cl] = bre[:, cl * B_WIDTH:(cl + 1) * B_WIDTH]
        o_ref[0, 1, cl] = bim[:, cl * B_WIDTH:(cl + 1) * B_WIDTH]


def _dft3_kernel(m3_ref, x_ref, o_ref):
    x = jnp.concatenate([x_ref[0, 0], x_ref[0, 1]], axis=0)
    o_ref[0] = _dot(m3_ref[...], x).astype(BF16)


def _seq_dft(re, im, bsz, n_rows, m1, twc, tws, m3):
    ncol = GRID_W * B_WIDTH
    tn = 2048
    re3 = re.reshape(bsz, n_rows, ncol)
    im3 = im.reshape(bsz, n_rows, ncol)
    st1 = pl.pallas_call(
        _dft1_kernel,
        grid=(ncol // tn, bsz),
        in_specs=[pl.BlockSpec((2 * n_rows, 2 * n_rows), lambda j, b: (0, 0)),
                  pl.BlockSpec((n_rows, tn), lambda j, b: (0, j)),
                  pl.BlockSpec((n_rows, tn), lambda j, b: (0, j)),
                  pl.BlockSpec((1, n_rows, tn), lambda j, b: (b, 0, j)),
                  pl.BlockSpec((1, n_rows, tn), lambda j, b: (b, 0, j))],
        out_specs=pl.BlockSpec((1, 2, tn // B_WIDTH, n_rows, B_WIDTH), lambda j, b: (b, 0, j, 0, 0)),
        out_shape=jax.ShapeDtypeStruct((bsz, 2, GRID_W, n_rows, B_WIDTH), BF16),
        compiler_params=_cparams(("parallel", "parallel")),
    )(m1, twc, tws, re3, im3)
    st1 = st1.reshape(bsz, 2, GRID_W, n_rows * B_WIDTH)
    tn3 = 4096
    y = pl.pallas_call(
        _dft3_kernel,
        grid=(bsz, n_rows * B_WIDTH // tn3),
        in_specs=[pl.BlockSpec((GRID_W, 2 * GRID_W), lambda b, j: (0, 0)),
                  pl.BlockSpec((1, 2, GRID_W, tn3), lambda b, j: (b, 0, 0, j))],
        out_specs=pl.BlockSpec((1, GRID_W, tn3), lambda b, j: (b, 0, j)),
        out_shape=jax.ShapeDtypeStruct((bsz, GRID_W, n_rows * B_WIDTH), BF16),
        compiler_params=_cparams(("parallel", "parallel")),
    )(m3, st1)
    return y.reshape(bsz * GRID_W * n_rows, B_WIDTH)


def _ctx_dft_kernel(m_ref, re_ref, im_ref, o_ref):
    x = jnp.concatenate([re_ref[0], im_ref[0]], axis=0)
    o_ref[0] = _dot(m_ref[...], x).astype(BF16)


def _ctx_dft(re, im, bsz, m, mat):
    re3 = re.reshape(bsz, m, B_WIDTH)
    im3 = im.reshape(bsz, m, B_WIDTH)
    y = pl.pallas_call(
        _ctx_dft_kernel,
        grid=(bsz,),
        in_specs=[pl.BlockSpec((m, 2 * m), lambda b: (0, 0)),
                  pl.BlockSpec((1, m, B_WIDTH), lambda b: (b, 0, 0)),
                  pl.BlockSpec((1, m, B_WIDTH), lambda b: (b, 0, 0))],
        out_specs=pl.BlockSpec((1, m, B_WIDTH), lambda b: (b, 0, 0)),
        out_shape=jax.ShapeDtypeStruct((bsz, m, B_WIDTH), BF16),
        compiler_params=_cparams(("parallel",)),
    )(mat, re3, im3)
    return y.reshape(bsz * m, B_WIDTH)


def _attn_kernel(sink_ref, *refs, local, nb):
    if local:
        q_ref, kp_ref, kc_ref, kn_ref, vp_ref, vc_ref, vn_ref, kx_ref, vx_ref, o_ref = refs
    else:
        q_ref, kx_ref, vx_ref, o_ref = refs
    q = q_ref[0]
    tq = q.shape[0]
    lo = lax.broadcasted_iota(jnp.int32, (tq, LANES), 1) < HEAD_DIM
    zero = jnp.zeros((tq, LANES), BF16)
    if local:
        i = pl.program_id(1)
        row = lax.broadcasted_iota(jnp.int32, (BLOCK, 3 * BLOCK), 0)
        col = lax.broadcasted_iota(jnp.int32, (BLOCK, 3 * BLOCK), 1)
        valid = jnp.abs(col - BLOCK - row) <= WINDOW
        valid = valid & ((col >= BLOCK) | (i > 0)) & ((col < 2 * BLOCK) | (i < nb - 1))
    for g in range(2):
        gl = slice(g * LANES, (g + 1) * LANES)
        if local:
            kg = jnp.concatenate([kp_ref[0][:, gl], kc_ref[0][:, gl], kn_ref[0][:, gl], kx_ref[0][:, gl]], axis=0)
            vg = jnp.concatenate([vp_ref[0][:, gl], vc_ref[0][:, gl], vn_ref[0][:, gl], vx_ref[0][:, gl]], axis=0)
        else:
            kg = kx_ref[0][:, gl]
            vg = vx_ref[0][:, gl]
        qs = []
        for p in range(2):
            qp = q[:, g * 2 * LANES + p * LANES: g * 2 * LANES + (p + 1) * LANES]
            qs.append(jnp.where(lo, qp, zero))
            qs.append(jnp.where(lo, zero, qp))
        q4 = jnp.concatenate(qs, axis=0)
        s4 = lax.dot_general(q4, kg, (((1,), (1,)), ((), ())), preferred_element_type=F32)
        ps, rden = [], []
        for hl in range(4):
            s = s4[hl * tq:(hl + 1) * tq]
            if local:
                s = jnp.concatenate(
                    [jnp.where(valid, s[:, :3 * BLOCK], NEG), s[:, 3 * BLOCK:]], axis=1)
            sk = sink_ref[g * 4 + hl]
            m = jnp.maximum(jnp.max(s, axis=-1, keepdims=True), sk)
            p = jnp.exp(s - m)
            den = jnp.sum(p, axis=-1, keepdims=True) + jnp.exp(sk - m)
            ps.append(p.astype(BF16))
            rden.append(1.0 / den)
        o4 = _dot(jnp.concatenate(ps, axis=0), vg) * jnp.concatenate(rden, axis=0)
        for p in range(2):
            o_pair = jnp.where(lo, o4[(2 * p) * tq:(2 * p + 1) * tq], o4[(2 * p + 1) * tq:(2 * p + 2) * tq])
            c0 = g * 2 * LANES + p * LANES
            o_ref[0, :, c0:c0 + LANES] = o_pair.astype(BF16)


def _window_attention(q, k, v, kx, vx, sink, bsz, n, m):
    nb = n // BLOCK
    q3 = q.reshape(bsz, n, C_WIDTH)
    k3 = k.reshape(bsz, n, 2 * KV_WIDTH)
    v3 = v.reshape(bsz, n, 2 * KV_WIDTH)
    kx3 = kx.reshape(bsz, m, 2 * KV_WIDTH)
    vx3 = vx.reshape(bsz, m, 2 * KV_WIDTH)
    blk = lambda f: pl.BlockSpec((1, BLOCK, 2 * KV_WIDTH), f)
    prev = lambda b, i, s: (b, jnp.maximum(i - 1, 0), 0)
    cur = lambda b, i, s: (b, i, 0)
    nxt = lambda b, i, s: (b, jnp.minimum(i + 1, nb - 1), 0)
    ctxs = pl.BlockSpec((1, m, 2 * KV_WIDTH), lambda b, i, s: (b, 0, 0))
    o = pl.pallas_call(
        functools.partial(_attn_kernel, local=True, nb=nb),
        grid_spec=pltpu.PrefetchScalarGridSpec(
            num_scalar_prefetch=1,
            grid=(bsz, nb),
            in_specs=[pl.BlockSpec((1, BLOCK, C_WIDTH), cur),
                      blk(prev), blk(cur), blk(nxt), blk(prev), blk(cur), blk(nxt), ctxs, ctxs],
            out_specs=pl.BlockSpec((1, BLOCK, C_WIDTH), cur)),
        out_shape=jax.ShapeDtypeStruct((bsz, n, C_WIDTH), BF16),
        compiler_params=_cparams(("parallel", "parallel")),
    )(sink, q3, k3, k3, k3, v3, v3, v3, kx3, vx3)
    return o.reshape(bsz * n, C_WIDTH)


def _context_attention(q, kx, vx, sink, bsz, m):
    q3 = q.reshape(bsz, m, C_WIDTH)
    kx3 = kx.reshape(bsz, m, 2 * KV_WIDTH)
    vx3 = vx.reshape(bsz, m, 2 * KV_WIDTH)
    ctxs = pl.BlockSpec((1, m, 2 * KV_WIDTH), lambda b, s: (b, 0, 0))
    o = pl.pallas_call(
        functools.partial(_attn_kernel, local=False, nb=1),
        grid_spec=pltpu.PrefetchScalarGridSpec(
            num_scalar_prefetch=1,
            grid=(bsz,),
            in_specs=[pl.BlockSpec((1, m, C_WIDTH), lambda b, s: (b, 0, 0)), ctxs, ctxs],
            out_specs=pl.BlockSpec((1, m, C_WIDTH), lambda b, s: (b, 0, 0))),
        out_shape=jax.ShapeDtypeStruct((bsz, m, C_WIDTH), BF16),
        compiler_params=_cparams(("parallel",)),
    )(sink, q3, kx3, vx3)
    return o.reshape(bsz * m, C_WIDTH)


def _postmix_kernel(*refs, route):
    (sgu_ref, yf_ref, at_ref, wf_ref, wo_ref, h_ref, gp_ref, gt_ref, gf_ref, sh_ref, sc_ref) = refs[:11]
    if route:
        wr_ref, br_ref, hn_ref, y_ref, cmb_ref = refs[11:]
    else:
        hn_ref, y_ref = refs[11:]
    fm = _dot(yf_ref[...], wf_ref[...]).astype(BF16)
    o = (_dot(sgu_ref[...], wo_ref[:A_WIDTH, :])
         + _dot(fm, wo_ref[A_WIDTH:A_WIDTH + B_WIDTH, :])
         + _dot(at_ref[...], wo_ref[A_WIDTH + B_WIDTH:, :]))
    ms = jnp.mean(o * o, axis=-1, keepdims=True)
    hn = h_ref[...] + gt_ref[...] * (o * lax.rsqrt(ms + EPS) * gp_ref[...])
    hn_ref[...] = hn
    ms2 = jnp.mean(hn * hn, axis=-1, keepdims=True)
    y = hn * lax.rsqrt(ms2 + EPS) * gf_ref[...] * (1.0 + sc_ref[...]) + sh_ref[...]
    y_ref[...] = y.astype(BF16)
    if route:
        yh = y.astype(BF16)
        yl = (y - yh.astype(F32)).astype(BF16)
        w = wr_ref[...]
        wh = w.astype(BF16)
        wl = (w - wh.astype(F32)).astype(BF16)
        lg = _dot(yh, wh) + _dot(yl, wh) + _dot(yh, wl) + br_ref[...]
        lane = lax.broadcasted_iota(jnp.int32, lg.shape, 1)
        lg = jnp.where(lane < N_EXPERTS, lg, NEG)
        m1 = jnp.max(lg, axis=-1, keepdims=True)
        i1 = jnp.min(jnp.where(lg == m1, lane, LANES), axis=-1, keepdims=True)
        lg2 = jnp.where(lane == i1, NEG, lg)
        m2 = jnp.max(lg2, axis=-1, keepdims=True)
        i2 = jnp.min(jnp.where(lg2 == m2, lane, LANES), axis=-1, keepdims=True)
        e2 = jnp.exp(m2 - m1)
        g1 = 1.0 / (1.0 + e2)
        g2 = e2 * g1
        cmb_ref[...] = jnp.where(lane == i1, g1, 0.0) + jnp.where(lane == i2, g2, 0.0)


def _postmix(sgu, yf, at, wf, wo, h, mods, gp, gf, router, *, tm, tiles_per_batch):
    T = h.shape[0]
    if tiles_per_batch is None:
        mrow = lambda i: 4
    else:
        mrow = lambda i: i // tiles_per_batch
    const2 = lambda i: (0, 0)
    row = lambda w: pl.BlockSpec((tm, w), lambda i: (i, 0))
    mod = lambda j: pl.BlockSpec((None, None, 1, D_MODEL), lambda i: (mrow(i), j, 0, 0))
    in_specs = [row(A_WIDTH), row(B_WIDTH), row(C_WIDTH),
                pl.BlockSpec((B_WIDTH, B_WIDTH), const2),
                pl.BlockSpec((D_MODEL, D_MODEL), const2),
                row(D_MODEL),
                pl.BlockSpec((1, D_MODEL), const2),
                mod(2),
                pl.BlockSpec((1, D_MODEL), const2),
                mod(3), mod(4)]
    args = [sgu, yf, at, wf, wo, h, gp, mods, gf, mods, mods]
    out_specs = [row(D_MODEL), row(D_MODEL)]
    out_shape = [jax.ShapeDtypeStruct((T, D_MODEL), F32), jax.ShapeDtypeStruct((T, D_MODEL), BF16)]
    route = router is not None
    if route:
        in_specs += [pl.BlockSpec((D_MODEL, LANES), const2), pl.BlockSpec((1, LANES), const2)]
        args += list(router)
        out_specs.append(row(LANES))
        out_shape.append(jax.ShapeDtypeStruct((T, LANES), F32))
    return pl.pallas_call(
        functools.partial(_postmix_kernel, route=route),
        grid=(T // tm,),
        in_specs=in_specs, out_specs=out_specs, out_shape=out_shape,
        compiler_params=_cparams(("parallel",)),
    )(*args)


def _ffn_epilogue(f, h_ref, gp_ref, gt_ref, o_ref):
    ms = jnp.mean(f * f, axis=-1, keepdims=True)
    o_ref[...] = h_ref[...] + gt_ref[...] * (f * lax.rsqrt(ms + EPS) * gp_ref[...])


def _ffn_dense_kernel(y_ref, wg_ref, wu_ref, wd_ref, h_ref, gp_ref, gt_ref, o_ref, acc_ref):
    k = pl.program_id(1)
    y = y_ref[...]
    gate = _dot(y, wg_ref[...])
    up = _dot(y, wu_ref[...])
    act = (gate * jax.nn.sigmoid(gate) * up).astype(BF16)
    part = _dot(act, wd_ref[...])

    @pl.when(k == 0)
    def _():
        acc_ref[...] = part

    @pl.when(k > 0)
    def _():
        acc_ref[...] += part

    @pl.when(k == pl.num_programs(1) - 1)
    def _():
        _ffn_epilogue(acc_ref[...], h_ref, gp_ref, gt_ref, o_ref)


def _ffn_dense(y, wg, wu, wd, h, mods, gp, *, tm, tf, tiles_per_batch):
    T = h.shape[0]
    if tiles_per_batch is None:
        mrow = lambda i, k: 4
    else:
        mrow = lambda i, k: i // tiles_per_batch
    return pl.pallas_call(
        _ffn_dense_kernel,
        grid=(T // tm, D_FF // tf),
        in_specs=[pl.BlockSpec((tm, D_MODEL), lambda i, k: (i, 0)),
                  pl.BlockSpec((D_MODEL, tf), lambda i, k: (0, k)),
                  pl.BlockSpec((D_MODEL, tf), lambda i, k: (0, k)),
                  pl.BlockSpec((tf, D_MODEL), lambda i, k: (k, 0)),
                  pl.BlockSpec((tm, D_MODEL), lambda i, k: (i, 0)),
                  pl.BlockSpec((1, D_MODEL), lambda i, k: (0, 0)),
                  pl.BlockSpec((None, None, 1, D_MODEL), lambda i, k: (mrow(i, k), 5, 0, 0))],
        out_specs=pl.BlockSpec((tm, D_MODEL), lambda i, k: (i, 0)),
        out_shape=jax.ShapeDtypeStruct((T, D_MODEL), F32),
        scratch_shapes=[pltpu.VMEM((tm, D_MODEL), F32)],
        compiler_params=_cparams(("parallel", "arbitrary")),
    )(y, wg, wu, wd, h, gp, mods)


def _ffn_moe_kernel(y_ref, cmb_ref, wg_ref, wu_ref, wd_ref, h_ref, gp_ref, gt_ref, o_ref, acc_ref):
    e = pl.program_id(1)
    k = pl.program_id(2)
    y = y_ref[...]
    cmb = cmb_ref[...]
    lane = lax.broadcasted_iota(jnp.int32, cmb.shape, 1)
    ce = jnp.sum(jnp.where(lane == e, cmb, 0.0), axis=-1, keepdims=True)
    gate = _dot(y, wg_ref[...])
    up = _dot(y, wu_ref[...])
    act = (gate * jax.nn.sigmoid(gate) * up * ce).astype(BF16)
    part = _dot(act, wd_ref[...])
    first = (e == 0) & (k == 0)

    @pl.when(first)
    def _():
        acc_ref[...] = part

    @pl.when(jnp.logical_not(first))
    def _():
        acc_ref[...] += part

    @pl.when((e == pl.num_programs(1) - 1) & (k == pl.num_programs(2) - 1))
    def _():
        _ffn_epilogue(acc_ref[...], h_ref, gp_ref, gt_ref, o_ref)


def _ffn_moe(y, cmb, wg, wu, wd, h, mods, gp, *, tm, tf, tiles_per_batch):
    T = h.shape[0]
    return pl.pallas_call(
        _ffn_moe_kernel,
        grid=(T // tm, N_EXPERTS, D_FF // tf),
        in_specs=[pl.BlockSpec((tm, D_MODEL), lambda i, e, k: (i, 0)),
                  pl.BlockSpec((tm, LANES), lambda i, e, k: (i, 0)),
                  pl.BlockSpec((None, D_MODEL, tf), lambda i, e, k: (e, 0, k)),
                  pl.BlockSpec((None, D_MODEL, tf), lambda i, e, k: (e, 0, k)),
                  pl.BlockSpec((None, tf, D_MODEL), lambda i, e, k: (e, k, 0)),
                  pl.BlockSpec((tm, D_MODEL), lambda i, e, k: (i, 0)),
                  pl.BlockSpec((1, D_MODEL), lambda i, e, k: (0, 0)),
                  pl.BlockSpec((None, None, 1, D_MODEL), lambda i, e, k: (i // tiles_per_batch, 5, 0, 0))],
        out_specs=pl.BlockSpec((tm, D_MODEL), lambda i, e, k: (i, 0)),
        out_shape=jax.ShapeDtypeStruct((T, D_MODEL), F32),
        scratch_shapes=[pltpu.VMEM((tm, D_MODEL), F32)],
        compiler_params=_cparams(("parallel", "arbitrary", "arbitrary")),
    )(y, cmb, wg, wu, wd, h, gp, mods)


def _blockdiag(m, reps):
    n = m.shape[0]
    out = np.zeros((n * reps, n * reps), np.float64)
    for r in range(reps):
        out[r * n:(r + 1) * n, r * n:(r + 1) * n] = m
    return out


def _dft_tables(n_rows):
    n = n_rows * GRID_W
    r = np.arange(n_rows)
    c = np.arange(GRID_W)
    a1 = 2 * np.pi * np.outer(r, r) / n_rows
    c1, s1 = np.cos(a1), np.sin(a1)
    m1 = np.block([[c1, s1], [-s1, c1]]) * 0.125
    at = 2 * np.pi * np.outer(r, c) / n
    twc = np.repeat(np.cos(at), B_WIDTH, axis=1)
    tws = np.repeat(np.sin(at), B_WIDTH, axis=1)
    a3 = 2 * np.pi * np.outer(c, c) / GRID_W
    m3 = np.concatenate([np.cos(a3), np.sin(a3)], axis=1) * (8.0 / np.sqrt(n))
    f32 = lambda t: jnp.asarray(t.astype(np.float32))
    return f32(m1).astype(BF16), f32(twc), f32(tws), f32(m3).astype(BF16)


def _channel_dft_table():
    d = np.arange(HEAD_DIM)
    a = 2 * np.pi * np.outer(d, d) / HEAD_DIM
    w = np.concatenate([_blockdiag(np.cos(a), 4), -_blockdiag(np.sin(a), 4)], axis=1) * 0.125
    return jnp.asarray(w.astype(np.float32)).astype(BF16)


def _ctx_dft_table(m):
    p = np.arange(m)
    a = 2 * np.pi * np.outer(p, p) / m
    w = np.concatenate([np.cos(a), np.sin(a)], axis=1) * (8.0 / np.sqrt(m * HEAD_DIM))
    return jnp.asarray(w.astype(np.float32)).astype(BF16)


def _rope_tables(n_tok):
    rows = n_tok // GRID_W
    row = jnp.broadcast_to(jnp.arange(rows)[:, None], (rows, GRID_W)).reshape(-1)
    col = jnp.broadcast_to(jnp.arange(GRID_W)[None, :], (rows, GRID_W)).reshape(-1)
    half = HEAD_DIM // 2
    inv = ROPE_BASE ** (-jnp.arange(0, half, 2, dtype=F32) / half)
    ang = jnp.stack([row.astype(F32)[:, None] * inv, col.astype(F32)[:, None] * inv], axis=1)
    cos, sin = jnp.cos(ang), jnp.sin(ang)
    zer = jnp.zeros_like(sin)
    lay = lambda a, b: jnp.tile(jnp.stack([a, b], axis=2).reshape(n_tok, HEAD_DIM), (1, LANES // HEAD_DIM))
    return lay(cos, cos), lay(-sin, zer), lay(zer, sin)


def kernel(x, c, ctx, c_ctx, w_ada, b_ada, g_mix_pre, g_mix_post, g_ffn_pre, g_ffn_post,
           w_in, w_s, b_s, g_v, w_f, sink, w_out, w_gate_d, w_up_d, w_down_d,
           w_router, b_router, w_gate_e, w_up_e, w_down_e):
    bsz, n_lat, _ = x.shape
    n_ctx = ctx.shape[1]
    T, Tc = bsz * n_lat, bsz * n_ctx
    tm = 512
    tpb = n_lat // tm

    rope_tabs = _rope_tables(n_lat)
    m1, twc, tws, m3 = _dft_tables(n_lat // GRID_W)
    wdft = _channel_dft_table()
    mctx = _ctx_dft_table(n_ctx)
    hm = jnp.asarray(_blockdiag(np.full((HEAD_DIM, HEAD_DIM), 1.0 / HEAD_DIM), A_HEADS).astype(np.float32)).astype(BF16)

    cond8 = jnp.zeros((8, D_MODEL), F32).at[:bsz].set(c).at[4].set(c_ctx)
    h = x.reshape(T, D_MODEL)
    hc = ctx.reshape(Tc, D_MODEL)

    for i in range(DEPTH):
        last = i == DEPTH - 1
        mods = _adaln(cond8, w_ada[i], b_ada[i][None, :]).reshape(8, 6, 1, D_MODEL)
        win = w_in[i].astype(BF16)
        ws = w_s[i].astype(BF16)
        bsx = jnp.repeat(b_s[i].T, HEAD_DIM, axis=1)
        gv = g_v[i].reshape(1, A_WIDTH)
        wf = jax.scipy.linalg.block_diag(*[w_f[i][g] for g in range(4)]).astype(BF16)
        wo = w_out[i].astype(BF16)
        gpre = g_mix_pre[i][None, :]
        gpost = g_mix_post[i][None, :]
        gfpre = g_ffn_pre[i][None, :]
        gfpost = g_ffn_post[i][None, :]
        sk = sink[i]

        sgu, fre, fim, q, k, v = _premix(h, mods, gpre, win, ws, bsx, gv, hm, wdft, rope_tabs,
                                         tm=tm, tiles_per_batch=tpb, n_pos=n_lat)
        sguc, frec, fimc, qc, kc, vc = _premix(hc, mods, gpre, win, ws, bsx, gv, hm, wdft, None,
                                               tm=tm, tiles_per_batch=None, n_pos=n_ctx)
        yf = _seq_dft(fre, fim, bsz, n_lat // GRID_W, m1, twc, tws, m3)
        at = _window_attention(q, k, v, kc, vc, sk, bsz, n_lat, n_ctx)

        if i % 2 == 0:
            router = None
        else:
            j = i // 2
            wr = jnp.zeros((D_MODEL, LANES), F32).at[:, :N_EXPERTS].set(w_router[j])
            br = jnp.zeros((1, LANES), F32).at[0, :N_EXPERTS].set(b_router[j])
            router = (wr, br)
        res = _postmix(sgu, yf, at, wf, wo, h, mods, gpost, gfpre, router, tm=tm, tiles_per_batch=tpb)
        if not last:
            yfc = _ctx_dft(frec, fimc, bsz, n_ctx, mctx)
            atc = _context_attention(qc, kc, vc, sk, bsz, n_ctx)
            resc = _postmix(sguc, yfc, atc, wf, wo, hc, mods, gpost, gfpre, router, tm=tm, tiles_per_batch=None)

        j = i // 2
        if i % 2 == 0:
            wg, wu, wd = w_gate_d[j].astype(BF16), w_up_d[j].astype(BF16), w_down_d[j].astype(BF16)
            h = _ffn_dense(res[1], wg, wu, wd, res[0], mods, gfpost, tm=1024, tf=512, tiles_per_batch=n_lat // 1024)
            if not last:
                hc = _ffn_dense(resc[1], wg, wu, wd, resc[0], mods, gfpost, tm=1024, tf=512, tiles_per_batch=None)
        else:
            wg, wu, wd = w_gate_e[j].astype(BF16), w_up_e[j].astype(BF16), w_down_e[j].astype(BF16)
            h = _ffn_moe(res[1], res[2], wg, wu, wd, res[0], mods, gfpost, tm=1024, tf=512,
                         tiles_per_batch=n_lat // 1024)
            if not last:
                raise NotImplementedError("expert FFN on context tokens is not needed at this depth")
    return h.reshape(bsz, n_lat, D_MODEL)
```

```python
import functools

import numpy as np
import jax
import jax.numpy as jnp
from jax import lax
from jax.experimental import pallas as pl
from jax.experimental.pallas import tpu as pltpu

D_MODEL = 1024
DEPTH = 2
GRID_W = 64
HEAD_DIM = 64
EPS = 1e-6
A_HEADS = 4
A_WIDTH = 256
CHUNK = 128
B_WIDTH = 256
C_Q_HEADS = 8
C_WIDTH = 512
KV_WIDTH = 128
WINDOW = 128
BLOCK = 128
ROPE_BASE = 10000.0
OFF_B = 512
OFF_Q = 768
OFF_K = 1280
OFF_V = 1408
N_IN = 1536
D_FF = 3584
N_EXPERTS = 8

LANES = 128
VMEM_LIMIT = 56 * 1024 * 1024
NEG = -1e30

F32 = jnp.float32
BF16 = jnp.bfloat16


def _dot(a, b):
    return jnp.dot(a, b, preferred_element_type=F32)


def _cparams(sem):
    return pltpu.CompilerParams(dimension_semantics=sem, vmem_limit_bytes=VMEM_LIMIT)


def _adaln_kernel(c_ref, w_ref, b_ref, o_ref):
    c = c_ref[...]
    s = (c * jax.nn.sigmoid(c)).astype(BF16)
    o_ref[...] = _dot(s, w_ref[...].astype(BF16)) + b_ref[...]


def _adaln(cond8, w, b):
    tn = 1536
    return pl.pallas_call(
        _adaln_kernel,
        grid=(6 * D_MODEL // tn,),
        in_specs=[pl.BlockSpec((8, D_MODEL), lambda j: (0, 0)),
                  pl.BlockSpec((D_MODEL, tn), lambda j: (0, j)),
                  pl.BlockSpec((1, tn), lambda j: (0, j))],
        out_specs=pl.BlockSpec((8, tn), lambda j: (0, j)),
        out_shape=jax.ShapeDtypeStruct((8, 6 * D_MODEL), F32),
        compiler_params=_cparams(("arbitrary",)),
    )(cond8, w, b)


def _gelu_tanh(x):
    return 0.5 * x * (1.0 + jnp.tanh(0.7978845608028654 * (x + 0.044715 * x * x * x)))


def _premix_kernel(*refs, rope):
    if rope:
        (h_ref, sh_ref, sc_ref, g_ref, win_ref, ws_ref, bs_ref, gv_ref, hm_ref, wdft_ref,
         cos_ref, s1_ref, s2_ref, sgu_ref, re_ref, im_ref, q_ref, k_ref, v_ref) = refs
    else:
        (h_ref, sh_ref, sc_ref, g_ref, win_ref, ws_ref, bs_ref, gv_ref, hm_ref, wdft_ref,
         sgu_ref, re_ref, im_ref, q_ref, k_ref, v_ref) = refs
    x = h_ref[...]
    tm = x.shape[0]
    ms = jnp.mean(x * x, axis=-1, keepdims=True)
    xn = x * lax.rsqrt(ms + EPS) * g_ref[...]
    xm = (xn * (1.0 + sc_ref[...]) + sh_ref[...]).astype(BF16)
    z = _dot(xm, win_ref[...])

    a = _gelu_tanh(z[:, :OFF_B])
    u = a[:, :A_WIDTH]
    v = a[:, A_WIDTH:]
    msv = _dot((v * v).astype(BF16), hm_ref[...])
    vn = (v * lax.rsqrt(msv + EPS) * gv_ref[...]).astype(BF16)
    head = lax.broadcasted_iota(jnp.int32, (CHUNK, A_WIDTH), 1) // HEAD_DIM
    for ck in range(tm // CHUNK):
        rows = slice(ck * CHUNK, (ck + 1) * CHUNK)
        vc = vn[rows]
        sv = bs_ref[...]
        for hh in range(A_HEADS):
            sv = sv + jnp.where(head == hh, _dot(ws_ref[hh], vc), 0.0)
        sgu_ref[rows, :] = (u[rows] * sv).astype(BF16)

    f = _dot(z[:, OFF_B:OFF_Q].astype(BF16), wdft_ref[...])
    re_ref[...] = f[:, :B_WIDTH].astype(BF16)
    im_ref[...] = f[:, B_WIDTH:].astype(BF16)

    def rot(t):
        if not rope:
            return t
        return (t * cos_ref[...] + pltpu.roll(t, LANES - 16, 1) * s1_ref[...]
                + pltpu.roll(t, 16, 1) * s2_ref[...])

    for j in range(C_WIDTH // LANES):
        cols = slice(OFF_Q + j * LANES, OFF_Q + (j + 1) * LANES)
        q_ref[:, j * LANES:(j + 1) * LANES] = (rot(z[:, cols]) * 0.125).astype(BF16)

    lo = lax.broadcasted_iota(jnp.int32, (tm, LANES), 1) < HEAD_DIM
    kk = rot(z[:, OFF_K:OFF_V])
    kr = pltpu.roll(kk, HEAD_DIM, 1)
    k_ref[:, :LANES] = jnp.where(lo, kk, kr).astype(BF16)
    k_ref[:, LANES:] = jnp.where(lo, kr, kk).astype(BF16)
    vv = z[:, OFF_V:]
    vr = pltpu.roll(vv, HEAD_DIM, 1)
    v_ref[:, :LANES] = jnp.where(lo, vv, vr).astype(BF16)
    v_ref[:, LANES:] = jnp.where(lo, vr, vv).astype(BF16)


def _premix(h, mods, g, win, ws, bsx, gv, hm, wdft, rope_tabs, *, tm, tiles_per_batch, n_pos):
    T = h.shape[0]
    rope = rope_tabs is not None
    if tiles_per_batch is None:
        mrow = lambda i: 4
    else:
        mrow = lambda i: i // tiles_per_batch
    const2 = lambda i: (0, 0)
    in_specs = [
        pl.BlockSpec((tm, D_MODEL), lambda i: (i, 0)),
        pl.BlockSpec((None, None, 1, D_MODEL), lambda i: (mrow(i), 0, 0, 0)),
        pl.BlockSpec((None, None, 1, D_MODEL), lambda i: (mrow(i), 1, 0, 0)),
        pl.BlockSpec((1, D_MODEL), const2),
        pl.BlockSpec((D_MODEL, N_IN), const2),
        pl.BlockSpec((A_HEADS, CHUNK, CHUNK), lambda i: (0, 0, 0)),
        pl.BlockSpec((CHUNK, A_WIDTH), const2),
        pl.BlockSpec((1, A_WIDTH), const2),
        pl.BlockSpec((A_WIDTH, A_WIDTH), const2),
        pl.BlockSpec((B_WIDTH, 2 * B_WIDTH), const2),
    ]
    args = [h, mods, mods, g, win, ws, bsx, gv, hm, wdft]
    if rope:
        nt = n_pos // tm
        for t in rope_tabs:
            in_specs.append(pl.BlockSpec((tm, LANES), lambda i: (i % nt, 0)))
            args.append(t)
    widths = (A_WIDTH, B_WIDTH, B_WIDTH, C_WIDTH, 2 * KV_WIDTH, 2 * KV_WIDTH)
    return pl.pallas_call(
        functools.partial(_premix_kernel, rope=rope),
        grid=(T // tm,),
        in_specs=in_specs,
        out_specs=[pl.BlockSpec((tm, w), lambda i: (i, 0)) for w in widths],
        out_shape=[jax.ShapeDtypeStruct((T, w), BF16) for w in widths],
        compiler_params=_cparams(("parallel",)),
    )(*args)


def _dft1_kernel(m1_ref, twc_ref, tws_ref, re_ref, im_ref, o_ref):
    x = jnp.concatenate([re_ref[0], im_ref[0]], axis=0)
    a = _dot(m1_ref[...], x)
    nr = a.shape[0] // 2
    are, aim = a[:nr], a[nr:]
    c, s = twc_ref[...], tws_ref[...]
    bre = (are * c + aim * s).astype(BF16)
    bim = (aim * c - are * s).astype(BF16)
    for cl in range(o_ref.shape[2]):
        o_ref[0, 0, cl] = bre[:, cl * B_WIDTH:(cl + 1) * B_WIDTH]
        o_ref[0, 1, cl] = bim[:, cl * B_WIDTH:(cl + 1) * B_WIDTH]


def _dft3_kernel(m3_ref, x_ref, o_ref):
    x = jnp.concatenate([x_ref[0, 0], x_ref[0, 1]], axis=0)
    o_ref[0] = _dot(m3_ref[...], x).astype(BF16)


def _seq_dft(re, im, bsz, n_rows, m1, twc, tws, m3):
    ncol = GRID_W * B_WIDTH
    tn = 2048
    re3 = re.reshape(bsz, n_rows, ncol)
    im3 = im.reshape(bsz, n_rows, ncol)
    st1 = pl.pallas_call(
        _dft1_kernel,
        grid=(ncol // tn, bsz),
        in_specs=[pl.BlockSpec((2 * n_rows, 2 * n_rows), lambda j, b: (0, 0)),
                  pl.BlockSpec((n_rows, tn), lambda j, b: (0, j)),
                  pl.BlockSpec((n_rows, tn), lambda j, b: (0, j)),
                  pl.BlockSpec((1, n_rows, tn), lambda j, b: (b, 0, j)),
                  pl.BlockSpec((1, n_rows, tn), lambda j, b: (b, 0, j))],
        out_specs=pl.BlockSpec((1, 2, tn // B_WIDTH, n_rows, B_WIDTH), lambda j, b: (b, 0, j, 0, 0)),
        out_shape=jax.ShapeDtypeStruct((bsz, 2, GRID_W, n_rows, B_WIDTH), BF16),
        compiler_params=_cparams(("parallel", "parallel")),
    )(m1, twc, tws, re3, im3)
    st1 = st1.reshape(bsz, 2, GRID_W, n_rows * B_WIDTH)
    tn3 = 4096
    y = pl.pallas_call(
        _dft3_kernel,
        grid=(bsz, n_rows * B_WIDTH // tn3),
        in_specs=[pl.BlockSpec((GRID_W, 2 * GRID_W), lambda b, j: (0, 0)),
                  pl.BlockSpec((1, 2, GRID_W, tn3), lambda b, j: (b, 0, 0, j))],
        out_specs=pl.BlockSpec((1, GRID_W, tn3), lambda b, j: (b, 0, j)),
        out_shape=jax.ShapeDtypeStruct((bsz, GRID_W, n_rows * B_WIDTH), BF16),
        compiler_params=_cparams(("parallel", "parallel")),
    )(m3, st1)
    return y.reshape(bsz * GRID_W * n_rows, B_WIDTH)


def _ctx_dft_kernel(m_ref, re_ref, im_ref, o_ref):
    x = jnp.concatenate([re_ref[0], im_ref[0]], axis=0)
    o_ref[0] = _dot(m_ref[...], x).astype(BF16)


def _ctx_dft(re, im, bsz, m, mat):
    re3 = re.reshape(bsz, m, B_WIDTH)
    im3 = im.reshape(bsz, m, B_WIDTH)
    y = pl.pallas_call(
        _ctx_dft_kernel,
        grid=(bsz,),
        in_specs=[pl.BlockSpec((m, 2 * m), lambda b: (0, 0)),
                  pl.BlockSpec((1, m, B_WIDTH), lambda b: (b, 0, 0)),
                  pl.BlockSpec((1, m, B_WIDTH), lambda b: (b, 0, 0))],
        out_specs=pl.BlockSpec((1, m, B_WIDTH), lambda b: (b, 0, 0)),
        out_shape=jax.ShapeDtypeStruct((bsz, m, B_WIDTH), BF16),
        compiler_params=_cparams(("parallel",)),
    )(mat, re3, im3)
    return y.reshape(bsz * m, B_WIDTH)


def _attn_kernel(sink_ref, *refs, local, nb):
    if local:
        q_ref, kp_ref, kc_ref, kn_ref, vp_ref, vc_ref, vn_ref, kx_ref, vx_ref, o_ref = refs
    else:
        q_ref, kx_ref, vx_ref, o_ref = refs
    q = q_ref[0]
    tq = q.shape[0]
    lo = lax.broadcasted_iota(jnp.int32, (tq, LANES), 1) < HEAD_DIM
    zero = jnp.zeros((tq, LANES), BF16)
    if local:
        i = pl.program_id(1)
        row = lax.broadcasted_iota(jnp.int32, (BLOCK, 3 * BLOCK), 0)
        col = lax.broadcasted_iota(jnp.int32, (BLOCK, 3 * BLOCK), 1)
        valid = jnp.abs(col - BLOCK - row) <= WINDOW
        valid = valid & ((col >= BLOCK) | (i > 0)) & ((col < 2 * BLOCK) | (i < nb - 1))
    for g in range(2):
        gl = slice(g * LANES, (g + 1) * LANES)
        if local:
            kg = jnp.concatenate([kp_ref[0][:, gl], kc_ref[0][:, gl], kn_ref[0][:, gl], kx_ref[0][:, gl]], axis=0)
            vg = jnp.concatenate([vp_ref[0][:, gl], vc_ref[0][:, gl], vn_ref[0][:, gl], vx_ref[0][:, gl]], axis=0)
        else:
            kg = kx_ref[0][:, gl]
            vg = vx_ref[0][:, gl]
        qs = []
        for p in range(2):
            qp = q[:, g * 2 * LANES + p * LANES: g * 2 * LANES + (p + 1) * LANES]
            qs.append(jnp.where(lo, qp, zero))
            qs.append(jnp.where(lo, zero, qp))
        q4 = jnp.concatenate(qs, axis=0)
        s4 = lax.dot_general(q4, kg, (((1,), (1,)), ((), ())), preferred_element_type=F32)
        ps, rden = [], []
        for hl in range(4):
            s = s4[hl * tq:(hl + 1) * tq]
            if local:
                s = jnp.concatenate(
                    [jnp.where(valid, s[:, :3 * BLOCK], NEG), s[:, 3 * BLOCK:]], axis=1)
            sk = sink_ref[g * 4 + hl]
            m = jnp.maximum(jnp.max(s, axis=-1, keepdims=True), sk)
            p = jnp.exp(s - m)
            den = jnp.sum(p, axis=-1, keepdims=True) + jnp.exp(sk - m)
            ps.append(p.astype(BF16))
            rden.append(1.0 / den)
        o4 = _dot(jnp.concatenate(ps, axis=0), vg) * jnp.concatenate(rden, axis=0)
        for p in range(2):
            o_pair = jnp.where(lo, o4[(2 * p) * tq:(2 * p + 1) * tq], o4[(2 * p + 1) * tq:(2 * p + 2) * tq])
            c0 = g * 2 * LANES + p * LANES
            o_ref[0, :, c0:c0 + LANES] = o_pair.astype(BF16)


def _window_attention(q, k, v, kx, vx, sink, bsz, n, m):
    nb = n // BLOCK
    q3 = q.reshape(bsz, n, C_WIDTH)
    k3 = k.reshape(bsz, n, 2 * KV_WIDTH)
    v3 = v.reshape(bsz, n, 2 * KV_WIDTH)
    kx3 = kx.reshape(bsz, m, 2 * KV_WIDTH)
    vx3 = vx.reshape(bsz, m, 2 * KV_WIDTH)
    blk = lambda f: pl.BlockSpec((1, BLOCK, 2 * KV_WIDTH), f)
    prev = lambda b, i, s: (b, jnp.maximum(i - 1, 0), 0)
    cur = lambda b, i, s: (b, i, 0)
    nxt = lambda b, i, s: (b, jnp.minimum(i + 1, nb - 1), 0)
    ctxs = pl.BlockSpec((1, m, 2 * KV_WIDTH), lambda b, i, s: (b, 0, 0))
    o = pl.pallas_call(
        functools.partial(_attn_kernel, local=True, nb=nb),
        grid_spec=pltpu.PrefetchScalarGridSpec(
            num_scalar_prefetch=1,
            grid=(bsz, nb),
            in_specs=[pl.BlockSpec((1, BLOCK, C_WIDTH), cur),
                      blk(prev), blk(cur), blk(nxt), blk(prev), blk(cur), blk(nxt), ctxs, ctxs],
            out_specs=pl.BlockSpec((1, BLOCK, C_WIDTH), cur)),
        out_shape=jax.ShapeDtypeStruct((bsz, n, C_WIDTH), BF16),
        compiler_params=_cparams(("parallel", "parallel")),
    )(sink, q3, k3, k3, k3, v3, v3, v3, kx3, vx3)
    return o.reshape(bsz * n, C_WIDTH)


def _context_attention(q, kx, vx, sink, bsz, m):
    q3 = q.reshape(bsz, m, C_WIDTH)
    kx3 = kx.reshape(bsz, m, 2 * KV_WIDTH)
    vx3 = vx.reshape(bsz, m, 2 * KV_WIDTH)
    ctxs = pl.BlockSpec((1, m, 2 * KV_WIDTH), lambda b, s: (b, 0, 0))
    o = pl.pallas_call(
        functools.partial(_attn_kernel, local=False, nb=1),
        grid_spec=pltpu.PrefetchScalarGridSpec(
            num_scalar_prefetch=1,
            grid=(bsz,),
            in_specs=[pl.BlockSpec((1, m, C_WIDTH), lambda b, s: (b, 0, 0)), ctxs, ctxs],
            out_specs=pl.BlockSpec((1, m, C_WIDTH), lambda b, s: (b, 0, 0))),
        out_shape=jax.ShapeDtypeStruct((bsz, m, C_WIDTH), BF16),
        compiler_params=_cparams(("parallel",)),
    )(sink, q3, kx3, vx3)
    return o.reshape(bsz * m, C_WIDTH)


def _postmix_kernel(*refs, route):
    (sgu_ref, yf_ref, at_ref, wf_ref, wo_ref, h_ref, gp_ref, gt_ref, gf_ref, sh_ref, sc_ref) = refs[:11]
    if route:
        wr_ref, br_ref, hn_ref, y_ref, cmb_ref, cnt_ref = refs[11:]
    else:
        hn_ref, y_ref = refs[11:]
    fm = _dot(yf_ref[...], wf_ref[...]).astype(BF16)
    o = (_dot(sgu_ref[...], wo_ref[:A_WIDTH, :])
         + _dot(fm, wo_ref[A_WIDTH:A_WIDTH + B_WIDTH, :])
         + _dot(at_ref[...], wo_ref[A_WIDTH + B_WIDTH:, :]))
    ms = jnp.mean(o * o, axis=-1, keepdims=True)
    hn = h_ref[...] + gt_ref[...] * (o * lax.rsqrt(ms + EPS) * gp_ref[...])
    hn_ref[...] = hn
    ms2 = jnp.mean(hn * hn, axis=-1, keepdims=True)
    y = hn * lax.rsqrt(ms2 + EPS) * gf_ref[...] * (1.0 + sc_ref[...]) + sh_ref[...]
    y_ref[...] = y.astype(BF16)
    if route:
        yh = y.astype(BF16)
        yl = (y - yh.astype(F32)).astype(BF16)
        w = wr_ref[...]
        wh = w.astype(BF16)
        wl = (w - wh.astype(F32)).astype(BF16)
        lg = _dot(yh, wh) + _dot(yl, wh) + _dot(yh, wl) + br_ref[...]
        lane = lax.broadcasted_iota(jnp.int32, lg.shape, 1)
        lg = jnp.where(lane < N_EXPERTS, lg, NEG)
        m1 = jnp.max(lg, axis=-1, keepdims=True)
        i1 = jnp.min(jnp.where(lg == m1, lane, LANES), axis=-1, keepdims=True)
        lg2 = jnp.where(lane == i1, NEG, lg)
        m2 = jnp.max(lg2, axis=-1, keepdims=True)
        i2 = jnp.min(jnp.where(lg2 == m2, lane, LANES), axis=-1, keepdims=True)
        e2 = jnp.exp(m2 - m1)
        g1 = 1.0 / (1.0 + e2)
        g2 = e2 * g1
        cmb = jnp.where(lane == i1, g1, 0.0) + jnp.where(lane == i2, g2, 0.0)
        cmb_ref[...] = cmb
        cnt_ref[...] = jnp.sum(jnp.where(cmb > 0.0, 1.0, 0.0), axis=0, keepdims=True)


def _postmix(sgu, yf, at, wf, wo, h, mods, gp, gf, router, *, tm, tiles_per_batch):
    T = h.shape[0]
    if tiles_per_batch is None:
        mrow = lambda i: 4
    else:
        mrow = lambda i: i // tiles_per_batch
    const2 = lambda i: (0, 0)
    row = lambda w: pl.BlockSpec((tm, w), lambda i: (i, 0))
    mod = lambda j: pl.BlockSpec((None, None, 1, D_MODEL), lambda i: (mrow(i), j, 0, 0))
    in_specs = [row(A_WIDTH), row(B_WIDTH), row(C_WIDTH),
                pl.BlockSpec((B_WIDTH, B_WIDTH), const2),
                pl.BlockSpec((D_MODEL, D_MODEL), const2),
                row(D_MODEL),
                pl.BlockSpec((1, D_MODEL), const2),
                mod(2),
                pl.BlockSpec((1, D_MODEL), const2),
                mod(3), mod(4)]
    args = [sgu, yf, at, wf, wo, h, gp, mods, gf, mods, mods]
    out_specs = [row(D_MODEL), row(D_MODEL)]
    out_shape = [jax.ShapeDtypeStruct((T, D_MODEL), F32), jax.ShapeDtypeStruct((T, D_MODEL), BF16)]
    route = router is not None
    if route:
        in_specs += [pl.BlockSpec((D_MODEL, LANES), const2), pl.BlockSpec((1, LANES), const2)]
        args += list(router)
        out_specs += [row(LANES), pl.BlockSpec((None, 1, LANES), lambda i: (i, 0, 0))]
        out_shape += [jax.ShapeDtypeStruct((T, LANES), F32), jax.ShapeDtypeStruct((T // tm, 1, LANES), F32)]
    return pl.pallas_call(
        functools.partial(_postmix_kernel, route=route),
        grid=(T // tm,),
        in_specs=in_specs, out_specs=out_specs, out_shape=out_shape,
        compiler_params=_cparams(("parallel",)),
    )(*args)


def _ffn_epilogue(f, h_ref, gp_ref, gt_ref, o_ref):
    ms = jnp.mean(f * f, axis=-1, keepdims=True)
    o_ref[...] = h_ref[...] + gt_ref[...] * (f * lax.rsqrt(ms + EPS) * gp_ref[...])


def _ffn_dense_kernel(y_ref, wg_ref, wu_ref, wd_ref, h_ref, gp_ref, gt_ref, o_ref, acc_ref):
    k = pl.program_id(1)
    y = y_ref[...]
    gate = _dot(y, wg_ref[...])
    up = _dot(y, wu_ref[...])
    act = (gate * jax.nn.sigmoid(gate) * up).astype(BF16)
    part = _dot(act, wd_ref[...])

    @pl.when(k == 0)
    def _():
        acc_ref[...] = part

    @pl.when(k > 0)
    def _():
        acc_ref[...] += part

    @pl.when(k == pl.num_programs(1) - 1)
    def _():
        _ffn_epilogue(acc_ref[...], h_ref, gp_ref, gt_ref, o_ref)


def _ffn_dense(y, wg, wu, wd, h, mods, gp, *, tm, tf, tiles_per_batch):
    T = h.shape[0]
    if tiles_per_batch is None:
        mrow = lambda i, k: 4
    else:
        mrow = lambda i, k: i // tiles_per_batch
    return pl.pallas_call(
        _ffn_dense_kernel,
        grid=(T // tm, D_FF // tf),
        in_specs=[pl.BlockSpec((tm, D_MODEL), lambda i, k: (i, 0)),
                  pl.BlockSpec((D_MODEL, tf), lambda i, k: (0, k)),
                  pl.BlockSpec((D_MODEL, tf), lambda i, k: (0, k)),
                  pl.BlockSpec((tf, D_MODEL), lambda i, k: (k, 0)),
                  pl.BlockSpec((tm, D_MODEL), lambda i, k: (i, 0)),
                  pl.BlockSpec((1, D_MODEL), lambda i, k: (0, 0)),
                  pl.BlockSpec((None, None, 1, D_MODEL), lambda i, k: (mrow(i, k), 5, 0, 0))],
        out_specs=pl.BlockSpec((tm, D_MODEL), lambda i, k: (i, 0)),
        out_shape=jax.ShapeDtypeStruct((T, D_MODEL), F32),
        scratch_shapes=[pltpu.VMEM((tm, D_MODEL), F32)],
        compiler_params=_cparams(("parallel", "arbitrary")),
    )(y, wg, wu, wd, h, gp, mods)


MOE_TB = 512
MOE_TM = 512
SUB = 128
NSUB = MOE_TB // SUB
RUN_ALIGN = 16


def _moe_tile_bound(T):
    nt = T // MOE_TB
    rows = 2 * T + nt * N_EXPERTS * (RUN_ALIGN - 1) + N_EXPERTS * (SUB + MOE_TM - 1)
    return -(-rows // MOE_TM)


def _moe_plan(cnt, T):
    p = (cnt + RUN_ALIGN - 1) // RUN_ALIGN * RUN_ALIGN
    base = jnp.cumsum(p, axis=0) - p
    used = jnp.sum(p, axis=0)
    tiles = (used + SUB + MOE_TM - 1) // MOE_TM
    tend = jnp.cumsum(tiles)
    off = (tend - tiles) * MOE_TM
    rowbase = (off[None, :] + base).astype(jnp.int32)
    nt_bound = _moe_tile_bound(T)
    j = jnp.arange(nt_bound, dtype=jnp.int32)
    tile_e = jnp.minimum(jnp.sum(j[:, None] >= tend[None, :], axis=1), N_EXPERTS - 1).astype(jnp.int32)
    nvalid = tend[-1].astype(jnp.int32)
    tile_blk = jnp.minimum(j, nvalid - 1)
    return rowbase, tile_e, tile_blk, nvalid.reshape(1)


def _run_copy_out(stage, ys_ref, sems, rb_ref, b, e, s):
    r0 = pl.multiple_of(rb_ref[b * N_EXPERTS + e] + SUB * s, RUN_ALIGN)
    return pltpu.make_async_copy(stage.at[e, s], ys_ref.at[pl.ds(r0, SUB), :], sems.at[e, s])


def _dispatch_kernel(rb_ref, cn_ref, y_ref, cmb_ref, rbv_ref, ltri_ref, ysin_ref, d_ref, ys_ref, stage, sems):
    del ysin_ref
    b = pl.program_id(0)
    sel = cmb_ref[...] > 0.0
    rk = _dot(ltri_ref[...], jnp.where(sel, 1.0, 0.0).astype(BF16))
    d_ref[...] = jnp.where(sel, rk.astype(jnp.int32) + rbv_ref[...], -1)
    rkt = jnp.where(sel, rk, -1.0).T
    y = y_ref[...]
    jj = lax.broadcasted_iota(jnp.int32, (SUB, y.shape[0]), 0).astype(F32)
    for e in range(N_EXPERTS):
        row = rkt[e:e + 1, :]
        for s in range(NSUB):
            @pl.when(cn_ref[b * N_EXPERTS + e] > SUB * s)
            def _(e=e, s=s, row=row):
                g = jnp.where(row == jj + float(SUB * s), 1.0, 0.0).astype(BF16)
                stage[e, s] = _dot(g, y).astype(BF16)
                _run_copy_out(stage, ys_ref, sems, rb_ref, b, e, s).start()
    for e in range(N_EXPERTS):
        for s in range(NSUB):
            @pl.when(cn_ref[b * N_EXPERTS + e] > SUB * s)
            def _(e=e, s=s):
                _run_copy_out(stage, ys_ref, sems, rb_ref, b, e, s).wait()


def _moe_dispatch(y, cmb, rowbase, cnt, nt_bound):
    T = y.shape[0]
    nt = T // MOE_TB
    rbv = jnp.zeros((nt, 1, LANES), jnp.int32).at[:, 0, :N_EXPERTS].set(rowbase)
    ltri = jnp.asarray(np.tril(np.ones((MOE_TB, MOE_TB), np.float32), -1)).astype(BF16)
    ys0 = jnp.zeros((nt_bound * MOE_TM, D_MODEL), BF16)
    return pl.pallas_call(
        _dispatch_kernel,
        grid_spec=pltpu.PrefetchScalarGridSpec(
            num_scalar_prefetch=2,
            grid=(nt,),
            in_specs=[pl.BlockSpec((MOE_TB, D_MODEL), lambda b, rb, cn: (b, 0)),
                      pl.BlockSpec((MOE_TB, LANES), lambda b, rb, cn: (b, 0)),
                      pl.BlockSpec((None, 1, LANES), lambda b, rb, cn: (b, 0, 0)),
                      pl.BlockSpec((MOE_TB, MOE_TB), lambda b, rb, cn: (0, 0)),
                      pl.BlockSpec(memory_space=pl.ANY)],
            out_specs=[pl.BlockSpec((MOE_TB, LANES), lambda b, rb, cn: (b, 0)),
                       pl.BlockSpec(memory_space=pl.ANY)],
            scratch_shapes=[pltpu.VMEM((N_EXPERTS, NSUB, SUB, D_MODEL), BF16),
                            pltpu.SemaphoreType.DMA((N_EXPERTS, NSUB))]),
        out_shape=[jax.ShapeDtypeStruct((T, LANES), jnp.int32),
                   jax.ShapeDtypeStruct((nt_bound * MOE_TM, D_MODEL), BF16)],
        input_output_aliases={6: 1},
        compiler_params=_cparams(("arbitrary",)),
    )(rowbase.reshape(-1), cnt.reshape(-1), y, cmb, rbv, ltri, ys0)


def _ffn_group_kernel(te_ref, tb_ref, nv_ref, y_ref, wg_ref, wu_ref, wd_ref, o_ref, acc_ref):
    del te_ref, tb_ref
    j = pl.program_id(0)
    k = pl.program_id(1)

    @pl.when(j < nv_ref[0])
    def _():
        y = y_ref[...]
        gate = _dot(y, wg_ref[...])
        up = _dot(y, wu_ref[...])
        act = (gate * jax.nn.sigmoid(gate) * up).astype(BF16)
        part = _dot(act, wd_ref[...])

        @pl.when(k == 0)
        def _():
            acc_ref[...] = part

        @pl.when(k > 0)
        def _():
            acc_ref[...] += part

        @pl.when(k == pl.num_programs(1) - 1)
        def _():
            o_ref[...] = acc_ref[...].astype(BF16)

    @pl.when((j >= nv_ref[0]) & (k == 0))
    def _():
        o_ref[...] = jnp.zeros_like(o_ref)


def _ffn_group(ys, wg, wu, wd, tile_e, tile_blk, nvalid, *, tf):
    rows = ys.shape[0]
    nk = D_FF // tf
    kk = lambda j, k, nv: jnp.where(j < nv[0], k, nk - 1)
    return pl.pallas_call(
        _ffn_group_kernel,
        grid_spec=pltpu.PrefetchScalarGridSpec(
            num_scalar_prefetch=3,
            grid=(rows // MOE_TM, nk),
            in_specs=[pl.BlockSpec((MOE_TM, D_MODEL), lambda j, k, te, tb, nv: (tb[j], 0)),
                      pl.BlockSpec((None, D_MODEL, tf), lambda j, k, te, tb, nv: (te[j], 0, kk(j, k, nv))),
                      pl.BlockSpec((None, D_MODEL, tf), lambda j, k, te, tb, nv: (te[j], 0, kk(j, k, nv))),
                      pl.BlockSpec((None, tf, D_MODEL), lambda j, k, te, tb, nv: (te[j], kk(j, k, nv), 0))],
            out_specs=pl.BlockSpec((MOE_TM, D_MODEL), lambda j, k, te, tb, nv: (j, 0)),
            scratch_shapes=[pltpu.VMEM((MOE_TM, D_MODEL), F32)]),
        out_shape=jax.ShapeDtypeStruct((rows, D_MODEL), BF16),
        compiler_params=_cparams(("arbitrary", "arbitrary")),
    )(tile_e, tile_blk, nvalid, ys, wg, wu, wd)


def _run_copy_in(fs_ref, fbuf, sems, rb_ref, b, e, s):
    r0 = pl.multiple_of(rb_ref[b * N_EXPERTS + e] + SUB * s, RUN_ALIGN)
    return pltpu.make_async_copy(fs_ref.at[pl.ds(r0, SUB), :], fbuf.at[e, s], sems.at[e, s])


def _combine_kernel(rb_ref, cn_ref, d_ref, cmb_ref, h_ref, gp_ref, gt_ref, fs_ref, o_ref, fbuf, sems, acc_ref):
    b = pl.program_id(0)
    for e in range(N_EXPERTS):
        for s in range(NSUB):
            @pl.when(cn_ref[b * N_EXPERTS + e] > SUB * s)
            def _(e=e, s=s):
                _run_copy_in(fs_ref, fbuf, sems, rb_ref, b, e, s).start()
    acc_ref[...] = jnp.zeros_like(acc_ref)
    d = d_ref[...]
    cmb = cmb_ref[...]
    rr = lax.broadcasted_iota(jnp.int32, (d.shape[0], SUB), 1)
    for e in range(N_EXPERTS):
        de = d[:, e:e + 1]
        ce = cmb[:, e:e + 1]
        for s in range(NSUB):
            @pl.when(cn_ref[b * N_EXPERTS + e] > SUB * s)
            def _(e=e, s=s, de=de, ce=ce):
                _run_copy_in(fs_ref, fbuf, sems, rb_ref, b, e, s).wait()
                r0 = rb_ref[b * N_EXPERTS + e] + SUB * s
                pick = jnp.where(de - r0 == rr, 1.0, 0.0).astype(BF16)
                acc_ref[...] += ce * _dot(pick, fbuf[e, s])
    _ffn_epilogue(acc_ref[...], h_ref, gp_ref, gt_ref, o_ref)


def _moe_combine(fs, dest, cmb, h, mods, gp, rowbase, cnt, *, tiles_per_batch):
    T = h.shape[0]
    nt = T // MOE_TB
    return pl.pallas_call(
        _combine_kernel,
        grid_spec=pltpu.PrefetchScalarGridSpec(
            num_scalar_prefetch=2,
            grid=(nt,),
            in_specs=[pl.BlockSpec((MOE_TB, LANES), lambda b, rb, cn: (b, 0)),
                      pl.BlockSpec((MOE_TB, LANES), lambda b, rb, cn: (b, 0)),
                      pl.BlockSpec((MOE_TB, D_MODEL), lambda b, rb, cn: (b, 0)),
                      pl.BlockSpec((1, D_MODEL), lambda b, rb, cn: (0, 0)),
                      pl.BlockSpec((None, None, 1, D_MODEL), lambda b, rb, cn: (b // tiles_per_batch, 5, 0, 0)),
                      pl.BlockSpec(memory_space=pl.ANY)],
            out_specs=pl.BlockSpec((MOE_TB, D_MODEL), lambda b, rb, cn: (b, 0)),
            scratch_shapes=[pltpu.VMEM((N_EXPERTS, NSUB, SUB, D_MODEL), BF16),
                            pltpu.SemaphoreType.DMA((N_EXPERTS, NSUB)),
                            pltpu.VMEM((MOE_TB, D_MODEL), F32)]),
        out_shape=jax.ShapeDtypeStruct((T, D_MODEL), F32),
        compiler_params=_cparams(("arbitrary",)),
    )(rowbase.reshape(-1), cnt.reshape(-1), dest, cmb, h, gp, mods, fs)


def _ffn_moe(y, cmb, cnt_tiles, wg, wu, wd, h, mods, gp, *, tf, tiles_per_batch):
    T = h.shape[0]
    cnt = cnt_tiles.reshape(T // MOE_TB, LANES)[:, :N_EXPERTS].astype(jnp.int32)
    rowbase, tile_e, tile_blk, nvalid = _moe_plan(cnt, T)
    dest, ys = _moe_dispatch(y, cmb, rowbase, cnt, _moe_tile_bound(T))
    fs = _ffn_group(ys, wg, wu, wd, tile_e, tile_blk, nvalid, tf=tf)
    return _moe_combine(fs, dest, cmb, h, mods, gp, rowbase, cnt, tiles_per_batch=tiles_per_batch)


def _blockdiag(m, reps):
    n = m.shape[0]
    out = np.zeros((n * reps, n * reps), np.float64)
    for r in range(reps):
        out[r * n:(r + 1) * n, r * n:(r + 1) * n] = m
    return out


def _dft_tables(n_rows):
    n = n_rows * GRID_W
    r = np.arange(n_rows)
    c = np.arange(GRID_W)
    a1 = 2 * np.pi * np.outer(r, r) / n_rows
    c1, s1 = np.cos(a1), np.sin(a1)
    m1 = np.block([[c1, s1], [-s1, c1]]) * 0.125
    at = 2 * np.pi * np.outer(r, c) / n
    twc = np.repeat(np.cos(at), B_WIDTH, axis=1)
    tws = np.repeat(np.sin(at), B_WIDTH, axis=1)
    a3 = 2 * np.pi * np.outer(c, c) / GRID_W
    m3 = np.concatenate([np.cos(a3), np.sin(a3)], axis=1) * (8.0 / np.sqrt(n))
    f32 = lambda t: jnp.asarray(t.astype(np.float32))
    return f32(m1).astype(BF16), f32(twc), f32(tws), f32(m3).astype(BF16)


def _channel_dft_table():
    d = np.arange(HEAD_DIM)
    a = 2 * np.pi * np.outer(d, d) / HEAD_DIM
    w = np.concatenate([_blockdiag(np.cos(a), 4), -_blockdiag(np.sin(a), 4)], axis=1) * 0.125
    return jnp.asarray(w.astype(np.float32)).astype(BF16)


def _ctx_dft_table(m):
    p = np.arange(m)
    a = 2 * np.pi * np.outer(p, p) / m
    w = np.concatenate([np.cos(a), np.sin(a)], axis=1) * (8.0 / np.sqrt(m * HEAD_DIM))
    return jnp.asarray(w.astype(np.float32)).astype(BF16)


def _rope_tables(n_tok):
    rows = n_tok // GRID_W
    row = jnp.broadcast_to(jnp.arange(rows)[:, None], (rows, GRID_W)).reshape(-1)
    col = jnp.broadcast_to(jnp.arange(GRID_W)[None, :], (rows, GRID_W)).reshape(-1)
    half = HEAD_DIM // 2
    inv = ROPE_BASE ** (-jnp.arange(0, half, 2, dtype=F32) / half)
    ang = jnp.stack([row.astype(F32)[:, None] * inv, col.astype(F32)[:, None] * inv], axis=1)
    cos, sin = jnp.cos(ang), jnp.sin(ang)
    zer = jnp.zeros_like(sin)
    lay = lambda a, b: jnp.tile(jnp.stack([a, b], axis=2).reshape(n_tok, HEAD_DIM), (1, LANES // HEAD_DIM))
    return lay(cos, cos), lay(-sin, zer), lay(zer, sin)


def kernel(x, c, ctx, c_ctx, w_ada, b_ada, g_mix_pre, g_mix_post, g_ffn_pre, g_ffn_post,
           w_in, w_s, b_s, g_v, w_f, sink, w_out, w_gate_d, w_up_d, w_down_d,
           w_router, b_router, w_gate_e, w_up_e, w_down_e):
    bsz, n_lat, _ = x.shape
    n_ctx = ctx.shape[1]
    T, Tc = bsz * n_lat, bsz * n_ctx
    tm = 512
    tpb = n_lat // tm

    rope_tabs = _rope_tables(n_lat)
    m1, twc, tws, m3 = _dft_tables(n_lat // GRID_W)
    wdft = _channel_dft_table()
    mctx = _ctx_dft_table(n_ctx)
    hm = jnp.asarray(_blockdiag(np.full((HEAD_DIM, HEAD_DIM), 1.0 / HEAD_DIM), A_HEADS).astype(np.float32)).astype(BF16)

    cond8 = jnp.zeros((8, D_MODEL), F32).at[:bsz].set(c).at[4].set(c_ctx)
    h = x.reshape(T, D_MODEL)
    hc = ctx.reshape(Tc, D_MODEL)

    for i in range(DEPTH):
        last = i == DEPTH - 1
        mods = _adaln(cond8, w_ada[i], b_ada[i][None, :]).reshape(8, 6, 1, D_MODEL)
        win = w_in[i].astype(BF16)
        ws = w_s[i].astype(BF16)
        bsx = jnp.repeat(b_s[i].T, HEAD_DIM, axis=1)
        gv = g_v[i].reshape(1, A_WIDTH)
        wf = jax.scipy.linalg.block_diag(*[w_f[i][g] for g in range(4)]).astype(BF16)
        wo = w_out[i].astype(BF16)
        gpre = g_mix_pre[i][None, :]
        gpost = g_mix_post[i][None, :]
        gfpre = g_ffn_pre[i][None, :]
        gfpost = g_ffn_post[i][None, :]
        sk = sink[i]

        sgu, fre, fim, q, k, v = _premix(h, mods, gpre, win, ws, bsx, gv, hm, wdft, rope_tabs,
                                         tm=tm, tiles_per_batch=tpb, n_pos=n_lat)
        sguc, frec, fimc, qc, kc, vc = _premix(hc, mods, gpre, win, ws, bsx, gv, hm, wdft, None,
                                               tm=tm, tiles_per_batch=None, n_pos=n_ctx)
        yf = _seq_dft(fre, fim, bsz, n_lat // GRID_W, m1, twc, tws, m3)
        at = _window_attention(q, k, v, kc, vc, sk, bsz, n_lat, n_ctx)

        if i % 2 == 0:
            router = None
        else:
            j = i // 2
            wr = jnp.zeros((D_MODEL, LANES), F32).at[:, :N_EXPERTS].set(w_router[j])
            br = jnp.zeros((1, LANES), F32).at[0, :N_EXPERTS].set(b_router[j])
            router = (wr, br)
        res = _postmix(sgu, yf, at, wf, wo, h, mods, gpost, gfpre, router, tm=tm, tiles_per_batch=tpb)
        if not last:
            yfc = _ctx_dft(frec, fimc, bsz, n_ctx, mctx)
            atc = _context_attention(qc, kc, vc, sk, bsz, n_ctx)
            resc = _postmix(sguc, yfc, atc, wf, wo, hc, mods, gpost, gfpre, router, tm=tm, tiles_per_batch=None)

        j = i // 2
        if i % 2 == 0:
            wg, wu, wd = w_gate_d[j].astype(BF16), w_up_d[j].astype(BF16), w_down_d[j].astype(BF16)
            h = _ffn_dense(res[1], wg, wu, wd, res[0], mods, gfpost, tm=1024, tf=512, tiles_per_batch=n_lat // 1024)
            if not last:
                hc = _ffn_dense(resc[1], wg, wu, wd, resc[0], mods, gfpost, tm=1024, tf=512, tiles_per_batch=None)
        else:
            wg, wu, wd = w_gate_e[j].astype(BF16), w_up_e[j].astype(BF16), w_down_e[j].astype(BF16)
            assert last and tm == MOE_TB, "the expert FFN is only built for the final layer's latent tokens"
            h = _ffn_moe(res[1], res[2], res[3], wg, wu, wd, res[0], mods, gfpost, tf=512,
                         tiles_per_batch=n_lat // MOE_TB)
    return h.reshape(bsz, n_lat, D_MODEL)
```

```python
import functools

import numpy as np
import jax
import jax.numpy as jnp
from jax import lax
from jax.experimental import pallas as pl
from jax.experimental.pallas import tpu as pltpu

D_MODEL = 1024
DEPTH = 2
GRID_W = 64
HEAD_DIM = 64
EPS = 1e-6
A_HEADS = 4
A_WIDTH = 256
CHUNK = 128
B_WIDTH = 256
C_Q_HEADS = 8
C_WIDTH = 512
KV_WIDTH = 128
WINDOW = 128
BLOCK = 128
ROPE_BASE = 10000.0
OFF_B = 512
OFF_Q = 768
OFF_K = 1280
OFF_V = 1408
N_IN = 1536
D_FF = 3584
N_EXPERTS = 8

LANES = 128
VMEM_LIMIT = 56 * 1024 * 1024
NEG = -1e30

F32 = jnp.float32
BF16 = jnp.bfloat16


def _dot(a, b):
    return jnp.dot(a, b, preferred_element_type=F32)


def _cparams(sem):
    return pltpu.CompilerParams(dimension_semantics=sem, vmem_limit_bytes=VMEM_LIMIT)


def _adaln_kernel(c_ref, w_ref, b_ref, o_ref):
    c = c_ref[...]
    s = (c * jax.nn.sigmoid(c)).astype(BF16)
    o_ref[...] = _dot(s, w_ref[...].astype(BF16)) + b_ref[...]


def _adaln(cond8, w, b):
    tn = 1536
    return pl.pallas_call(
        _adaln_kernel,
        grid=(6 * D_MODEL // tn,),
        in_specs=[pl.BlockSpec((8, D_MODEL), lambda j: (0, 0)),
                  pl.BlockSpec((D_MODEL, tn), lambda j: (0, j)),
                  pl.BlockSpec((1, tn), lambda j: (0, j))],
        out_specs=pl.BlockSpec((8, tn), lambda j: (0, j)),
        out_shape=jax.ShapeDtypeStruct((8, 6 * D_MODEL), F32),
        compiler_params=_cparams(("arbitrary",)),
    )(cond8, w, b)


def _gelu_tanh(x):
    return 0.5 * x * (1.0 + jnp.tanh(0.7978845608028654 * (x + 0.044715 * x * x * x)))


def _premix_kernel(*refs, rope):
    if rope:
        (h_ref, sh_ref, sc_ref, g_ref, win_ref, ws_ref, bs_ref, gv_ref, hm_ref, wdft_ref,
         cos_ref, s1_ref, s2_ref, sgu_ref, re_ref, im_ref, q_ref, k_ref, v_ref) = refs
    else:
        (h_ref, sh_ref, sc_ref, g_ref, win_ref, ws_ref, bs_ref, gv_ref, hm_ref, wdft_ref,
         sgu_ref, re_ref, im_ref, q_ref, k_ref, v_ref) = refs
    x = h_ref[...]
    tm = x.shape[0]
    ms = jnp.mean(x * x, axis=-1, keepdims=True)
    xn = x * lax.rsqrt(ms + EPS) * g_ref[...]
    xm = (xn * (1.0 + sc_ref[...]) + sh_ref[...]).astype(BF16)
    z = _dot(xm, win_ref[...])

    a = _gelu_tanh(z[:, :OFF_B])
    u = a[:, :A_WIDTH]
    v = a[:, A_WIDTH:]
    msv = _dot((v * v).astype(BF16), hm_ref[...])
    vn = (v * lax.rsqrt(msv + EPS) * gv_ref[...]).astype(BF16)
    head = lax.broadcasted_iota(jnp.int32, (CHUNK, A_WIDTH), 1) // HEAD_DIM
    for ck in range(tm // CHUNK):
        rows = slice(ck * CHUNK, (ck + 1) * CHUNK)
        vc = vn[rows]
        sv = bs_ref[...]
        for hh in range(A_HEADS):
            sv = sv + jnp.where(head == hh, _dot(ws_ref[hh], vc), 0.0)
        sgu_ref[rows, :] = (u[rows] * sv).astype(BF16)

    f = _dot(z[:, OFF_B:OFF_Q].astype(BF16), wdft_ref[...])
    re_ref[...] = f[:, :B_WIDTH].astype(BF16)
    im_ref[...] = f[:, B_WIDTH:].astype(BF16)

    def rot(t):
        if not rope:
            return t
        return (t * cos_ref[...] + pltpu.roll(t, LANES - 16, 1) * s1_ref[...]
                + pltpu.roll(t, 16, 1) * s2_ref[...])

    for j in range(C_WIDTH // LANES):
        cols = slice(OFF_Q + j * LANES, OFF_Q + (j + 1) * LANES)
        q_ref[:, j * LANES:(j + 1) * LANES] = (rot(z[:, cols]) * 0.125).astype(BF16)

    lo = lax.broadcasted_iota(jnp.int32, (tm, LANES), 1) < HEAD_DIM
    kk = rot(z[:, OFF_K:OFF_V])
    kr = pltpu.roll(kk, HEAD_DIM, 1)
    k_ref[:, :LANES] = jnp.where(lo, kk, kr).astype(BF16)
    k_ref[:, LANES:] = jnp.where(lo, kr, kk).astype(BF16)
    vv = z[:, OFF_V:]
    vr = pltpu.roll(vv, HEAD_DIM, 1)
    v_ref[:, :LANES] = jnp.where(lo, vv, vr).astype(BF16)
    v_ref[:, LANES:] = jnp.where(lo, vr, vv).astype(BF16)


def _premix(h, mods, g, win, ws, bsx, gv, hm, wdft, rope_tabs, *, tm, tiles_per_batch, n_pos):
    T = h.shape[0]
    rope = rope_tabs is not None
    if tiles_per_batch is None:
        mrow = lambda i: 4
    else:
        mrow = lambda i: i // tiles_per_batch
    const2 = lambda i: (0, 0)
    in_specs = [
        pl.BlockSpec((tm, D_MODEL), lambda i: (i, 0)),
        pl.BlockSpec((None, None, 1, D_MODEL), lambda i: (mrow(i), 0, 0, 0)),
        pl.BlockSpec((None, None, 1, D_MODEL), lambda i: (mrow(i), 1, 0, 0)),
        pl.BlockSpec((1, D_MODEL), const2),
        pl.BlockSpec((D_MODEL, N_IN), const2),
        pl.BlockSpec((A_HEADS, CHUNK, CHUNK), lambda i: (0, 0, 0)),
        pl.BlockSpec((CHUNK, A_WIDTH), const2),
        pl.BlockSpec((1, A_WIDTH), const2),
        pl.BlockSpec((A_WIDTH, A_WIDTH), const2),
        pl.BlockSpec((B_WIDTH, 2 * B_WIDTH), const2),
    ]
    args = [h, mods, mods, g, win, ws, bsx, gv, hm, wdft]
    if rope:
        nt = n_pos // tm
        for t in rope_tabs:
            in_specs.append(pl.BlockSpec((tm, LANES), lambda i: (i % nt, 0)))
            args.append(t)
    widths = (A_WIDTH, B_WIDTH, B_WIDTH, C_WIDTH, 2 * KV_WIDTH, 2 * KV_WIDTH)
    return pl.pallas_call(
        functools.partial(_premix_kernel, rope=rope),
        grid=(T // tm,),
        in_specs=in_specs,
        out_specs=[pl.BlockSpec((tm, w), lambda i: (i, 0)) for w in widths],
        out_shape=[jax.ShapeDtypeStruct((T, w), BF16) for w in widths],
        compiler_params=_cparams(("parallel",)),
    )(*args)


def _dft1_kernel(m1_ref, twc_ref, tws_ref, re_ref, im_ref, o_ref):
    x = jnp.concatenate([re_ref[0], im_ref[0]], axis=0)
    a = _dot(m1_ref[...], x)
    nr = a.shape[0] // 2
    are, aim = a[:nr], a[nr:]
    c, s = twc_ref[...], tws_ref[...]
    bre = (are * c + aim * s).astype(BF16)
    bim = (aim * c - are * s).astype(BF16)
    for cl in range(o_ref.shape[2]):
        o_ref[0, 0, cl] = bre[:, cl * B_WIDTH:(cl + 1) * B_WIDTH]
        o_ref[0, 1, cl] = bim[:, cl * B_WIDTH:(cl + 1) * B_WIDTH]


def _dft3_kernel(m3_ref, x_ref, o_ref):
    x = jnp.concatenate([x_ref[0, 0], x_ref[0, 1]], axis=0)
    o_ref[0] = _dot(m3_ref[...], x).astype(BF16)


def _seq_dft(re, im, bsz, n_rows, m1, twc, tws, m3):
    ncol = GRID_W * B_WIDTH
    tn = 2048
    re3 = re.reshape(bsz, n_rows, ncol)
    im3 = im.reshape(bsz, n_rows, ncol)
    st1 = pl.pallas_call(
        _dft1_kernel,
        grid=(ncol // tn, bsz),
        in_specs=[pl.BlockSpec((2 * n_rows, 2 * n_rows), lambda j, b: (0, 0)),
                  pl.BlockSpec((n_rows, tn), lambda j, b: (0, j)),
                  pl.BlockSpec((n_rows, tn), lambda j, b: (0, j)),
                  pl.BlockSpec((1, n_rows, tn), lambda j, b: (b, 0, j)),
                  pl.BlockSpec((1, n_rows, tn), lambda j, b: (b, 0, j))],
        out_specs=pl.BlockSpec((1, 2, tn // B_WIDTH, n_rows, B_WIDTH), lambda j, b: (b, 0, j, 0, 0)),
        out_shape=jax.ShapeDtypeStruct((bsz, 2, GRID_W, n_rows, B_WIDTH), BF16),
        compiler_params=_cparams(("parallel", "parallel")),
    )(m1, twc, tws, re3, im3)
    st1 = st1.reshape(bsz, 2, GRID_W, n_rows * B_WIDTH)
    tn3 = 4096
    y = pl.pallas_call(
        _dft3_kernel,
        grid=(bsz, n_rows * B_WIDTH // tn3),
        in_specs=[pl.BlockSpec((GRID_W, 2 * GRID_W), lambda b, j: (0, 0)),
                  pl.BlockSpec((1, 2, GRID_W, tn3), lambda b, j: (b, 0, 0, j))],
        out_specs=pl.BlockSpec((1, GRID_W, tn3), lambda b, j: (b, 0, j)),
        out_shape=jax.ShapeDtypeStruct((bsz, GRID_W, n_rows * B_WIDTH), BF16),
        compiler_params=_cparams(("parallel", "parallel")),
    )(m3, st1)
    return y.reshape(bsz * GRID_W * n_rows, B_WIDTH)


def _ctx_dft_kernel(m_ref, re_ref, im_ref, o_ref):
    x = jnp.concatenate([re_ref[0], im_ref[0]], axis=0)
    o_ref[0] = _dot(m_ref[...], x).astype(BF16)


def _ctx_dft(re, im, bsz, m, mat):
    re3 = re.reshape(bsz, m, B_WIDTH)
    im3 = im.reshape(bsz, m, B_WIDTH)
    y = pl.pallas_call(
        _ctx_dft_kernel,
        grid=(bsz,),
        in_specs=[pl.BlockSpec((m, 2 * m), lambda b: (0, 0)),
                  pl.BlockSpec((1, m, B_WIDTH), lambda b: (b, 0, 0)),
                  pl.BlockSpec((1, m, B_WIDTH), lambda b: (b, 0, 0))],
        out_specs=pl.BlockSpec((1, m, B_WIDTH), lambda b: (b, 0, 0)),
        out_shape=jax.ShapeDtypeStruct((bsz, m, B_WIDTH), BF16),
        compiler_params=_cparams(("parallel",)),
    )(mat, re3, im3)
    return y.reshape(bsz * m, B_WIDTH)


def _attn_kernel(sink_ref, *refs, local, nb):
    if local:
        q_ref, kp_ref, kc_ref, kn_ref, vp_ref, vc_ref, vn_ref, kx_ref, vx_ref, o_ref = refs
    else:
        q_ref, kx_ref, vx_ref, o_ref = refs
    q = q_ref[0]
    tq = q.shape[0]
    lo = lax.broadcasted_iota(jnp.int32, (tq, LANES), 1) < HEAD_DIM
    zero = jnp.zeros((tq, LANES), BF16)
    if local:
        i = pl.program_id(1)
        row = lax.broadcasted_iota(jnp.int32, (BLOCK, 3 * BLOCK), 0)
        col = lax.broadcasted_iota(jnp.int32, (BLOCK, 3 * BLOCK), 1)
        valid = jnp.abs(col - BLOCK - row) <= WINDOW
        valid = valid & ((col >= BLOCK) | (i > 0)) & ((col < 2 * BLOCK) | (i < nb - 1))
    for g in range(2):
        gl = slice(g * LANES, (g + 1) * LANES)
        if local:
            kg = jnp.concatenate([kp_ref[0][:, gl], kc_ref[0][:, gl], kn_ref[0][:, gl], kx_ref[0][:, gl]], axis=0)
            vg = jnp.concatenate([vp_ref[0][:, gl], vc_ref[0][:, gl], vn_ref[0][:, gl], vx_ref[0][:, gl]], axis=0)
        else:
            kg = kx_ref[0][:, gl]
            vg = vx_ref[0][:, gl]
        qs = []
        for p in range(2):
            qp = q[:, g * 2 * LANES + p * LANES: g * 2 * LANES + (p + 1) * LANES]
            qs.append(jnp.where(lo, qp, zero))
            qs.append(jnp.where(lo, zero, qp))
        q4 = jnp.concatenate(qs, axis=0)
        s4 = lax.dot_general(q4, kg, (((1,), (1,)), ((), ())), preferred_element_type=F32)
        ps, rden = [], []
        for hl in range(4):
            s = s4[hl * tq:(hl + 1) * tq]
            if local:
                s = jnp.concatenate(
                    [jnp.where(valid, s[:, :3 * BLOCK], NEG), s[:, 3 * BLOCK:]], axis=1)
            sk = sink_ref[g * 4 + hl]
            m = jnp.maximum(jnp.max(s, axis=-1, keepdims=True), sk)
            p = jnp.exp(s - m)
            den = jnp.sum(p, axis=-1, keepdims=True) + jnp.exp(sk - m)
            ps.append(p.astype(BF16))
            rden.append(1.0 / den)
        o4 = _dot(jnp.concatenate(ps, axis=0), vg) * jnp.concatenate(rden, axis=0)
        for p in range(2):
            o_pair = jnp.where(lo, o4[(2 * p) * tq:(2 * p + 1) * tq], o4[(2 * p + 1) * tq:(2 * p + 2) * tq])
            c0 = g * 2 * LANES + p * LANES
            o_ref[0, :, c0:c0 + LANES] = o_pair.astype(BF16)


def _window_attention(q, k, v, kx, vx, sink, bsz, n, m):
    nb = n // BLOCK
    q3 = q.reshape(bsz, n, C_WIDTH)
    k3 = k.reshape(bsz, n, 2 * KV_WIDTH)
    v3 = v.reshape(bsz, n, 2 * KV_WIDTH)
    kx3 = kx.reshape(bsz, m, 2 * KV_WIDTH)
    vx3 = vx.reshape(bsz, m, 2 * KV_WIDTH)
    blk = lambda f: pl.BlockSpec((1, BLOCK, 2 * KV_WIDTH), f)
    prev = lambda b, i, s: (b, jnp.maximum(i - 1, 0), 0)
    cur = lambda b, i, s: (b, i, 0)
    nxt = lambda b, i, s: (b, jnp.minimum(i + 1, nb - 1), 0)
    ctxs = pl.BlockSpec((1, m, 2 * KV_WIDTH), lambda b, i, s: (b, 0, 0))
    o = pl.pallas_call(
        functools.partial(_attn_kernel, local=True, nb=nb),
        grid_spec=pltpu.PrefetchScalarGridSpec(
            num_scalar_prefetch=1,
            grid=(bsz, nb),
            in_specs=[pl.BlockSpec((1, BLOCK, C_WIDTH), cur),
                      blk(prev), blk(cur), blk(nxt), blk(prev), blk(cur), blk(nxt), ctxs, ctxs],
            out_specs=pl.BlockSpec((1, BLOCK, C_WIDTH), cur)),
        out_shape=jax.ShapeDtypeStruct((bsz, n, C_WIDTH), BF16),
        compiler_params=_cparams(("parallel", "parallel")),
    )(sink, q3, k3, k3, k3, v3, v3, v3, kx3, vx3)
    return o.reshape(bsz * n, C_WIDTH)


def _context_attention(q, kx, vx, sink, bsz, m):
    q3 = q.reshape(bsz, m, C_WIDTH)
    kx3 = kx.reshape(bsz, m, 2 * KV_WIDTH)
    vx3 = vx.reshape(bsz, m, 2 * KV_WIDTH)
    ctxs = pl.BlockSpec((1, m, 2 * KV_WIDTH), lambda b, s: (b, 0, 0))
    o = pl.pallas_call(
        functools.partial(_attn_kernel, local=False, nb=1),
        grid_spec=pltpu.PrefetchScalarGridSpec(
            num_scalar_prefetch=1,
            grid=(bsz,),
            in_specs=[pl.BlockSpec((1, m, C_WIDTH), lambda b, s: (b, 0, 0)), ctxs, ctxs],
            out_specs=pl.BlockSpec((1, m, C_WIDTH), lambda b, s: (b, 0, 0))),
        out_shape=jax.ShapeDtypeStruct((bsz, m, C_WIDTH), BF16),
        compiler_params=_cparams(("parallel",)),
    )(sink, q3, kx3, vx3)
    return o.reshape(bsz * m, C_WIDTH)


def _postmix_kernel(*refs, route):
    (sgu_ref, yf_ref, at_ref, wf_ref, wo_ref, h_ref, gp_ref, gt_ref, gf_ref, sh_ref, sc_ref) = refs[:11]
    if route:
        wr_ref, br_ref, hn_ref, y_ref, cmb_ref, cnt_ref = refs[11:]
    else:
        hn_ref, y_ref = refs[11:]
    fm = _dot(yf_ref[...], wf_ref[...]).astype(BF16)
    o = (_dot(sgu_ref[...], wo_ref[:A_WIDTH, :])
         + _dot(fm, wo_ref[A_WIDTH:A_WIDTH + B_WIDTH, :])
         + _dot(at_ref[...], wo_ref[A_WIDTH + B_WIDTH:, :]))
    ms = jnp.mean(o * o, axis=-1, keepdims=True)
    hn = h_ref[...] + gt_ref[...] * (o * lax.rsqrt(ms + EPS) * gp_ref[...])
    hn_ref[...] = hn
    ms2 = jnp.mean(hn * hn, axis=-1, keepdims=True)
    y = hn * lax.rsqrt(ms2 + EPS) * gf_ref[...] * (1.0 + sc_ref[...]) + sh_ref[...]
    y_ref[...] = y.astype(BF16)
    if route:
        yh = y.astype(BF16)
        yl = (y - yh.astype(F32)).astype(BF16)
        w = wr_ref[...]
        wh = w.astype(BF16)
        wl = (w - wh.astype(F32)).astype(BF16)
        lg = _dot(yh, wh) + _dot(yl, wh) + _dot(yh, wl) + br_ref[...]
        lane = lax.broadcasted_iota(jnp.int32, lg.shape, 1)
        lg = jnp.where(lane < N_EXPERTS, lg, NEG)
        m1 = jnp.max(lg, axis=-1, keepdims=True)
        i1 = jnp.min(jnp.where(lg == m1, lane, LANES), axis=-1, keepdims=True)
        lg2 = jnp.where(lane == i1, NEG, lg)
        m2 = jnp.max(lg2, axis=-1, keepdims=True)
        i2 = jnp.min(jnp.where(lg2 == m2, lane, LANES), axis=-1, keepdims=True)
        e2 = jnp.exp(m2 - m1)
        g1 = 1.0 / (1.0 + e2)
        g2 = e2 * g1
        cmb = jnp.where(lane == i1, g1, 0.0) + jnp.where(lane == i2, g2, 0.0)
        cmb_ref[...] = cmb
        cnt_ref[...] = jnp.sum(jnp.where(cmb > 0.0, 1.0, 0.0), axis=0, keepdims=True)


def _postmix(sgu, yf, at, wf, wo, h, mods, gp, gf, router, *, tm, tiles_per_batch):
    T = h.shape[0]
    if tiles_per_batch is None:
        mrow = lambda i: 4
    else:
        mrow = lambda i: i // tiles_per_batch
    const2 = lambda i: (0, 0)
    row = lambda w: pl.BlockSpec((tm, w), lambda i: (i, 0))
    mod = lambda j: pl.BlockSpec((None, None, 1, D_MODEL), lambda i: (mrow(i), j, 0, 0))
    in_specs = [row(A_WIDTH), row(B_WIDTH), row(C_WIDTH),
                pl.BlockSpec((B_WIDTH, B_WIDTH), const2),
                pl.BlockSpec((D_MODEL, D_MODEL), const2),
                row(D_MODEL),
                pl.BlockSpec((1, D_MODEL), const2),
                mod(2),
                pl.BlockSpec((1, D_MODEL), const2),
                mod(3), mod(4)]
    args = [sgu, yf, at, wf, wo, h, gp, mods, gf, mods, mods]
    out_specs = [row(D_MODEL), row(D_MODEL)]
    out_shape = [jax.ShapeDtypeStruct((T, D_MODEL), F32), jax.ShapeDtypeStruct((T, D_MODEL), BF16)]
    route = router is not None
    if route:
        in_specs += [pl.BlockSpec((D_MODEL, LANES), const2), pl.BlockSpec((1, LANES), const2)]
        args += list(router)
        out_specs += [row(LANES), pl.BlockSpec((None, 1, LANES), lambda i: (i, 0, 0))]
        out_shape += [jax.ShapeDtypeStruct((T, LANES), F32), jax.ShapeDtypeStruct((T // tm, 1, LANES), F32)]
    return pl.pallas_call(
        functools.partial(_postmix_kernel, route=route),
        grid=(T // tm,),
        in_specs=in_specs, out_specs=out_specs, out_shape=out_shape,
        compiler_params=_cparams(("parallel",)),
    )(*args)


def _ffn_epilogue(f, h_ref, gp_ref, gt_ref, o_ref):
    ms = jnp.mean(f * f, axis=-1, keepdims=True)
    o_ref[...] = h_ref[...] + gt_ref[...] * (f * lax.rsqrt(ms + EPS) * gp_ref[...])


FF_CHUNK = 512


def _swiglu_tile(y, wg_ref, wu_ref, wd_ref, act_ref):
    for c in range(D_FF // FF_CHUNK):
        cols = slice(c * FF_CHUNK, (c + 1) * FF_CHUNK)
        gate = _dot(y, wg_ref[:, cols])
        up = _dot(y, wu_ref[:, cols])
        act_ref[:, cols] = (gate * jax.nn.sigmoid(gate) * up).astype(BF16)
    return _dot(act_ref[...], wd_ref[...])


def _ffn_dense_kernel(y_ref, wg_ref, wu_ref, wd_ref, h_ref, gp_ref, gt_ref, o_ref, act_ref):
    f = _swiglu_tile(y_ref[...], wg_ref, wu_ref, wd_ref, act_ref)
    _ffn_epilogue(f, h_ref, gp_ref, gt_ref, o_ref)


def _ffn_dense(y, wg, wu, wd, h, mods, gp, *, tm, tiles_per_batch):
    T = h.shape[0]
    if tiles_per_batch is None:
        mrow = lambda i: 4
    else:
        mrow = lambda i: i // tiles_per_batch
    resident = pl.Buffered(1)
    return pl.pallas_call(
        _ffn_dense_kernel,
        grid=(T // tm,),
        in_specs=[pl.BlockSpec((tm, D_MODEL), lambda i: (i, 0)),
                  pl.BlockSpec((D_MODEL, D_FF), lambda i: (0, 0), pipeline_mode=resident),
                  pl.BlockSpec((D_MODEL, D_FF), lambda i: (0, 0), pipeline_mode=resident),
                  pl.BlockSpec((D_FF, D_MODEL), lambda i: (0, 0), pipeline_mode=resident),
                  pl.BlockSpec((tm, D_MODEL), lambda i: (i, 0)),
                  pl.BlockSpec((1, D_MODEL), lambda i: (0, 0)),
                  pl.BlockSpec((None, None, 1, D_MODEL), lambda i: (mrow(i), 5, 0, 0))],
        out_specs=pl.BlockSpec((tm, D_MODEL), lambda i: (i, 0)),
        out_shape=jax.ShapeDtypeStruct((T, D_MODEL), F32),
        scratch_shapes=[pltpu.VMEM((tm, D_FF), BF16)],
        compiler_params=_cparams(("parallel",)),
    )(y, wg, wu, wd, h, gp, mods)


MOE_TB = 512
MOE_TM = 512
SUB = 128
NSUB = MOE_TB // SUB
RUN_ALIGN = 16


def _moe_tile_bound(T):
    nt = T // MOE_TB
    rows = 2 * T + nt * N_EXPERTS * (RUN_ALIGN - 1) + N_EXPERTS * (SUB + MOE_TM - 1)
    return -(-rows // MOE_TM)


def _moe_plan(cnt, T):
    p = (cnt + RUN_ALIGN - 1) // RUN_ALIGN * RUN_ALIGN
    base = jnp.cumsum(p, axis=0) - p
    used = jnp.sum(p, axis=0)
    tiles = (used + SUB + MOE_TM - 1) // MOE_TM
    tend = jnp.cumsum(tiles)
    off = (tend - tiles) * MOE_TM
    rowbase = (off[None, :] + base).astype(jnp.int32)
    nt_bound = _moe_tile_bound(T)
    j = jnp.arange(nt_bound, dtype=jnp.int32)
    tile_e = jnp.minimum(jnp.sum(j[:, None] >= tend[None, :], axis=1), N_EXPERTS - 1).astype(jnp.int32)
    nvalid = tend[-1].astype(jnp.int32)
    tile_blk = jnp.minimum(j, nvalid - 1)
    return rowbase, tile_e, tile_blk, nvalid.reshape(1)


def _run_copy_out(stage, ys_ref, sems, rb_ref, b, e, s):
    r0 = pl.multiple_of(rb_ref[b * N_EXPERTS + e] + SUB * s, RUN_ALIGN)
    return pltpu.make_async_copy(stage.at[e, s], ys_ref.at[pl.ds(r0, SUB), :], sems.at[e, s])


def _dispatch_kernel(rb_ref, cn_ref, y_ref, cmb_ref, rbv_ref, ltri_ref, ysin_ref, d_ref, ys_ref, stage, sems):
    del ysin_ref
    b = pl.program_id(0)
    sel = cmb_ref[...] > 0.0
    rk = _dot(ltri_ref[...], jnp.where(sel, 1.0, 0.0).astype(BF16))
    d_ref[...] = jnp.where(sel, rk.astype(jnp.int32) + rbv_ref[...], -1)
    rkt = jnp.where(sel, rk, -1.0).T
    y = y_ref[...]
    jj = lax.broadcasted_iota(jnp.int32, (SUB, y.shape[0]), 0).astype(F32)
    for e in range(N_EXPERTS):
        row = rkt[e:e + 1, :]
        for s in range(NSUB):
            @pl.when(cn_ref[b * N_EXPERTS + e] > SUB * s)
            def _(e=e, s=s, row=row):
                g = jnp.where(row == jj + float(SUB * s), 1.0, 0.0).astype(BF16)
                stage[e, s] = _dot(g, y).astype(BF16)
                _run_copy_out(stage, ys_ref, sems, rb_ref, b, e, s).start()
    for e in range(N_EXPERTS):
        for s in range(NSUB):
            @pl.when(cn_ref[b * N_EXPERTS + e] > SUB * s)
            def _(e=e, s=s):
                _run_copy_out(stage, ys_ref, sems, rb_ref, b, e, s).wait()


def _moe_dispatch(y, cmb, rowbase, cnt, nt_bound):
    T = y.shape[0]
    nt = T // MOE_TB
    rbv = jnp.zeros((nt, 1, LANES), jnp.int32).at[:, 0, :N_EXPERTS].set(rowbase)
    ltri = jnp.asarray(np.tril(np.ones((MOE_TB, MOE_TB), np.float32), -1)).astype(BF16)
    ys0 = jnp.zeros((nt_bound * MOE_TM, D_MODEL), BF16)
    return pl.pallas_call(
        _dispatch_kernel,
        grid_spec=pltpu.PrefetchScalarGridSpec(
            num_scalar_prefetch=2,
            grid=(nt,),
            in_specs=[pl.BlockSpec((MOE_TB, D_MODEL), lambda b, rb, cn: (b, 0)),
                      pl.BlockSpec((MOE_TB, LANES), lambda b, rb, cn: (b, 0)),
                      pl.BlockSpec((None, 1, LANES), lambda b, rb, cn: (b, 0, 0)),
                      pl.BlockSpec((MOE_TB, MOE_TB), lambda b, rb, cn: (0, 0)),
                      pl.BlockSpec(memory_space=pl.ANY)],
            out_specs=[pl.BlockSpec((MOE_TB, LANES), lambda b, rb, cn: (b, 0)),
                       pl.BlockSpec(memory_space=pl.ANY)],
            scratch_shapes=[pltpu.VMEM((N_EXPERTS, NSUB, SUB, D_MODEL), BF16),
                            pltpu.SemaphoreType.DMA((N_EXPERTS, NSUB))]),
        out_shape=[jax.ShapeDtypeStruct((T, LANES), jnp.int32),
                   jax.ShapeDtypeStruct((nt_bound * MOE_TM, D_MODEL), BF16)],
        input_output_aliases={6: 1},
        compiler_params=_cparams(("arbitrary",)),
    )(rowbase.reshape(-1), cnt.reshape(-1), y, cmb, rbv, ltri, ys0)


def _ffn_group_kernel(te_ref, tb_ref, nv_ref, y_ref, wg_ref, wu_ref, wd_ref, o_ref, act_ref):
    del te_ref, tb_ref
    j = pl.program_id(0)

    @pl.when(j < nv_ref[0])
    def _():
        o_ref[...] = _swiglu_tile(y_ref[...], wg_ref, wu_ref, wd_ref, act_ref).astype(BF16)

    @pl.when(j >= nv_ref[0])
    def _():
        o_ref[...] = jnp.zeros_like(o_ref)


def _ffn_group(ys, wg, wu, wd, tile_e, tile_blk, nvalid):
    rows = ys.shape[0]
    resident = pl.Buffered(1)
    return pl.pallas_call(
        _ffn_group_kernel,
        grid_spec=pltpu.PrefetchScalarGridSpec(
            num_scalar_prefetch=3,
            grid=(rows // MOE_TM,),
            in_specs=[pl.BlockSpec((MOE_TM, D_MODEL), lambda j, te, tb, nv: (tb[j], 0)),
                      pl.BlockSpec((None, D_MODEL, D_FF), lambda j, te, tb, nv: (te[j], 0, 0), pipeline_mode=resident),
                      pl.BlockSpec((None, D_MODEL, D_FF), lambda j, te, tb, nv: (te[j], 0, 0), pipeline_mode=resident),
                      pl.BlockSpec((None, D_FF, D_MODEL), lambda j, te, tb, nv: (te[j], 0, 0), pipeline_mode=resident)],
            out_specs=pl.BlockSpec((MOE_TM, D_MODEL), lambda j, te, tb, nv: (j, 0)),
            scratch_shapes=[pltpu.VMEM((MOE_TM, D_FF), BF16)]),
        out_shape=jax.ShapeDtypeStruct((rows, D_MODEL), BF16),
        compiler_params=_cparams(("arbitrary",)),
    )(tile_e, tile_blk, nvalid, ys, wg, wu, wd)


def _run_copy_in(fs_ref, fbuf, sems, rb_ref, b, e, s):
    r0 = pl.multiple_of(rb_ref[b * N_EXPERTS + e] + SUB * s, RUN_ALIGN)
    return pltpu.make_async_copy(fs_ref.at[pl.ds(r0, SUB), :], fbuf.at[e, s], sems.at[e, s])


def _combine_kernel(rb_ref, cn_ref, d_ref, cmb_ref, h_ref, gp_ref, gt_ref, fs_ref, o_ref, fbuf, sems, acc_ref):
    b = pl.program_id(0)
    for e in range(N_EXPERTS):
        for s in range(NSUB):
            @pl.when(cn_ref[b * N_EXPERTS + e] > SUB * s)
            def _(e=e, s=s):
                _run_copy_in(fs_ref, fbuf, sems, rb_ref, b, e, s).start()
    acc_ref[...] = jnp.zeros_like(acc_ref)
    d = d_ref[...]
    cmb = cmb_ref[...]
    rr = lax.broadcasted_iota(jnp.int32, (d.shape[0], SUB), 1)
    for e in range(N_EXPERTS):
        de = d[:, e:e + 1]
        ce = cmb[:, e:e + 1]
        for s in range(NSUB):
            @pl.when(cn_ref[b * N_EXPERTS + e] > SUB * s)
            def _(e=e, s=s, de=de, ce=ce):
                _run_copy_in(fs_ref, fbuf, sems, rb_ref, b, e, s).wait()
                r0 = rb_ref[b * N_EXPERTS + e] + SUB * s
                pick = jnp.where(de - r0 == rr, 1.0, 0.0).astype(BF16)
                acc_ref[...] += ce * _dot(pick, fbuf[e, s])
    _ffn_epilogue(acc_ref[...], h_ref, gp_ref, gt_ref, o_ref)


def _moe_combine(fs, dest, cmb, h, mods, gp, rowbase, cnt, *, tiles_per_batch):
    T = h.shape[0]
    nt = T // MOE_TB
    return pl.pallas_call(
        _combine_kernel,
        grid_spec=pltpu.PrefetchScalarGridSpec(
            num_scalar_prefetch=2,
            grid=(nt,),
            in_specs=[pl.BlockSpec((MOE_TB, LANES), lambda b, rb, cn: (b, 0)),
                      pl.BlockSpec((MOE_TB, LANES), lambda b, rb, cn: (b, 0)),
                      pl.BlockSpec((MOE_TB, D_MODEL), lambda b, rb, cn: (b, 0)),
                      pl.BlockSpec((1, D_MODEL), lambda b, rb, cn: (0, 0)),
                      pl.BlockSpec((None, None, 1, D_MODEL), lambda b, rb, cn: (b // tiles_per_batch, 5, 0, 0)),
                      pl.BlockSpec(memory_space=pl.ANY)],
            out_specs=pl.BlockSpec((MOE_TB, D_MODEL), lambda b, rb, cn: (b, 0)),
            scratch_shapes=[pltpu.VMEM((N_EXPERTS, NSUB, SUB, D_MODEL), BF16),
                            pltpu.SemaphoreType.DMA((N_EXPERTS, NSUB)),
                            pltpu.VMEM((MOE_TB, D_MODEL), F32)]),
        out_shape=jax.ShapeDtypeStruct((T, D_MODEL), F32),
        compiler_params=_cparams(("arbitrary",)),
    )(rowbase.reshape(-1), cnt.reshape(-1), dest, cmb, h, gp, mods, fs)


def _ffn_moe(y, cmb, cnt_tiles, wg, wu, wd, h, mods, gp, *, tiles_per_batch):
    T = h.shape[0]
    cnt = cnt_tiles.reshape(T // MOE_TB, LANES)[:, :N_EXPERTS].astype(jnp.int32)
    rowbase, tile_e, tile_blk, nvalid = _moe_plan(cnt, T)
    dest, ys = _moe_dispatch(y, cmb, rowbase, cnt, _moe_tile_bound(T))
    fs = _ffn_group(ys, wg, wu, wd, tile_e, tile_blk, nvalid)
    return _moe_combine(fs, dest, cmb, h, mods, gp, rowbase, cnt, tiles_per_batch=tiles_per_batch)


def _blockdiag(m, reps):
    n = m.shape[0]
    out = np.zeros((n * reps, n * reps), np.float64)
    for r in range(reps):
        out[r * n:(r + 1) * n, r * n:(r + 1) * n] = m
    return out


def _dft_tables(n_rows):
    n = n_rows * GRID_W
    r = np.arange(n_rows)
    c = np.arange(GRID_W)
    a1 = 2 * np.pi * np.outer(r, r) / n_rows
    c1, s1 = np.cos(a1), np.sin(a1)
    m1 = np.block([[c1, s1], [-s1, c1]]) * 0.125
    at = 2 * np.pi * np.outer(r, c) / n
    twc = np.repeat(np.cos(at), B_WIDTH, axis=1)
    tws = np.repeat(np.sin(at), B_WIDTH, axis=1)
    a3 = 2 * np.pi * np.outer(c, c) / GRID_W
    m3 = np.concatenate([np.cos(a3), np.sin(a3)], axis=1) * (8.0 / np.sqrt(n))
    f32 = lambda t: jnp.asarray(t.astype(np.float32))
    return f32(m1).astype(BF16), f32(twc), f32(tws), f32(m3).astype(BF16)


def _channel_dft_table():
    d = np.arange(HEAD_DIM)
    a = 2 * np.pi * np.outer(d, d) / HEAD_DIM
    w = np.concatenate([_blockdiag(np.cos(a), 4), -_blockdiag(np.sin(a), 4)], axis=1) * 0.125
    return jnp.asarray(w.astype(np.float32)).astype(BF16)


def _ctx_dft_table(m):
    p = np.arange(m)
    a = 2 * np.pi * np.outer(p, p) / m
    w = np.concatenate([np.cos(a), np.sin(a)], axis=1) * (8.0 / np.sqrt(m * HEAD_DIM))
    return jnp.asarray(w.astype(np.float32)).astype(BF16)


def _rope_tables(n_tok):
    rows = n_tok // GRID_W
    row = jnp.broadcast_to(jnp.arange(rows)[:, None], (rows, GRID_W)).reshape(-1)
    col = jnp.broadcast_to(jnp.arange(GRID_W)[None, :], (rows, GRID_W)).reshape(-1)
    half = HEAD_DIM // 2
    inv = ROPE_BASE ** (-jnp.arange(0, half, 2, dtype=F32) / half)
    ang = jnp.stack([row.astype(F32)[:, None] * inv, col.astype(F32)[:, None] * inv], axis=1)
    cos, sin = jnp.cos(ang), jnp.sin(ang)
    zer = jnp.zeros_like(sin)
    lay = lambda a, b: jnp.tile(jnp.stack([a, b], axis=2).reshape(n_tok, HEAD_DIM), (1, LANES // HEAD_DIM))
    return lay(cos, cos), lay(-sin, zer), lay(zer, sin)


def kernel(x, c, ctx, c_ctx, w_ada, b_ada, g_mix_pre, g_mix_post, g_ffn_pre, g_ffn_post,
           w_in, w_s, b_s, g_v, w_f, sink, w_out, w_gate_d, w_up_d, w_down_d,
           w_router, b_router, w_gate_e, w_up_e, w_down_e):
    bsz, n_lat, _ = x.shape
    n_ctx = ctx.shape[1]
    T, Tc = bsz * n_lat, bsz * n_ctx
    tm = 512
    tpb = n_lat // tm

    rope_tabs = _rope_tables(n_lat)
    m1, twc, tws, m3 = _dft_tables(n_lat // GRID_W)
    wdft = _channel_dft_table()
    mctx = _ctx_dft_table(n_ctx)
    hm = jnp.asarray(_blockdiag(np.full((HEAD_DIM, HEAD_DIM), 1.0 / HEAD_DIM), A_HEADS).astype(np.float32)).astype(BF16)

    cond8 = jnp.zeros((8, D_MODEL), F32).at[:bsz].set(c).at[4].set(c_ctx)
    h = x.reshape(T, D_MODEL)
    hc = ctx.reshape(Tc, D_MODEL)

    for i in range(DEPTH):
        last = i == DEPTH - 1
        mods = _adaln(cond8, w_ada[i], b_ada[i][None, :]).reshape(8, 6, 1, D_MODEL)
        win = w_in[i].astype(BF16)
        ws = w_s[i].astype(BF16)
        bsx = jnp.repeat(b_s[i].T, HEAD_DIM, axis=1)
        gv = g_v[i].reshape(1, A_WIDTH)
        wf = jax.scipy.linalg.block_diag(*[w_f[i][g] for g in range(4)]).astype(BF16)
        wo = w_out[i].astype(BF16)
        gpre = g_mix_pre[i][None, :]
        gpost = g_mix_post[i][None, :]
        gfpre = g_ffn_pre[i][None, :]
        gfpost = g_ffn_post[i][None, :]
        sk = sink[i]

        sgu, fre, fim, q, k, v = _premix(h, mods, gpre, win, ws, bsx, gv, hm, wdft, rope_tabs,
                                         tm=tm, tiles_per_batch=tpb, n_pos=n_lat)
        sguc, frec, fimc, qc, kc, vc = _premix(hc, mods, gpre, win, ws, bsx, gv, hm, wdft, None,
                                               tm=tm, tiles_per_batch=None, n_pos=n_ctx)
        yf = _seq_dft(fre, fim, bsz, n_lat // GRID_W, m1, twc, tws, m3)
        at = _window_attention(q, k, v, kc, vc, sk, bsz, n_lat, n_ctx)

        if i % 2 == 0:
            router = None
        else:
            j = i // 2
            wr = jnp.zeros((D_MODEL, LANES), F32).at[:, :N_EXPERTS].set(w_router[j])
            br = jnp.zeros((1, LANES), F32).at[0, :N_EXPERTS].set(b_router[j])
            router = (wr, br)
        res = _postmix(sgu, yf, at, wf, wo, h, mods, gpost, gfpre, router, tm=tm, tiles_per_batch=tpb)
        if not last:
            yfc = _ctx_dft(frec, fimc, bsz, n_ctx, mctx)
            atc = _context_attention(qc, kc, vc, sk, bsz, n_ctx)
            resc = _postmix(sguc, yfc, atc, wf, wo, hc, mods, gpost, gfpre, router, tm=tm, tiles_per_batch=None)

        j = i // 2
        if i % 2 == 0:
            wg, wu, wd = w_gate_d[j].astype(BF16), w_up_d[j].astype(BF16), w_down_d[j].astype(BF16)
            h = _ffn_dense(res[1], wg, wu, wd, res[0], mods, gfpost, tm=tm, tiles_per_batch=tpb)
            if not last:
                hc = _ffn_dense(resc[1], wg, wu, wd, resc[0], mods, gfpost, tm=tm, tiles_per_batch=None)
        else:
            wg, wu, wd = w_gate_e[j].astype(BF16), w_up_e[j].astype(BF16), w_down_e[j].astype(BF16)
            assert last and tm == MOE_TB, "the expert FFN is only built for the final layer's latent tokens"
            h = _ffn_moe(res[1], res[2], res[3], wg, wu, wd, res[0], mods, gfpost,
                         tiles_per_batch=n_lat // MOE_TB)
    return h.reshape(bsz, n_lat, D_MODEL)
```

```python
import functools

import numpy as np
import jax
import jax.numpy as jnp
from jax import lax
from jax.experimental import pallas as pl
from jax.experimental.pallas import tpu as pltpu

D_MODEL = 1024
DEPTH = 2
GRID_W = 64
HEAD_DIM = 64
EPS = 1e-6
A_HEADS = 4
A_WIDTH = 256
CHUNK = 128
B_WIDTH = 256
C_Q_HEADS = 8
C_WIDTH = 512
KV_WIDTH = 128
WINDOW = 128
BLOCK = 128
ROPE_BASE = 10000.0
OFF_B = 512
OFF_Q = 768
OFF_K = 1280
OFF_V = 1408
N_IN = 1536
D_FF = 3584
N_EXPERTS = 8

LANES = 128
VMEM_LIMIT = 56 * 1024 * 1024
NEG = -1e30

F32 = jnp.float32
BF16 = jnp.bfloat16


def _dot(a, b):
    return jnp.dot(a, b, preferred_element_type=F32)


def _cparams(sem):
    return pltpu.CompilerParams(dimension_semantics=sem, vmem_limit_bytes=VMEM_LIMIT)


def _adaln_kernel(c_ref, w_ref, b_ref, o_ref):
    c = c_ref[...]
    s = (c * jax.nn.sigmoid(c)).astype(BF16)
    o_ref[...] = _dot(s, w_ref[...].astype(BF16)) + b_ref[...]


def _adaln(cond8, w, b):
    tn = 1536
    return pl.pallas_call(
        _adaln_kernel,
        grid=(6 * D_MODEL // tn,),
        in_specs=[pl.BlockSpec((8, D_MODEL), lambda j: (0, 0)),
                  pl.BlockSpec((D_MODEL, tn), lambda j: (0, j)),
                  pl.BlockSpec((1, tn), lambda j: (0, j))],
        out_specs=pl.BlockSpec((8, tn), lambda j: (0, j)),
        out_shape=jax.ShapeDtypeStruct((8, 6 * D_MODEL), F32),
        compiler_params=_cparams(("arbitrary",)),
    )(cond8, w, b)


def _gelu_tanh(x):
    return 0.5 * x * (1.0 + jnp.tanh(0.7978845608028654 * (x + 0.044715 * x * x * x)))


def _premix_kernel(*refs, rope):
    if rope:
        (h_ref, sh_ref, sc_ref, g_ref, win_ref, ws_ref, bs_ref, gv_ref, hm_ref, wdft_ref,
         cos_ref, s1_ref, s2_ref, sgu_ref, re_ref, im_ref, q_ref, k_ref, v_ref) = refs
    else:
        (h_ref, sh_ref, sc_ref, g_ref, win_ref, ws_ref, bs_ref, gv_ref, hm_ref, wdft_ref,
         sgu_ref, re_ref, im_ref, q_ref, k_ref, v_ref) = refs
    x = h_ref[...]
    tm = x.shape[0]
    ms = jnp.mean(x * x, axis=-1, keepdims=True)
    xn = x * lax.rsqrt(ms + EPS) * g_ref[...]
    xm = (xn * (1.0 + sc_ref[...]) + sh_ref[...]).astype(BF16)
    z = _dot(xm, win_ref[...])

    a = _gelu_tanh(z[:, :OFF_B])
    u = a[:, :A_WIDTH]
    v = a[:, A_WIDTH:]
    msv = _dot((v * v).astype(BF16), hm_ref[...])
    vn = (v * lax.rsqrt(msv + EPS) * gv_ref[...]).astype(BF16)
    head = lax.broadcasted_iota(jnp.int32, (CHUNK, A_WIDTH), 1) // HEAD_DIM
    for ck in range(tm // CHUNK):
        rows = slice(ck * CHUNK, (ck + 1) * CHUNK)
        vc = vn[rows]
        sv = bs_ref[...]
        for hh in range(A_HEADS):
            sv = sv + jnp.where(head == hh, _dot(ws_ref[hh], vc), 0.0)
        sgu_ref[rows, :] = (u[rows] * sv).astype(BF16)

    f = _dot(z[:, OFF_B:OFF_Q].astype(BF16), wdft_ref[...])
    re_ref[...] = f[:, :B_WIDTH].astype(BF16)
    im_ref[...] = f[:, B_WIDTH:].astype(BF16)

    def rot(t):
        if not rope:
            return t
        return (t * cos_ref[...] + pltpu.roll(t, LANES - 16, 1) * s1_ref[...]
                + pltpu.roll(t, 16, 1) * s2_ref[...])

    for j in range(C_WIDTH // LANES):
        cols = slice(OFF_Q + j * LANES, OFF_Q + (j + 1) * LANES)
        q_ref[:, j * LANES:(j + 1) * LANES] = (rot(z[:, cols]) * 0.125).astype(BF16)

    lo = lax.broadcasted_iota(jnp.int32, (tm, LANES), 1) < HEAD_DIM
    kk = rot(z[:, OFF_K:OFF_V])
    kr = pltpu.roll(kk, HEAD_DIM, 1)
    k_ref[:, :LANES] = jnp.where(lo, kk, kr).astype(BF16)
    k_ref[:, LANES:] = jnp.where(lo, kr, kk).astype(BF16)
    vv = z[:, OFF_V:]
    vr = pltpu.roll(vv, HEAD_DIM, 1)
    v_ref[:, :LANES] = jnp.where(lo, vv, vr).astype(BF16)
    v_ref[:, LANES:] = jnp.where(lo, vr, vv).astype(BF16)


def _premix(h, mods, g, win, ws, bsx, gv, hm, wdft, rope_tabs, *, tm, tiles_per_batch, n_pos):
    T = h.shape[0]
    rope = rope_tabs is not None
    if tiles_per_batch is None:
        mrow = lambda i: 4
    else:
        mrow = lambda i: i // tiles_per_batch
    const2 = lambda i: (0, 0)
    in_specs = [
        pl.BlockSpec((tm, D_MODEL), lambda i: (i, 0)),
        pl.BlockSpec((None, None, 1, D_MODEL), lambda i: (mrow(i), 0, 0, 0)),
        pl.BlockSpec((None, None, 1, D_MODEL), lambda i: (mrow(i), 1, 0, 0)),
        pl.BlockSpec((1, D_MODEL), const2),
        pl.BlockSpec((D_MODEL, N_IN), const2),
        pl.BlockSpec((A_HEADS, CHUNK, CHUNK), lambda i: (0, 0, 0)),
        pl.BlockSpec((CHUNK, A_WIDTH), const2),
        pl.BlockSpec((1, A_WIDTH), const2),
        pl.BlockSpec((A_WIDTH, A_WIDTH), const2),
        pl.BlockSpec((B_WIDTH, 2 * B_WIDTH), const2),
    ]
    args = [h, mods, mods, g, win, ws, bsx, gv, hm, wdft]
    if rope:
        nt = n_pos // tm
        for t in rope_tabs:
            in_specs.append(pl.BlockSpec((tm, LANES), lambda i: (i % nt, 0)))
            args.append(t)
    widths = (A_WIDTH, B_WIDTH, B_WIDTH, C_WIDTH, 2 * KV_WIDTH, 2 * KV_WIDTH)
    return pl.pallas_call(
        functools.partial(_premix_kernel, rope=rope),
        grid=(T // tm,),
        in_specs=in_specs,
        out_specs=[pl.BlockSpec((tm, w), lambda i: (i, 0)) for w in widths],
        out_shape=[jax.ShapeDtypeStruct((T, w), BF16) for w in widths],
        compiler_params=_cparams(("parallel",)),
    )(*args)


def _dft1_kernel(m1_ref, twc_ref, tws_ref, re_ref, im_ref, o_ref):
    x = jnp.concatenate([re_ref[0], im_ref[0]], axis=0)
    a = _dot(m1_ref[...], x)
    nr = a.shape[0] // 2
    are, aim = a[:nr], a[nr:]
    c, s = twc_ref[...], tws_ref[...]
    bre = (are * c + aim * s).astype(BF16)
    bim = (aim * c - are * s).astype(BF16)
    for cl in range(o_ref.shape[2]):
        o_ref[0, 0, cl] = bre[:, cl * B_WIDTH:(cl + 1) * B_WIDTH]
        o_ref[0, 1, cl] = bim[:, cl * B_WIDTH:(cl + 1) * B_WIDTH]


def _dft3_kernel(m3_ref, x_ref, o_ref):
    x = jnp.concatenate([x_ref[0, 0], x_ref[0, 1]], axis=0)
    o_ref[0] = _dot(m3_ref[...], x).astype(BF16)


def _seq_dft(re, im, bsz, n_rows, m1, twc, tws, m3):
    ncol = GRID_W * B_WIDTH
    tn = 2048
    re3 = re.reshape(bsz, n_rows, ncol)
    im3 = im.reshape(bsz, n_rows, ncol)
    st1 = pl.pallas_call(
        _dft1_kernel,
        grid=(ncol // tn, bsz),
        in_specs=[pl.BlockSpec((2 * n_rows, 2 * n_rows), lambda j, b: (0, 0)),
                  pl.BlockSpec((n_rows, tn), lambda j, b: (0, j)),
                  pl.BlockSpec((n_rows, tn), lambda j, b: (0, j)),
                  pl.BlockSpec((1, n_rows, tn), lambda j, b: (b, 0, j)),
                  pl.BlockSpec((1, n_rows, tn), lambda j, b: (b, 0, j))],
        out_specs=pl.BlockSpec((1, 2, tn // B_WIDTH, n_rows, B_WIDTH), lambda j, b: (b, 0, j, 0, 0)),
        out_shape=jax.ShapeDtypeStruct((bsz, 2, GRID_W, n_rows, B_WIDTH), BF16),
        compiler_params=_cparams(("parallel", "parallel")),
    )(m1, twc, tws, re3, im3)
    st1 = st1.reshape(bsz, 2, GRID_W, n_rows * B_WIDTH)
    tn3 = 4096
    y = pl.pallas_call(
        _dft3_kernel,
        grid=(bsz, n_rows * B_WIDTH // tn3),
        in_specs=[pl.BlockSpec((GRID_W, 2 * GRID_W), lambda b, j: (0, 0)),
                  pl.BlockSpec((1, 2, GRID_W, tn3), lambda b, j: (b, 0, 0, j))],
        out_specs=pl.BlockSpec((1, GRID_W, tn3), lambda b, j: (b, 0, j)),
        out_shape=jax.ShapeDtypeStruct((bsz, GRID_W, n_rows * B_WIDTH), BF16),
        compiler_params=_cparams(("parallel", "parallel")),
    )(m3, st1)
    return y.reshape(bsz * GRID_W * n_rows, B_WIDTH)


def _ctx_dft_kernel(m_ref, re_ref, im_ref, o_ref):
    x = jnp.concatenate([re_ref[0], im_ref[0]], axis=0)
    o_ref[0] = _dot(m_ref[...], x).astype(BF16)


def _ctx_dft(re, im, bsz, m, mat):
    re3 = re.reshape(bsz, m, B_WIDTH)
    im3 = im.reshape(bsz, m, B_WIDTH)
    y = pl.pallas_call(
        _ctx_dft_kernel,
        grid=(bsz,),
        in_specs=[pl.BlockSpec((m, 2 * m), lambda b: (0, 0)),
                  pl.BlockSpec((1, m, B_WIDTH), lambda b: (b, 0, 0)),
                  pl.BlockSpec((1, m, B_WIDTH), lambda b: (b, 0, 0))],
        out_specs=pl.BlockSpec((1, m, B_WIDTH), lambda b: (b, 0, 0)),
        out_shape=jax.ShapeDtypeStruct((bsz, m, B_WIDTH), BF16),
        compiler_params=_cparams(("parallel",)),
    )(mat, re3, im3)
    return y.reshape(bsz * m, B_WIDTH)


def _attn_kernel(sink_ref, *refs, local, nb):
    if local:
        q_ref, kp_ref, kc_ref, kn_ref, vp_ref, vc_ref, vn_ref, kx_ref, vx_ref, o_ref = refs
    else:
        q_ref, kx_ref, vx_ref, o_ref = refs
    q = q_ref[0]
    tq = q.shape[0]
    lo = lax.broadcasted_iota(jnp.int32, (tq, LANES), 1) < HEAD_DIM
    zero = jnp.zeros((tq, LANES), BF16)
    if local:
        i = pl.program_id(1)
        row = lax.broadcasted_iota(jnp.int32, (BLOCK, 3 * BLOCK), 0)
        col = lax.broadcasted_iota(jnp.int32, (BLOCK, 3 * BLOCK), 1)
        valid = jnp.abs(col - BLOCK - row) <= WINDOW
        valid = valid & ((col >= BLOCK) | (i > 0)) & ((col < 2 * BLOCK) | (i < nb - 1))
    for g in range(2):
        gl = slice(g * LANES, (g + 1) * LANES)
        if local:
            kg = jnp.concatenate([kp_ref[0][:, gl], kc_ref[0][:, gl], kn_ref[0][:, gl], kx_ref[0][:, gl]], axis=0)
            vg = jnp.concatenate([vp_ref[0][:, gl], vc_ref[0][:, gl], vn_ref[0][:, gl], vx_ref[0][:, gl]], axis=0)
        else:
            kg = kx_ref[0][:, gl]
            vg = vx_ref[0][:, gl]
        qs = []
        for p in range(2):
            qp = q[:, g * 2 * LANES + p * LANES: g * 2 * LANES + (p + 1) * LANES]
            qs.append(jnp.where(lo, qp, zero))
            qs.append(jnp.where(lo, zero, qp))
        q4 = jnp.concatenate(qs, axis=0)
        s4 = lax.dot_general(q4, kg, (((1,), (1,)), ((), ())), preferred_element_type=F32)
        ps, rden = [], []
        for hl in range(4):
            s = s4[hl * tq:(hl + 1) * tq]
            if local:
                s = jnp.concatenate(
                    [jnp.where(valid, s[:, :3 * BLOCK], NEG), s[:, 3 * BLOCK:]], axis=1)
            sk = sink_ref[g * 4 + hl]
            m = jnp.maximum(jnp.max(s, axis=-1, keepdims=True), sk)
            p = jnp.exp(s - m)
            den = jnp.sum(p, axis=-1, keepdims=True) + jnp.exp(sk - m)
            ps.append(p.astype(BF16))
            rden.append(1.0 / den)
        o4 = _dot(jnp.concatenate(ps, axis=0), vg) * jnp.concatenate(rden, axis=0)
        for p in range(2):
            o_pair = jnp.where(lo, o4[(2 * p) * tq:(2 * p + 1) * tq], o4[(2 * p + 1) * tq:(2 * p + 2) * tq])
            c0 = g * 2 * LANES + p * LANES
            o_ref[0, :, c0:c0 + LANES] = o_pair.astype(BF16)


def _window_attention(q, k, v, kx, vx, sink, bsz, n, m):
    nb = n // BLOCK
    q3 = q.reshape(bsz, n, C_WIDTH)
    k3 = k.reshape(bsz, n, 2 * KV_WIDTH)
    v3 = v.reshape(bsz, n, 2 * KV_WIDTH)
    kx3 = kx.reshape(bsz, m, 2 * KV_WIDTH)
    vx3 = vx.reshape(bsz, m, 2 * KV_WIDTH)
    blk = lambda f: pl.BlockSpec((1, BLOCK, 2 * KV_WIDTH), f)
    prev = lambda b, i, s: (b, jnp.maximum(i - 1, 0), 0)
    cur = lambda b, i, s: (b, i, 0)
    nxt = lambda b, i, s: (b, jnp.minimum(i + 1, nb - 1), 0)
    ctxs = pl.BlockSpec((1, m, 2 * KV_WIDTH), lambda b, i, s: (b, 0, 0))
    o = pl.pallas_call(
        functools.partial(_attn_kernel, local=True, nb=nb),
        grid_spec=pltpu.PrefetchScalarGridSpec(
            num_scalar_prefetch=1,
            grid=(bsz, nb),
            in_specs=[pl.BlockSpec((1, BLOCK, C_WIDTH), cur),
                      blk(prev), blk(cur), blk(nxt), blk(prev), blk(cur), blk(nxt), ctxs, ctxs],
            out_specs=pl.BlockSpec((1, BLOCK, C_WIDTH), cur)),
        out_shape=jax.ShapeDtypeStruct((bsz, n, C_WIDTH), BF16),
        compiler_params=_cparams(("parallel", "parallel")),
    )(sink, q3, k3, k3, k3, v3, v3, v3, kx3, vx3)
    return o.reshape(bsz * n, C_WIDTH)


def _context_attention(q, kx, vx, sink, bsz, m):
    q3 = q.reshape(bsz, m, C_WIDTH)
    kx3 = kx.reshape(bsz, m, 2 * KV_WIDTH)
    vx3 = vx.reshape(bsz, m, 2 * KV_WIDTH)
    ctxs = pl.BlockSpec((1, m, 2 * KV_WIDTH), lambda b, s: (b, 0, 0))
    o = pl.pallas_call(
        functools.partial(_attn_kernel, local=False, nb=1),
        grid_spec=pltpu.PrefetchScalarGridSpec(
            num_scalar_prefetch=1,
            grid=(bsz,),
            in_specs=[pl.BlockSpec((1, m, C_WIDTH), lambda b, s: (b, 0, 0)), ctxs, ctxs],
            out_specs=pl.BlockSpec((1, m, C_WIDTH), lambda b, s: (b, 0, 0))),
        out_shape=jax.ShapeDtypeStruct((bsz, m, C_WIDTH), BF16),
        compiler_params=_cparams(("parallel",)),
    )(sink, q3, kx3, vx3)
    return o.reshape(bsz * m, C_WIDTH)


def _postmix_kernel(*refs, route):
    (sgu_ref, yf_ref, at_ref, wf_ref, wo_ref, h_ref, gp_ref, gt_ref, gf_ref, sh_ref, sc_ref) = refs[:11]
    if route:
        wr_ref, br_ref, hn_ref, y_ref, cmb_ref, cnt_ref = refs[11:]
    else:
        hn_ref, y_ref = refs[11:]
    fm = _dot(yf_ref[...], wf_ref[...]).astype(BF16)
    o = (_dot(sgu_ref[...], wo_ref[:A_WIDTH, :])
         + _dot(fm, wo_ref[A_WIDTH:A_WIDTH + B_WIDTH, :])
         + _dot(at_ref[...], wo_ref[A_WIDTH + B_WIDTH:, :]))
    ms = jnp.mean(o * o, axis=-1, keepdims=True)
    hn = h_ref[...] + gt_ref[...] * (o * lax.rsqrt(ms + EPS) * gp_ref[...])
    hn_ref[...] = hn
    ms2 = jnp.mean(hn * hn, axis=-1, keepdims=True)
    y = hn * lax.rsqrt(ms2 + EPS) * gf_ref[...] * (1.0 + sc_ref[...]) + sh_ref[...]
    y_ref[...] = y.astype(BF16)
    if route:
        yh = y.astype(BF16)
        yl = (y - yh.astype(F32)).astype(BF16)
        w = wr_ref[...]
        wh = w.astype(BF16)
        wl = (w - wh.astype(F32)).astype(BF16)
        lg = _dot(yh, wh) + _dot(yl, wh) + _dot(yh, wl) + br_ref[...]
        lane = lax.broadcasted_iota(jnp.int32, lg.shape, 1)
        lg = jnp.where(lane < N_EXPERTS, lg, NEG)
        m1 = jnp.max(lg, axis=-1, keepdims=True)
        i1 = jnp.min(jnp.where(lg == m1, lane, LANES), axis=-1, keepdims=True)
        lg2 = jnp.where(lane == i1, NEG, lg)
        m2 = jnp.max(lg2, axis=-1, keepdims=True)
        i2 = jnp.min(jnp.where(lg2 == m2, lane, LANES), axis=-1, keepdims=True)
        e2 = jnp.exp(m2 - m1)
        g1 = 1.0 / (1.0 + e2)
        g2 = e2 * g1
        cmb = jnp.where(lane == i1, g1, 0.0) + jnp.where(lane == i2, g2, 0.0)
        cmb_ref[...] = cmb
        cnt_ref[...] = jnp.sum(jnp.where(cmb > 0.0, 1.0, 0.0), axis=0, keepdims=True)


def _postmix(sgu, yf, at, wf, wo, h, mods, gp, gf, router, *, tm, tiles_per_batch):
    T = h.shape[0]
    if tiles_per_batch is None:
        mrow = lambda i: 4
    else:
        mrow = lambda i: i // tiles_per_batch
    const2 = lambda i: (0, 0)
    row = lambda w: pl.BlockSpec((tm, w), lambda i: (i, 0))
    mod = lambda j: pl.BlockSpec((None, None, 1, D_MODEL), lambda i: (mrow(i), j, 0, 0))
    in_specs = [row(A_WIDTH), row(B_WIDTH), row(C_WIDTH),
                pl.BlockSpec((B_WIDTH, B_WIDTH), const2),
                pl.BlockSpec((D_MODEL, D_MODEL), const2),
                row(D_MODEL),
                pl.BlockSpec((1, D_MODEL), const2),
                mod(2),
                pl.BlockSpec((1, D_MODEL), const2),
                mod(3), mod(4)]
    args = [sgu, yf, at, wf, wo, h, gp, mods, gf, mods, mods]
    out_specs = [row(D_MODEL), row(D_MODEL)]
    out_shape = [jax.ShapeDtypeStruct((T, D_MODEL), F32), jax.ShapeDtypeStruct((T, D_MODEL), BF16)]
    route = router is not None
    if route:
        in_specs += [pl.BlockSpec((D_MODEL, LANES), const2), pl.BlockSpec((1, LANES), const2)]
        args += list(router)
        out_specs += [row(LANES), pl.BlockSpec((None, 1, LANES), lambda i: (i, 0, 0))]
        out_shape += [jax.ShapeDtypeStruct((T, LANES), F32), jax.ShapeDtypeStruct((T // tm, 1, LANES), F32)]
    return pl.pallas_call(
        functools.partial(_postmix_kernel, route=route),
        grid=(T // tm,),
        in_specs=in_specs, out_specs=out_specs, out_shape=out_shape,
        compiler_params=_cparams(("parallel",)),
    )(*args)


def _ffn_epilogue(f, h_ref, gp_ref, gt_ref, o_ref):
    ms = jnp.mean(f * f, axis=-1, keepdims=True)
    o_ref[...] = h_ref[...] + gt_ref[...] * (f * lax.rsqrt(ms + EPS) * gp_ref[...])


FF_CHUNK = 512


def _swiglu_tile(y, wg_ref, wu_ref, wd_ref, act_ref):
    for c in range(D_FF // FF_CHUNK):
        cols = slice(c * FF_CHUNK, (c + 1) * FF_CHUNK)
        gate = _dot(y, wg_ref[:, cols])
        up = _dot(y, wu_ref[:, cols])
        act_ref[:, cols] = (gate * jax.nn.sigmoid(gate) * up).astype(BF16)
    return _dot(act_ref[...], wd_ref[...])


def _ffn_dense_kernel(y_ref, wg_ref, wu_ref, wd_ref, h_ref, gp_ref, gt_ref, o_ref, act_ref):
    f = _swiglu_tile(y_ref[...], wg_ref, wu_ref, wd_ref, act_ref)
    _ffn_epilogue(f, h_ref, gp_ref, gt_ref, o_ref)


def _ffn_dense(y, wg, wu, wd, h, mods, gp, *, tm, tiles_per_batch):
    T = h.shape[0]
    if tiles_per_batch is None:
        mrow = lambda i: 4
    else:
        mrow = lambda i: i // tiles_per_batch
    resident = pl.Buffered(1)
    return pl.pallas_call(
        _ffn_dense_kernel,
        grid=(T // tm,),
        in_specs=[pl.BlockSpec((tm, D_MODEL), lambda i: (i, 0)),
                  pl.BlockSpec((D_MODEL, D_FF), lambda i: (0, 0), pipeline_mode=resident),
                  pl.BlockSpec((D_MODEL, D_FF), lambda i: (0, 0), pipeline_mode=resident),
                  pl.BlockSpec((D_FF, D_MODEL), lambda i: (0, 0), pipeline_mode=resident),
                  pl.BlockSpec((tm, D_MODEL), lambda i: (i, 0)),
                  pl.BlockSpec((1, D_MODEL), lambda i: (0, 0)),
                  pl.BlockSpec((None, None, 1, D_MODEL), lambda i: (mrow(i), 5, 0, 0))],
        out_specs=pl.BlockSpec((tm, D_MODEL), lambda i: (i, 0)),
        out_shape=jax.ShapeDtypeStruct((T, D_MODEL), F32),
        scratch_shapes=[pltpu.VMEM((tm, D_FF), BF16)],
        compiler_params=_cparams(("parallel",)),
    )(y, wg, wu, wd, h, gp, mods)


MOE_TB = 512
MOE_TM = 512
SUB = 128
NSUB = MOE_TB // SUB
RUN_ALIGN = 16


def _moe_tile_bound(T):
    nt = T // MOE_TB
    rows = 2 * T + nt * N_EXPERTS * (RUN_ALIGN - 1) + N_EXPERTS * (SUB + MOE_TM - 1)
    return -(-rows // MOE_TM)


def _moe_plan(cnt, T):
    p = (cnt + RUN_ALIGN - 1) // RUN_ALIGN * RUN_ALIGN
    base = jnp.cumsum(p, axis=0) - p
    used = jnp.sum(p, axis=0)
    tiles = (used + SUB + MOE_TM - 1) // MOE_TM
    tend = jnp.cumsum(tiles)
    off = (tend - tiles) * MOE_TM
    rowbase = (off[None, :] + base).astype(jnp.int32)
    nsub = (cnt + SUB - 1) // SUB
    slotbase = (jnp.cumsum(nsub, axis=1) - nsub).astype(jnp.int32)
    nt_bound = _moe_tile_bound(T)
    j = jnp.arange(nt_bound, dtype=jnp.int32)
    tile_e = jnp.minimum(jnp.sum(j[:, None] >= tend[None, :], axis=1), N_EXPERTS - 1).astype(jnp.int32)
    nvalid = tend[-1].astype(jnp.int32)
    tile_blk = jnp.minimum(j, nvalid - 1)
    return rowbase, slotbase, tile_e, tile_blk, nvalid.reshape(1)


NSLOT = 2 * MOE_TB // SUB + N_EXPERTS
SLOT_ROWS = NSLOT * SUB


def _for_each_run_slot(cn_ref, b, fn):
    for e in range(N_EXPERTS):
        for s in range(NSUB):
            @pl.when(cn_ref[b * N_EXPERTS + e] > SUB * s)
            def _(e=e, s=s):
                fn(e, s)


def _run_copy_out(stage, ys_ref, sems, rb_ref, sb_ref, b, par, e, s):
    slot = sb_ref[b * N_EXPERTS + e] + s
    r0 = pl.multiple_of(rb_ref[b * N_EXPERTS + e] + SUB * s, RUN_ALIGN)
    return pltpu.make_async_copy(stage.at[par, pl.ds(pl.multiple_of(slot * SUB, SUB), SUB), :],
                                 ys_ref.at[pl.ds(r0, SUB), :], sems.at[par, e, s])


def _dispatch_kernel(rb_ref, sb_ref, cn_ref, y_ref, cmb_ref, sbv_ref, ltri_ref, ysin_ref,
                     tok_ref, ys_ref, stage, sems):
    del ysin_ref
    b = pl.program_id(0)
    nb = pl.num_programs(0)
    par = b % 2
    cmb = cmb_ref[...]
    sel = cmb > 0.0
    rk = _dot(ltri_ref[...], jnp.where(sel, 1.0, 0.0).astype(BF16))
    srow = jnp.where(sel, rk + sbv_ref[...], -1.0)
    nsel = jnp.sum(jnp.where(sel, 1.0, 0.0), axis=-1, keepdims=True)
    sa = jnp.max(srow, axis=-1, keepdims=True)
    sb = jnp.where(nsel > 1.5, jnp.sum(jnp.where(sel, srow, 0.0), axis=-1, keepdims=True) - sa, -1.0)
    ga = jnp.sum(jnp.where(srow == sa, cmb, 0.0), axis=-1, keepdims=True)
    gb = jnp.sum(cmb, axis=-1, keepdims=True) - ga
    lane = lax.broadcasted_iota(jnp.int32, cmb.shape, 1)
    tok = jnp.where(lane == 0, sa, jnp.where(lane == 1, sb, jnp.where(lane == 2, ga, jnp.where(lane == 3, gb, 0.0))))
    tok_ref[...] = tok
    tok_t = tok.T
    rr = lax.broadcasted_iota(jnp.int32, (SLOT_ROWS, cmb.shape[0]), 0).astype(F32)
    g = jnp.where((rr == tok_t[0:1, :]) | (rr == tok_t[1:2, :]), 1.0, 0.0).astype(BF16)
    stage[par] = _dot(g, y_ref[...]).astype(BF16)

    @pl.when(b > 0)
    def _():
        _for_each_run_slot(cn_ref, b - 1, lambda e, s: _run_copy_out(
            stage, ys_ref, sems, rb_ref, sb_ref, b - 1, 1 - par, e, s).wait())

    _for_each_run_slot(cn_ref, b, lambda e, s: _run_copy_out(
        stage, ys_ref, sems, rb_ref, sb_ref, b, par, e, s).start())

    @pl.when(b == nb - 1)
    def _():
        _for_each_run_slot(cn_ref, b, lambda e, s: _run_copy_out(
            stage, ys_ref, sems, rb_ref, sb_ref, b, par, e, s).wait())


def _moe_dispatch(y, cmb, rowbase, slotbase, cnt, nt_bound):
    T = y.shape[0]
    nt = T // MOE_TB
    sbv = jnp.zeros((nt, 1, LANES), F32).at[:, 0, :N_EXPERTS].set((slotbase * SUB).astype(F32))
    ltri = jnp.asarray(np.tril(np.ones((MOE_TB, MOE_TB), np.float32), -1)).astype(BF16)
    ys0 = jnp.zeros((nt_bound * MOE_TM, D_MODEL), BF16)
    imap2 = lambda b, rb, sb, cn: (b, 0)
    return pl.pallas_call(
        _dispatch_kernel,
        grid_spec=pltpu.PrefetchScalarGridSpec(
            num_scalar_prefetch=3,
            grid=(nt,),
            in_specs=[pl.BlockSpec((MOE_TB, D_MODEL), imap2),
                      pl.BlockSpec((MOE_TB, LANES), imap2),
                      pl.BlockSpec((None, 1, LANES), lambda b, rb, sb, cn: (b, 0, 0)),
                      pl.BlockSpec((MOE_TB, MOE_TB), lambda b, rb, sb, cn: (0, 0)),
                      pl.BlockSpec(memory_space=pl.ANY)],
            out_specs=[pl.BlockSpec((MOE_TB, LANES), imap2),
                       pl.BlockSpec(memory_space=pl.ANY)],
            scratch_shapes=[pltpu.VMEM((2, SLOT_ROWS, D_MODEL), BF16),
                            pltpu.SemaphoreType.DMA((2, N_EXPERTS, NSUB))]),
        out_shape=[jax.ShapeDtypeStruct((T, LANES), F32),
                   jax.ShapeDtypeStruct((nt_bound * MOE_TM, D_MODEL), BF16)],
        input_output_aliases={7: 1},
        compiler_params=_cparams(("arbitrary",)),
    )(rowbase.reshape(-1), slotbase.reshape(-1), cnt.reshape(-1), y, cmb, sbv, ltri, ys0)


def _ffn_group_kernel(te_ref, tb_ref, nv_ref, y_ref, wg_ref, wu_ref, wd_ref, o_ref, act_ref):
    del te_ref, tb_ref
    j = pl.program_id(0)

    @pl.when(j < nv_ref[0])
    def _():
        o_ref[...] = _swiglu_tile(y_ref[...], wg_ref, wu_ref, wd_ref, act_ref).astype(BF16)

    @pl.when(j >= nv_ref[0])
    def _():
        o_ref[...] = jnp.zeros_like(o_ref)


def _ffn_group(ys, wg, wu, wd, tile_e, tile_blk, nvalid):
    rows = ys.shape[0]
    resident = pl.Buffered(1)
    return pl.pallas_call(
        _ffn_group_kernel,
        grid_spec=pltpu.PrefetchScalarGridSpec(
            num_scalar_prefetch=3,
            grid=(rows // MOE_TM,),
            in_specs=[pl.BlockSpec((MOE_TM, D_MODEL), lambda j, te, tb, nv: (tb[j], 0)),
                      pl.BlockSpec((None, D_MODEL, D_FF), lambda j, te, tb, nv: (te[j], 0, 0), pipeline_mode=resident),
                      pl.BlockSpec((None, D_MODEL, D_FF), lambda j, te, tb, nv: (te[j], 0, 0), pipeline_mode=resident),
                      pl.BlockSpec((None, D_FF, D_MODEL), lambda j, te, tb, nv: (te[j], 0, 0), pipeline_mode=resident)],
            out_specs=pl.BlockSpec((MOE_TM, D_MODEL), lambda j, te, tb, nv: (j, 0)),
            scratch_shapes=[pltpu.VMEM((MOE_TM, D_FF), BF16)]),
        out_shape=jax.ShapeDtypeStruct((rows, D_MODEL), BF16),
        compiler_params=_cparams(("arbitrary",)),
    )(tile_e, tile_blk, nvalid, ys, wg, wu, wd)


def _run_copy_in(fs_ref, fbuf, sems, rb_ref, sb_ref, b, par, e, s):
    slot = sb_ref[b * N_EXPERTS + e] + s
    r0 = pl.multiple_of(rb_ref[b * N_EXPERTS + e] + SUB * s, RUN_ALIGN)
    return pltpu.make_async_copy(fs_ref.at[pl.ds(r0, SUB), :],
                                 fbuf.at[par, pl.ds(pl.multiple_of(slot * SUB, SUB), SUB), :], sems.at[par, e, s])


def _combine_kernel(rb_ref, sb_ref, cn_ref, tok_ref, h_ref, gp_ref, gt_ref, fs_ref, o_ref, fbuf, sems):
    b = pl.program_id(0)
    nb = pl.num_programs(0)
    par = b % 2

    @pl.when(b == 0)
    def _():
        fbuf[...] = jnp.zeros_like(fbuf)
        _for_each_run_slot(cn_ref, b, lambda e, s: _run_copy_in(
            fs_ref, fbuf, sems, rb_ref, sb_ref, b, par, e, s).start())

    @pl.when(b + 1 < nb)
    def _():
        _for_each_run_slot(cn_ref, b + 1, lambda e, s: _run_copy_in(
            fs_ref, fbuf, sems, rb_ref, sb_ref, b + 1, 1 - par, e, s).start())

    tok = tok_ref[...]
    sa, sb, ga, gb = tok[:, 0:1], tok[:, 1:2], tok[:, 2:3], tok[:, 3:4]
    cc = lax.broadcasted_iota(jnp.int32, (tok.shape[0], SLOT_ROWS), 1).astype(F32)
    pick = jnp.where(cc == sa, ga, jnp.where(cc == sb, gb, 0.0)).astype(BF16)
    _for_each_run_slot(cn_ref, b, lambda e, s: _run_copy_in(
        fs_ref, fbuf, sems, rb_ref, sb_ref, b, par, e, s).wait())
    f = _dot(pick, fbuf[par])
    _ffn_epilogue(f, h_ref, gp_ref, gt_ref, o_ref)


def _moe_combine(fs, tok, h, mods, gp, rowbase, slotbase, cnt, *, tiles_per_batch):
    T = h.shape[0]
    nt = T // MOE_TB
    imap2 = lambda b, rb, sb, cn: (b, 0)
    return pl.pallas_call(
        _combine_kernel,
        grid_spec=pltpu.PrefetchScalarGridSpec(
            num_scalar_prefetch=3,
            grid=(nt,),
            in_specs=[pl.BlockSpec((MOE_TB, LANES), imap2),
                      pl.BlockSpec((MOE_TB, D_MODEL), imap2),
                      pl.BlockSpec((1, D_MODEL), lambda b, rb, sb, cn: (0, 0)),
                      pl.BlockSpec((None, None, 1, D_MODEL), lambda b, rb, sb, cn: (b // tiles_per_batch, 5, 0, 0)),
                      pl.BlockSpec(memory_space=pl.ANY)],
            out_specs=pl.BlockSpec((MOE_TB, D_MODEL), imap2),
            scratch_shapes=[pltpu.VMEM((2, SLOT_ROWS, D_MODEL), BF16),
                            pltpu.SemaphoreType.DMA((2, N_EXPERTS, NSUB))]),
        out_shape=jax.ShapeDtypeStruct((T, D_MODEL), F32),
        compiler_params=_cparams(("arbitrary",)),
    )(rowbase.reshape(-1), slotbase.reshape(-1), cnt.reshape(-1), tok, h, gp, mods, fs)


def _ffn_moe(y, cmb, cnt_tiles, wg, wu, wd, h, mods, gp, *, tiles_per_batch):
    T = h.shape[0]
    cnt = cnt_tiles.reshape(T // MOE_TB, LANES)[:, :N_EXPERTS].astype(jnp.int32)
    rowbase, slotbase, tile_e, tile_blk, nvalid = _moe_plan(cnt, T)
    tok, ys = _moe_dispatch(y, cmb, rowbase, slotbase, cnt, _moe_tile_bound(T))
    fs = _ffn_group(ys, wg, wu, wd, tile_e, tile_blk, nvalid)
    return _moe_combine(fs, tok, h, mods, gp, rowbase, slotbase, cnt, tiles_per_batch=tiles_per_batch)


def _blockdiag(m, reps):
    n = m.shape[0]
    out = np.zeros((n * reps, n * reps), np.float64)
    for r in range(reps):
        out[r * n:(r + 1) * n, r * n:(r + 1) * n] = m
    return out


def _dft_tables(n_rows):
    n = n_rows * GRID_W
    r = np.arange(n_rows)
    c = np.arange(GRID_W)
    a1 = 2 * np.pi * np.outer(r, r) / n_rows
    c1, s1 = np.cos(a1), np.sin(a1)
    m1 = np.block([[c1, s1], [-s1, c1]]) * 0.125
    at = 2 * np.pi * np.outer(r, c) / n
    twc = np.repeat(np.cos(at), B_WIDTH, axis=1)
    tws = np.repeat(np.sin(at), B_WIDTH, axis=1)
    a3 = 2 * np.pi * np.outer(c, c) / GRID_W
    m3 = np.concatenate([np.cos(a3), np.sin(a3)], axis=1) * (8.0 / np.sqrt(n))
    f32 = lambda t: jnp.asarray(t.astype(np.float32))
    return f32(m1).astype(BF16), f32(twc), f32(tws), f32(m3).astype(BF16)


def _channel_dft_table():
    d = np.arange(HEAD_DIM)
    a = 2 * np.pi * np.outer(d, d) / HEAD_DIM
    w = np.concatenate([_blockdiag(np.cos(a), 4), -_blockdiag(np.sin(a), 4)], axis=1) * 0.125
    return jnp.asarray(w.astype(np.float32)).astype(BF16)


def _ctx_dft_table(m):
    p = np.arange(m)
    a = 2 * np.pi * np.outer(p, p) / m
    w = np.concatenate([np.cos(a), np.sin(a)], axis=1) * (8.0 / np.sqrt(m * HEAD_DIM))
    return jnp.asarray(w.astype(np.float32)).astype(BF16)


def _rope_tables(n_tok):
    rows = n_tok // GRID_W
    row = jnp.broadcast_to(jnp.arange(rows)[:, None], (rows, GRID_W)).reshape(-1)
    col = jnp.broadcast_to(jnp.arange(GRID_W)[None, :], (rows, GRID_W)).reshape(-1)
    half = HEAD_DIM // 2
    inv = ROPE_BASE ** (-jnp.arange(0, half, 2, dtype=F32) / half)
    ang = jnp.stack([row.astype(F32)[:, None] * inv, col.astype(F32)[:, None] * inv], axis=1)
    cos, sin = jnp.cos(ang), jnp.sin(ang)
    zer = jnp.zeros_like(sin)
    lay = lambda a, b: jnp.tile(jnp.stack([a, b], axis=2).reshape(n_tok, HEAD_DIM), (1, LANES // HEAD_DIM))
    return lay(cos, cos), lay(-sin, zer), lay(zer, sin)


def kernel(x, c, ctx, c_ctx, w_ada, b_ada, g_mix_pre, g_mix_post, g_ffn_pre, g_ffn_post,
           w_in, w_s, b_s, g_v, w_f, sink, w_out, w_gate_d, w_up_d, w_down_d,
           w_router, b_router, w_gate_e, w_up_e, w_down_e):
    bsz, n_lat, _ = x.shape
    n_ctx = ctx.shape[1]
    T, Tc = bsz * n_lat, bsz * n_ctx
    tm = 512
    tpb = n_lat // tm

    rope_tabs = _rope_tables(n_lat)
    m1, twc, tws, m3 = _dft_tables(n_lat // GRID_W)
    wdft = _channel_dft_table()
    mctx = _ctx_dft_table(n_ctx)
    hm = jnp.asarray(_blockdiag(np.full((HEAD_DIM, HEAD_DIM), 1.0 / HEAD_DIM), A_HEADS).astype(np.float32)).astype(BF16)

    cond8 = jnp.zeros((8, D_MODEL), F32).at[:bsz].set(c).at[4].set(c_ctx)
    h = x.reshape(T, D_MODEL)
    hc = ctx.reshape(Tc, D_MODEL)

    for i in range(DEPTH):
        last = i == DEPTH - 1
        mods = _adaln(cond8, w_ada[i], b_ada[i][None, :]).reshape(8, 6, 1, D_MODEL)
        win = w_in[i].astype(BF16)
        ws = w_s[i].astype(BF16)
        bsx = jnp.repeat(b_s[i].T, HEAD_DIM, axis=1)
        gv = g_v[i].reshape(1, A_WIDTH)
        wf = jax.scipy.linalg.block_diag(*[w_f[i][g] for g in range(4)]).astype(BF16)
        wo = w_out[i].astype(BF16)
        gpre = g_mix_pre[i][None, :]
        gpost = g_mix_post[i][None, :]
        gfpre = g_ffn_pre[i][None, :]
        gfpost = g_ffn_post[i][None, :]
        sk = sink[i]

        sgu, fre, fim, q, k, v = _premix(h, mods, gpre, win, ws, bsx, gv, hm, wdft, rope_tabs,
                                         tm=tm, tiles_per_batch=tpb, n_pos=n_lat)
        sguc, frec, fimc, qc, kc, vc = _premix(hc, mods, gpre, win, ws, bsx, gv, hm, wdft, None,
                                               tm=tm, tiles_per_batch=None, n_pos=n_ctx)
        yf = _seq_dft(fre, fim, bsz, n_lat // GRID_W, m1, twc, tws, m3)
        at = _window_attention(q, k, v, kc, vc, sk, bsz, n_lat, n_ctx)

        if i % 2 == 0:
            router = None
        else:
            j = i // 2
            wr = jnp.zeros((D_MODEL, LANES), F32).at[:, :N_EXPERTS].set(w_router[j])
            br = jnp.zeros((1, LANES), F32).at[0, :N_EXPERTS].set(b_router[j])
            router = (wr, br)
        res = _postmix(sgu, yf, at, wf, wo, h, mods, gpost, gfpre, router, tm=tm, tiles_per_batch=tpb)
        if not last:
            yfc = _ctx_dft(frec, fimc, bsz, n_ctx, mctx)
            atc = _context_attention(qc, kc, vc, sk, bsz, n_ctx)
            resc = _postmix(sguc, yfc, atc, wf, wo, hc, mods, gpost, gfpre, router, tm=tm, tiles_per_batch=None)

        j = i // 2
        if i % 2 == 0:
            wg, wu, wd = w_gate_d[j].astype(BF16), w_up_d[j].astype(BF16), w_down_d[j].astype(BF16)
            h = _ffn_dense(res[1], wg, wu, wd, res[0], mods, gfpost, tm=tm, tiles_per_batch=tpb)
            if not last:
                hc = _ffn_dense(resc[1], wg, wu, wd, resc[0], mods, gfpost, tm=tm, tiles_per_batch=None)
        else:
            wg, wu, wd = w_gate_e[j].astype(BF16), w_up_e[j].astype(BF16), w_down_e[j].astype(BF16)
            assert last and tm == MOE_TB, "the expert FFN is only built for the final layer's latent tokens"
            h = _ffn_moe(res[1], res[2], res[3], wg, wu, wd, res[0], mods, gfpost,
                         tiles_per_batch=n_lat // MOE_TB)
    return h.reshape(bsz, n_lat, D_MODEL)
```

```python
import functools

import numpy as np
import jax
import jax.numpy as jnp
from jax import lax
from jax.experimental import pallas as pl
from jax.experimental.pallas import tpu as pltpu

D_MODEL = 1024
DEPTH = 2
GRID_W = 64
HEAD_DIM = 64
EPS = 1e-6
A_HEADS = 4
A_WIDTH = 256
CHUNK = 128
B_WIDTH = 256
C_Q_HEADS = 8
C_WIDTH = 512
KV_WIDTH = 128
WINDOW = 128
BLOCK = 128
ROPE_BASE = 10000.0
OFF_B = 512
OFF_Q = 768
OFF_K = 1280
OFF_V = 1408
N_IN = 1536
D_FF = 3584
N_EXPERTS = 8

LANES = 128
VMEM_LIMIT = 56 * 1024 * 1024
NEG = -1e30

F32 = jnp.float32
BF16 = jnp.bfloat16


def _dot(a, b):
    return jnp.dot(a, b, preferred_element_type=F32)


def _cparams(sem):
    return pltpu.CompilerParams(dimension_semantics=sem, vmem_limit_bytes=VMEM_LIMIT)


def _adaln_kernel(c_ref, w_ref, b_ref, o_ref):
    c = c_ref[...]
    s = (c * jax.nn.sigmoid(c)).astype(BF16)
    o_ref[...] = _dot(s, w_ref[...].astype(BF16)) + b_ref[...]


def _adaln(cond8, w, b):
    tn = 1536
    return pl.pallas_call(
        _adaln_kernel,
        grid=(6 * D_MODEL // tn,),
        in_specs=[pl.BlockSpec((8, D_MODEL), lambda j: (0, 0)),
                  pl.BlockSpec((D_MODEL, tn), lambda j: (0, j)),
                  pl.BlockSpec((1, tn), lambda j: (0, j))],
        out_specs=pl.BlockSpec((8, tn), lambda j: (0, j)),
        out_shape=jax.ShapeDtypeStruct((8, 6 * D_MODEL), F32),
        compiler_params=_cparams(("arbitrary",)),
    )(cond8, w, b)


def _gelu_tanh(x):
    return 0.5 * x * (1.0 + jnp.tanh(0.7978845608028654 * (x + 0.044715 * x * x * x)))


def _premix_kernel(*refs, rope):
    if rope:
        (h_ref, sh_ref, sc_ref, g_ref, win_ref, ws_ref, bs_ref, gv_ref, hm_ref, wdft_ref,
         cos_ref, s1_ref, s2_ref, sgu_ref, re_ref, im_ref, q_ref, k_ref, v_ref, fsc_ref) = refs
    else:
        (h_ref, sh_ref, sc_ref, g_ref, win_ref, ws_ref, bs_ref, gv_ref, hm_ref, wdft_ref,
         sgu_ref, re_ref, im_ref, q_ref, k_ref, v_ref) = refs
    x = h_ref[...]
    tm = x.shape[0]
    ms = jnp.mean(x * x, axis=-1, keepdims=True)
    xn = x * lax.rsqrt(ms + EPS) * g_ref[...]
    xm = (xn * (1.0 + sc_ref[...]) + sh_ref[...]).astype(BF16)
    z = _dot(xm, win_ref[...])

    a = _gelu_tanh(z[:, :OFF_B])
    u = a[:, :A_WIDTH]
    v = a[:, A_WIDTH:]
    msv = _dot((v * v).astype(BF16), hm_ref[...])
    vn = (v * lax.rsqrt(msv + EPS) * gv_ref[...]).astype(BF16)
    head = lax.broadcasted_iota(jnp.int32, (CHUNK, A_WIDTH), 1) // HEAD_DIM
    for ck in range(tm // CHUNK):
        rows = slice(ck * CHUNK, (ck + 1) * CHUNK)
        vc = vn[rows]
        sv = bs_ref[...]
        for hh in range(A_HEADS):
            sv = sv + jnp.where(head == hh, _dot(ws_ref[hh], vc), 0.0)
        sgu_ref[rows, :] = (u[rows] * sv).astype(BF16)

    f = _dot(z[:, OFF_B:OFF_Q].astype(BF16), wdft_ref[...])
    if rope:
        nl = 2 * B_WIDTH // LANES
        for j in range(nl):
            fsc_ref[j] = f[:, j * LANES:(j + 1) * LANES]
        nr = tm // GRID_W
        for cc in range(GRID_W):
            for j in range(nl):
                blk = fsc_ref[j, pl.ds(cc, nr, stride=GRID_W), :].astype(BF16)
                dst = re_ref if j < nl // 2 else im_ref
                c0 = cc * B_WIDTH + (j % (nl // 2)) * LANES
                dst[:, c0:c0 + LANES] = blk
    else:
        re_ref[...] = f[:, :B_WIDTH].astype(BF16)
        im_ref[...] = f[:, B_WIDTH:].astype(BF16)

    def rot(t):
        if not rope:
            return t
        return (t * cos_ref[...] + pltpu.roll(t, LANES - 16, 1) * s1_ref[...]
                + pltpu.roll(t, 16, 1) * s2_ref[...])

    for j in range(C_WIDTH // LANES):
        cols = slice(OFF_Q + j * LANES, OFF_Q + (j + 1) * LANES)
        q_ref[:, j * LANES:(j + 1) * LANES] = (rot(z[:, cols]) * 0.125).astype(BF16)

    lo = lax.broadcasted_iota(jnp.int32, (tm, LANES), 1) < HEAD_DIM
    kk = rot(z[:, OFF_K:OFF_V])
    kr = pltpu.roll(kk, HEAD_DIM, 1)
    k_ref[:, :LANES] = jnp.where(lo, kk, kr).astype(BF16)
    k_ref[:, LANES:] = jnp.where(lo, kr, kk).astype(BF16)
    vv = z[:, OFF_V:]
    vr = pltpu.roll(vv, HEAD_DIM, 1)
    v_ref[:, :LANES] = jnp.where(lo, vv, vr).astype(BF16)
    v_ref[:, LANES:] = jnp.where(lo, vr, vv).astype(BF16)


def _premix(h, mods, g, win, ws, bsx, gv, hm, wdft, rope_tabs, *, tm, tiles_per_batch, n_pos):
    T = h.shape[0]
    rope = rope_tabs is not None
    if tiles_per_batch is None:
        mrow = lambda i: 4
    else:
        mrow = lambda i: i // tiles_per_batch
    const2 = lambda i: (0, 0)
    in_specs = [
        pl.BlockSpec((tm, D_MODEL), lambda i: (i, 0)),
        pl.BlockSpec((None, None, 1, D_MODEL), lambda i: (mrow(i), 0, 0, 0)),
        pl.BlockSpec((None, None, 1, D_MODEL), lambda i: (mrow(i), 1, 0, 0)),
        pl.BlockSpec((1, D_MODEL), const2),
        pl.BlockSpec((D_MODEL, N_IN), const2),
        pl.BlockSpec((A_HEADS, CHUNK, CHUNK), lambda i: (0, 0, 0)),
        pl.BlockSpec((CHUNK, A_WIDTH), const2),
        pl.BlockSpec((1, A_WIDTH), const2),
        pl.BlockSpec((A_WIDTH, A_WIDTH), const2),
        pl.BlockSpec((B_WIDTH, 2 * B_WIDTH), const2),
    ]
    args = [h, mods, mods, g, win, ws, bsx, gv, hm, wdft]
    if rope:
        nt = n_pos // tm
        for t in rope_tabs:
            in_specs.append(pl.BlockSpec((tm, LANES), lambda i: (i % nt, 0)))
            args.append(t)
    widths = (A_WIDTH, B_WIDTH, B_WIDTH, C_WIDTH, 2 * KV_WIDTH, 2 * KV_WIDTH)
    out_specs = [pl.BlockSpec((tm, w), lambda i: (i, 0)) for w in widths]
    out_shape = [jax.ShapeDtypeStruct((T, w), BF16) for w in widths]
    scratch = []
    if rope:
        for o in (1, 2):
            out_specs[o] = pl.BlockSpec((tm // GRID_W, GRID_W * B_WIDTH), lambda i: (i, 0))
            out_shape[o] = jax.ShapeDtypeStruct((T // GRID_W, GRID_W * B_WIDTH), BF16)
        scratch = [pltpu.VMEM((2 * B_WIDTH // LANES, tm, LANES), F32)]
    return pl.pallas_call(
        functools.partial(_premix_kernel, rope=rope),
        grid=(T // tm,),
        in_specs=in_specs, out_specs=out_specs, out_shape=out_shape, scratch_shapes=scratch,
        compiler_params=_cparams(("parallel",)),
    )(*args)


def _dft1_kernel(m1_ref, twc_ref, tws_ref, re_ref, im_ref, o_ref):
    x = jnp.concatenate([re_ref[0], im_ref[0]], axis=0)
    a = _dot(m1_ref[...], x)
    nr = a.shape[0] // 2
    are, aim = a[:nr], a[nr:]
    c, s = twc_ref[...], tws_ref[...]
    bre = (are * c + aim * s).astype(BF16)
    bim = (aim * c - are * s).astype(BF16)
    for cl in range(o_ref.shape[2]):
        o_ref[0, 0, cl] = bre[:, cl * B_WIDTH:(cl + 1) * B_WIDTH]
        o_ref[0, 1, cl] = bim[:, cl * B_WIDTH:(cl + 1) * B_WIDTH]


def _dft3_kernel(m3_ref, x_ref, o_ref, xs_ref, os_ref):
    kb = x_ref.shape[3]
    nl = B_WIDTH // LANES
    for p in range(2):
        for c in range(GRID_W):
            xc = x_ref[0, p, c].astype(F32)
            for j in range(nl):
                xs_ref[p * nl + j, c * kb:(c + 1) * kb, :] = xc[:, j * LANES:(j + 1) * LANES]
    for k in range(kb):
        rows = [jnp.concatenate([xs_ref[p * nl + j, pl.ds(k, GRID_W, stride=kb), :] for j in range(nl)], axis=1)
                for p in range(2)]
        res = _dot(m3_ref[...], jnp.concatenate(rows, axis=0).astype(BF16))
        for j in range(nl):
            os_ref[j, pl.ds(k, GRID_W, stride=kb), :] = res[:, j * LANES:(j + 1) * LANES]
    for k2 in range(GRID_W):
        o_ref[0, k2] = jnp.concatenate(
            [os_ref[j, k2 * kb:(k2 + 1) * kb, :] for j in range(nl)], axis=1).astype(BF16)


def _seq_dft(re, im, bsz, n_rows, m1, twc, tws, m3):
    ncol = GRID_W * B_WIDTH
    tn = 2048
    re3 = re.reshape(bsz, n_rows, ncol)
    im3 = im.reshape(bsz, n_rows, ncol)
    st1 = pl.pallas_call(
        _dft1_kernel,
        grid=(ncol // tn, bsz),
        in_specs=[pl.BlockSpec((2 * n_rows, 2 * n_rows), lambda j, b: (0, 0)),
                  pl.BlockSpec((n_rows, tn), lambda j, b: (0, j)),
                  pl.BlockSpec((n_rows, tn), lambda j, b: (0, j)),
                  pl.BlockSpec((1, n_rows, tn), lambda j, b: (b, 0, j)),
                  pl.BlockSpec((1, n_rows, tn), lambda j, b: (b, 0, j))],
        out_specs=pl.BlockSpec((1, 2, tn // B_WIDTH, n_rows, B_WIDTH), lambda j, b: (b, 0, j, 0, 0)),
        out_shape=jax.ShapeDtypeStruct((bsz, 2, GRID_W, n_rows, B_WIDTH), BF16),
        compiler_params=_cparams(("parallel", "parallel")),
    )(m1, twc, tws, re3, im3)
    kb = min(n_rows, 32)
    y = pl.pallas_call(
        _dft3_kernel,
        grid=(bsz, n_rows // kb),
        in_specs=[pl.BlockSpec((GRID_W, 2 * GRID_W), lambda b, j: (0, 0)),
                  pl.BlockSpec((1, 2, GRID_W, kb, B_WIDTH), lambda b, j: (b, 0, 0, j, 0))],
        out_specs=pl.BlockSpec((1, GRID_W, kb, B_WIDTH), lambda b, j: (b, 0, j, 0)),
        out_shape=jax.ShapeDtypeStruct((bsz, GRID_W, n_rows, B_WIDTH), BF16),
        scratch_shapes=[pltpu.VMEM((2 * B_WIDTH // LANES, GRID_W * kb, LANES), F32),
                        pltpu.VMEM((B_WIDTH // LANES, GRID_W * kb, LANES), F32)],
        compiler_params=_cparams(("parallel", "parallel")),
    )(m3, st1)
    return y.reshape(bsz * GRID_W * n_rows, B_WIDTH)


def _ctx_dft_kernel(m_ref, re_ref, im_ref, o_ref):
    x = jnp.concatenate([re_ref[0], im_ref[0]], axis=0)
    o_ref[0] = _dot(m_ref[...], x).astype(BF16)


def _ctx_dft(re, im, bsz, m, mat):
    re3 = re.reshape(bsz, m, B_WIDTH)
    im3 = im.reshape(bsz, m, B_WIDTH)
    y = pl.pallas_call(
        _ctx_dft_kernel,
        grid=(bsz,),
        in_specs=[pl.BlockSpec((m, 2 * m), lambda b: (0, 0)),
                  pl.BlockSpec((1, m, B_WIDTH), lambda b: (b, 0, 0)),
                  pl.BlockSpec((1, m, B_WIDTH), lambda b: (b, 0, 0))],
        out_specs=pl.BlockSpec((1, m, B_WIDTH), lambda b: (b, 0, 0)),
        out_shape=jax.ShapeDtypeStruct((bsz, m, B_WIDTH), BF16),
        compiler_params=_cparams(("parallel",)),
    )(mat, re3, im3)
    return y.reshape(bsz * m, B_WIDTH)


def _attn_kernel(sink_ref, *refs, local, nb):
    if local:
        q_ref, kp_ref, kc_ref, kn_ref, vp_ref, vc_ref, vn_ref, kx_ref, vx_ref, o_ref = refs
    else:
        q_ref, kx_ref, vx_ref, o_ref = refs
    q = q_ref[0]
    tq = q.shape[0]
    lo = lax.broadcasted_iota(jnp.int32, (tq, LANES), 1) < HEAD_DIM
    zero = jnp.zeros((tq, LANES), BF16)
    if local:
        i = pl.program_id(1)
        row = lax.broadcasted_iota(jnp.int32, (BLOCK, BLOCK), 0)
        col = lax.broadcasted_iota(jnp.int32, (BLOCK, BLOCK), 1)
        valid_prev = (col >= row) & (i > 0)
        valid_next = (col <= row) & (i < nb - 1)
    for g in range(2):
        gl = slice(g * LANES, (g + 1) * LANES)
        if local:
            kg = jnp.concatenate([kp_ref[0][:, gl], kc_ref[0][:, gl], kn_ref[0][:, gl], kx_ref[0][:, gl]], axis=0)
            vg = jnp.concatenate([vp_ref[0][:, gl], vc_ref[0][:, gl], vn_ref[0][:, gl], vx_ref[0][:, gl]], axis=0)
        else:
            kg = kx_ref[0][:, gl]
            vg = vx_ref[0][:, gl]
        qs = []
        for p in range(2):
            qp = q[:, g * 2 * LANES + p * LANES: g * 2 * LANES + (p + 1) * LANES]
            qs.append(jnp.where(lo, qp, zero))
            qs.append(jnp.where(lo, zero, qp))
        q4 = jnp.concatenate(qs, axis=0)
        s4 = lax.dot_general(q4, kg, (((1,), (1,)), ((), ())), preferred_element_type=F32)
        ps, rden = [], []
        for hl in range(4):
            s = s4[hl * tq:(hl + 1) * tq]
            if local:
                s = jnp.concatenate(
                    [jnp.where(valid_prev, s[:, :BLOCK], NEG), s[:, BLOCK:2 * BLOCK],
                     jnp.where(valid_next, s[:, 2 * BLOCK:3 * BLOCK], NEG), s[:, 3 * BLOCK:]], axis=1)
            sk = sink_ref[g * 4 + hl]
            m = jnp.maximum(jnp.max(s, axis=-1, keepdims=True), sk)
            p = jnp.exp(s - m)
            den = jnp.sum(p, axis=-1, keepdims=True) + jnp.exp(sk - m)
            ps.append(p.astype(BF16))
            rden.append(1.0 / den)
        o4 = _dot(jnp.concatenate(ps, axis=0), vg) * jnp.concatenate(rden, axis=0)
        for p in range(2):
            o_pair = jnp.where(lo, o4[(2 * p) * tq:(2 * p + 1) * tq], o4[(2 * p + 1) * tq:(2 * p + 2) * tq])
            c0 = g * 2 * LANES + p * LANES
            o_ref[0, :, c0:c0 + LANES] = o_pair.astype(BF16)


def _window_attention(q, k, v, kx, vx, sink, bsz, n, m):
    nb = n // BLOCK
    q3 = q.reshape(bsz, n, C_WIDTH)
    k3 = k.reshape(bsz, n, 2 * KV_WIDTH)
    v3 = v.reshape(bsz, n, 2 * KV_WIDTH)
    kx3 = kx.reshape(bsz, m, 2 * KV_WIDTH)
    vx3 = vx.reshape(bsz, m, 2 * KV_WIDTH)
    blk = lambda f: pl.BlockSpec((1, BLOCK, 2 * KV_WIDTH), f)
    prev = lambda b, i, s: (b, jnp.maximum(i - 1, 0), 0)
    cur = lambda b, i, s: (b, i, 0)
    nxt = lambda b, i, s: (b, jnp.minimum(i + 1, nb - 1), 0)
    ctxs = pl.BlockSpec((1, m, 2 * KV_WIDTH), lambda b, i, s: (b, 0, 0))
    o = pl.pallas_call(
        functools.partial(_attn_kernel, local=True, nb=nb),
        grid_spec=pltpu.PrefetchScalarGridSpec(
            num_scalar_prefetch=1,
            grid=(bsz, nb),
            in_specs=[pl.BlockSpec((1, BLOCK, C_WIDTH), cur),
                      blk(prev), blk(cur), blk(nxt), blk(prev), blk(cur), blk(nxt), ctxs, ctxs],
            out_specs=pl.BlockSpec((1, BLOCK, C_WIDTH), cur)),
        out_shape=jax.ShapeDtypeStruct((bsz, n, C_WIDTH), BF16),
        compiler_params=_cparams(("parallel", "parallel")),
    )(sink, q3, k3, k3, k3, v3, v3, v3, kx3, vx3)
    return o.reshape(bsz * n, C_WIDTH)


def _context_attention(q, kx, vx, sink, bsz, m):
    q3 = q.reshape(bsz, m, C_WIDTH)
    kx3 = kx.reshape(bsz, m, 2 * KV_WIDTH)
    vx3 = vx.reshape(bsz, m, 2 * KV_WIDTH)
    ctxs = pl.BlockSpec((1, m, 2 * KV_WIDTH), lambda b, s: (b, 0, 0))
    o = pl.pallas_call(
        functools.partial(_attn_kernel, local=False, nb=1),
        grid_spec=pltpu.PrefetchScalarGridSpec(
            num_scalar_prefetch=1,
            grid=(bsz,),
            in_specs=[pl.BlockSpec((1, m, C_WIDTH), lambda b, s: (b, 0, 0)), ctxs, ctxs],
            out_specs=pl.BlockSpec((1, m, C_WIDTH), lambda b, s: (b, 0, 0))),
        out_shape=jax.ShapeDtypeStruct((bsz, m, C_WIDTH), BF16),
        compiler_params=_cparams(("parallel",)),
    )(sink, q3, kx3, vx3)
    return o.reshape(bsz * m, C_WIDTH)


def _postmix_kernel(*refs, route):
    (sgu_ref, yf_ref, at_ref, wf_ref, wo_ref, h_ref, gp_ref, gt_ref, gf_ref, sh_ref, sc_ref) = refs[:11]
    if route:
        wr_ref, br_ref, hn_ref, y_ref, cmb_ref, cnt_ref = refs[11:]
    else:
        hn_ref, y_ref = refs[11:]
    fm = _dot(yf_ref[...], wf_ref[...]).astype(BF16)
    o = (_dot(sgu_ref[...], wo_ref[:A_WIDTH, :])
         + _dot(fm, wo_ref[A_WIDTH:A_WIDTH + B_WIDTH, :])
         + _dot(at_ref[...], wo_ref[A_WIDTH + B_WIDTH:, :]))
    ms = jnp.mean(o * o, axis=-1, keepdims=True)
    hn = h_ref[...] + gt_ref[...] * (o * lax.rsqrt(ms + EPS) * gp_ref[...])
    hn_ref[...] = hn
    ms2 = jnp.mean(hn * hn, axis=-1, keepdims=True)
    y = hn * lax.rsqrt(ms2 + EPS) * gf_ref[...] * (1.0 + sc_ref[...]) + sh_ref[...]
    y_ref[...] = y.astype(BF16)
    if route:
        yh = y.astype(BF16)
        yl = (y - yh.astype(F32)).astype(BF16)
        w = wr_ref[...]
        wh = w.astype(BF16)
        wl = (w - wh.astype(F32)).astype(BF16)
        lg = _dot(yh, wh) + _dot(yl, wh) + _dot(yh, wl) + br_ref[...]
        lane = lax.broadcasted_iota(jnp.int32, lg.shape, 1)
        lg = jnp.where(lane < N_EXPERTS, lg, NEG)
        m1 = jnp.max(lg, axis=-1, keepdims=True)
        i1 = jnp.min(jnp.where(lg == m1, lane, LANES), axis=-1, keepdims=True)
        lg2 = jnp.where(lane == i1, NEG, lg)
        m2 = jnp.max(lg2, axis=-1, keepdims=True)
        i2 = jnp.min(jnp.where(lg2 == m2, lane, LANES), axis=-1, keepdims=True)
        e2 = jnp.exp(m2 - m1)
        g1 = 1.0 / (1.0 + e2)
        g2 = e2 * g1
        cmb = jnp.where(lane == i1, g1, 0.0) + jnp.where(lane == i2, g2, 0.0)
        cmb_ref[...] = cmb
        cnt_ref[...] = jnp.sum(jnp.where(cmb > 0.0, 1.0, 0.0), axis=0, keepdims=True)


def _postmix(sgu, yf, at, wf, wo, h, mods, gp, gf, router, *, tm, tiles_per_batch):
    T = h.shape[0]
    if tiles_per_batch is None:
        mrow = lambda i: 4
    else:
        mrow = lambda i: i // tiles_per_batch
    const2 = lambda i: (0, 0)
    row = lambda w: pl.BlockSpec((tm, w), lambda i: (i, 0))
    mod = lambda j: pl.BlockSpec((None, None, 1, D_MODEL), lambda i: (mrow(i), j, 0, 0))
    in_specs = [row(A_WIDTH), row(B_WIDTH), row(C_WIDTH),
                pl.BlockSpec((B_WIDTH, B_WIDTH), const2),
                pl.BlockSpec((D_MODEL, D_MODEL), const2),
                row(D_MODEL),
                pl.BlockSpec((1, D_MODEL), const2),
                mod(2),
                pl.BlockSpec((1, D_MODEL), const2),
                mod(3), mod(4)]
    args = [sgu, yf, at, wf, wo, h, gp, mods, gf, mods, mods]
    out_specs = [row(D_MODEL), row(D_MODEL)]
    out_shape = [jax.ShapeDtypeStruct((T, D_MODEL), F32), jax.ShapeDtypeStruct((T, D_MODEL), BF16)]
    route = router is not None
    if route:
        in_specs += [pl.BlockSpec((D_MODEL, LANES), const2), pl.BlockSpec((1, LANES), const2)]
        args += list(router)
        out_specs += [row(LANES), pl.BlockSpec((None, 1, LANES), lambda i: (i, 0, 0))]
        out_shape += [jax.ShapeDtypeStruct((T, LANES), F32), jax.ShapeDtypeStruct((T // tm, 1, LANES), F32)]
    return pl.pallas_call(
        functools.partial(_postmix_kernel, route=route),
        grid=(T // tm,),
        in_specs=in_specs, out_specs=out_specs, out_shape=out_shape,
        compiler_params=_cparams(("parallel",)),
    )(*args)


def _ffn_epilogue(f, h_ref, gp_ref, gt_ref, o_ref):
    ms = jnp.mean(f * f, axis=-1, keepdims=True)
    o_ref[...] = h_ref[...] + gt_ref[...] * (f * lax.rsqrt(ms + EPS) * gp_ref[...])


FF_CHUNK = 512


def _swiglu_tile(y, wg_ref, wu_ref, wd_ref, act_ref):
    for c in range(D_FF // FF_CHUNK):
        cols = slice(c * FF_CHUNK, (c + 1) * FF_CHUNK)
        gate = _dot(y, wg_ref[:, cols])
        up = _dot(y, wu_ref[:, cols])
        act_ref[:, cols] = (gate * jax.nn.sigmoid(gate) * up).astype(BF16)
    return _dot(act_ref[...], wd_ref[...])


def _ffn_dense_kernel(y_ref, wg_ref, wu_ref, wd_ref, h_ref, gp_ref, gt_ref, o_ref, act_ref):
    f = _swiglu_tile(y_ref[...], wg_ref, wu_ref, wd_ref, act_ref)
    _ffn_epilogue(f, h_ref, gp_ref, gt_ref, o_ref)


def _ffn_dense(y, wg, wu, wd, h, mods, gp, *, tm, tiles_per_batch):
    T = h.shape[0]
    if tiles_per_batch is None:
        mrow = lambda i: 4
    else:
        mrow = lambda i: i // tiles_per_batch
    resident = pl.Buffered(1)
    return pl.pallas_call(
        _ffn_dense_kernel,
        grid=(T // tm,),
        in_specs=[pl.BlockSpec((tm, D_MODEL), lambda i: (i, 0)),
                  pl.BlockSpec((D_MODEL, D_FF), lambda i: (0, 0), pipeline_mode=resident),
                  pl.BlockSpec((D_MODEL, D_FF), lambda i: (0, 0), pipeline_mode=resident),
                  pl.BlockSpec((D_FF, D_MODEL), lambda i: (0, 0), pipeline_mode=resident),
                  pl.BlockSpec((tm, D_MODEL), lambda i: (i, 0)),
                  pl.BlockSpec((1, D_MODEL), lambda i: (0, 0)),
                  pl.BlockSpec((None, None, 1, D_MODEL), lambda i: (mrow(i), 5, 0, 0))],
        out_specs=pl.BlockSpec((tm, D_MODEL), lambda i: (i, 0)),
        out_shape=jax.ShapeDtypeStruct((T, D_MODEL), F32),
        scratch_shapes=[pltpu.VMEM((tm, D_FF), BF16)],
        compiler_params=_cparams(("parallel",)),
    )(y, wg, wu, wd, h, gp, mods)


MOE_TB = 512
MOE_TM = 512
SUB = 128
NSUB = MOE_TB // SUB
RUN_ALIGN = 16


def _moe_tile_bound(T):
    nt = T // MOE_TB
    rows = 2 * T + nt * N_EXPERTS * (RUN_ALIGN - 1) + N_EXPERTS * (SUB + MOE_TM - 1)
    return -(-rows // MOE_TM)


def _moe_plan(cnt, T):
    p = (cnt + RUN_ALIGN - 1) // RUN_ALIGN * RUN_ALIGN
    base = jnp.cumsum(p, axis=0) - p
    used = jnp.sum(p, axis=0)
    tiles = (used + SUB + MOE_TM - 1) // MOE_TM
    tend = jnp.cumsum(tiles)
    off = (tend - tiles) * MOE_TM
    rowbase = (off[None, :] + base).astype(jnp.int32)
    nsub = (cnt + SUB - 1) // SUB
    slotbase = (jnp.cumsum(nsub, axis=1) - nsub).astype(jnp.int32)
    nt_bound = _moe_tile_bound(T)
    j = jnp.arange(nt_bound, dtype=jnp.int32)
    tile_e = jnp.minimum(jnp.sum(j[:, None] >= tend[None, :], axis=1), N_EXPERTS - 1).astype(jnp.int32)
    nvalid = tend[-1].astype(jnp.int32)
    tile_blk = jnp.minimum(j, nvalid - 1)
    return rowbase, slotbase, tile_e, tile_blk, nvalid.reshape(1)


NSLOT = 2 * MOE_TB // SUB + N_EXPERTS
SLOT_ROWS = NSLOT * SUB


def _for_each_run_slot(cn_ref, b, fn):
    for e in range(N_EXPERTS):
        for s in range(NSUB):
            @pl.when(cn_ref[b * N_EXPERTS + e] > SUB * s)
            def _(e=e, s=s):
                fn(e, s)


def _run_copy_out(stage, ys_ref, sems, rb_ref, sb_ref, b, par, e, s):
    slot = sb_ref[b * N_EXPERTS + e] + s
    r0 = pl.multiple_of(rb_ref[b * N_EXPERTS + e] + SUB * s, RUN_ALIGN)
    return pltpu.make_async_copy(stage.at[par, pl.ds(pl.multiple_of(slot * SUB, SUB), SUB), :],
                                 ys_ref.at[pl.ds(r0, SUB), :], sems.at[par, e, s])


def _dispatch_kernel(rb_ref, sb_ref, cn_ref, y_ref, cmb_ref, sbv_ref, ltri_ref, ysin_ref,
                     tok_ref, ys_ref, stage, sems):
    del ysin_ref
    b = pl.program_id(0)
    nb = pl.num_programs(0)
    par = b % 2
    cmb = cmb_ref[...]
    sel = cmb > 0.0
    rk = _dot(ltri_ref[...], jnp.where(sel, 1.0, 0.0).astype(BF16))
    srow = jnp.where(sel, rk + sbv_ref[...], -1.0)
    nsel = jnp.sum(jnp.where(sel, 1.0, 0.0), axis=-1, keepdims=True)
    sa = jnp.max(srow, axis=-1, keepdims=True)
    sb = jnp.where(nsel > 1.5, jnp.sum(jnp.where(sel, srow, 0.0), axis=-1, keepdims=True) - sa, -1.0)
    ga = jnp.sum(jnp.where(srow == sa, cmb, 0.0), axis=-1, keepdims=True)
    gb = jnp.sum(cmb, axis=-1, keepdims=True) - ga
    lane = lax.broadcasted_iota(jnp.int32, cmb.shape, 1)
    tok = jnp.where(lane == 0, sa, jnp.where(lane == 1, sb, jnp.where(lane == 2, ga, jnp.where(lane == 3, gb, 0.0))))
    tok_ref[...] = tok
    tok_t = tok.T
    rr = lax.broadcasted_iota(jnp.int32, (SLOT_ROWS, cmb.shape[0]), 0).astype(F32)
    g = jnp.where((rr == tok_t[0:1, :]) | (rr == tok_t[1:2, :]), 1.0, 0.0).astype(BF16)
    stage[par] = _dot(g, y_ref[...]).astype(BF16)

    @pl.when(b > 0)
    def _():
        _for_each_run_slot(cn_ref, b - 1, lambda e, s: _run_copy_out(
            stage, ys_ref, sems, rb_ref, sb_ref, b - 1, 1 - par, e, s).wait())

    _for_each_run_slot(cn_ref, b, lambda e, s: _run_copy_out(
        stage, ys_ref, sems, rb_ref, sb_ref, b, par, e, s).start())

    @pl.when(b == nb - 1)
    def _():
        _for_each_run_slot(cn_ref, b, lambda e, s: _run_copy_out(
            stage, ys_ref, sems, rb_ref, sb_ref, b, par, e, s).wait())


def _moe_dispatch(y, cmb, rowbase, slotbase, cnt, nt_bound):
    T = y.shape[0]
    nt = T // MOE_TB
    sbv = jnp.zeros((nt, 1, LANES), F32).at[:, 0, :N_EXPERTS].set((slotbase * SUB).astype(F32))
    ltri = jnp.asarray(np.tril(np.ones((MOE_TB, MOE_TB), np.float32), -1)).astype(BF16)
    ys0 = jnp.zeros((nt_bound * MOE_TM, D_MODEL), BF16)
    imap2 = lambda b, rb, sb, cn: (b, 0)
    return pl.pallas_call(
        _dispatch_kernel,
        grid_spec=pltpu.PrefetchScalarGridSpec(
            num_scalar_prefetch=3,
            grid=(nt,),
            in_specs=[pl.BlockSpec((MOE_TB, D_MODEL), imap2),
                      pl.BlockSpec((MOE_TB, LANES), imap2),
                      pl.BlockSpec((None, 1, LANES), lambda b, rb, sb, cn: (b, 0, 0)),
                      pl.BlockSpec((MOE_TB, MOE_TB), lambda b, rb, sb, cn: (0, 0)),
                      pl.BlockSpec(memory_space=pl.ANY)],
            out_specs=[pl.BlockSpec((MOE_TB, LANES), imap2),
                       pl.BlockSpec(memory_space=pl.ANY)],
            scratch_shapes=[pltpu.VMEM((2, SLOT_ROWS, D_MODEL), BF16),
                            pltpu.SemaphoreType.DMA((2, N_EXPERTS, NSUB))]),
        out_shape=[jax.ShapeDtypeStruct((T, LANES), F32),
                   jax.ShapeDtypeStruct((nt_bound * MOE_TM, D_MODEL), BF16)],
        input_output_aliases={7: 1},
        compiler_params=_cparams(("arbitrary",)),
    )(rowbase.reshape(-1), slotbase.reshape(-1), cnt.reshape(-1), y, cmb, sbv, ltri, ys0)


def _ffn_group_kernel(te_ref, tb_ref, nv_ref, y_ref, wg_ref, wu_ref, wd_ref, o_ref, act_ref):
    del te_ref, tb_ref
    j = pl.program_id(0)

    @pl.when(j < nv_ref[0])
    def _():
        o_ref[...] = _swiglu_tile(y_ref[...], wg_ref, wu_ref, wd_ref, act_ref).astype(BF16)

    @pl.when(j >= nv_ref[0])
    def _():
        o_ref[...] = jnp.zeros_like(o_ref)


def _ffn_group(ys, wg, wu, wd, tile_e, tile_blk, nvalid):
    rows = ys.shape[0]
    resident = pl.Buffered(1)
    return pl.pallas_call(
        _ffn_group_kernel,
        grid_spec=pltpu.PrefetchScalarGridSpec(
            num_scalar_prefetch=3,
            grid=(rows // MOE_TM,),
            in_specs=[pl.BlockSpec((MOE_TM, D_MODEL), lambda j, te, tb, nv: (tb[j], 0)),
                      pl.BlockSpec((None, D_MODEL, D_FF), lambda j, te, tb, nv: (te[j], 0, 0), pipeline_mode=resident),
                      pl.BlockSpec((None, D_MODEL, D_FF), lambda j, te, tb, nv: (te[j], 0, 0), pipeline_mode=resident),
                      pl.BlockSpec((None, D_FF, D_MODEL), lambda j, te, tb, nv: (te[j], 0, 0), pipeline_mode=resident)],
            out_specs=pl.BlockSpec((MOE_TM, D_MODEL), lambda j, te, tb, nv: (j, 0)),
            scratch_shapes=[pltpu.VMEM((MOE_TM, D_FF), BF16)]),
        out_shape=jax.ShapeDtypeStruct((rows, D_MODEL), BF16),
        compiler_params=_cparams(("arbitrary",)),
    )(tile_e, tile_blk, nvalid, ys, wg, wu, wd)


def _run_copy_in(fs_ref, fbuf, sems, rb_ref, sb_ref, b, par, e, s):
    slot = sb_ref[b * N_EXPERTS + e] + s
    r0 = pl.multiple_of(rb_ref[b * N_EXPERTS + e] + SUB * s, RUN_ALIGN)
    return pltpu.make_async_copy(fs_ref.at[pl.ds(r0, SUB), :],
                                 fbuf.at[par, pl.ds(pl.multiple_of(slot * SUB, SUB), SUB), :], sems.at[par, e, s])


def _combine_kernel(rb_ref, sb_ref, cn_ref, tok_ref, h_ref, gp_ref, gt_ref, fs_ref, o_ref, fbuf, sems):
    b = pl.program_id(0)
    nb = pl.num_programs(0)
    par = b % 2

    @pl.when(b == 0)
    def _():
        fbuf[...] = jnp.zeros_like(fbuf)
        _for_each_run_slot(cn_ref, b, lambda e, s: _run_copy_in(
            fs_ref, fbuf, sems, rb_ref, sb_ref, b, par, e, s).start())

    @pl.when(b + 1 < nb)
    def _():
        _for_each_run_slot(cn_ref, b + 1, lambda e, s: _run_copy_in(
            fs_ref, fbuf, sems, rb_ref, sb_ref, b + 1, 1 - par, e, s).start())

    tok = tok_ref[...]
    sa, sb, ga, gb = tok[:, 0:1], tok[:, 1:2], tok[:, 2:3], tok[:, 3:4]
    cc = lax.broadcasted_iota(jnp.int32, (tok.shape[0], SLOT_ROWS), 1).astype(F32)
    pick = jnp.where(cc == sa, ga, jnp.where(cc == sb, gb, 0.0)).astype(BF16)
    _for_each_run_slot(cn_ref, b, lambda e, s: _run_copy_in(
        fs_ref, fbuf, sems, rb_ref, sb_ref, b, par, e, s).wait())
    f = _dot(pick, fbuf[par])
    _ffn_epilogue(f, h_ref, gp_ref, gt_ref, o_ref)


def _moe_combine(fs, tok, h, mods, gp, rowbase, slotbase, cnt, *, tiles_per_batch):
    T = h.shape[0]
    nt = T // MOE_TB
    imap2 = lambda b, rb, sb, cn: (b, 0)
    return pl.pallas_call(
        _combine_kernel,
        grid_spec=pltpu.PrefetchScalarGridSpec(
            num_scalar_prefetch=3,
            grid=(nt,),
            in_specs=[pl.BlockSpec((MOE_TB, LANES), imap2),
                      pl.BlockSpec((MOE_TB, D_MODEL), imap2),
                      pl.BlockSpec((1, D_MODEL), lambda b, rb, sb, cn: (0, 0)),
                      pl.BlockSpec((None, None, 1, D_MODEL), lambda b, rb, sb, cn: (b // tiles_per_batch, 5, 0, 0)),
                      pl.BlockSpec(memory_space=pl.ANY)],
            out_specs=pl.BlockSpec((MOE_TB, D_MODEL), imap2),
            scratch_shapes=[pltpu.VMEM((2, SLOT_ROWS, D_MODEL), BF16),
                            pltpu.SemaphoreType.DMA((2, N_EXPERTS, NSUB))]),
        out_shape=jax.ShapeDtypeStruct((T, D_MODEL), F32),
        compiler_params=_cparams(("arbitrary",)),
    )(rowbase.reshape(-1), slotbase.reshape(-1), cnt.reshape(-1), tok, h, gp, mods, fs)


def _ffn_moe(y, cmb, cnt_tiles, wg, wu, wd, h, mods, gp, *, tiles_per_batch):
    T = h.shape[0]
    cnt = cnt_tiles.reshape(T // MOE_TB, LANES)[:, :N_EXPERTS].astype(jnp.int32)
    rowbase, slotbase, tile_e, tile_blk, nvalid = _moe_plan(cnt, T)
    tok, ys = _moe_dispatch(y, cmb, rowbase, slotbase, cnt, _moe_tile_bound(T))
    fs = _ffn_group(ys, wg, wu, wd, tile_e, tile_blk, nvalid)
    return _moe_combine(fs, tok, h, mods, gp, rowbase, slotbase, cnt, tiles_per_batch=tiles_per_batch)


def _blockdiag(m, reps):
    n = m.shape[0]
    out = np.zeros((n * reps, n * reps), np.float64)
    for r in range(reps):
        out[r * n:(r + 1) * n, r * n:(r + 1) * n] = m
    return out


def _dft_tables(n_rows):
    n = n_rows * GRID_W
    r = np.arange(n_rows)
    c = np.arange(GRID_W)
    a1 = 2 * np.pi * np.outer(r, r) / n_rows
    c1, s1 = np.cos(a1), np.sin(a1)
    m1 = np.block([[c1, s1], [-s1, c1]]) * 0.125
    at = 2 * np.pi * np.outer(r, c) / n
    twc = np.repeat(np.cos(at), B_WIDTH, axis=1)
    tws = np.repeat(np.sin(at), B_WIDTH, axis=1)
    a3 = 2 * np.pi * np.outer(c, c) / GRID_W
    m3 = np.concatenate([np.cos(a3), np.sin(a3)], axis=1) * (8.0 / np.sqrt(n))
    f32 = lambda t: jnp.asarray(t.astype(np.float32))
    return f32(m1).astype(BF16), f32(twc), f32(tws), f32(m3).astype(BF16)


def _channel_dft_table():
    d = np.arange(HEAD_DIM)
    a = 2 * np.pi * np.outer(d, d) / HEAD_DIM
    w = np.concatenate([_blockdiag(np.cos(a), 4), -_blockdiag(np.sin(a), 4)], axis=1) * 0.125
    return jnp.asarray(w.astype(np.float32)).astype(BF16)


def _ctx_dft_table(m):
    p = np.arange(m)
    a = 2 * np.pi * np.outer(p, p) / m
    w = np.concatenate([np.cos(a), np.sin(a)], axis=1) * (8.0 / np.sqrt(m * HEAD_DIM))
    return jnp.asarray(w.astype(np.float32)).astype(BF16)


def _rope_tables(n_tok):
    rows = n_tok // GRID_W
    row = jnp.broadcast_to(jnp.arange(rows)[:, None], (rows, GRID_W)).reshape(-1)
    col = jnp.broadcast_to(jnp.arange(GRID_W)[None, :], (rows, GRID_W)).reshape(-1)
    half = HEAD_DIM // 2
    inv = ROPE_BASE ** (-jnp.arange(0, half, 2, dtype=F32) / half)
    ang = jnp.stack([row.astype(F32)[:, None] * inv, col.astype(F32)[:, None] * inv], axis=1)
    cos, sin = jnp.cos(ang), jnp.sin(ang)
    zer = jnp.zeros_like(sin)
    lay = lambda a, b: jnp.tile(jnp.stack([a, b], axis=2).reshape(n_tok, HEAD_DIM), (1, LANES // HEAD_DIM))
    return lay(cos, cos), lay(-sin, zer), lay(zer, sin)


def kernel(x, c, ctx, c_ctx, w_ada, b_ada, g_mix_pre, g_mix_post, g_ffn_pre, g_ffn_post,
           w_in, w_s, b_s, g_v, w_f, sink, w_out, w_gate_d, w_up_d, w_down_d,
           w_router, b_router, w_gate_e, w_up_e, w_down_e):
    bsz, n_lat, _ = x.shape
    n_ctx = ctx.shape[1]
    T, Tc = bsz * n_lat, bsz * n_ctx
    tm = 512
    tpb = n_lat // tm

    rope_tabs = _rope_tables(n_lat)
    m1, twc, tws, m3 = _dft_tables(n_lat // GRID_W)
    wdft = _channel_dft_table()
    mctx = _ctx_dft_table(n_ctx)
    hm = jnp.asarray(_blockdiag(np.full((HEAD_DIM, HEAD_DIM), 1.0 / HEAD_DIM), A_HEADS).astype(np.float32)).astype(BF16)

    cond8 = jnp.zeros((8, D_MODEL), F32).at[:bsz].set(c).at[4].set(c_ctx)
    h = x.reshape(T, D_MODEL)
    hc = ctx.reshape(Tc, D_MODEL)

    for i in range(DEPTH):
        last = i == DEPTH - 1
        mods = _adaln(cond8, w_ada[i], b_ada[i][None, :]).reshape(8, 6, 1, D_MODEL)
        win = w_in[i].astype(BF16)
        ws = w_s[i].astype(BF16)
        bsx = jnp.repeat(b_s[i].T, HEAD_DIM, axis=1)
        gv = g_v[i].reshape(1, A_WIDTH)
        wf = jax.scipy.linalg.block_diag(*[w_f[i][g] for g in range(4)]).astype(BF16)
        wo = w_out[i].astype(BF16)
        gpre = g_mix_pre[i][None, :]
        gpost = g_mix_post[i][None, :]
        gfpre = g_ffn_pre[i][None, :]
        gfpost = g_ffn_post[i][None, :]
        sk = sink[i]

        sgu, fre, fim, q, k, v = _premix(h, mods, gpre, win, ws, bsx, gv, hm, wdft, rope_tabs,
                                         tm=2 * tm, tiles_per_batch=tpb // 2, n_pos=n_lat)
        sguc, frec, fimc, qc, kc, vc = _premix(hc, mods, gpre, win, ws, bsx, gv, hm, wdft, None,
                                               tm=tm, tiles_per_batch=None, n_pos=n_ctx)
        yf = _seq_dft(fre, fim, bsz, n_lat // GRID_W, m1, twc, tws, m3)
        at = _window_attention(q, k, v, kc, vc, sk, bsz, n_lat, n_ctx)

        if i % 2 == 0:
            router = None
        else:
            j = i // 2
            wr = jnp.zeros((D_MODEL, LANES), F32).at[:, :N_EXPERTS].set(w_router[j])
            br = jnp.zeros((1, LANES), F32).at[0, :N_EXPERTS].set(b_router[j])
            router = (wr, br)
        res = _postmix(sgu, yf, at, wf, wo, h, mods, gpost, gfpre, router, tm=tm, tiles_per_batch=tpb)
        if not last:
            yfc = _ctx_dft(frec, fimc, bsz, n_ctx, mctx)
            atc = _context_attention(qc, kc, vc, sk, bsz, n_ctx)
            resc = _postmix(sguc, yfc, atc, wf, wo, hc, mods, gpost, gfpre, router, tm=tm, tiles_per_batch=None)

        j = i // 2
        if i % 2 == 0:
            wg, wu, wd = w_gate_d[j].astype(BF16), w_up_d[j].astype(BF16), w_down_d[j].astype(BF16)
            h = _ffn_dense(res[1], wg, wu, wd, res[0], mods, gfpost, tm=tm, tiles_per_batch=tpb)
            if not last:
                hc = _ffn_dense(resc[1], wg, wu, wd, resc[0], mods, gfpost, tm=tm, tiles_per_batch=None)
        else:
            wg, wu, wd = w_gate_e[j].astype(BF16), w_up_e[j].astype(BF16), w_down_e[j].astype(BF16)
            assert last and tm == MOE_TB, "the expert FFN is only built for the final layer's latent tokens"
            h = _ffn_moe(res[1], res[2], res[3], wg, wu, wd, res[0], mods, gfpost,
                         tiles_per_batch=n_lat // MOE_TB)
    return h.reshape(bsz, n_lat, D_MODEL)
```

```python
import functools

import numpy as np
import jax
import jax.numpy as jnp
from jax import lax
from jax.experimental import pallas as pl
from jax.experimental.pallas import tpu as pltpu

D_MODEL = 1024
DEPTH = 2
GRID_W = 64
HEAD_DIM = 64
EPS = 1e-6
A_HEADS = 4
A_WIDTH = 256
CHUNK = 128
B_WIDTH = 256
C_Q_HEADS = 8
C_WIDTH = 512
KV_WIDTH = 128
WINDOW = 128
BLOCK = 128
ROPE_BASE = 10000.0
OFF_B = 512
OFF_Q = 768
OFF_K = 1280
OFF_V = 1408
N_IN = 1536
D_FF = 3584
N_EXPERTS = 8

LANES = 128
VMEM_LIMIT = 56 * 1024 * 1024
NEG = -1e30
LOG2E = 1.4426950408889634
assert WINDOW == BLOCK

F32 = jnp.float32
BF16 = jnp.bfloat16


def _dot(a, b):
    return jnp.dot(a, b, preferred_element_type=F32)


def _cparams(sem):
    return pltpu.CompilerParams(dimension_semantics=sem, vmem_limit_bytes=VMEM_LIMIT)


def _adaln_kernel(c_ref, w_ref, b_ref, o_ref):
    c = c_ref[...]
    s = (c * jax.nn.sigmoid(c)).astype(BF16)
    o_ref[...] = _dot(s, w_ref[...].astype(BF16)) + b_ref[...]


def _adaln(cond8, w, b):
    tn = 1536
    nl = w.shape[0]
    return pl.pallas_call(
        _adaln_kernel,
        grid=(nl, 6 * D_MODEL // tn),
        in_specs=[pl.BlockSpec((8, D_MODEL), lambda l, j: (0, 0)),
                  pl.BlockSpec((None, D_MODEL, tn), lambda l, j: (l, 0, j)),
                  pl.BlockSpec((None, 1, tn), lambda l, j: (l, 0, j))],
        out_specs=pl.BlockSpec((None, 8, tn), lambda l, j: (l, 0, j)),
        out_shape=jax.ShapeDtypeStruct((nl, 8, 6 * D_MODEL), F32),
        compiler_params=_cparams(("arbitrary", "arbitrary")),
    )(cond8, w, b)


def _gelu_tanh(x):
    return 0.5 * x * (1.0 + jnp.tanh(0.7978845608028654 * (x + 0.044715 * x * x * x)))


PRE_SUB = 512


def _premix_kernel(*refs, rope):
    if rope:
        (h_ref, sh_ref, sc_ref, g_ref, win_ref, ws_ref, bs_ref, gv_ref, hm_ref, wdft_ref,
         cos_ref, s1_ref, s2_ref, sgu_ref, re_ref, im_ref, q_ref, k_ref, v_ref, fsc_ref) = refs
    else:
        (h_ref, sh_ref, sc_ref, g_ref, win_ref, ws_ref, bs_ref, gv_ref, hm_ref, wdft_ref,
         sgu_ref, re_ref, im_ref, q_ref, k_ref, v_ref) = refs
    tm = h_ref.shape[0]
    nl = 2 * B_WIDTH // LANES
    head = lax.broadcasted_iota(jnp.int32, (CHUNK, A_WIDTH), 1) // HEAD_DIM
    lo = lax.broadcasted_iota(jnp.int32, (PRE_SUB, LANES), 1) < HEAD_DIM
    for sb in range(tm // PRE_SUB):
        rs = slice(sb * PRE_SUB, (sb + 1) * PRE_SUB)
        x = h_ref[rs, :]
        ms = jnp.mean(x * x, axis=-1, keepdims=True)
        xn = x * lax.rsqrt(ms + EPS) * g_ref[...]
        xm = (xn * (1.0 + sc_ref[...]) + sh_ref[...]).astype(BF16)
        z = _dot(xm, win_ref[...])

        a = _gelu_tanh(z[:, :OFF_B])
        u = a[:, :A_WIDTH]
        v = a[:, A_WIDTH:]
        msv = _dot((v * v).astype(BF16), hm_ref[...])
        vn = (v * lax.rsqrt(msv + EPS) * gv_ref[...]).astype(BF16)
        for ck in range(PRE_SUB // CHUNK):
            rows = slice(ck * CHUNK, (ck + 1) * CHUNK)
            vc = vn[rows]
            sv = bs_ref[...]
            for hh in range(A_HEADS):
                sv = sv + jnp.where(head == hh, _dot(ws_ref[hh], vc), 0.0)
            sgu_ref[sb * PRE_SUB + ck * CHUNK:sb * PRE_SUB + (ck + 1) * CHUNK, :] = (u[rows] * sv).astype(BF16)

        f = _dot(z[:, OFF_B:OFF_Q].astype(BF16), wdft_ref[...])
        if rope:
            for j in range(nl):
                fsc_ref[j, rs, :] = f[:, j * LANES:(j + 1) * LANES]
        else:
            re_ref[rs, :] = f[:, :B_WIDTH].astype(BF16)
            im_ref[rs, :] = f[:, B_WIDTH:].astype(BF16)

        def rot(t):
            if not rope:
                return t
            return (t * cos_ref[rs, :] + pltpu.roll(t, LANES - 16, 1) * s1_ref[rs, :]
                    + pltpu.roll(t, 16, 1) * s2_ref[rs, :])

        for j in range(C_WIDTH // LANES):
            cols = slice(OFF_Q + j * LANES, OFF_Q + (j + 1) * LANES)
            q_ref[rs, j * LANES:(j + 1) * LANES] = (rot(z[:, cols]) * (HEAD_DIM ** -0.5 * LOG2E)).astype(BF16)

        kk = rot(z[:, OFF_K:OFF_V])
        kr = pltpu.roll(kk, HEAD_DIM, 1)
        k_ref[rs, :LANES] = jnp.where(lo, kk, kr).astype(BF16)
        k_ref[rs, LANES:] = jnp.where(lo, kr, kk).astype(BF16)
        vv = z[:, OFF_V:]
        vr = pltpu.roll(vv, HEAD_DIM, 1)
        v_ref[rs, :LANES] = jnp.where(lo, vv, vr).astype(BF16)
        v_ref[rs, LANES:] = jnp.where(lo, vr, vv).astype(BF16)

    if rope:
        nr = tm // GRID_W
        for cc in range(GRID_W):
            for j in range(nl):
                blk = fsc_ref[j, pl.ds(cc, nr, stride=GRID_W), :].astype(BF16)
                dst = re_ref if j < nl // 2 else im_ref
                c0 = cc * B_WIDTH + (j % (nl // 2)) * LANES
                dst[:, c0:c0 + LANES] = blk


def _premix(h, mods, g, win, ws, bsx, gv, hm, wdft, rope_tabs, *, tm, tiles_per_batch, n_pos):
    T = h.shape[0]
    rope = rope_tabs is not None
    if tiles_per_batch is None:
        mrow = lambda i: 4
    else:
        mrow = lambda i: i // tiles_per_batch
    const2 = lambda i: (0, 0)
    in_specs = [
        pl.BlockSpec((tm, D_MODEL), lambda i: (i, 0)),
        pl.BlockSpec((None, None, 1, D_MODEL), lambda i: (mrow(i), 0, 0, 0)),
        pl.BlockSpec((None, None, 1, D_MODEL), lambda i: (mrow(i), 1, 0, 0)),
        pl.BlockSpec((1, D_MODEL), const2),
        pl.BlockSpec((D_MODEL, N_IN), const2),
        pl.BlockSpec((A_HEADS, CHUNK, CHUNK), lambda i: (0, 0, 0)),
        pl.BlockSpec((CHUNK, A_WIDTH), const2),
        pl.BlockSpec((1, A_WIDTH), const2),
        pl.BlockSpec((A_WIDTH, A_WIDTH), const2),
        pl.BlockSpec((B_WIDTH, 2 * B_WIDTH), const2),
    ]
    args = [h, mods, mods, g, win, ws, bsx, gv, hm, wdft]
    if rope:
        nt = n_pos // tm
        for t in rope_tabs:
            in_specs.append(pl.BlockSpec((tm, LANES), lambda i: (i % nt, 0)))
            args.append(t)
    widths = (A_WIDTH, B_WIDTH, B_WIDTH, C_WIDTH, 2 * KV_WIDTH, 2 * KV_WIDTH)
    out_specs = [pl.BlockSpec((tm, w), lambda i: (i, 0)) for w in widths]
    out_shape = [jax.ShapeDtypeStruct((T, w), BF16) for w in widths]
    scratch = []
    if rope:
        for o in (1, 2):
            out_specs[o] = pl.BlockSpec((tm // GRID_W, GRID_W * B_WIDTH), lambda i: (i, 0))
            out_shape[o] = jax.ShapeDtypeStruct((T // GRID_W, GRID_W * B_WIDTH), BF16)
        scratch = [pltpu.VMEM((2 * B_WIDTH // LANES, tm, LANES), F32)]
    return pl.pallas_call(
        functools.partial(_premix_kernel, rope=rope),
        grid=(T // tm,),
        in_specs=in_specs, out_specs=out_specs, out_shape=out_shape, scratch_shapes=scratch,
        compiler_params=_cparams(("parallel",)),
    )(*args)


def _dft1_kernel(m1_ref, twc_ref, tws_ref, re_ref, im_ref, o_ref):
    x = jnp.concatenate([re_ref[0], im_ref[0]], axis=0)
    a = _dot(m1_ref[...], x)
    nr = a.shape[0] // 2
    are, aim = a[:nr], a[nr:]
    c, s = twc_ref[...], tws_ref[...]
    bre = (are * c + aim * s).astype(BF16)
    bim = (aim * c - are * s).astype(BF16)
    for cl in range(o_ref.shape[2]):
        o_ref[0, 0, cl] = bre[:, cl * B_WIDTH:(cl + 1) * B_WIDTH]
        o_ref[0, 1, cl] = bim[:, cl * B_WIDTH:(cl + 1) * B_WIDTH]


def _dft3_kernel(m3_ref, x_ref, o_ref, xs_ref, os_ref):
    kb = x_ref.shape[3]
    nl = B_WIDTH // LANES
    for p in range(2):
        for c in range(GRID_W):
            xc = x_ref[0, p, c].astype(F32)
            for j in range(nl):
                xs_ref[p * nl + j, c * kb:(c + 1) * kb, :] = xc[:, j * LANES:(j + 1) * LANES]
    for k in range(kb):
        rows = [jnp.concatenate([xs_ref[p * nl + j, pl.ds(k, GRID_W, stride=kb), :] for j in range(nl)], axis=1)
                for p in range(2)]
        res = _dot(m3_ref[...], jnp.concatenate(rows, axis=0).astype(BF16))
        for j in range(nl):
            os_ref[j, pl.ds(k, GRID_W, stride=kb), :] = res[:, j * LANES:(j + 1) * LANES]
    for k2 in range(GRID_W):
        o_ref[0, k2] = jnp.concatenate(
            [os_ref[j, k2 * kb:(k2 + 1) * kb, :] for j in range(nl)], axis=1).astype(BF16)


def _seq_dft(re, im, bsz, n_rows, m1, twc, tws, m3):
    ncol = GRID_W * B_WIDTH
    tn = 2048
    re3 = re.reshape(bsz, n_rows, ncol)
    im3 = im.reshape(bsz, n_rows, ncol)
    st1 = pl.pallas_call(
        _dft1_kernel,
        grid=(ncol // tn, bsz),
        in_specs=[pl.BlockSpec((2 * n_rows, 2 * n_rows), lambda j, b: (0, 0)),
                  pl.BlockSpec((n_rows, tn), lambda j, b: (0, j)),
                  pl.BlockSpec((n_rows, tn), lambda j, b: (0, j)),
                  pl.BlockSpec((1, n_rows, tn), lambda j, b: (b, 0, j)),
                  pl.BlockSpec((1, n_rows, tn), lambda j, b: (b, 0, j))],
        out_specs=pl.BlockSpec((1, 2, tn // B_WIDTH, n_rows, B_WIDTH), lambda j, b: (b, 0, j, 0, 0)),
        out_shape=jax.ShapeDtypeStruct((bsz, 2, GRID_W, n_rows, B_WIDTH), BF16),
        compiler_params=_cparams(("parallel", "parallel")),
    )(m1, twc, tws, re3, im3)
    kb = min(n_rows, 32)
    y = pl.pallas_call(
        _dft3_kernel,
        grid=(bsz, n_rows // kb),
        in_specs=[pl.BlockSpec((GRID_W, 2 * GRID_W), lambda b, j: (0, 0)),
                  pl.BlockSpec((1, 2, GRID_W, kb, B_WIDTH), lambda b, j: (b, 0, 0, j, 0))],
        out_specs=pl.BlockSpec((1, GRID_W, kb, B_WIDTH), lambda b, j: (b, 0, j, 0)),
        out_shape=jax.ShapeDtypeStruct((bsz, GRID_W, n_rows, B_WIDTH), BF16),
        scratch_shapes=[pltpu.VMEM((2 * B_WIDTH // LANES, GRID_W * kb, LANES), F32),
                        pltpu.VMEM((B_WIDTH // LANES, GRID_W * kb, LANES), F32)],
        compiler_params=_cparams(("parallel", "parallel")),
    )(m3, st1)
    return y.reshape(bsz * GRID_W * n_rows, B_WIDTH)


def _ctx_dft_kernel(m_ref, re_ref, im_ref, o_ref):
    x = jnp.concatenate([re_ref[0], im_ref[0]], axis=0)
    o_ref[0] = _dot(m_ref[...], x).astype(BF16)


def _ctx_dft(re, im, bsz, m, mat):
    re3 = re.reshape(bsz, m, B_WIDTH)
    im3 = im.reshape(bsz, m, B_WIDTH)
    y = pl.pallas_call(
        _ctx_dft_kernel,
        grid=(bsz,),
        in_specs=[pl.BlockSpec((m, 2 * m), lambda b: (0, 0)),
                  pl.BlockSpec((1, m, B_WIDTH), lambda b: (b, 0, 0)),
                  pl.BlockSpec((1, m, B_WIDTH), lambda b: (b, 0, 0))],
        out_specs=pl.BlockSpec((1, m, B_WIDTH), lambda b: (b, 0, 0)),
        out_shape=jax.ShapeDtypeStruct((bsz, m, B_WIDTH), BF16),
        compiler_params=_cparams(("parallel",)),
    )(mat, re3, im3)
    return y.reshape(bsz * m, B_WIDTH)


ATTN_SUBBLOCKS = 4


def _attn_kernel(sink_ref, *refs, local, nb):
    if local:
        q_ref, kp_ref, kc_ref, kn_ref, vp_ref, vc_ref, vn_ref, kx_ref, vx_ref, o_ref = refs
    else:
        q_ref, kx_ref, vx_ref, o_ref = refs
    tq = BLOCK if local else q_ref.shape[1]
    nsb = q_ref.shape[1] // tq
    lo = lax.broadcasted_iota(jnp.int32, (tq, LANES), 1) < HEAD_DIM
    zero = jnp.zeros((tq, LANES), BF16)
    if local:
        j = pl.program_id(1)
        row = lax.broadcasted_iota(jnp.int32, (BLOCK, BLOCK), 0)
        col = lax.broadcasted_iota(jnp.int32, (BLOCK, BLOCK), 1)

        def key_blocks(p_ref, c_ref, n_ref, gl):
            return ([p_ref[0][:, gl]] + [c_ref[0][sb * BLOCK:(sb + 1) * BLOCK, gl] for sb in range(nsb)]
                    + [n_ref[0][:, gl]])
    jobs = []
    for g in range(2):
        gl = slice(g * LANES, (g + 1) * LANES)
        if local:
            kb = key_blocks(kp_ref, kc_ref, kn_ref, gl)
            vb = key_blocks(vp_ref, vc_ref, vn_ref, gl)
        for sb in range(nsb):
            if local:
                kg = jnp.concatenate(kb[sb:sb + 3] + [kx_ref[0][:, gl]], axis=0)
                vg = jnp.concatenate(vb[sb:sb + 3] + [vx_ref[0][:, gl]], axis=0)
            else:
                kg = kx_ref[0][:, gl]
                vg = vx_ref[0][:, gl]
            q = q_ref[0, sb * tq:(sb + 1) * tq, :]
            qs = []
            for p in range(2):
                qp = q[:, g * 2 * LANES + p * LANES: g * 2 * LANES + (p + 1) * LANES]
                qs.append(jnp.where(lo, qp, zero))
                qs.append(jnp.where(lo, zero, qp))
            q4 = jnp.concatenate(qs, axis=0)
            s4 = lax.dot_general(q4, kg, (((1,), (1,)), ((), ())), preferred_element_type=F32)
            jobs.append((g, sb, s4, vg))
    for g, sb, s4, vg in jobs:
        if local:
            valid_prev = (col >= row) & (j > 0) if sb == 0 else (col >= row)
            valid_next = (col <= row) & (j < nb // nsb - 1) if sb == nsb - 1 else (col <= row)
        ps, rden = [], []
        for hl in range(4):
            s = s4[hl * tq:(hl + 1) * tq]
            if local:
                s = jnp.concatenate(
                    [jnp.where(valid_prev, s[:, :BLOCK], NEG), s[:, BLOCK:2 * BLOCK],
                     jnp.where(valid_next, s[:, 2 * BLOCK:3 * BLOCK], NEG), s[:, 3 * BLOCK:]], axis=1)
            sk = sink_ref[g * 4 + hl] * LOG2E
            m = jnp.maximum(jnp.max(s, axis=-1, keepdims=True), sk)
            p = jnp.exp2(s - m)
            den = jnp.sum(p, axis=-1, keepdims=True) + jnp.exp2(sk - m)
            ps.append(p.astype(BF16))
            rden.append(1.0 / den)
        o4 = _dot(jnp.concatenate(ps, axis=0), vg) * jnp.concatenate(rden, axis=0)
        for p in range(2):
            o_pair = jnp.where(lo, o4[(2 * p) * tq:(2 * p + 1) * tq], o4[(2 * p + 1) * tq:(2 * p + 2) * tq])
            c0 = g * 2 * LANES + p * LANES
            o_ref[0, sb * tq:(sb + 1) * tq, c0:c0 + LANES] = o_pair.astype(BF16)


def _window_attention(q, k, v, kx, vx, sink, bsz, n, m):
    nb = n // BLOCK
    q3 = q.reshape(bsz, n, C_WIDTH)
    k3 = k.reshape(bsz, n, 2 * KV_WIDTH)
    v3 = v.reshape(bsz, n, 2 * KV_WIDTH)
    kx3 = kx.reshape(bsz, m, 2 * KV_WIDTH)
    vx3 = vx.reshape(bsz, m, 2 * KV_WIDTH)
    nsb = ATTN_SUBBLOCKS
    edge = lambda f: pl.BlockSpec((1, BLOCK, 2 * KV_WIDTH), f)
    own = pl.BlockSpec((1, nsb * BLOCK, 2 * KV_WIDTH), lambda b, i, s: (b, i, 0))
    prev = lambda b, i, s: (b, jnp.maximum(i * nsb - 1, 0), 0)
    cur = lambda b, i, s: (b, i, 0)
    nxt = lambda b, i, s: (b, jnp.minimum((i + 1) * nsb, nb - 1), 0)
    ctxs = pl.BlockSpec((1, m, 2 * KV_WIDTH), lambda b, i, s: (b, 0, 0))
    o = pl.pallas_call(
        functools.partial(_attn_kernel, local=True, nb=nb),
        grid_spec=pltpu.PrefetchScalarGridSpec(
            num_scalar_prefetch=1,
            grid=(bsz, nb // nsb),
            in_specs=[pl.BlockSpec((1, nsb * BLOCK, C_WIDTH), cur),
                      edge(prev), own, edge(nxt), edge(prev), own, edge(nxt), ctxs, ctxs],
            out_specs=pl.BlockSpec((1, nsb * BLOCK, C_WIDTH), cur)),
        out_shape=jax.ShapeDtypeStruct((bsz, n, C_WIDTH), BF16),
        compiler_params=_cparams(("parallel", "parallel")),
    )(sink, q3, k3, k3, k3, v3, v3, v3, kx3, vx3)
    return o.reshape(bsz * n, C_WIDTH)


def _context_attention(q, kx, vx, sink, bsz, m):
    q3 = q.reshape(bsz, m, C_WIDTH)
    kx3 = kx.reshape(bsz, m, 2 * KV_WIDTH)
    vx3 = vx.reshape(bsz, m, 2 * KV_WIDTH)
    ctxs = pl.BlockSpec((1, m, 2 * KV_WIDTH), lambda b, s: (b, 0, 0))
    o = pl.pallas_call(
        functools.partial(_attn_kernel, local=False, nb=1),
        grid_spec=pltpu.PrefetchScalarGridSpec(
            num_scalar_prefetch=1,
            grid=(bsz,),
            in_specs=[pl.BlockSpec((1, m, C_WIDTH), lambda b, s: (b, 0, 0)), ctxs, ctxs],
            out_specs=pl.BlockSpec((1, m, C_WIDTH), lambda b, s: (b, 0, 0))),
        out_shape=jax.ShapeDtypeStruct((bsz, m, C_WIDTH), BF16),
        compiler_params=_cparams(("parallel",)),
    )(sink, q3, kx3, vx3)
    return o.reshape(bsz * m, C_WIDTH)


def _postmix_kernel(*refs, route):
    h_ref = refs[5]
    for sb in range(h_ref.shape[0] // MOE_TB):
        _postmix_rows(slice(sb * MOE_TB, (sb + 1) * MOE_TB), sb, refs, route)


def _postmix_rows(rows, sb, refs, route):
    (sgu_ref, yf_ref, at_ref, wf_ref, wo_ref, h_ref, gp_ref, gt_ref, gf_ref, sh_ref, sc_ref) = refs[:11]
    if route:
        wr_ref, br_ref, hn_ref, y_ref, cmb_ref, cnt_ref = refs[11:]
    else:
        hn_ref, y_ref = refs[11:]
    fm = _dot(yf_ref[rows, :], wf_ref[...]).astype(BF16)
    o = (_dot(sgu_ref[rows, :], wo_ref[:A_WIDTH, :])
         + _dot(fm, wo_ref[A_WIDTH:A_WIDTH + B_WIDTH, :])
         + _dot(at_ref[rows, :], wo_ref[A_WIDTH + B_WIDTH:, :]))
    ms = jnp.mean(o * o, axis=-1, keepdims=True)
    hn = h_ref[rows, :] + gt_ref[...] * (o * lax.rsqrt(ms + EPS) * gp_ref[...])
    hn_ref[rows, :] = hn
    ms2 = jnp.mean(hn * hn, axis=-1, keepdims=True)
    y = hn * lax.rsqrt(ms2 + EPS) * gf_ref[...] * (1.0 + sc_ref[...]) + sh_ref[...]
    y_ref[rows, :] = y.astype(BF16)
    if route:
        yh = y.astype(BF16)
        yl = (y - yh.astype(F32)).astype(BF16)
        w = wr_ref[...]
        wh = w.astype(BF16)
        wl = (w - wh.astype(F32)).astype(BF16)
        lg = _dot(yh, wh) + _dot(yl, wh) + _dot(yh, wl) + br_ref[...]
        lane = lax.broadcasted_iota(jnp.int32, lg.shape, 1)
        lg = jnp.where(lane < N_EXPERTS, lg, NEG)
        m1 = jnp.max(lg, axis=-1, keepdims=True)
        i1 = jnp.min(jnp.where(lg == m1, lane, LANES), axis=-1, keepdims=True)
        lg2 = jnp.where(lane == i1, NEG, lg)
        m2 = jnp.max(lg2, axis=-1, keepdims=True)
        i2 = jnp.min(jnp.where(lg2 == m2, lane, LANES), axis=-1, keepdims=True)
        e2 = jnp.exp(m2 - m1)
        g1 = 1.0 / (1.0 + e2)
        g2 = e2 * g1
        cmb = jnp.where(lane == i1, g1, 0.0) + jnp.where(lane == i2, g2, 0.0)
        cmb_ref[rows, :] = cmb
        cnt_ref[sb] = jnp.sum(jnp.where(cmb > 0.0, 1.0, 0.0), axis=0, keepdims=True)


def _postmix(sgu, yf, at, wf, wo, h, mods, gp, gf, router, *, tm, tiles_per_batch):
    T = h.shape[0]
    if tiles_per_batch is None:
        mrow = lambda i: 4
    else:
        mrow = lambda i: i // tiles_per_batch
    const2 = lambda i: (0, 0)
    row = lambda w: pl.BlockSpec((tm, w), lambda i: (i, 0))
    mod = lambda j: pl.BlockSpec((None, None, 1, D_MODEL), lambda i: (mrow(i), j, 0, 0))
    in_specs = [row(A_WIDTH), row(B_WIDTH), row(C_WIDTH),
                pl.BlockSpec((B_WIDTH, B_WIDTH), const2),
                pl.BlockSpec((D_MODEL, D_MODEL), const2),
                row(D_MODEL),
                pl.BlockSpec((1, D_MODEL), const2),
                mod(2),
                pl.BlockSpec((1, D_MODEL), const2),
                mod(3), mod(4)]
    args = [sgu, yf, at, wf, wo, h, gp, mods, gf, mods, mods]
    out_specs = [row(D_MODEL), row(D_MODEL)]
    out_shape = [jax.ShapeDtypeStruct((T, D_MODEL), F32), jax.ShapeDtypeStruct((T, D_MODEL), BF16)]
    route = router is not None
    if route:
        in_specs += [pl.BlockSpec((D_MODEL, LANES), const2), pl.BlockSpec((1, LANES), const2)]
        args += list(router)
        out_specs += [row(LANES), pl.BlockSpec((tm // MOE_TB, 1, LANES), lambda i: (i, 0, 0))]
        out_shape += [jax.ShapeDtypeStruct((T, LANES), F32), jax.ShapeDtypeStruct((T // MOE_TB, 1, LANES), F32)]
    return pl.pallas_call(
        functools.partial(_postmix_kernel, route=route),
        grid=(T // tm,),
        in_specs=in_specs, out_specs=out_specs, out_shape=out_shape,
        compiler_params=_cparams(("parallel",)),
    )(*args)


def _ffn_epilogue(f, h_ref, gp_ref, gt_ref, o_ref):
    ms = jnp.mean(f * f, axis=-1, keepdims=True)
    o_ref[...] = h_ref[...] + gt_ref[...] * (f * lax.rsqrt(ms + EPS) * gp_ref[...])


FF_CHUNK = 512


def _swiglu_tile(y, wg_ref, wu_ref, wd_ref, act_ref):
    for c in range(D_FF // FF_CHUNK):
        cols = slice(c * FF_CHUNK, (c + 1) * FF_CHUNK)
        gate = _dot(y, wg_ref[:, cols])
        up = _dot(y, wu_ref[:, cols])
        act_ref[:, cols] = (gate * jax.nn.sigmoid(gate) * up).astype(BF16)
    return _dot(act_ref[...], wd_ref[...])


def _ffn_dense_kernel(y_ref, wg_ref, wu_ref, wd_ref, h_ref, gp_ref, gt_ref, o_ref, act_ref):
    f = _swiglu_tile(y_ref[...], wg_ref, wu_ref, wd_ref, act_ref)
    _ffn_epilogue(f, h_ref, gp_ref, gt_ref, o_ref)


def _ffn_dense(y, wg, wu, wd, h, mods, gp, *, tm, tiles_per_batch):
    T = h.shape[0]
    if tiles_per_batch is None:
        mrow = lambda i: 4
    else:
        mrow = lambda i: i // tiles_per_batch
    resident = pl.Buffered(1)
    return pl.pallas_call(
        _ffn_dense_kernel,
        grid=(T // tm,),
        in_specs=[pl.BlockSpec((tm, D_MODEL), lambda i: (i, 0)),
                  pl.BlockSpec((D_MODEL, D_FF), lambda i: (0, 0), pipeline_mode=resident),
                  pl.BlockSpec((D_MODEL, D_FF), lambda i: (0, 0), pipeline_mode=resident),
                  pl.BlockSpec((D_FF, D_MODEL), lambda i: (0, 0), pipeline_mode=resident),
                  pl.BlockSpec((tm, D_MODEL), lambda i: (i, 0)),
                  pl.BlockSpec((1, D_MODEL), lambda i: (0, 0)),
                  pl.BlockSpec((None, None, 1, D_MODEL), lambda i: (mrow(i), 5, 0, 0))],
        out_specs=pl.BlockSpec((tm, D_MODEL), lambda i: (i, 0)),
        out_shape=jax.ShapeDtypeStruct((T, D_MODEL), F32),
        scratch_shapes=[pltpu.VMEM((tm, D_FF), BF16)],
        compiler_params=_cparams(("parallel",)),
    )(y, wg, wu, wd, h, gp, mods)


MOE_TB = 512
MOE_TM = 512
SUB = 128
NSUB = MOE_TB // SUB
RUN_ALIGN = 16


def _moe_tile_bound(T):
    nt = T // MOE_TB
    rows = 2 * T + nt * N_EXPERTS * (RUN_ALIGN - 1) + N_EXPERTS * (SUB + MOE_TM - 1)
    return -(-rows // MOE_TM)


def _moe_plan(cnt, T):
    p = (cnt + RUN_ALIGN - 1) // RUN_ALIGN * RUN_ALIGN
    base = jnp.cumsum(p, axis=0) - p
    used = jnp.sum(p, axis=0)
    tiles = (used + SUB + MOE_TM - 1) // MOE_TM
    tend = jnp.cumsum(tiles)
    off = (tend - tiles) * MOE_TM
    rowbase = (off[None, :] + base).astype(jnp.int32)
    nsub = (cnt + SUB - 1) // SUB
    slotbase = (jnp.cumsum(nsub, axis=1) - nsub).astype(jnp.int32)
    nt_bound = _moe_tile_bound(T)
    j = jnp.arange(nt_bound, dtype=jnp.int32)
    tile_e = jnp.minimum(jnp.sum(j[:, None] >= tend[None, :], axis=1), N_EXPERTS - 1).astype(jnp.int32)
    nvalid = tend[-1].astype(jnp.int32)
    tile_blk = jnp.minimum(j, nvalid - 1)
    return rowbase, slotbase, tile_e, tile_blk, nvalid.reshape(1)


NSLOT = 2 * MOE_TB // SUB + N_EXPERTS
SLOT_ROWS = NSLOT * SUB


def _for_each_run_slot(cn_ref, b, fn):
    for e in range(N_EXPERTS):
        for s in range(NSUB):
            @pl.when(cn_ref[b * N_EXPERTS + e] > SUB * s)
            def _(e=e, s=s):
                fn(e, s)


def _run_copy_out(stage, ys_ref, sems, rb_ref, sb_ref, b, par, e, s):
    slot = sb_ref[b * N_EXPERTS + e] + s
    r0 = pl.multiple_of(rb_ref[b * N_EXPERTS + e] + SUB * s, RUN_ALIGN)
    return pltpu.make_async_copy(stage.at[par, pl.ds(pl.multiple_of(slot * SUB, SUB), SUB), :],
                                 ys_ref.at[pl.ds(r0, SUB), :], sems.at[par, e, s])


def _dispatch_kernel(rb_ref, sb_ref, cn_ref, y_ref, cmb_ref, sbv_ref, ltri_ref, ysin_ref,
                     tok_ref, ys_ref, stage, sems):
    del ysin_ref
    b = pl.program_id(0)
    nb = pl.num_programs(0)
    par = b % 2
    cmb = cmb_ref[...]
    sel = cmb > 0.0
    rk = _dot(ltri_ref[...], jnp.where(sel, 1.0, 0.0).astype(BF16))
    srow = jnp.where(sel, rk + sbv_ref[...], -1.0)
    nsel = jnp.sum(jnp.where(sel, 1.0, 0.0), axis=-1, keepdims=True)
    sa = jnp.max(srow, axis=-1, keepdims=True)
    sb = jnp.where(nsel > 1.5, jnp.sum(jnp.where(sel, srow, 0.0), axis=-1, keepdims=True) - sa, -1.0)
    ga = jnp.sum(jnp.where(srow == sa, cmb, 0.0), axis=-1, keepdims=True)
    gb = jnp.sum(cmb, axis=-1, keepdims=True) - ga
    lane = lax.broadcasted_iota(jnp.int32, cmb.shape, 1)
    tok = jnp.where(lane == 0, sa, jnp.where(lane == 1, sb, jnp.where(lane == 2, ga, jnp.where(lane == 3, gb, 0.0))))
    tok_ref[...] = tok
    tok_t = tok.T
    rr = lax.broadcasted_iota(jnp.int32, (SLOT_ROWS, cmb.shape[0]), 0).astype(F32)
    g = jnp.where((rr == tok_t[0:1, :]) | (rr == tok_t[1:2, :]), 1.0, 0.0).astype(BF16)
    stage[par] = _dot(g, y_ref[...]).astype(BF16)

    @pl.when(b > 0)
    def _():
        _for_each_run_slot(cn_ref, b - 1, lambda e, s: _run_copy_out(
            stage, ys_ref, sems, rb_ref, sb_ref, b - 1, 1 - par, e, s).wait())

    _for_each_run_slot(cn_ref, b, lambda e, s: _run_copy_out(
        stage, ys_ref, sems, rb_ref, sb_ref, b, par, e, s).start())

    @pl.when(b == nb - 1)
    def _():
        _for_each_run_slot(cn_ref, b, lambda e, s: _run_copy_out(
            stage, ys_ref, sems, rb_ref, sb_ref, b, par, e, s).wait())


def _moe_dispatch(y, cmb, rowbase, slotbase, cnt, nt_bound):
    T = y.shape[0]
    nt = T // MOE_TB
    sbv = jnp.zeros((nt, 1, LANES), F32).at[:, 0, :N_EXPERTS].set((slotbase * SUB).astype(F32))
    ltri = jnp.asarray(np.tril(np.ones((MOE_TB, MOE_TB), np.float32), -1)).astype(BF16)
    ys0 = jnp.zeros((nt_bound * MOE_TM, D_MODEL), BF16)
    imap2 = lambda b, rb, sb, cn: (b, 0)
    return pl.pallas_call(
        _dispatch_kernel,
        grid_spec=pltpu.PrefetchScalarGridSpec(
            num_scalar_prefetch=3,
            grid=(nt,),
            in_specs=[pl.BlockSpec((MOE_TB, D_MODEL), imap2),
                      pl.BlockSpec((MOE_TB, LANES), imap2),
                      pl.BlockSpec((None, 1, LANES), lambda b, rb, sb, cn: (b, 0, 0)),
                      pl.BlockSpec((MOE_TB, MOE_TB), lambda b, rb, sb, cn: (0, 0)),
                      pl.BlockSpec(memory_space=pl.ANY)],
            out_specs=[pl.BlockSpec((MOE_TB, LANES), imap2),
                       pl.BlockSpec(memory_space=pl.ANY)],
            scratch_shapes=[pltpu.VMEM((2, SLOT_ROWS, D_MODEL), BF16),
                            pltpu.SemaphoreType.DMA((2, N_EXPERTS, NSUB))]),
        out_shape=[jax.ShapeDtypeStruct((T, LANES), F32),
                   jax.ShapeDtypeStruct((nt_bound * MOE_TM, D_MODEL), BF16)],
        input_output_aliases={7: 1},
        compiler_params=_cparams(("arbitrary",)),
    )(rowbase.reshape(-1), slotbase.reshape(-1), cnt.reshape(-1), y, cmb, sbv, ltri, ys0)


def _ffn_group_kernel(te_ref, tb_ref, nv_ref, y_ref, wg_ref, wu_ref, wd_ref, o_ref, act_ref):
    del te_ref, tb_ref
    j = pl.program_id(0)

    @pl.when(j < nv_ref[0])
    def _():
        o_ref[...] = _swiglu_tile(y_ref[...], wg_ref, wu_ref, wd_ref, act_ref).astype(BF16)

    @pl.when(j >= nv_ref[0])
    def _():
        o_ref[...] = jnp.zeros_like(o_ref)


def _ffn_group(ys, wg, wu, wd, tile_e, tile_blk, nvalid):
    rows = ys.shape[0]
    resident = pl.Buffered(1)
    return pl.pallas_call(
        _ffn_group_kernel,
        grid_spec=pltpu.PrefetchScalarGridSpec(
            num_scalar_prefetch=3,
            grid=(rows // MOE_TM,),
            in_specs=[pl.BlockSpec((MOE_TM, D_MODEL), lambda j, te, tb, nv: (tb[j], 0)),
                      pl.BlockSpec((None, D_MODEL, D_FF), lambda j, te, tb, nv: (te[j], 0, 0), pipeline_mode=resident),
                      pl.BlockSpec((None, D_MODEL, D_FF), lambda j, te, tb, nv: (te[j], 0, 0), pipeline_mode=resident),
                      pl.BlockSpec((None, D_FF, D_MODEL), lambda j, te, tb, nv: (te[j], 0, 0), pipeline_mode=resident)],
            out_specs=pl.BlockSpec((MOE_TM, D_MODEL), lambda j, te, tb, nv: (j, 0)),
            scratch_shapes=[pltpu.VMEM((MOE_TM, D_FF), BF16)]),
        out_shape=jax.ShapeDtypeStruct((rows, D_MODEL), BF16),
        compiler_params=_cparams(("arbitrary",)),
    )(tile_e, tile_blk, nvalid, ys, wg, wu, wd)


def _run_copy_in(fs_ref, fbuf, sems, rb_ref, sb_ref, b, par, e, s):
    slot = sb_ref[b * N_EXPERTS + e] + s
    r0 = pl.multiple_of(rb_ref[b * N_EXPERTS + e] + SUB * s, RUN_ALIGN)
    return pltpu.make_async_copy(fs_ref.at[pl.ds(r0, SUB), :],
                                 fbuf.at[par, pl.ds(pl.multiple_of(slot * SUB, SUB), SUB), :], sems.at[par, e, s])


def _combine_kernel(rb_ref, sb_ref, cn_ref, tok_ref, h_ref, gp_ref, gt_ref, fs_ref, o_ref, fbuf, sems):
    b = pl.program_id(0)
    nb = pl.num_programs(0)
    par = b % 2

    @pl.when(b == 0)
    def _():
        fbuf[...] = jnp.zeros_like(fbuf)
        _for_each_run_slot(cn_ref, b, lambda e, s: _run_copy_in(
            fs_ref, fbuf, sems, rb_ref, sb_ref, b, par, e, s).start())

    @pl.when(b + 1 < nb)
    def _():
        _for_each_run_slot(cn_ref, b + 1, lambda e, s: _run_copy_in(
            fs_ref, fbuf, sems, rb_ref, sb_ref, b + 1, 1 - par, e, s).start())

    tok = tok_ref[...]
    sa, sb, ga, gb = tok[:, 0:1], tok[:, 1:2], tok[:, 2:3], tok[:, 3:4]
    cc = lax.broadcasted_iota(jnp.int32, (tok.shape[0], SLOT_ROWS), 1).astype(F32)
    pick = jnp.where(cc == sa, ga, jnp.where(cc == sb, gb, 0.0)).astype(BF16)
    _for_each_run_slot(cn_ref, b, lambda e, s: _run_copy_in(
        fs_ref, fbuf, sems, rb_ref, sb_ref, b, par, e, s).wait())
    f = _dot(pick, fbuf[par])
    _ffn_epilogue(f, h_ref, gp_ref, gt_ref, o_ref)


def _moe_combine(fs, tok, h, mods, gp, rowbase, slotbase, cnt, *, tiles_per_batch):
    T = h.shape[0]
    nt = T // MOE_TB
    imap2 = lambda b, rb, sb, cn: (b, 0)
    return pl.pallas_call(
        _combine_kernel,
        grid_spec=pltpu.PrefetchScalarGridSpec(
            num_scalar_prefetch=3,
            grid=(nt,),
            in_specs=[pl.BlockSpec((MOE_TB, LANES), imap2),
                      pl.BlockSpec((MOE_TB, D_MODEL), imap2),
                      pl.BlockSpec((1, D_MODEL), lambda b, rb, sb, cn: (0, 0)),
                      pl.BlockSpec((None, None, 1, D_MODEL), lambda b, rb, sb, cn: (b // tiles_per_batch, 5, 0, 0)),
                      pl.BlockSpec(memory_space=pl.ANY)],
            out_specs=pl.BlockSpec((MOE_TB, D_MODEL), imap2),
            scratch_shapes=[pltpu.VMEM((2, SLOT_ROWS, D_MODEL), BF16),
                            pltpu.SemaphoreType.DMA((2, N_EXPERTS, NSUB))]),
        out_shape=jax.ShapeDtypeStruct((T, D_MODEL), F32),
        compiler_params=_cparams(("arbitrary",)),
    )(rowbase.reshape(-1), slotbase.reshape(-1), cnt.reshape(-1), tok, h, gp, mods, fs)


def _ffn_moe(y, cmb, cnt_tiles, wg, wu, wd, h, mods, gp, *, tiles_per_batch):
    T = h.shape[0]
    cnt = cnt_tiles.reshape(T // MOE_TB, LANES)[:, :N_EXPERTS].astype(jnp.int32)
    rowbase, slotbase, tile_e, tile_blk, nvalid = _moe_plan(cnt, T)
    tok, ys = _moe_dispatch(y, cmb, rowbase, slotbase, cnt, _moe_tile_bound(T))
    fs = _ffn_group(ys, wg, wu, wd, tile_e, tile_blk, nvalid)
    return _moe_combine(fs, tok, h, mods, gp, rowbase, slotbase, cnt, tiles_per_batch=tiles_per_batch)


def _blockdiag(m, reps):
    n = m.shape[0]
    out = np.zeros((n * reps, n * reps), np.float64)
    for r in range(reps):
        out[r * n:(r + 1) * n, r * n:(r + 1) * n] = m
    return out


def _dft_tables(n_rows):
    n = n_rows * GRID_W
    r = np.arange(n_rows)
    c = np.arange(GRID_W)
    a1 = 2 * np.pi * np.outer(r, r) / n_rows
    c1, s1 = np.cos(a1), np.sin(a1)
    m1 = np.block([[c1, s1], [-s1, c1]]) * 0.125
    at = 2 * np.pi * np.outer(r, c) / n
    twc = np.repeat(np.cos(at), B_WIDTH, axis=1)
    tws = np.repeat(np.sin(at), B_WIDTH, axis=1)
    a3 = 2 * np.pi * np.outer(c, c) / GRID_W
    m3 = np.concatenate([np.cos(a3), np.sin(a3)], axis=1) * (8.0 / np.sqrt(n))
    f32 = lambda t: jnp.asarray(t.astype(np.float32))
    return f32(m1).astype(BF16), f32(twc), f32(tws), f32(m3).astype(BF16)


def _channel_dft_table():
    d = np.arange(HEAD_DIM)
    a = 2 * np.pi * np.outer(d, d) / HEAD_DIM
    w = np.concatenate([_blockdiag(np.cos(a), 4), -_blockdiag(np.sin(a), 4)], axis=1) * 0.125
    return jnp.asarray(w.astype(np.float32)).astype(BF16)


def _ctx_dft_table(m):
    p = np.arange(m)
    a = 2 * np.pi * np.outer(p, p) / m
    w = np.concatenate([np.cos(a), np.sin(a)], axis=1) * (8.0 / np.sqrt(m * HEAD_DIM))
    return jnp.asarray(w.astype(np.float32)).astype(BF16)


def _rope_tables(n_tok):
    rows = n_tok // GRID_W
    row = jnp.broadcast_to(jnp.arange(rows)[:, None], (rows, GRID_W)).reshape(-1)
    col = jnp.broadcast_to(jnp.arange(GRID_W)[None, :], (rows, GRID_W)).reshape(-1)
    half = HEAD_DIM // 2
    inv = ROPE_BASE ** (-jnp.arange(0, half, 2, dtype=F32) / half)
    ang = jnp.stack([row.astype(F32)[:, None] * inv, col.astype(F32)[:, None] * inv], axis=1)
    cos, sin = jnp.cos(ang), jnp.sin(ang)
    zer = jnp.zeros_like(sin)
    lay = lambda a, b: jnp.tile(jnp.stack([a, b], axis=2).reshape(n_tok, HEAD_DIM), (1, LANES // HEAD_DIM))
    return lay(cos, cos), lay(-sin, zer), lay(zer, sin)


def kernel(x, c, ctx, c_ctx, w_ada, b_ada, g_mix_pre, g_mix_post, g_ffn_pre, g_ffn_post,
           w_in, w_s, b_s, g_v, w_f, sink, w_out, w_gate_d, w_up_d, w_down_d,
           w_router, b_router, w_gate_e, w_up_e, w_down_e):
    bsz, n_lat, _ = x.shape
    n_ctx = ctx.shape[1]
    T, Tc = bsz * n_lat, bsz * n_ctx
    tm = 512
    tpb = n_lat // tm

    rope_tabs = _rope_tables(n_lat)
    m1, twc, tws, m3 = _dft_tables(n_lat // GRID_W)
    wdft = _channel_dft_table()
    mctx = _ctx_dft_table(n_ctx)
    hm = jnp.asarray(_blockdiag(np.full((HEAD_DIM, HEAD_DIM), 1.0 / HEAD_DIM), A_HEADS).astype(np.float32)).astype(BF16)

    cond8 = jnp.zeros((8, D_MODEL), F32).at[:bsz].set(c).at[4].set(c_ctx)
    h = x.reshape(T, D_MODEL)
    hc = ctx.reshape(Tc, D_MODEL)

    mods_all = _adaln(cond8, w_ada, b_ada[:, None, :]).reshape(DEPTH, 8, 6, 1, D_MODEL)

    for i in range(DEPTH):
        last = i == DEPTH - 1
        mods = mods_all[i]
        win = w_in[i].astype(BF16)
        ws = w_s[i].astype(BF16)
        bsx = jnp.repeat(b_s[i].T, HEAD_DIM, axis=1)
        gv = g_v[i].reshape(1, A_WIDTH)
        wf = jax.scipy.linalg.block_diag(*[w_f[i][g] for g in range(4)]).astype(BF16)
        wo = w_out[i].astype(BF16)
        gpre = g_mix_pre[i][None, :]
        gpost = g_mix_post[i][None, :]
        gfpre = g_ffn_pre[i][None, :]
        gfpost = g_ffn_post[i][None, :]
        sk = sink[i]

        sgu, fre, fim, q, k, v = _premix(h, mods, gpre, win, ws, bsx, gv, hm, wdft, rope_tabs,
                                         tm=2 * tm, tiles_per_batch=tpb // 2, n_pos=n_lat)
        sguc, frec, fimc, qc, kc, vc = _premix(hc, mods, gpre, win, ws, bsx, gv, hm, wdft, None,
                                               tm=tm, tiles_per_batch=None, n_pos=n_ctx)
        yf = _seq_dft(fre, fim, bsz, n_lat // GRID_W, m1, twc, tws, m3)
        at = _window_attention(q, k, v, kc, vc, sk, bsz, n_lat, n_ctx)

        if i % 2 == 0:
            router = None
        else:
            j = i // 2
            wr = jnp.zeros((D_MODEL, LANES), F32).at[:, :N_EXPERTS].set(w_router[j])
            br = jnp.zeros((1, LANES), F32).at[0, :N_EXPERTS].set(b_router[j])
            router = (wr, br)
        res = _postmix(sgu, yf, at, wf, wo, h, mods, gpost, gfpre, router, tm=2 * tm, tiles_per_batch=tpb // 2)
        if not last:
            yfc = _ctx_dft(frec, fimc, bsz, n_ctx, mctx)
            atc = _context_attention(qc, kc, vc, sk, bsz, n_ctx)
            resc = _postmix(sguc, yfc, atc, wf, wo, hc, mods, gpost, gfpre, router, tm=2 * tm, tiles_per_batch=None)

        j = i // 2
        if i % 2 == 0:
            wg, wu, wd = w_gate_d[j].astype(BF16), w_up_d[j].astype(BF16), w_down_d[j].astype(BF16)
            h = _ffn_dense(res[1], wg, wu, wd, res[0], mods, gfpost, tm=tm, tiles_per_batch=tpb)
            if not last:
                hc = _ffn_dense(resc[1], wg, wu, wd, resc[0], mods, gfpost, tm=tm, tiles_per_batch=None)
        else:
            wg, wu, wd = w_gate_e[j].astype(BF16), w_up_e[j].astype(BF16), w_down_e[j].astype(BF16)
            assert last and tm == MOE_TB, "the expert FFN is only built for the final layer's latent tokens"
            h = _ffn_moe(res[1], res[2], res[3], wg, wu, wd, res[0], mods, gfpost,
                         tiles_per_batch=n_lat // MOE_TB)
    return h.reshape(bsz, n_lat, D_MODEL)
```

```python
import functools

import numpy as np
import jax
import jax.numpy as jnp
from jax import lax
from jax.experimental import pallas as pl
from jax.experimental.pallas import tpu as pltpu

D_MODEL = 1024
DEPTH = 2
GRID_W = 64
HEAD_DIM = 64
EPS = 1e-6
A_HEADS = 4
A_WIDTH = 256
CHUNK = 128
B_WIDTH = 256
C_Q_HEADS = 8
C_WIDTH = 512
KV_WIDTH = 128
WINDOW = 128
BLOCK = 128
ROPE_BASE = 10000.0
OFF_B = 512
OFF_Q = 768
OFF_K = 1280
OFF_V = 1408
N_IN = 1536
D_FF = 3584
N_EXPERTS = 8

LANES = 128
VMEM_LIMIT = 56 * 1024 * 1024
NEG = -1e30
LOG2E = 1.4426950408889634
assert WINDOW == BLOCK

F32 = jnp.float32
BF16 = jnp.bfloat16


def _dot(a, b):
    return jnp.dot(a, b, preferred_element_type=F32)


def _cparams(sem):
    return pltpu.CompilerParams(dimension_semantics=sem, vmem_limit_bytes=VMEM_LIMIT)


def _adaln_kernel(c_ref, w_ref, b_ref, o_ref):
    c = c_ref[...]
    s = (c * jax.nn.sigmoid(c)).astype(BF16)
    o_ref[...] = _dot(s, w_ref[...].astype(BF16)) + b_ref[...]


def _adaln(cond8, w, b):
    tn = 1536
    nl = w.shape[0]
    return pl.pallas_call(
        _adaln_kernel,
        grid=(nl, 6 * D_MODEL // tn),
        in_specs=[pl.BlockSpec((8, D_MODEL), lambda l, j: (0, 0)),
                  pl.BlockSpec((None, D_MODEL, tn), lambda l, j: (l, 0, j)),
                  pl.BlockSpec((None, 1, tn), lambda l, j: (l, 0, j))],
        out_specs=pl.BlockSpec((None, 8, tn), lambda l, j: (l, 0, j)),
        out_shape=jax.ShapeDtypeStruct((nl, 8, 6 * D_MODEL), F32),
        compiler_params=_cparams(("arbitrary", "arbitrary")),
    )(cond8, w, b)


def _gelu_tanh(x):
    return 0.5 * x * (1.0 + jnp.tanh(0.7978845608028654 * (x + 0.044715 * x * x * x)))


PRE_SUB = 512


def _premix_kernel(*refs, rope):
    if rope:
        (h_ref, sh_ref, sc_ref, g_ref, win_ref, ws_ref, bs_ref, gv_ref, hm_ref, wdft_ref,
         cos_ref, s1_ref, s2_ref, sgu_ref, re_ref, im_ref, q_ref, k_ref, v_ref, fsc_ref) = refs
    else:
        (h_ref, sh_ref, sc_ref, g_ref, win_ref, ws_ref, bs_ref, gv_ref, hm_ref, wdft_ref,
         sgu_ref, re_ref, im_ref, q_ref, k_ref, v_ref) = refs
    tm = h_ref.shape[0]
    nl = 2 * B_WIDTH // LANES
    head = lax.broadcasted_iota(jnp.int32, (CHUNK, A_WIDTH), 1) // HEAD_DIM
    lo = lax.broadcasted_iota(jnp.int32, (PRE_SUB, LANES), 1) < HEAD_DIM
    for sb in range(tm // PRE_SUB):
        rs = slice(sb * PRE_SUB, (sb + 1) * PRE_SUB)
        x = h_ref[rs, :]
        ms = jnp.mean(x * x, axis=-1, keepdims=True)
        xn = x * lax.rsqrt(ms + EPS) * g_ref[...]
        xm = (xn * (1.0 + sc_ref[...]) + sh_ref[...]).astype(BF16)
        z = _dot(xm, win_ref[...])

        a = _gelu_tanh(z[:, :OFF_B])
        u = a[:, :A_WIDTH]
        v = a[:, A_WIDTH:]
        msv = _dot((v * v).astype(BF16), hm_ref[...])
        vn = (v * lax.rsqrt(msv + EPS) * gv_ref[...]).astype(BF16)
        for ck in range(PRE_SUB // CHUNK):
            rows = slice(ck * CHUNK, (ck + 1) * CHUNK)
            vc = vn[rows]
            sv = bs_ref[...]
            for hh in range(A_HEADS):
                sv = sv + jnp.where(head == hh, _dot(ws_ref[hh], vc), 0.0)
            sgu_ref[sb * PRE_SUB + ck * CHUNK:sb * PRE_SUB + (ck + 1) * CHUNK, :] = (u[rows] * sv).astype(BF16)

        f = _dot(z[:, OFF_B:OFF_Q].astype(BF16), wdft_ref[...])
        if rope:
            for j in range(nl):
                fsc_ref[j, rs, :] = f[:, j * LANES:(j + 1) * LANES]
        else:
            re_ref[rs, :] = f[:, :B_WIDTH].astype(BF16)
            im_ref[rs, :] = f[:, B_WIDTH:].astype(BF16)

        def rot(t):
            if not rope:
                return t
            return (t * cos_ref[rs, :] + pltpu.roll(t, LANES - 16, 1) * s1_ref[rs, :]
                    + pltpu.roll(t, 16, 1) * s2_ref[rs, :])

        for j in range(C_WIDTH // LANES):
            cols = slice(OFF_Q + j * LANES, OFF_Q + (j + 1) * LANES)
            q_ref[rs, j * LANES:(j + 1) * LANES] = (rot(z[:, cols]) * (HEAD_DIM ** -0.5 * LOG2E)).astype(BF16)

        kk = rot(z[:, OFF_K:OFF_V])
        kr = pltpu.roll(kk, HEAD_DIM, 1)
        k_ref[rs, :LANES] = jnp.where(lo, kk, kr).astype(BF16)
        k_ref[rs, LANES:] = jnp.where(lo, kr, kk).astype(BF16)
        vv = z[:, OFF_V:]
        vr = pltpu.roll(vv, HEAD_DIM, 1)
        v_ref[rs, :LANES] = jnp.where(lo, vv, vr).astype(BF16)
        v_ref[rs, LANES:] = jnp.where(lo, vr, vv).astype(BF16)

    if rope:
        nr = tm // GRID_W
        for cc in range(GRID_W):
            for j in range(nl):
                blk = fsc_ref[j, pl.ds(cc, nr, stride=GRID_W), :].astype(BF16)
                dst = re_ref if j < nl // 2 else im_ref
                c0 = cc * B_WIDTH + (j % (nl // 2)) * LANES
                dst[:, c0:c0 + LANES] = blk


def _premix(h, mods, g, win, ws, bsx, gv, hm, wdft, rope_tabs, *, tm, tiles_per_batch, n_pos):
    T = h.shape[0]
    rope = rope_tabs is not None
    if tiles_per_batch is None:
        mrow = lambda i: 4
    else:
        mrow = lambda i: i // tiles_per_batch
    const2 = lambda i: (0, 0)
    in_specs = [
        pl.BlockSpec((tm, D_MODEL), lambda i: (i, 0)),
        pl.BlockSpec((None, None, 1, D_MODEL), lambda i: (mrow(i), 0, 0, 0)),
        pl.BlockSpec((None, None, 1, D_MODEL), lambda i: (mrow(i), 1, 0, 0)),
        pl.BlockSpec((1, D_MODEL), const2),
        pl.BlockSpec((D_MODEL, N_IN), const2),
        pl.BlockSpec((A_HEADS, CHUNK, CHUNK), lambda i: (0, 0, 0)),
        pl.BlockSpec((CHUNK, A_WIDTH), const2),
        pl.BlockSpec((1, A_WIDTH), const2),
        pl.BlockSpec((A_WIDTH, A_WIDTH), const2),
        pl.BlockSpec((B_WIDTH, 2 * B_WIDTH), const2),
    ]
    args = [h, mods, mods, g, win, ws, bsx, gv, hm, wdft]
    if rope:
        nt = n_pos // tm
        for t in rope_tabs:
            in_specs.append(pl.BlockSpec((tm, LANES), lambda i: (i % nt, 0)))
            args.append(t)
    widths = (A_WIDTH, B_WIDTH, B_WIDTH, C_WIDTH, 2 * KV_WIDTH, 2 * KV_WIDTH)
    out_specs = [pl.BlockSpec((tm, w), lambda i: (i, 0)) for w in widths]
    out_shape = [jax.ShapeDtypeStruct((T, w), BF16) for w in widths]
    scratch = []
    if rope:
        for o in (1, 2):
            out_specs[o] = pl.BlockSpec((tm // GRID_W, GRID_W * B_WIDTH), lambda i: (i, 0))
            out_shape[o] = jax.ShapeDtypeStruct((T // GRID_W, GRID_W * B_WIDTH), BF16)
        scratch = [pltpu.VMEM((2 * B_WIDTH // LANES, tm, LANES), F32)]
    return pl.pallas_call(
        functools.partial(_premix_kernel, rope=rope),
        grid=(T // tm,),
        in_specs=in_specs, out_specs=out_specs, out_shape=out_shape, scratch_shapes=scratch,
        compiler_params=_cparams(("parallel",)),
    )(*args)


def _dft1_kernel(m1_ref, twc_ref, tws_ref, re_ref, im_ref, o_ref):
    x = jnp.concatenate([re_ref[0], im_ref[0]], axis=0)
    a = _dot(m1_ref[...], x)
    nr = a.shape[0] // 2
    are, aim = a[:nr], a[nr:]
    c, s = twc_ref[...], tws_ref[...]
    bre = (are * c + aim * s).astype(BF16)
    bim = (aim * c - are * s).astype(BF16)
    for cl in range(o_ref.shape[2]):
        o_ref[0, 0, cl] = bre[:, cl * B_WIDTH:(cl + 1) * B_WIDTH]
        o_ref[0, 1, cl] = bim[:, cl * B_WIDTH:(cl + 1) * B_WIDTH]


def _dft3_kernel(m3_ref, x_ref, o_ref, xs_ref, os_ref):
    kb = x_ref.shape[3]
    nl = B_WIDTH // LANES
    for p in range(2):
        for c in range(GRID_W):
            xc = x_ref[0, p, c].astype(F32)
            for j in range(nl):
                xs_ref[p * nl + j, c * kb:(c + 1) * kb, :] = xc[:, j * LANES:(j + 1) * LANES]
    for k in range(kb):
        rows = [jnp.concatenate([xs_ref[p * nl + j, pl.ds(k, GRID_W, stride=kb), :] for j in range(nl)], axis=1)
                for p in range(2)]
        res = _dot(m3_ref[...], jnp.concatenate(rows, axis=0).astype(BF16))
        for j in range(nl):
            os_ref[j, pl.ds(k, GRID_W, stride=kb), :] = res[:, j * LANES:(j + 1) * LANES]
    for k2 in range(GRID_W):
        o_ref[0, k2] = jnp.concatenate(
            [os_ref[j, k2 * kb:(k2 + 1) * kb, :] for j in range(nl)], axis=1).astype(BF16)


def _seq_dft(re, im, bsz, n_rows, m1, twc, tws, m3):
    ncol = GRID_W * B_WIDTH
    tn = 2048
    re3 = re.reshape(bsz, n_rows, ncol)
    im3 = im.reshape(bsz, n_rows, ncol)
    st1 = pl.pallas_call(
        _dft1_kernel,
        grid=(ncol // tn, bsz),
        in_specs=[pl.BlockSpec((2 * n_rows, 2 * n_rows), lambda j, b: (0, 0)),
                  pl.BlockSpec((n_rows, tn), lambda j, b: (0, j)),
                  pl.BlockSpec((n_rows, tn), lambda j, b: (0, j)),
                  pl.BlockSpec((1, n_rows, tn), lambda j, b: (b, 0, j)),
                  pl.BlockSpec((1, n_rows, tn), lambda j, b: (b, 0, j))],
        out_specs=pl.BlockSpec((1, 2, tn // B_WIDTH, n_rows, B_WIDTH), lambda j, b: (b, 0, j, 0, 0)),
        out_shape=jax.ShapeDtypeStruct((bsz, 2, GRID_W, n_rows, B_WIDTH), BF16),
        compiler_params=_cparams(("parallel", "parallel")),
    )(m1, twc, tws, re3, im3)
    kb = min(n_rows, 16)
    y = pl.pallas_call(
        _dft3_kernel,
        grid=(bsz, n_rows // kb),
        in_specs=[pl.BlockSpec((GRID_W, 2 * GRID_W), lambda b, j: (0, 0)),
                  pl.BlockSpec((1, 2, GRID_W, kb, B_WIDTH), lambda b, j: (b, 0, 0, j, 0))],
        out_specs=pl.BlockSpec((1, GRID_W, kb, B_WIDTH), lambda b, j: (b, 0, j, 0)),
        out_shape=jax.ShapeDtypeStruct((bsz, GRID_W, n_rows, B_WIDTH), BF16),
        scratch_shapes=[pltpu.VMEM((2 * B_WIDTH // LANES, GRID_W * kb, LANES), F32),
                        pltpu.VMEM((B_WIDTH // LANES, GRID_W * kb, LANES), F32)],
        compiler_params=_cparams(("parallel", "parallel")),
    )(m3, st1)
    return y.reshape(bsz * GRID_W * n_rows, B_WIDTH)


def _ctx_dft_kernel(m_ref, re_ref, im_ref, o_ref):
    x = jnp.concatenate([re_ref[0], im_ref[0]], axis=0)
    o_ref[0] = _dot(m_ref[...], x).astype(BF16)


def _ctx_dft(re, im, bsz, m, mat):
    re3 = re.reshape(bsz, m, B_WIDTH)
    im3 = im.reshape(bsz, m, B_WIDTH)
    y = pl.pallas_call(
        _ctx_dft_kernel,
        grid=(bsz,),
        in_specs=[pl.BlockSpec((m, 2 * m), lambda b: (0, 0)),
                  pl.BlockSpec((1, m, B_WIDTH), lambda b: (b, 0, 0)),
                  pl.BlockSpec((1, m, B_WIDTH), lambda b: (b, 0, 0))],
        out_specs=pl.BlockSpec((1, m, B_WIDTH), lambda b: (b, 0, 0)),
        out_shape=jax.ShapeDtypeStruct((bsz, m, B_WIDTH), BF16),
        compiler_params=_cparams(("parallel",)),
    )(mat, re3, im3)
    return y.reshape(bsz * m, B_WIDTH)


ATTN_SUBBLOCKS = 4


def _attn_kernel(sink_ref, *refs, local, nb):
    if local:
        q_ref, kp_ref, kc_ref, kn_ref, vp_ref, vc_ref, vn_ref, kx_ref, vx_ref, o_ref = refs
    else:
        q_ref, kx_ref, vx_ref, o_ref = refs
    tq = BLOCK if local else q_ref.shape[1]
    nsb = q_ref.shape[1] // tq
    lo = lax.broadcasted_iota(jnp.int32, (tq, LANES), 1) < HEAD_DIM
    zero = jnp.zeros((tq, LANES), BF16)
    if local:
        j = pl.program_id(1)
        row = lax.broadcasted_iota(jnp.int32, (BLOCK, BLOCK), 0)
        col = lax.broadcasted_iota(jnp.int32, (BLOCK, BLOCK), 1)

        def key_blocks(p_ref, c_ref, n_ref, gl):
            return ([p_ref[0][:, gl]] + [c_ref[0][sb * BLOCK:(sb + 1) * BLOCK, gl] for sb in range(nsb)]
                    + [n_ref[0][:, gl]])
    jobs = []
    for g in range(2):
        gl = slice(g * LANES, (g + 1) * LANES)
        if local:
            kb = key_blocks(kp_ref, kc_ref, kn_ref, gl)
            vb = key_blocks(vp_ref, vc_ref, vn_ref, gl)
        for sb in range(nsb):
            if local:
                kg = jnp.concatenate(kb[sb:sb + 3] + [kx_ref[0][:, gl]], axis=0)
                vg = jnp.concatenate(vb[sb:sb + 3] + [vx_ref[0][:, gl]], axis=0)
            else:
                kg = kx_ref[0][:, gl]
                vg = vx_ref[0][:, gl]
            q = q_ref[0, sb * tq:(sb + 1) * tq, :]
            qs = []
            for p in range(2):
                qp = q[:, g * 2 * LANES + p * LANES: g * 2 * LANES + (p + 1) * LANES]
                qs.append(jnp.where(lo, qp, zero))
                qs.append(jnp.where(lo, zero, qp))
            q4 = jnp.concatenate(qs, axis=0)
            s4 = lax.dot_general(q4, kg, (((1,), (1,)), ((), ())), preferred_element_type=F32)
            jobs.append((g, sb, s4, vg))
    for g, sb, s4, vg in jobs:
        if local:
            valid_prev = (col >= row) & (j > 0) if sb == 0 else (col >= row)
            valid_next = (col <= row) & (j < nb // nsb - 1) if sb == nsb - 1 else (col <= row)
        ps, rden = [], []
        for hl in range(4):
            s = s4[hl * tq:(hl + 1) * tq]
            if local:
                s = jnp.concatenate(
                    [jnp.where(valid_prev, s[:, :BLOCK], NEG), s[:, BLOCK:2 * BLOCK],
                     jnp.where(valid_next, s[:, 2 * BLOCK:3 * BLOCK], NEG), s[:, 3 * BLOCK:]], axis=1)
            sk = sink_ref[g * 4 + hl] * LOG2E
            m = jnp.maximum(jnp.max(s, axis=-1, keepdims=True), sk)
            p = jnp.exp2(s - m)
            den = jnp.sum(p, axis=-1, keepdims=True) + jnp.exp2(sk - m)
            ps.append(p.astype(BF16))
            rden.append(1.0 / den)
        o4 = _dot(jnp.concatenate(ps, axis=0), vg) * jnp.concatenate(rden, axis=0)
        for p in range(2):
            o_pair = jnp.where(lo, o4[(2 * p) * tq:(2 * p + 1) * tq], o4[(2 * p + 1) * tq:(2 * p + 2) * tq])
            c0 = g * 2 * LANES + p * LANES
            o_ref[0, sb * tq:(sb + 1) * tq, c0:c0 + LANES] = o_pair.astype(BF16)


def _window_attention(q, k, v, kx, vx, sink, bsz, n, m):
    nb = n // BLOCK
    q3 = q.reshape(bsz, n, C_WIDTH)
    k3 = k.reshape(bsz, n, 2 * KV_WIDTH)
    v3 = v.reshape(bsz, n, 2 * KV_WIDTH)
    kx3 = kx.reshape(bsz, m, 2 * KV_WIDTH)
    vx3 = vx.reshape(bsz, m, 2 * KV_WIDTH)
    nsb = ATTN_SUBBLOCKS
    edge = lambda f: pl.BlockSpec((1, BLOCK, 2 * KV_WIDTH), f)
    own = pl.BlockSpec((1, nsb * BLOCK, 2 * KV_WIDTH), lambda b, i, s: (b, i, 0))
    prev = lambda b, i, s: (b, jnp.maximum(i * nsb - 1, 0), 0)
    cur = lambda b, i, s: (b, i, 0)
    nxt = lambda b, i, s: (b, jnp.minimum((i + 1) * nsb, nb - 1), 0)
    ctxs = pl.BlockSpec((1, m, 2 * KV_WIDTH), lambda b, i, s: (b, 0, 0))
    o = pl.pallas_call(
        functools.partial(_attn_kernel, local=True, nb=nb),
        grid_spec=pltpu.PrefetchScalarGridSpec(
            num_scalar_prefetch=1,
            grid=(bsz, nb // nsb),
            in_specs=[pl.BlockSpec((1, nsb * BLOCK, C_WIDTH), cur),
                      edge(prev), own, edge(nxt), edge(prev), own, edge(nxt), ctxs, ctxs],
            out_specs=pl.BlockSpec((1, nsb * BLOCK, C_WIDTH), cur)),
        out_shape=jax.ShapeDtypeStruct((bsz, n, C_WIDTH), BF16),
        compiler_params=_cparams(("parallel", "parallel")),
    )(sink, q3, k3, k3, k3, v3, v3, v3, kx3, vx3)
    return o.reshape(bsz * n, C_WIDTH)


def _context_attention(q, kx, vx, sink, bsz, m):
    q3 = q.reshape(bsz, m, C_WIDTH)
    kx3 = kx.reshape(bsz, m, 2 * KV_WIDTH)
    vx3 = vx.reshape(bsz, m, 2 * KV_WIDTH)
    ctxs = pl.BlockSpec((1, m, 2 * KV_WIDTH), lambda b, s: (b, 0, 0))
    o = pl.pallas_call(
        functools.partial(_attn_kernel, local=False, nb=1),
        grid_spec=pltpu.PrefetchScalarGridSpec(
            num_scalar_prefetch=1,
            grid=(bsz,),
            in_specs=[pl.BlockSpec((1, m, C_WIDTH), lambda b, s: (b, 0, 0)), ctxs, ctxs],
            out_specs=pl.BlockSpec((1, m, C_WIDTH), lambda b, s: (b, 0, 0))),
        out_shape=jax.ShapeDtypeStruct((bsz, m, C_WIDTH), BF16),
        compiler_params=_cparams(("parallel",)),
    )(sink, q3, kx3, vx3)
    return o.reshape(bsz * m, C_WIDTH)


def _postmix_kernel(*refs, route):
    h_ref = refs[5]
    for sb in range(h_ref.shape[0] // MOE_TB):
        _postmix_rows(slice(sb * MOE_TB, (sb + 1) * MOE_TB), sb, refs, route)


def _postmix_rows(rows, sb, refs, route):
    (sgu_ref, yf_ref, at_ref, wf_ref, wo_ref, h_ref, gp_ref, gt_ref, gf_ref, sh_ref, sc_ref) = refs[:11]
    if route:
        wr_ref, br_ref, hn_ref, y_ref, cmb_ref, cnt_ref = refs[11:]
    else:
        hn_ref, y_ref = refs[11:]
    fm = _dot(yf_ref[rows, :], wf_ref[...]).astype(BF16)
    o = (_dot(sgu_ref[rows, :], wo_ref[:A_WIDTH, :])
         + _dot(fm, wo_ref[A_WIDTH:A_WIDTH + B_WIDTH, :])
         + _dot(at_ref[rows, :], wo_ref[A_WIDTH + B_WIDTH:, :]))
    ms = jnp.mean(o * o, axis=-1, keepdims=True)
    hn = h_ref[rows, :] + gt_ref[...] * (o * lax.rsqrt(ms + EPS) * gp_ref[...])
    hn_ref[rows, :] = hn
    ms2 = jnp.mean(hn * hn, axis=-1, keepdims=True)
    y = hn * lax.rsqrt(ms2 + EPS) * gf_ref[...] * (1.0 + sc_ref[...]) + sh_ref[...]
    y_ref[rows, :] = y.astype(BF16)
    if route:
        yh = y.astype(BF16)
        yl = (y - yh.astype(F32)).astype(BF16)
        w = wr_ref[...]
        wh = w.astype(BF16)
        wl = (w - wh.astype(F32)).astype(BF16)
        lg = _dot(yh, wh) + _dot(yl, wh) + _dot(yh, wl) + br_ref[...]
        lane = lax.broadcasted_iota(jnp.int32, lg.shape, 1)
        lg = jnp.where(lane < N_EXPERTS, lg, NEG)
        m1 = jnp.max(lg, axis=-1, keepdims=True)
        i1 = jnp.min(jnp.where(lg == m1, lane, LANES), axis=-1, keepdims=True)
        lg2 = jnp.where(lane == i1, NEG, lg)
        m2 = jnp.max(lg2, axis=-1, keepdims=True)
        i2 = jnp.min(jnp.where(lg2 == m2, lane, LANES), axis=-1, keepdims=True)
        e2 = jnp.exp(m2 - m1)
        g1 = 1.0 / (1.0 + e2)
        g2 = e2 * g1
        cmb = jnp.where(lane == i1, g1, 0.0) + jnp.where(lane == i2, g2, 0.0)
        cmb_ref[rows, :] = cmb
        cnt_ref[sb] = jnp.sum(jnp.where(cmb > 0.0, 1.0, 0.0), axis=0, keepdims=True)


def _postmix(sgu, yf, at, wf, wo, h, mods, gp, gf, router, *, tm, tiles_per_batch):
    T = h.shape[0]
    if tiles_per_batch is None:
        mrow = lambda i: 4
    else:
        mrow = lambda i: i // tiles_per_batch
    const2 = lambda i: (0, 0)
    row = lambda w: pl.BlockSpec((tm, w), lambda i: (i, 0))
    mod = lambda j: pl.BlockSpec((None, None, 1, D_MODEL), lambda i: (mrow(i), j, 0, 0))
    in_specs = [row(A_WIDTH), row(B_WIDTH), row(C_WIDTH),
                pl.BlockSpec((B_WIDTH, B_WIDTH), const2),
                pl.BlockSpec((D_MODEL, D_MODEL), const2),
                row(D_MODEL),
                pl.BlockSpec((1, D_MODEL), const2),
                mod(2),
                pl.BlockSpec((1, D_MODEL), const2),
                mod(3), mod(4)]
    args = [sgu, yf, at, wf, wo, h, gp, mods, gf, mods, mods]
    out_specs = [row(D_MODEL), row(D_MODEL)]
    out_shape = [jax.ShapeDtypeStruct((T, D_MODEL), F32), jax.ShapeDtypeStruct((T, D_MODEL), BF16)]
    route = router is not None
    if route:
        in_specs += [pl.BlockSpec((D_MODEL, LANES), const2), pl.BlockSpec((1, LANES), const2)]
        args += list(router)
        out_specs += [row(LANES), pl.BlockSpec((tm // MOE_TB, 1, LANES), lambda i: (i, 0, 0))]
        out_shape += [jax.ShapeDtypeStruct((T, LANES), F32), jax.ShapeDtypeStruct((T // MOE_TB, 1, LANES), F32)]
    return pl.pallas_call(
        functools.partial(_postmix_kernel, route=route),
        grid=(T // tm,),
        in_specs=in_specs, out_specs=out_specs, out_shape=out_shape,
        compiler_params=_cparams(("parallel",)),
    )(*args)


def _ffn_epilogue(f, h_ref, gp_ref, gt_ref, o_ref):
    ms = jnp.mean(f * f, axis=-1, keepdims=True)
    o_ref[...] = h_ref[...] + gt_ref[...] * (f * lax.rsqrt(ms + EPS) * gp_ref[...])


FF_CHUNK = 512


def _swiglu_tile(y, wg_ref, wu_ref, wd_ref, act_ref):
    for c in range(D_FF // FF_CHUNK):
        cols = slice(c * FF_CHUNK, (c + 1) * FF_CHUNK)
        gate = _dot(y, wg_ref[:, cols])
        up = _dot(y, wu_ref[:, cols])
        act_ref[:, cols] = (gate * jax.nn.sigmoid(gate) * up).astype(BF16)
    return _dot(act_ref[...], wd_ref[...])


def _ffn_dense_kernel(y_ref, wg_ref, wu_ref, wd_ref, h_ref, gp_ref, gt_ref, o_ref, act_ref):
    f = _swiglu_tile(y_ref[...], wg_ref, wu_ref, wd_ref, act_ref)
    _ffn_epilogue(f, h_ref, gp_ref, gt_ref, o_ref)


def _ffn_dense(y, wg, wu, wd, h, mods, gp, *, tm, tiles_per_batch):
    T = h.shape[0]
    if tiles_per_batch is None:
        mrow = lambda i: 4
    else:
        mrow = lambda i: i // tiles_per_batch
    resident = pl.Buffered(1)
    return pl.pallas_call(
        _ffn_dense_kernel,
        grid=(T // tm,),
        in_specs=[pl.BlockSpec((tm, D_MODEL), lambda i: (i, 0)),
                  pl.BlockSpec((D_MODEL, D_FF), lambda i: (0, 0), pipeline_mode=resident),
                  pl.BlockSpec((D_MODEL, D_FF), lambda i: (0, 0), pipeline_mode=resident),
                  pl.BlockSpec((D_FF, D_MODEL), lambda i: (0, 0), pipeline_mode=resident),
                  pl.BlockSpec((tm, D_MODEL), lambda i: (i, 0)),
                  pl.BlockSpec((1, D_MODEL), lambda i: (0, 0)),
                  pl.BlockSpec((None, None, 1, D_MODEL), lambda i: (mrow(i), 5, 0, 0))],
        out_specs=pl.BlockSpec((tm, D_MODEL), lambda i: (i, 0)),
        out_shape=jax.ShapeDtypeStruct((T, D_MODEL), F32),
        scratch_shapes=[pltpu.VMEM((tm, D_FF), BF16)],
        compiler_params=_cparams(("parallel",)),
    )(y, wg, wu, wd, h, gp, mods)


MOE_TB = 512
MOE_TM = 512
SUB = 128
NSUB = MOE_TB // SUB
RUN_ALIGN = 16


def _moe_tile_bound(T):
    nt = T // MOE_TB
    rows = 2 * T + nt * N_EXPERTS * (RUN_ALIGN - 1) + N_EXPERTS * (SUB + MOE_TM - 1)
    return -(-rows // MOE_TM)


def _moe_plan(cnt, T):
    p = (cnt + RUN_ALIGN - 1) // RUN_ALIGN * RUN_ALIGN
    base = jnp.cumsum(p, axis=0) - p
    used = jnp.sum(p, axis=0)
    tiles = (used + SUB + MOE_TM - 1) // MOE_TM
    tend = jnp.cumsum(tiles)
    off = (tend - tiles) * MOE_TM
    rowbase = (off[None, :] + base).astype(jnp.int32)
    nsub = (cnt + SUB - 1) // SUB
    slotbase = (jnp.cumsum(nsub, axis=1) - nsub).astype(jnp.int32)
    nt_bound = _moe_tile_bound(T)
    j = jnp.arange(nt_bound, dtype=jnp.int32)
    tile_e = jnp.minimum(jnp.sum(j[:, None] >= tend[None, :], axis=1), N_EXPERTS - 1).astype(jnp.int32)
    nvalid = tend[-1].astype(jnp.int32)
    tile_blk = jnp.minimum(j, nvalid - 1)
    return rowbase, slotbase, tile_e, tile_blk, nvalid.reshape(1)


NSLOT = 2 * MOE_TB // SUB + N_EXPERTS
SLOT_ROWS = NSLOT * SUB
SLOT_CHUNK = 512


def _block_slots(sb_ref, cn_ref, b):
    last = b * N_EXPERTS + N_EXPERTS - 1
    return sb_ref[last] + (cn_ref[last] + SUB - 1) // SUB


def _for_each_run_slot(cn_ref, b, fn):
    for e in range(N_EXPERTS):
        for s in range(NSUB):
            @pl.when(cn_ref[b * N_EXPERTS + e] > SUB * s)
            def _(e=e, s=s):
                fn(e, s)


def _run_copy_out(stage, ys_ref, sems, rb_ref, sb_ref, b, par, e, s):
    slot = sb_ref[b * N_EXPERTS + e] + s
    r0 = pl.multiple_of(rb_ref[b * N_EXPERTS + e] + SUB * s, RUN_ALIGN)
    return pltpu.make_async_copy(stage.at[par, pl.ds(pl.multiple_of(slot * SUB, SUB), SUB), :],
                                 ys_ref.at[pl.ds(r0, SUB), :], sems.at[par, e, s])


def _dispatch_kernel(rb_ref, sb_ref, cn_ref, y_ref, cmb_ref, sbv_ref, ltri_ref, ysin_ref,
                     tok_ref, ys_ref, stage, sems):
    del ysin_ref
    b = pl.program_id(0)
    nb = pl.num_programs(0)
    par = b % 2
    cmb = cmb_ref[...]
    sel = cmb > 0.0
    rk = _dot(ltri_ref[...], jnp.where(sel, 1.0, 0.0).astype(BF16))
    srow = jnp.where(sel, rk + sbv_ref[...], -1.0)
    nsel = jnp.sum(jnp.where(sel, 1.0, 0.0), axis=-1, keepdims=True)
    sa = jnp.max(srow, axis=-1, keepdims=True)
    sb = jnp.where(nsel > 1.5, jnp.sum(jnp.where(sel, srow, 0.0), axis=-1, keepdims=True) - sa, -1.0)
    ga = jnp.sum(jnp.where(srow == sa, cmb, 0.0), axis=-1, keepdims=True)
    gb = jnp.sum(cmb, axis=-1, keepdims=True) - ga
    lane = lax.broadcasted_iota(jnp.int32, cmb.shape, 1)
    tok = jnp.where(lane == 0, sa, jnp.where(lane == 1, sb, jnp.where(lane == 2, ga, jnp.where(lane == 3, gb, 0.0))))
    tok_ref[...] = tok
    tok_t = tok.T
    nslots = _block_slots(sb_ref, cn_ref, b)
    rr0 = lax.broadcasted_iota(jnp.int32, (SLOT_CHUNK, cmb.shape[0]), 0).astype(F32)
    for c in range(SLOT_ROWS // SLOT_CHUNK):
        @pl.when(nslots * SUB > c * SLOT_CHUNK)
        def _(c=c):
            rr = rr0 + float(c * SLOT_CHUNK)
            g = jnp.where((rr == tok_t[0:1, :]) | (rr == tok_t[1:2, :]), 1.0, 0.0).astype(BF16)
            stage[par, c * SLOT_CHUNK:(c + 1) * SLOT_CHUNK, :] = _dot(g, y_ref[...]).astype(BF16)

    @pl.when(b > 0)
    def _():
        _for_each_run_slot(cn_ref, b - 1, lambda e, s: _run_copy_out(
            stage, ys_ref, sems, rb_ref, sb_ref, b - 1, 1 - par, e, s).wait())

    _for_each_run_slot(cn_ref, b, lambda e, s: _run_copy_out(
        stage, ys_ref, sems, rb_ref, sb_ref, b, par, e, s).start())

    @pl.when(b == nb - 1)
    def _():
        _for_each_run_slot(cn_ref, b, lambda e, s: _run_copy_out(
            stage, ys_ref, sems, rb_ref, sb_ref, b, par, e, s).wait())


def _moe_dispatch(y, cmb, rowbase, slotbase, cnt, nt_bound):
    T = y.shape[0]
    nt = T // MOE_TB
    sbv = jnp.zeros((nt, 1, LANES), F32).at[:, 0, :N_EXPERTS].set((slotbase * SUB).astype(F32))
    ltri = jnp.asarray(np.tril(np.ones((MOE_TB, MOE_TB), np.float32), -1)).astype(BF16)
    ys0 = jnp.zeros((nt_bound * MOE_TM, D_MODEL), BF16)
    imap2 = lambda b, rb, sb, cn: (b, 0)
    return pl.pallas_call(
        _dispatch_kernel,
        grid_spec=pltpu.PrefetchScalarGridSpec(
            num_scalar_prefetch=3,
            grid=(nt,),
            in_specs=[pl.BlockSpec((MOE_TB, D_MODEL), imap2),
                      pl.BlockSpec((MOE_TB, LANES), imap2),
                      pl.BlockSpec((None, 1, LANES), lambda b, rb, sb, cn: (b, 0, 0)),
                      pl.BlockSpec((MOE_TB, MOE_TB), lambda b, rb, sb, cn: (0, 0)),
                      pl.BlockSpec(memory_space=pl.ANY)],
            out_specs=[pl.BlockSpec((MOE_TB, LANES), imap2),
                       pl.BlockSpec(memory_space=pl.ANY)],
            scratch_shapes=[pltpu.VMEM((2, SLOT_ROWS, D_MODEL), BF16),
                            pltpu.SemaphoreType.DMA((2, N_EXPERTS, NSUB))]),
        out_shape=[jax.ShapeDtypeStruct((T, LANES), F32),
                   jax.ShapeDtypeStruct((nt_bound * MOE_TM, D_MODEL), BF16)],
        input_output_aliases={7: 1},
        compiler_params=_cparams(("arbitrary",)),
    )(rowbase.reshape(-1), slotbase.reshape(-1), cnt.reshape(-1), y, cmb, sbv, ltri, ys0)


def _ffn_group_kernel(te_ref, tb_ref, nv_ref, y_ref, wg_ref, wu_ref, wd_ref, o_ref, act_ref):
    del te_ref, tb_ref
    j = pl.program_id(0)

    @pl.when(j < nv_ref[0])
    def _():
        o_ref[...] = _swiglu_tile(y_ref[...], wg_ref, wu_ref, wd_ref, act_ref).astype(BF16)

    @pl.when(j >= nv_ref[0])
    def _():
        o_ref[...] = jnp.zeros_like(o_ref)


def _ffn_group(ys, wg, wu, wd, tile_e, tile_blk, nvalid):
    rows = ys.shape[0]
    resident = pl.Buffered(1)
    return pl.pallas_call(
        _ffn_group_kernel,
        grid_spec=pltpu.PrefetchScalarGridSpec(
            num_scalar_prefetch=3,
            grid=(rows // MOE_TM,),
            in_specs=[pl.BlockSpec((MOE_TM, D_MODEL), lambda j, te, tb, nv: (tb[j], 0)),
                      pl.BlockSpec((None, D_MODEL, D_FF), lambda j, te, tb, nv: (te[j], 0, 0), pipeline_mode=resident),
                      pl.BlockSpec((None, D_MODEL, D_FF), lambda j, te, tb, nv: (te[j], 0, 0), pipeline_mode=resident),
                      pl.BlockSpec((None, D_FF, D_MODEL), lambda j, te, tb, nv: (te[j], 0, 0), pipeline_mode=resident)],
            out_specs=pl.BlockSpec((MOE_TM, D_MODEL), lambda j, te, tb, nv: (j, 0)),
            scratch_shapes=[pltpu.VMEM((MOE_TM, D_FF), BF16)]),
        out_shape=jax.ShapeDtypeStruct((rows, D_MODEL), BF16),
        compiler_params=_cparams(("arbitrary",)),
    )(tile_e, tile_blk, nvalid, ys, wg, wu, wd)


def _run_copy_in(fs_ref, fbuf, sems, rb_ref, sb_ref, b, par, e, s):
    slot = sb_ref[b * N_EXPERTS + e] + s
    r0 = pl.multiple_of(rb_ref[b * N_EXPERTS + e] + SUB * s, RUN_ALIGN)
    return pltpu.make_async_copy(fs_ref.at[pl.ds(r0, SUB), :],
                                 fbuf.at[par, pl.ds(pl.multiple_of(slot * SUB, SUB), SUB), :], sems.at[par, e, s])


def _combine_kernel(rb_ref, sb_ref, cn_ref, tok_ref, h_ref, gp_ref, gt_ref, fs_ref, o_ref, fbuf, sems):
    b = pl.program_id(0)
    nb = pl.num_programs(0)
    par = b % 2

    @pl.when(b == 0)
    def _():
        fbuf[...] = jnp.zeros_like(fbuf)
        _for_each_run_slot(cn_ref, b, lambda e, s: _run_copy_in(
            fs_ref, fbuf, sems, rb_ref, sb_ref, b, par, e, s).start())

    @pl.when(b + 1 < nb)
    def _():
        _for_each_run_slot(cn_ref, b + 1, lambda e, s: _run_copy_in(
            fs_ref, fbuf, sems, rb_ref, sb_ref, b + 1, 1 - par, e, s).start())

    tok = tok_ref[...]
    sa, sb, ga, gb = tok[:, 0:1], tok[:, 1:2], tok[:, 2:3], tok[:, 3:4]
    _for_each_run_slot(cn_ref, b, lambda e, s: _run_copy_in(
        fs_ref, fbuf, sems, rb_ref, sb_ref, b, par, e, s).wait())
    nslots = _block_slots(sb_ref, cn_ref, b)
    nck = SLOT_ROWS // SLOT_CHUNK
    for c in range(1, nck + 1):
        lo_rows, hi_rows = (c - 1) * SLOT_CHUNK, c * SLOT_CHUNK
        cond = (nslots * SUB > lo_rows) if c == nck else ((nslots * SUB > lo_rows) & (nslots * SUB <= hi_rows))
        if c == 1:
            cond = nslots * SUB <= hi_rows

        @pl.when(cond)
        def _(depth=hi_rows):
            cc = lax.broadcasted_iota(jnp.int32, (tok.shape[0], depth), 1).astype(F32)
            pick = jnp.where(cc == sa, ga, jnp.where(cc == sb, gb, 0.0)).astype(BF16)
            f = _dot(pick, fbuf[par, :depth, :])
            _ffn_epilogue(f, h_ref, gp_ref, gt_ref, o_ref)


def _moe_combine(fs, tok, h, mods, gp, rowbase, slotbase, cnt, *, tiles_per_batch):
    T = h.shape[0]
    nt = T // MOE_TB
    imap2 = lambda b, rb, sb, cn: (b, 0)
    return pl.pallas_call(
        _combine_kernel,
        grid_spec=pltpu.PrefetchScalarGridSpec(
            num_scalar_prefetch=3,
            grid=(nt,),
            in_specs=[pl.BlockSpec((MOE_TB, LANES), imap2),
                      pl.BlockSpec((MOE_TB, D_MODEL), imap2),
                      pl.BlockSpec((1, D_MODEL), lambda b, rb, sb, cn: (0, 0)),
                      pl.BlockSpec((None, None, 1, D_MODEL), lambda b, rb, sb, cn: (b // tiles_per_batch, 5, 0, 0)),
                      pl.BlockSpec(memory_space=pl.ANY)],
            out_specs=pl.BlockSpec((MOE_TB, D_MODEL), imap2),
            scratch_shapes=[pltpu.VMEM((2, SLOT_ROWS, D_MODEL), BF16),
                            pltpu.SemaphoreType.DMA((2, N_EXPERTS, NSUB))]),
        out_shape=jax.ShapeDtypeStruct((T, D_MODEL), F32),
        compiler_params=_cparams(("arbitrary",)),
    )(rowbase.reshape(-1), slotbase.reshape(-1), cnt.reshape(-1), tok, h, gp, mods, fs)


def _ffn_moe(y, cmb, cnt_tiles, wg, wu, wd, h, mods, gp, *, tiles_per_batch):
    T = h.shape[0]
    cnt = cnt_tiles.reshape(T // MOE_TB, LANES)[:, :N_EXPERTS].astype(jnp.int32)
    rowbase, slotbase, tile_e, tile_blk, nvalid = _moe_plan(cnt, T)
    tok, ys = _moe_dispatch(y, cmb, rowbase, slotbase, cnt, _moe_tile_bound(T))
    fs = _ffn_group(ys, wg, wu, wd, tile_e, tile_blk, nvalid)
    return _moe_combine(fs, tok, h, mods, gp, rowbase, slotbase, cnt, tiles_per_batch=tiles_per_batch)


def _blockdiag(m, reps):
    n = m.shape[0]
    out = np.zeros((n * reps, n * reps), np.float64)
    for r in range(reps):
        out[r * n:(r + 1) * n, r * n:(r + 1) * n] = m
    return out


def _dft_tables(n_rows):
    n = n_rows * GRID_W
    r = np.arange(n_rows)
    c = np.arange(GRID_W)
    a1 = 2 * np.pi * np.outer(r, r) / n_rows
    c1, s1 = np.cos(a1), np.sin(a1)
    m1 = np.block([[c1, s1], [-s1, c1]]) * 0.125
    at = 2 * np.pi * np.outer(r, c) / n
    twc = np.repeat(np.cos(at), B_WIDTH, axis=1)
    tws = np.repeat(np.sin(at), B_WIDTH, axis=1)
    a3 = 2 * np.pi * np.outer(c, c) / GRID_W
    m3 = np.concatenate([np.cos(a3), np.sin(a3)], axis=1) * (8.0 / np.sqrt(n))
    f32 = lambda t: jnp.asarray(t.astype(np.float32))
    return f32(m1).astype(BF16), f32(twc), f32(tws), f32(m3).astype(BF16)


def _channel_dft_table():
    d = np.arange(HEAD_DIM)
    a = 2 * np.pi * np.outer(d, d) / HEAD_DIM
    w = np.concatenate([_blockdiag(np.cos(a), 4), -_blockdiag(np.sin(a), 4)], axis=1) * 0.125
    return jnp.asarray(w.astype(np.float32)).astype(BF16)


def _ctx_dft_table(m):
    p = np.arange(m)
    a = 2 * np.pi * np.outer(p, p) / m
    w = np.concatenate([np.cos(a), np.sin(a)], axis=1) * (8.0 / np.sqrt(m * HEAD_DIM))
    return jnp.asarray(w.astype(np.float32)).astype(BF16)


def _rope_tables(n_tok):
    rows = n_tok // GRID_W
    row = jnp.broadcast_to(jnp.arange(rows)[:, None], (rows, GRID_W)).reshape(-1)
    col = jnp.broadcast_to(jnp.arange(GRID_W)[None, :], (rows, GRID_W)).reshape(-1)
    half = HEAD_DIM // 2
    inv = ROPE_BASE ** (-jnp.arange(0, half, 2, dtype=F32) / half)
    ang = jnp.stack([row.astype(F32)[:, None] * inv, col.astype(F32)[:, None] * inv], axis=1)
    cos, sin = jnp.cos(ang), jnp.sin(ang)
    zer = jnp.zeros_like(sin)
    lay = lambda a, b: jnp.tile(jnp.stack([a, b], axis=2).reshape(n_tok, HEAD_DIM), (1, LANES // HEAD_DIM))
    return lay(cos, cos), lay(-sin, zer), lay(zer, sin)


def kernel(x, c, ctx, c_ctx, w_ada, b_ada, g_mix_pre, g_mix_post, g_ffn_pre, g_ffn_post,
           w_in, w_s, b_s, g_v, w_f, sink, w_out, w_gate_d, w_up_d, w_down_d,
           w_router, b_router, w_gate_e, w_up_e, w_down_e):
    bsz, n_lat, _ = x.shape
    n_ctx = ctx.shape[1]
    T, Tc = bsz * n_lat, bsz * n_ctx
    tm = 512
    tpb = n_lat // tm

    rope_tabs = _rope_tables(n_lat)
    m1, twc, tws, m3 = _dft_tables(n_lat // GRID_W)
    wdft = _channel_dft_table()
    mctx = _ctx_dft_table(n_ctx)
    hm = jnp.asarray(_blockdiag(np.full((HEAD_DIM, HEAD_DIM), 1.0 / HEAD_DIM), A_HEADS).astype(np.float32)).astype(BF16)

    cond8 = jnp.zeros((8, D_MODEL), F32).at[:bsz].set(c).at[4].set(c_ctx)
    h = x.reshape(T, D_MODEL)
    hc = ctx.reshape(Tc, D_MODEL)

    mods_all = _adaln(cond8, w_ada, b_ada[:, None, :]).reshape(DEPTH, 8, 6, 1, D_MODEL)

    for i in range(DEPTH):
        last = i == DEPTH - 1
        mods = mods_all[i]
        win = w_in[i].astype(BF16)
        ws = w_s[i].astype(BF16)
        bsx = jnp.repeat(b_s[i].T, HEAD_DIM, axis=1)
        gv = g_v[i].reshape(1, A_WIDTH)
        wf = jax.scipy.linalg.block_diag(*[w_f[i][g] for g in range(4)]).astype(BF16)
        wo = w_out[i].astype(BF16)
        gpre = g_mix_pre[i][None, :]
        gpost = g_mix_post[i][None, :]
        gfpre = g_ffn_pre[i][None, :]
        gfpost = g_ffn_post[i][None, :]
        sk = sink[i]

        sgu, fre, fim, q, k, v = _premix(h, mods, gpre, win, ws, bsx, gv, hm, wdft, rope_tabs,
                                         tm=2 * tm, tiles_per_batch=tpb // 2, n_pos=n_lat)
        sguc, frec, fimc, qc, kc, vc = _premix(hc, mods, gpre, win, ws, bsx, gv, hm, wdft, None,
                                               tm=tm, tiles_per_batch=None, n_pos=n_ctx)
        yf = _seq_dft(fre, fim, bsz, n_lat // GRID_W, m1, twc, tws, m3)
        at = _window_attention(q, k, v, kc, vc, sk, bsz, n_lat, n_ctx)

        if i % 2 == 0:
            router = None
        else:
            j = i // 2
            wr = jnp.zeros((D_MODEL, LANES), F32).at[:, :N_EXPERTS].set(w_router[j])
            br = jnp.zeros((1, LANES), F32).at[0, :N_EXPERTS].set(b_router[j])
            router = (wr, br)
        res = _postmix(sgu, yf, at, wf, wo, h, mods, gpost, gfpre, router, tm=2 * tm, tiles_per_batch=tpb // 2)
        if not last:
            yfc = _ctx_dft(frec, fimc, bsz, n_ctx, mctx)
            atc = _context_attention(qc, kc, vc, sk, bsz, n_ctx)
            resc = _postmix(sguc, yfc, atc, wf, wo, hc, mods, gpost, gfpre, router, tm=2 * tm, tiles_per_batch=None)

        j = i // 2
        if i % 2 == 0:
            wg, wu, wd = w_gate_d[j].astype(BF16), w_up_d[j].astype(BF16), w_down_d[j].astype(BF16)
            h = _ffn_dense(res[1], wg, wu, wd, res[0], mods, gfpost, tm=tm, tiles_per_batch=tpb)
            if not last:
                hc = _ffn_dense(resc[1], wg, wu, wd, resc[0], mods, gfpost, tm=tm, tiles_per_batch=None)
        else:
            wg, wu, wd = w_gate_e[j].astype(BF16), w_up_e[j].astype(BF16), w_down_e[j].astype(BF16)
            assert last and tm == MOE_TB, "the expert FFN is only built for the final layer's latent tokens"
            h = _ffn_moe(res[1], res[2], res[3], wg, wu, wd, res[0], mods, gfpost,
                         tiles_per_batch=n_lat // MOE_TB)
    return h.reshape(bsz, n_lat, D_MODEL)
```

```python
import functools

import numpy as np
import jax
import jax.numpy as jnp
from jax import lax
from jax.experimental import pallas as pl
from jax.experimental.pallas import tpu as pltpu

D_MODEL = 1024
DEPTH = 2
GRID_W = 64
HEAD_DIM = 64
EPS = 1e-6
A_HEADS = 4
A_WIDTH = 256
CHUNK = 128
B_WIDTH = 256
C_Q_HEADS = 8
C_WIDTH = 512
KV_WIDTH = 128
WINDOW = 128
BLOCK = 128
ROPE_BASE = 10000.0
OFF_B = 512
OFF_Q = 768
OFF_K = 1280
OFF_V = 1408
N_IN = 1536
D_FF = 3584
N_EXPERTS = 8

LANES = 128
VMEM_LIMIT = 56 * 1024 * 1024
NEG = -1e30
LOG2E = 1.4426950408889634
assert WINDOW == BLOCK

F32 = jnp.float32
BF16 = jnp.bfloat16


def _dot(a, b):
    return jnp.dot(a, b, preferred_element_type=F32)


def _cparams(sem):
    return pltpu.CompilerParams(dimension_semantics=sem, vmem_limit_bytes=VMEM_LIMIT)


def _adaln_kernel(c_ref, w_ref, b_ref, o_ref):
    c = c_ref[...]
    s = (c * jax.nn.sigmoid(c)).astype(BF16)
    o_ref[...] = _dot(s, w_ref[...].astype(BF16)) + b_ref[...]


def _adaln(cond8, w, b):
    tn = 1536
    nl = w.shape[0]
    return pl.pallas_call(
        _adaln_kernel,
        grid=(nl, 6 * D_MODEL // tn),
        in_specs=[pl.BlockSpec((8, D_MODEL), lambda l, j: (0, 0)),
                  pl.BlockSpec((None, D_MODEL, tn), lambda l, j: (l, 0, j)),
                  pl.BlockSpec((None, 1, tn), lambda l, j: (l, 0, j))],
        out_specs=pl.BlockSpec((None, 8, tn), lambda l, j: (l, 0, j)),
        out_shape=jax.ShapeDtypeStruct((nl, 8, 6 * D_MODEL), F32),
        compiler_params=_cparams(("arbitrary", "arbitrary")),
    )(cond8, w, b)


def _gelu_tanh(x):
    return 0.5 * x * (1.0 + jnp.tanh(0.7978845608028654 * (x + 0.044715 * x * x * x)))


PRE_SUB = 512


def _premix_kernel(*refs, rope):
    if rope:
        (h_ref, sh_ref, sc_ref, g_ref, win_ref, ws_ref, bs_ref, gv_ref, hm_ref, wdft_ref,
         cos_ref, s1_ref, s2_ref, sgu_ref, re_ref, im_ref, q_ref, k_ref, v_ref, fsc_ref) = refs
    else:
        (h_ref, sh_ref, sc_ref, g_ref, win_ref, ws_ref, bs_ref, gv_ref, hm_ref, wdft_ref,
         sgu_ref, re_ref, im_ref, q_ref, k_ref, v_ref) = refs
    tm = h_ref.shape[0]
    nl = 2 * B_WIDTH // LANES
    head = lax.broadcasted_iota(jnp.int32, (CHUNK, A_WIDTH), 1) // HEAD_DIM
    lo = lax.broadcasted_iota(jnp.int32, (PRE_SUB, LANES), 1) < HEAD_DIM
    for sb in range(tm // PRE_SUB):
        rs = slice(sb * PRE_SUB, (sb + 1) * PRE_SUB)
        x = h_ref[rs, :]
        ms = jnp.mean(x * x, axis=-1, keepdims=True)
        xn = x * lax.rsqrt(ms + EPS) * g_ref[...]
        xm = (xn * (1.0 + sc_ref[...]) + sh_ref[...]).astype(BF16)
        z = _dot(xm, win_ref[...])
        zcols = lambda c0, c1: z[:, c0:c1]

        a = _gelu_tanh(zcols(0, OFF_B))
        u = a[:, :A_WIDTH]
        v = a[:, A_WIDTH:]
        msv = _dot((v * v).astype(BF16), hm_ref[...])
        vn = (v * lax.rsqrt(msv + EPS) * gv_ref[...]).astype(BF16)
        for ck in range(PRE_SUB // CHUNK):
            rows = slice(ck * CHUNK, (ck + 1) * CHUNK)
            vc = vn[rows]
            vstack = jnp.concatenate([jnp.where(head == hh, vc, jnp.zeros_like(vc)) for hh in range(A_HEADS)], axis=0)
            sv = bs_ref[...] + _dot(ws_ref[...], vstack)
            sgu_ref[sb * PRE_SUB + ck * CHUNK:sb * PRE_SUB + (ck + 1) * CHUNK, :] = (u[rows] * sv).astype(BF16)

        f = _dot(zcols(OFF_B, OFF_Q).astype(BF16), wdft_ref[...])
        if rope:
            for j in range(nl):
                fsc_ref[j, rs, :] = f[:, j * LANES:(j + 1) * LANES]
        else:
            re_ref[rs, :] = f[:, :B_WIDTH].astype(BF16)
            im_ref[rs, :] = f[:, B_WIDTH:].astype(BF16)

        def rot(t):
            if not rope:
                return t
            return (t * cos_ref[rs, :] + pltpu.roll(t, LANES - 16, 1) * s1_ref[rs, :]
                    + pltpu.roll(t, 16, 1) * s2_ref[rs, :])

        zq = zcols(OFF_Q, OFF_K)
        for j in range(C_WIDTH // LANES):
            q_ref[rs, j * LANES:(j + 1) * LANES] = (
                rot(zq[:, j * LANES:(j + 1) * LANES]) * (HEAD_DIM ** -0.5 * LOG2E)).astype(BF16)

        zkv = zcols(OFF_K, N_IN)
        kk = rot(zkv[:, :KV_WIDTH])
        kr = pltpu.roll(kk, HEAD_DIM, 1)
        k_ref[rs, :LANES] = jnp.where(lo, kk, kr).astype(BF16)
        k_ref[rs, LANES:] = jnp.where(lo, kr, kk).astype(BF16)
        vv = zkv[:, KV_WIDTH:]
        vr = pltpu.roll(vv, HEAD_DIM, 1)
        v_ref[rs, :LANES] = jnp.where(lo, vv, vr).astype(BF16)
        v_ref[rs, LANES:] = jnp.where(lo, vr, vv).astype(BF16)

    if rope:
        nr = tm // GRID_W
        for cc in range(GRID_W):
            for j in range(nl):
                blk = fsc_ref[j, pl.ds(cc, nr, stride=GRID_W), :].astype(BF16)
                dst = re_ref if j < nl // 2 else im_ref
                c0 = cc * B_WIDTH + (j % (nl // 2)) * LANES
                dst[:, c0:c0 + LANES] = blk


def _premix(h, mods, g, win, ws, bsx, gv, hm, wdft, rope_tabs, *, tm, tiles_per_batch, n_pos):
    T = h.shape[0]
    rope = rope_tabs is not None
    if tiles_per_batch is None:
        mrow = lambda i: 4
    else:
        mrow = lambda i: i // tiles_per_batch
    const2 = lambda i: (0, 0)
    in_specs = [
        pl.BlockSpec((tm, D_MODEL), lambda i: (i, 0)),
        pl.BlockSpec((None, None, 1, D_MODEL), lambda i: (mrow(i), 0, 0, 0)),
        pl.BlockSpec((None, None, 1, D_MODEL), lambda i: (mrow(i), 1, 0, 0)),
        pl.BlockSpec((1, D_MODEL), const2),
        pl.BlockSpec((D_MODEL, N_IN), const2),
        pl.BlockSpec((CHUNK, A_HEADS * CHUNK), const2),
        pl.BlockSpec((CHUNK, A_WIDTH), const2),
        pl.BlockSpec((1, A_WIDTH), const2),
        pl.BlockSpec((A_WIDTH, A_WIDTH), const2),
        pl.BlockSpec((B_WIDTH, 2 * B_WIDTH), const2),
    ]
    args = [h, mods, mods, g, win, ws, bsx, gv, hm, wdft]
    if rope:
        nt = n_pos // tm
        for t in rope_tabs:
            in_specs.append(pl.BlockSpec((tm, LANES), lambda i: (i % nt, 0)))
            args.append(t)
    widths = (A_WIDTH, B_WIDTH, B_WIDTH, C_WIDTH, 2 * KV_WIDTH, 2 * KV_WIDTH)
    out_specs = [pl.BlockSpec((tm, w), lambda i: (i, 0)) for w in widths]
    out_shape = [jax.ShapeDtypeStruct((T, w), BF16) for w in widths]
    scratch = []
    if rope:
        for o in (1, 2):
            out_specs[o] = pl.BlockSpec((tm // GRID_W, GRID_W * B_WIDTH), lambda i: (i, 0))
            out_shape[o] = jax.ShapeDtypeStruct((T // GRID_W, GRID_W * B_WIDTH), BF16)
        scratch = [pltpu.VMEM((2 * B_WIDTH // LANES, tm, LANES), F32)]
    return pl.pallas_call(
        functools.partial(_premix_kernel, rope=rope),
        grid=(T // tm,),
        in_specs=in_specs, out_specs=out_specs, out_shape=out_shape, scratch_shapes=scratch,
        compiler_params=_cparams(("parallel",)),
    )(*args)


def _dft1_kernel(m1_ref, twc_ref, tws_ref, re_ref, im_ref, o_ref):
    x = jnp.concatenate([re_ref[0], im_ref[0]], axis=0)
    a = _dot(m1_ref[...], x)
    nr = a.shape[0] // 2
    are, aim = a[:nr], a[nr:]
    c, s = twc_ref[...], tws_ref[...]
    bre = (are * c + aim * s).astype(BF16)
    bim = (aim * c - are * s).astype(BF16)
    for cl in range(o_ref.shape[2]):
        o_ref[0, 0, cl] = bre[:, cl * B_WIDTH:(cl + 1) * B_WIDTH]
        o_ref[0, 1, cl] = bim[:, cl * B_WIDTH:(cl + 1) * B_WIDTH]


def _dft3_kernel(m3_ref, x_ref, o_ref, xs_ref, os_ref):
    kb = x_ref.shape[3]
    nl = B_WIDTH // LANES
    for p in range(2):
        for c in range(GRID_W):
            xc = x_ref[0, p, c].astype(F32)
            for j in range(nl):
                xs_ref[p * nl + j, c * kb:(c + 1) * kb, :] = xc[:, j * LANES:(j + 1) * LANES]
    for k in range(kb):
        rows = [jnp.concatenate([xs_ref[p * nl + j, pl.ds(k, GRID_W, stride=kb), :] for j in range(nl)], axis=1)
                for p in range(2)]
        res = _dot(m3_ref[...], jnp.concatenate(rows, axis=0).astype(BF16))
        for j in range(nl):
            os_ref[j, pl.ds(k, GRID_W, stride=kb), :] = res[:, j * LANES:(j + 1) * LANES]
    for k2 in range(GRID_W):
        o_ref[0, k2] = jnp.concatenate(
            [os_ref[j, k2 * kb:(k2 + 1) * kb, :] for j in range(nl)], axis=1).astype(BF16)


def _seq_dft(re, im, bsz, n_rows, m1, twc, tws, m3):
    ncol = GRID_W * B_WIDTH
    tn = 2048
    re3 = re.reshape(bsz, n_rows, ncol)
    im3 = im.reshape(bsz, n_rows, ncol)
    st1 = pl.pallas_call(
        _dft1_kernel,
        grid=(ncol // tn, bsz),
        in_specs=[pl.BlockSpec((2 * n_rows, 2 * n_rows), lambda j, b: (0, 0)),
                  pl.BlockSpec((n_rows, tn), lambda j, b: (0, j)),
                  pl.BlockSpec((n_rows, tn), lambda j, b: (0, j)),
                  pl.BlockSpec((1, n_rows, tn), lambda j, b: (b, 0, j)),
                  pl.BlockSpec((1, n_rows, tn), lambda j, b: (b, 0, j))],
        out_specs=pl.BlockSpec((1, 2, tn // B_WIDTH, n_rows, B_WIDTH), lambda j, b: (b, 0, j, 0, 0)),
        out_shape=jax.ShapeDtypeStruct((bsz, 2, GRID_W, n_rows, B_WIDTH), BF16),
        compiler_params=_cparams(("parallel", "parallel")),
    )(m1, twc, tws, re3, im3)
    kb = min(n_rows, 16)
    y = pl.pallas_call(
        _dft3_kernel,
        grid=(bsz, n_rows // kb),
        in_specs=[pl.BlockSpec((GRID_W, 2 * GRID_W), lambda b, j: (0, 0)),
                  pl.BlockSpec((1, 2, GRID_W, kb, B_WIDTH), lambda b, j: (b, 0, 0, j, 0))],
        out_specs=pl.BlockSpec((1, GRID_W, kb, B_WIDTH), lambda b, j: (b, 0, j, 0)),
        out_shape=jax.ShapeDtypeStruct((bsz, GRID_W, n_rows, B_WIDTH), BF16),
        scratch_shapes=[pltpu.VMEM((2 * B_WIDTH // LANES, GRID_W * kb, LANES), F32),
                        pltpu.VMEM((B_WIDTH // LANES, GRID_W * kb, LANES), F32)],
        compiler_params=_cparams(("parallel", "parallel")),
    )(m3, st1)
    return y.reshape(bsz * GRID_W * n_rows, B_WIDTH)


def _ctx_dft_kernel(m_ref, re_ref, im_ref, o_ref):
    x = jnp.concatenate([re_ref[0], im_ref[0]], axis=0)
    o_ref[0] = _dot(m_ref[...], x).astype(BF16)


def _ctx_dft(re, im, bsz, m, mat):
    re3 = re.reshape(bsz, m, B_WIDTH)
    im3 = im.reshape(bsz, m, B_WIDTH)
    y = pl.pallas_call(
        _ctx_dft_kernel,
        grid=(bsz,),
        in_specs=[pl.BlockSpec((m, 2 * m), lambda b: (0, 0)),
                  pl.BlockSpec((1, m, B_WIDTH), lambda b: (b, 0, 0)),
                  pl.BlockSpec((1, m, B_WIDTH), lambda b: (b, 0, 0))],
        out_specs=pl.BlockSpec((1, m, B_WIDTH), lambda b: (b, 0, 0)),
        out_shape=jax.ShapeDtypeStruct((bsz, m, B_WIDTH), BF16),
        compiler_params=_cparams(("parallel",)),
    )(mat, re3, im3)
    return y.reshape(bsz * m, B_WIDTH)


ATTN_SUBBLOCKS = 4


def _attn_kernel(sink_ref, *refs, local, nb):
    if local:
        q_ref, kp_ref, kc_ref, kn_ref, vp_ref, vc_ref, vn_ref, kx_ref, vx_ref, o_ref = refs
    else:
        q_ref, kx_ref, vx_ref, o_ref = refs
    tq = BLOCK if local else q_ref.shape[1]
    nsb = q_ref.shape[1] // tq
    lo = lax.broadcasted_iota(jnp.int32, (tq, LANES), 1) < HEAD_DIM
    zero = jnp.zeros((tq, LANES), BF16)
    if local:
        j = pl.program_id(1)
        row = lax.broadcasted_iota(jnp.int32, (BLOCK, BLOCK), 0)
        col = lax.broadcasted_iota(jnp.int32, (BLOCK, BLOCK), 1)

        def key_blocks(p_ref, c_ref, n_ref, gl):
            return ([p_ref[0][:, gl]] + [c_ref[0][sb * BLOCK:(sb + 1) * BLOCK, gl] for sb in range(nsb)]
                    + [n_ref[0][:, gl]])
    jobs = []
    for g in range(2):
        gl = slice(g * LANES, (g + 1) * LANES)
        if local:
            kb = key_blocks(kp_ref, kc_ref, kn_ref, gl)
            vb = key_blocks(vp_ref, vc_ref, vn_ref, gl)
        for sb in range(nsb):
            if local:
                kg = jnp.concatenate(kb[sb:sb + 3] + [kx_ref[0][:, gl]], axis=0)
                vg = jnp.concatenate(vb[sb:sb + 3] + [vx_ref[0][:, gl]], axis=0)
            else:
                kg = kx_ref[0][:, gl]
                vg = vx_ref[0][:, gl]
            q = q_ref[0, sb * tq:(sb + 1) * tq, :]
            qs = []
            for p in range(2):
                qp = q[:, g * 2 * LANES + p * LANES: g * 2 * LANES + (p + 1) * LANES]
                qs.append(jnp.where(lo, qp, zero))
                qs.append(jnp.where(lo, zero, qp))
            q4 = jnp.concatenate(qs, axis=0)
            s4 = lax.dot_general(q4, kg, (((1,), (1,)), ((), ())), preferred_element_type=F32)
            jobs.append((g, sb, s4, vg))
    for g, sb, s4, vg in jobs:
        if local:
            valid_prev = (col >= row) & (j > 0) if sb == 0 else (col >= row)
            valid_next = (col <= row) & (j < nb // nsb - 1) if sb == nsb - 1 else (col <= row)
        ps, rden = [], []
        for hl in range(4):
            s = s4[hl * tq:(hl + 1) * tq]
            if local:
                s = jnp.concatenate(
                    [jnp.where(valid_prev, s[:, :BLOCK], NEG), s[:, BLOCK:2 * BLOCK],
                     jnp.where(valid_next, s[:, 2 * BLOCK:3 * BLOCK], NEG), s[:, 3 * BLOCK:]], axis=1)
            sk = sink_ref[g * 4 + hl] * LOG2E
            m = jnp.maximum(jnp.max(s, axis=-1, keepdims=True), sk)
            p = jnp.exp2(s - m)
            den = jnp.sum(p, axis=-1, keepdims=True) + jnp.exp2(sk - m)
            ps.append(p.astype(BF16))
            rden.append(1.0 / den)
        o4 = _dot(jnp.concatenate(ps, axis=0), vg) * jnp.concatenate(rden, axis=0)
        for p in range(2):
            o_pair = jnp.where(lo, o4[(2 * p) * tq:(2 * p + 1) * tq], o4[(2 * p + 1) * tq:(2 * p + 2) * tq])
            c0 = g * 2 * LANES + p * LANES
            o_ref[0, sb * tq:(sb + 1) * tq, c0:c0 + LANES] = o_pair.astype(BF16)


def _window_attention(q, k, v, kx, vx, sink, bsz, n, m):
    nb = n // BLOCK
    q3 = q.reshape(bsz, n, C_WIDTH)
    k3 = k.reshape(bsz, n, 2 * KV_WIDTH)
    v3 = v.reshape(bsz, n, 2 * KV_WIDTH)
    kx3 = kx.reshape(bsz, m, 2 * KV_WIDTH)
    vx3 = vx.reshape(bsz, m, 2 * KV_WIDTH)
    nsb = ATTN_SUBBLOCKS
    edge = lambda f: pl.BlockSpec((1, BLOCK, 2 * KV_WIDTH), f)
    own = pl.BlockSpec((1, nsb * BLOCK, 2 * KV_WIDTH), lambda b, i, s: (b, i, 0))
    prev = lambda b, i, s: (b, jnp.maximum(i * nsb - 1, 0), 0)
    cur = lambda b, i, s: (b, i, 0)
    nxt = lambda b, i, s: (b, jnp.minimum((i + 1) * nsb, nb - 1), 0)
    ctxs = pl.BlockSpec((1, m, 2 * KV_WIDTH), lambda b, i, s: (b, 0, 0))
    o = pl.pallas_call(
        functools.partial(_attn_kernel, local=True, nb=nb),
        grid_spec=pltpu.PrefetchScalarGridSpec(
            num_scalar_prefetch=1,
            grid=(bsz, nb // nsb),
            in_specs=[pl.BlockSpec((1, nsb * BLOCK, C_WIDTH), cur),
                      edge(prev), own, edge(nxt), edge(prev), own, edge(nxt), ctxs, ctxs],
            out_specs=pl.BlockSpec((1, nsb * BLOCK, C_WIDTH), cur)),
        out_shape=jax.ShapeDtypeStruct((bsz, n, C_WIDTH), BF16),
        compiler_params=_cparams(("parallel", "parallel")),
    )(sink, q3, k3, k3, k3, v3, v3, v3, kx3, vx3)
    return o.reshape(bsz * n, C_WIDTH)


def _context_attention(q, kx, vx, sink, bsz, m):
    q3 = q.reshape(bsz, m, C_WIDTH)
    kx3 = kx.reshape(bsz, m, 2 * KV_WIDTH)
    vx3 = vx.reshape(bsz, m, 2 * KV_WIDTH)
    ctxs = pl.BlockSpec((1, m, 2 * KV_WIDTH), lambda b, s: (b, 0, 0))
    o = pl.pallas_call(
        functools.partial(_attn_kernel, local=False, nb=1),
        grid_spec=pltpu.PrefetchScalarGridSpec(
            num_scalar_prefetch=1,
            grid=(bsz,),
            in_specs=[pl.BlockSpec((1, m, C_WIDTH), lambda b, s: (b, 0, 0)), ctxs, ctxs],
            out_specs=pl.BlockSpec((1, m, C_WIDTH), lambda b, s: (b, 0, 0))),
        out_shape=jax.ShapeDtypeStruct((bsz, m, C_WIDTH), BF16),
        compiler_params=_cparams(("parallel",)),
    )(sink, q3, kx3, vx3)
    return o.reshape(bsz * m, C_WIDTH)


def _postmix_kernel(*refs, route, ncast):
    if ncast:
        n_in = 11 + (2 if route else 0)
        for src, dst in zip(refs[n_in:n_in + ncast], refs[len(refs) - ncast:]):
            dst[...] = src[...].astype(BF16)
        refs = refs[:n_in] + refs[n_in + ncast:len(refs) - ncast]
    h_ref = refs[5]
    for sb in range(h_ref.shape[0] // MOE_TB):
        _postmix_rows(slice(sb * MOE_TB, (sb + 1) * MOE_TB), sb, refs, route)


def _postmix_rows(rows, sb, refs, route):
    (sgu_ref, yf_ref, at_ref, wf_ref, wo_ref, h_ref, gp_ref, gt_ref, gf_ref, sh_ref, sc_ref) = refs[:11]
    if route:
        wr_ref, br_ref, hn_ref, y_ref, cmb_ref, cnt_ref = refs[11:]
    else:
        hn_ref, y_ref = refs[11:]
    fm = _dot(yf_ref[rows, :], wf_ref[...]).astype(BF16)
    o = (_dot(sgu_ref[rows, :], wo_ref[:A_WIDTH, :])
         + _dot(fm, wo_ref[A_WIDTH:A_WIDTH + B_WIDTH, :])
         + _dot(at_ref[rows, :], wo_ref[A_WIDTH + B_WIDTH:, :]))
    ms = jnp.mean(o * o, axis=-1, keepdims=True)
    hn = h_ref[rows, :] + gt_ref[...] * (o * lax.rsqrt(ms + EPS) * gp_ref[...])
    hn_ref[rows, :] = hn
    ms2 = jnp.mean(hn * hn, axis=-1, keepdims=True)
    y = hn * lax.rsqrt(ms2 + EPS) * gf_ref[...] * (1.0 + sc_ref[...]) + sh_ref[...]
    y_ref[rows, :] = y.astype(BF16)
    if route:
        yh = y.astype(BF16)
        yl = (y - yh.astype(F32)).astype(BF16)
        w = wr_ref[...]
        wh = w.astype(BF16)
        wl = (w - wh.astype(F32)).astype(BF16)
        lg = _dot(yh, wh) + _dot(yl, wh) + _dot(yh, wl) + br_ref[...]
        lane = lax.broadcasted_iota(jnp.int32, lg.shape, 1)
        lg = jnp.where(lane < N_EXPERTS, lg, NEG)
        m1 = jnp.max(lg, axis=-1, keepdims=True)
        i1 = jnp.min(jnp.where(lg == m1, lane, LANES), axis=-1, keepdims=True)
        lg2 = jnp.where(lane == i1, NEG, lg)
        m2 = jnp.max(lg2, axis=-1, keepdims=True)
        i2 = jnp.min(jnp.where(lg2 == m2, lane, LANES), axis=-1, keepdims=True)
        e2 = jnp.exp(m2 - m1)
        g1 = 1.0 / (1.0 + e2)
        g2 = e2 * g1
        cmb = jnp.where(lane == i1, g1, 0.0) + jnp.where(lane == i2, g2, 0.0)
        cmb_ref[rows, :] = cmb
        cnt_ref[sb] = jnp.sum(jnp.where(cmb > 0.0, 1.0, 0.0), axis=0, keepdims=True)


def _postmix(sgu, yf, at, wf, wo, h, mods, gp, gf, router, *, tm, tiles_per_batch, side_cast=()):
    T = h.shape[0]
    if tiles_per_batch is None:
        mrow = lambda i: 4
    else:
        mrow = lambda i: i // tiles_per_batch
    const2 = lambda i: (0, 0)
    row = lambda w: pl.BlockSpec((tm, w), lambda i: (i, 0))
    mod = lambda j: pl.BlockSpec((None, None, 1, D_MODEL), lambda i: (mrow(i), j, 0, 0))
    in_specs = [row(A_WIDTH), row(B_WIDTH), row(C_WIDTH),
                pl.BlockSpec((B_WIDTH, B_WIDTH), const2),
                pl.BlockSpec((D_MODEL, D_MODEL), const2),
                row(D_MODEL),
                pl.BlockSpec((1, D_MODEL), const2),
                mod(2),
                pl.BlockSpec((1, D_MODEL), const2),
                mod(3), mod(4)]
    args = [sgu, yf, at, wf, wo, h, gp, mods, gf, mods, mods]
    out_specs = [row(D_MODEL), row(D_MODEL)]
    out_shape = [jax.ShapeDtypeStruct((T, D_MODEL), F32), jax.ShapeDtypeStruct((T, D_MODEL), BF16)]
    route = router is not None
    if route:
        in_specs += [pl.BlockSpec((D_MODEL, LANES), const2), pl.BlockSpec((1, LANES), const2)]
        args += list(router)
        out_specs += [row(LANES), pl.BlockSpec((tm // MOE_TB, 1, LANES), lambda i: (i, 0, 0))]
        out_shape += [jax.ShapeDtypeStruct((T, LANES), F32), jax.ShapeDtypeStruct((T // MOE_TB, 1, LANES), F32)]
    for a in side_cast:
        spec = pl.BlockSpec((a.shape[0] // (T // tm), a.shape[1]), lambda i: (i, 0))
        in_specs.append(spec)
        args.append(a)
        out_specs.append(spec)
        out_shape.append(jax.ShapeDtypeStruct(a.shape, BF16))
    return pl.pallas_call(
        functools.partial(_postmix_kernel, route=route, ncast=len(side_cast)),
        grid=(T // tm,),
        in_specs=in_specs, out_specs=out_specs, out_shape=out_shape,
        compiler_params=_cparams(("parallel",)),
    )(*args)


def _ffn_epilogue(f, h_ref, gp_ref, gt_ref, o_ref):
    ms = jnp.mean(f * f, axis=-1, keepdims=True)
    o_ref[...] = h_ref[...] + gt_ref[...] * (f * lax.rsqrt(ms + EPS) * gp_ref[...])


FF_CHUNK = 512


def _swiglu_tile(y, wg_ref, wu_ref, wd_ref, act_ref):
    for c in range(D_FF // FF_CHUNK):
        cols = slice(c * FF_CHUNK, (c + 1) * FF_CHUNK)
        gate = _dot(y, wg_ref[:, cols])
        up = _dot(y, wu_ref[:, cols])
        act_ref[:, cols] = (gate * jax.nn.sigmoid(gate) * up).astype(BF16)
    return _dot(act_ref[...], wd_ref[...])


def _ffn_dense_kernel(y_ref, wg_ref, wu_ref, wd_ref, h_ref, gp_ref, gt_ref, o_ref, act_ref):
    f = _swiglu_tile(y_ref[...], wg_ref, wu_ref, wd_ref, act_ref)
    _ffn_epilogue(f, h_ref, gp_ref, gt_ref, o_ref)


def _ffn_dense_cast_kernel(y_ref, wg_ref, wu_ref, wd_ref, h_ref, gp_ref, gt_ref, c0_ref, c1_ref, c2_ref,
                           o_ref, d0_ref, d1_ref, d2_ref, act_ref):
    d0_ref[...] = c0_ref[...].astype(BF16)
    d1_ref[...] = c1_ref[...].astype(BF16)
    d2_ref[...] = c2_ref[...].astype(BF16)
    f = _swiglu_tile(y_ref[...], wg_ref, wu_ref, wd_ref, act_ref)
    _ffn_epilogue(f, h_ref, gp_ref, gt_ref, o_ref)


def _ffn_dense(y, wg, wu, wd, h, mods, gp, *, tm, tiles_per_batch, side_cast=()):
    T = h.shape[0]
    nt = T // tm
    if tiles_per_batch is None:
        mrow = lambda i: 4
    else:
        mrow = lambda i: i // tiles_per_batch
    resident = pl.Buffered(1)
    in_specs = [pl.BlockSpec((tm, D_MODEL), lambda i: (i, 0)),
                pl.BlockSpec((D_MODEL, D_FF), lambda i: (0, 0), pipeline_mode=resident),
                pl.BlockSpec((D_MODEL, D_FF), lambda i: (0, 0), pipeline_mode=resident),
                pl.BlockSpec((D_FF, D_MODEL), lambda i: (0, 0), pipeline_mode=resident),
                pl.BlockSpec((tm, D_MODEL), lambda i: (i, 0)),
                pl.BlockSpec((1, D_MODEL), lambda i: (0, 0)),
                pl.BlockSpec((None, None, 1, D_MODEL), lambda i: (mrow(i), 5, 0, 0))]
    out_specs = [pl.BlockSpec((tm, D_MODEL), lambda i: (i, 0))]
    out_shape = [jax.ShapeDtypeStruct((T, D_MODEL), F32)]
    for a in side_cast:
        spec = pl.BlockSpec((a.shape[0] // nt, a.shape[1]), lambda i: (i, 0))
        in_specs.append(spec)
        out_specs.append(spec)
        out_shape.append(jax.ShapeDtypeStruct(a.shape, BF16))
    res = pl.pallas_call(
        _ffn_dense_cast_kernel if side_cast else _ffn_dense_kernel,
        grid=(nt,),
        in_specs=in_specs, out_specs=out_specs, out_shape=out_shape,
        scratch_shapes=[pltpu.VMEM((tm, D_FF), BF16)],
        compiler_params=_cparams(("parallel",)),
    )(y, wg, wu, wd, h, gp, mods, *side_cast)
    return res[0], tuple(res[1:])


MOE_TB = 512
MOE_TM = 512
SUB = 128
NSUB = MOE_TB // SUB
RUN_ALIGN = 16


def _moe_tile_bound(T):
    nt = T // MOE_TB
    rows = 2 * T + nt * N_EXPERTS * (RUN_ALIGN - 1) + N_EXPERTS * (SUB + MOE_TM - 1)
    return -(-rows // MOE_TM)


def _moe_plan(cnt, T):
    p = (cnt + RUN_ALIGN - 1) // RUN_ALIGN * RUN_ALIGN
    base = jnp.cumsum(p, axis=0) - p
    used = jnp.sum(p, axis=0)
    tiles = (used + SUB + MOE_TM - 1) // MOE_TM
    tend = jnp.cumsum(tiles)
    off = (tend - tiles) * MOE_TM
    rowbase = (off[None, :] + base).astype(jnp.int32)
    nsub = (cnt + SUB - 1) // SUB
    slotbase = (jnp.cumsum(nsub, axis=1) - nsub).astype(jnp.int32)
    nt_bound = _moe_tile_bound(T)
    j = jnp.arange(nt_bound, dtype=jnp.int32)
    tile_e = jnp.minimum(jnp.sum(j[:, None] >= tend[None, :], axis=1), N_EXPERTS - 1).astype(jnp.int32)
    nvalid = tend[-1].astype(jnp.int32)
    tile_blk = jnp.minimum(j, nvalid - 1)
    return rowbase, slotbase, tile_e, tile_blk, nvalid.reshape(1)


NSLOT = 2 * MOE_TB // SUB + N_EXPERTS
SLOT_ROWS = NSLOT * SUB
SLOT_CHUNK = 512


def _block_slots(sb_ref, cn_ref, b):
    last = b * N_EXPERTS + N_EXPERTS - 1
    return sb_ref[last] + (cn_ref[last] + SUB - 1) // SUB


def _for_each_run_slot(cn_ref, b, fn):
    for e in range(N_EXPERTS):
        for s in range(NSUB):
            @pl.when(cn_ref[b * N_EXPERTS + e] > SUB * s)
            def _(e=e, s=s):
                fn(e, s)


def _run_copy_out(stage, ys_ref, sems, rb_ref, sb_ref, b, par, e, s):
    slot = sb_ref[b * N_EXPERTS + e] + s
    r0 = pl.multiple_of(rb_ref[b * N_EXPERTS + e] + SUB * s, RUN_ALIGN)
    return pltpu.make_async_copy(stage.at[par, pl.ds(pl.multiple_of(slot * SUB, SUB), SUB), :],
                                 ys_ref.at[pl.ds(r0, SUB), :], sems.at[par, e, s])


def _dispatch_kernel(rb_ref, sb_ref, cn_ref, y_ref, cmb_ref, sbv_ref, ltri_ref, ysin_ref,
                     tok_ref, ys_ref, stage, sems):
    del ysin_ref
    b = pl.program_id(0)
    nb = pl.num_programs(0)
    par = b % 2
    cmb = cmb_ref[...]
    sel = cmb > 0.0
    rk = _dot(ltri_ref[...], jnp.where(sel, 1.0, 0.0).astype(BF16))
    srow = jnp.where(sel, rk + sbv_ref[...], -1.0)
    nsel = jnp.sum(jnp.where(sel, 1.0, 0.0), axis=-1, keepdims=True)
    sa = jnp.max(srow, axis=-1, keepdims=True)
    sb = jnp.where(nsel > 1.5, jnp.sum(jnp.where(sel, srow, 0.0), axis=-1, keepdims=True) - sa, -1.0)
    ga = jnp.sum(jnp.where(srow == sa, cmb, 0.0), axis=-1, keepdims=True)
    gb = jnp.sum(cmb, axis=-1, keepdims=True) - ga
    lane = lax.broadcasted_iota(jnp.int32, cmb.shape, 1)
    tok = jnp.where(lane == 0, sa, jnp.where(lane == 1, sb, jnp.where(lane == 2, ga, jnp.where(lane == 3, gb, 0.0))))
    tok_ref[...] = tok
    tok_t = tok.T
    nslots = _block_slots(sb_ref, cn_ref, b)
    rr0 = lax.broadcasted_iota(jnp.int32, (SLOT_CHUNK, cmb.shape[0]), 0).astype(F32)
    for c in range(SLOT_ROWS // SLOT_CHUNK):
        @pl.when(nslots * SUB > c * SLOT_CHUNK)
        def _(c=c):
            rr = rr0 + float(c * SLOT_CHUNK)
            g = jnp.where((rr == tok_t[0:1, :]) | (rr == tok_t[1:2, :]), 1.0, 0.0).astype(BF16)
            stage[par, c * SLOT_CHUNK:(c + 1) * SLOT_CHUNK, :] = _dot(g, y_ref[...]).astype(BF16)

    @pl.when(b > 0)
    def _():
        _for_each_run_slot(cn_ref, b - 1, lambda e, s: _run_copy_out(
            stage, ys_ref, sems, rb_ref, sb_ref, b - 1, 1 - par, e, s).wait())

    _for_each_run_slot(cn_ref, b, lambda e, s: _run_copy_out(
        stage, ys_ref, sems, rb_ref, sb_ref, b, par, e, s).start())

    @pl.when(b == nb - 1)
    def _():
        _for_each_run_slot(cn_ref, b, lambda e, s: _run_copy_out(
            stage, ys_ref, sems, rb_ref, sb_ref, b, par, e, s).wait())


def _moe_dispatch(y, cmb, rowbase, slotbase, cnt, nt_bound):
    T = y.shape[0]
    nt = T // MOE_TB
    sbv = jnp.zeros((nt, 1, LANES), F32).at[:, 0, :N_EXPERTS].set((slotbase * SUB).astype(F32))
    ltri = jnp.asarray(np.tril(np.ones((MOE_TB, MOE_TB), np.float32), -1)).astype(BF16)
    ys0 = jnp.zeros((nt_bound * MOE_TM, D_MODEL), BF16)
    imap2 = lambda b, rb, sb, cn: (b, 0)
    return pl.pallas_call(
        _dispatch_kernel,
        grid_spec=pltpu.PrefetchScalarGridSpec(
            num_scalar_prefetch=3,
            grid=(nt,),
            in_specs=[pl.BlockSpec((MOE_TB, D_MODEL), imap2),
                      pl.BlockSpec((MOE_TB, LANES), imap2),
                      pl.BlockSpec((None, 1, LANES), lambda b, rb, sb, cn: (b, 0, 0)),
                      pl.BlockSpec((MOE_TB, MOE_TB), lambda b, rb, sb, cn: (0, 0)),
                      pl.BlockSpec(memory_space=pl.ANY)],
            out_specs=[pl.BlockSpec((MOE_TB, LANES), imap2),
                       pl.BlockSpec(memory_space=pl.ANY)],
            scratch_shapes=[pltpu.VMEM((2, SLOT_ROWS, D_MODEL), BF16),
                            pltpu.SemaphoreType.DMA((2, N_EXPERTS, NSUB))]),
        out_shape=[jax.ShapeDtypeStruct((T, LANES), F32),
                   jax.ShapeDtypeStruct((nt_bound * MOE_TM, D_MODEL), BF16)],
        input_output_aliases={7: 1},
        compiler_params=_cparams(("arbitrary",)),
    )(rowbase.reshape(-1), slotbase.reshape(-1), cnt.reshape(-1), y, cmb, sbv, ltri, ys0)


def _ffn_group_kernel(te_ref, tb_ref, nv_ref, y_ref, wg_ref, wu_ref, wd_ref, o_ref, act_ref):
    del te_ref, tb_ref
    j = pl.program_id(0)

    @pl.when(j < nv_ref[0])
    def _():
        o_ref[...] = _swiglu_tile(y_ref[...], wg_ref, wu_ref, wd_ref, act_ref).astype(BF16)

    @pl.when(j >= nv_ref[0])
    def _():
        o_ref[...] = jnp.zeros_like(o_ref)


def _ffn_group(ys, wg, wu, wd, tile_e, tile_blk, nvalid):
    rows = ys.shape[0]
    resident = pl.Buffered(2)
    return pl.pallas_call(
        _ffn_group_kernel,
        grid_spec=pltpu.PrefetchScalarGridSpec(
            num_scalar_prefetch=3,
            grid=(rows // MOE_TM,),
            in_specs=[pl.BlockSpec((MOE_TM, D_MODEL), lambda j, te, tb, nv: (tb[j], 0)),
                      pl.BlockSpec((None, D_MODEL, D_FF), lambda j, te, tb, nv: (te[j], 0, 0), pipeline_mode=resident),
                      pl.BlockSpec((None, D_MODEL, D_FF), lambda j, te, tb, nv: (te[j], 0, 0), pipeline_mode=resident),
                      pl.BlockSpec((None, D_FF, D_MODEL), lambda j, te, tb, nv: (te[j], 0, 0), pipeline_mode=resident)],
            out_specs=pl.BlockSpec((MOE_TM, D_MODEL), lambda j, te, tb, nv: (j, 0)),
            scratch_shapes=[pltpu.VMEM((MOE_TM, D_FF), BF16)]),
        out_shape=jax.ShapeDtypeStruct((rows, D_MODEL), BF16),
        compiler_params=_cparams(("arbitrary",)),
    )(tile_e, tile_blk, nvalid, ys, wg, wu, wd)


def _run_copy_in(fs_ref, fbuf, sems, rb_ref, sb_ref, b, par, e, s):
    slot = sb_ref[b * N_EXPERTS + e] + s
    r0 = pl.multiple_of(rb_ref[b * N_EXPERTS + e] + SUB * s, RUN_ALIGN)
    return pltpu.make_async_copy(fs_ref.at[pl.ds(r0, SUB), :],
                                 fbuf.at[par, pl.ds(pl.multiple_of(slot * SUB, SUB), SUB), :], sems.at[par, e, s])


def _combine_kernel(rb_ref, sb_ref, cn_ref, tok_ref, h_ref, gp_ref, gt_ref, fs_ref, o_ref, fbuf, sems):
    b = pl.program_id(0)
    nb = pl.num_programs(0)
    par = b % 2

    @pl.when(b == 0)
    def _():
        fbuf[...] = jnp.zeros_like(fbuf)
        _for_each_run_slot(cn_ref, b, lambda e, s: _run_copy_in(
            fs_ref, fbuf, sems, rb_ref, sb_ref, b, par, e, s).start())

    @pl.when(b + 1 < nb)
    def _():
        _for_each_run_slot(cn_ref, b + 1, lambda e, s: _run_copy_in(
            fs_ref, fbuf, sems, rb_ref, sb_ref, b + 1, 1 - par, e, s).start())

    tok = tok_ref[...]
    sa, sb, ga, gb = tok[:, 0:1], tok[:, 1:2], tok[:, 2:3], tok[:, 3:4]
    _for_each_run_slot(cn_ref, b, lambda e, s: _run_copy_in(
        fs_ref, fbuf, sems, rb_ref, sb_ref, b, par, e, s).wait())
    nslots = _block_slots(sb_ref, cn_ref, b)
    nck = SLOT_ROWS // SLOT_CHUNK
    for c in range(1, nck + 1):
        lo_rows, hi_rows = (c - 1) * SLOT_CHUNK, c * SLOT_CHUNK
        cond = (nslots * SUB > lo_rows) if c == nck else ((nslots * SUB > lo_rows) & (nslots * SUB <= hi_rows))
        if c == 1:
            cond = nslots * SUB <= hi_rows

        @pl.when(cond)
        def _(depth=hi_rows):
            cc = lax.broadcasted_iota(jnp.int32, (tok.shape[0], depth), 1).astype(F32)
            pick = jnp.where(cc == sa, ga, jnp.where(cc == sb, gb, 0.0)).astype(BF16)
            f = _dot(pick, fbuf[par, :depth, :])
            _ffn_epilogue(f, h_ref, gp_ref, gt_ref, o_ref)


def _moe_combine(fs, tok, h, mods, gp, rowbase, slotbase, cnt, *, tiles_per_batch):
    T = h.shape[0]
    nt = T // MOE_TB
    imap2 = lambda b, rb, sb, cn: (b, 0)
    return pl.pallas_call(
        _combine_kernel,
        grid_spec=pltpu.PrefetchScalarGridSpec(
            num_scalar_prefetch=3,
            grid=(nt,),
            in_specs=[pl.BlockSpec((MOE_TB, LANES), imap2),
                      pl.BlockSpec((MOE_TB, D_MODEL), imap2),
                      pl.BlockSpec((1, D_MODEL), lambda b, rb, sb, cn: (0, 0)),
                      pl.BlockSpec((None, None, 1, D_MODEL), lambda b, rb, sb, cn: (b // tiles_per_batch, 5, 0, 0)),
                      pl.BlockSpec(memory_space=pl.ANY)],
            out_specs=pl.BlockSpec((MOE_TB, D_MODEL), imap2),
            scratch_shapes=[pltpu.VMEM((2, SLOT_ROWS, D_MODEL), BF16),
                            pltpu.SemaphoreType.DMA((2, N_EXPERTS, NSUB))]),
        out_shape=jax.ShapeDtypeStruct((T, D_MODEL), F32),
        compiler_params=_cparams(("arbitrary",)),
    )(rowbase.reshape(-1), slotbase.reshape(-1), cnt.reshape(-1), tok, h, gp, mods, fs)


def _ffn_moe(y, cmb, cnt_tiles, wg, wu, wd, h, mods, gp, *, tiles_per_batch):
    T = h.shape[0]
    cnt = cnt_tiles.reshape(T // MOE_TB, LANES)[:, :N_EXPERTS].astype(jnp.int32)
    rowbase, slotbase, tile_e, tile_blk, nvalid = _moe_plan(cnt, T)
    tok, ys = _moe_dispatch(y, cmb, rowbase, slotbase, cnt, _moe_tile_bound(T))
    fs = _ffn_group(ys, wg, wu, wd, tile_e, tile_blk, nvalid)
    return _moe_combine(fs, tok, h, mods, gp, rowbase, slotbase, cnt, tiles_per_batch=tiles_per_batch)


def _blockdiag(m, reps):
    n = m.shape[0]
    out = np.zeros((n * reps, n * reps), np.float64)
    for r in range(reps):
        out[r * n:(r + 1) * n, r * n:(r + 1) * n] = m
    return out


def _dft_tables(n_rows):
    n = n_rows * GRID_W
    r = np.arange(n_rows)
    c = np.arange(GRID_W)
    a1 = 2 * np.pi * np.outer(r, r) / n_rows
    c1, s1 = np.cos(a1), np.sin(a1)
    m1 = np.block([[c1, s1], [-s1, c1]]) * 0.125
    at = 2 * np.pi * np.outer(r, c) / n
    twc = np.repeat(np.cos(at), B_WIDTH, axis=1)
    tws = np.repeat(np.sin(at), B_WIDTH, axis=1)
    a3 = 2 * np.pi * np.outer(c, c) / GRID_W
    m3 = np.concatenate([np.cos(a3), np.sin(a3)], axis=1) * (8.0 / np.sqrt(n))
    f32 = lambda t: jnp.asarray(t.astype(np.float32))
    return f32(m1).astype(BF16), f32(twc), f32(tws), f32(m3).astype(BF16)


def _channel_dft_table():
    d = np.arange(HEAD_DIM)
    a = 2 * np.pi * np.outer(d, d) / HEAD_DIM
    w = np.concatenate([_blockdiag(np.cos(a), 4), -_blockdiag(np.sin(a), 4)], axis=1) * 0.125
    return jnp.asarray(w.astype(np.float32)).astype(BF16)


def _ctx_dft_table(m):
    p = np.arange(m)
    a = 2 * np.pi * np.outer(p, p) / m
    w = np.concatenate([np.cos(a), np.sin(a)], axis=1) * (8.0 / np.sqrt(m * HEAD_DIM))
    return jnp.asarray(w.astype(np.float32)).astype(BF16)


def _rope_tables(n_tok):
    rows = n_tok // GRID_W
    row = jnp.broadcast_to(jnp.arange(rows)[:, None], (rows, GRID_W)).reshape(-1)
    col = jnp.broadcast_to(jnp.arange(GRID_W)[None, :], (rows, GRID_W)).reshape(-1)
    half = HEAD_DIM // 2
    inv = ROPE_BASE ** (-jnp.arange(0, half, 2, dtype=F32) / half)
    ang = jnp.stack([row.astype(F32)[:, None] * inv, col.astype(F32)[:, None] * inv], axis=1)
    cos, sin = jnp.cos(ang), jnp.sin(ang)
    zer = jnp.zeros_like(sin)
    lay = lambda a, b: jnp.tile(jnp.stack([a, b], axis=2).reshape(n_tok, HEAD_DIM), (1, LANES // HEAD_DIM))
    return lay(cos, cos), lay(-sin, zer), lay(zer, sin)


def kernel(x, c, ctx, c_ctx, w_ada, b_ada, g_mix_pre, g_mix_post, g_ffn_pre, g_ffn_post,
           w_in, w_s, b_s, g_v, w_f, sink, w_out, w_gate_d, w_up_d, w_down_d,
           w_router, b_router, w_gate_e, w_up_e, w_down_e):
    bsz, n_lat, _ = x.shape
    n_ctx = ctx.shape[1]
    T, Tc = bsz * n_lat, bsz * n_ctx
    tm = 512
    tpb = n_lat // tm

    rope_tabs = _rope_tables(n_lat)
    m1, twc, tws, m3 = _dft_tables(n_lat // GRID_W)
    wdft = _channel_dft_table()
    mctx = _ctx_dft_table(n_ctx)
    hm = jnp.asarray(_blockdiag(np.full((HEAD_DIM, HEAD_DIM), 1.0 / HEAD_DIM), A_HEADS).astype(np.float32)).astype(BF16)

    cond8 = jnp.zeros((8, D_MODEL), F32).at[:bsz].set(c).at[4].set(c_ctx)
    h = x.reshape(T, D_MODEL)
    hc = ctx.reshape(Tc, D_MODEL)

    mods_all = _adaln(cond8, w_ada, b_ada[:, None, :]).reshape(DEPTH, 8, 6, 1, D_MODEL)

    for i in range(DEPTH):
        last = i == DEPTH - 1
        mods = mods_all[i]
        win = w_in[i].astype(BF16)
        ws = jnp.transpose(w_s[i], (1, 0, 2)).reshape(CHUNK, A_HEADS * CHUNK).astype(BF16)
        bsx = jnp.repeat(b_s[i].T, HEAD_DIM, axis=1)
        gv = g_v[i].reshape(1, A_WIDTH)
        wf = jax.scipy.linalg.block_diag(*[w_f[i][g] for g in range(4)]).astype(BF16)
        wo = w_out[i].astype(BF16)
        gpre = g_mix_pre[i][None, :]
        gpost = g_mix_post[i][None, :]
        gfpre = g_ffn_pre[i][None, :]
        gfpost = g_ffn_post[i][None, :]
        sk = sink[i]

        sgu, fre, fim, q, k, v = _premix(h, mods, gpre, win, ws, bsx, gv, hm, wdft, rope_tabs,
                                         tm=2 * tm, tiles_per_batch=tpb // 2, n_pos=n_lat)
        sguc, frec, fimc, qc, kc, vc = _premix(hc, mods, gpre, win, ws, bsx, gv, hm, wdft, None,
                                               tm=tm, tiles_per_batch=None, n_pos=n_ctx)
        yf = _seq_dft(fre, fim, bsz, n_lat // GRID_W, m1, twc, tws, m3)
        at = _window_attention(q, k, v, kc, vc, sk, bsz, n_lat, n_ctx)

        if i % 2 == 0:
            router = None
        else:
            j = i // 2
            wr = jnp.zeros((D_MODEL, LANES), F32).at[:, :N_EXPERTS].set(w_router[j])
            br = jnp.zeros((1, LANES), F32).at[0, :N_EXPERTS].set(b_router[j])
            router = (wr, br)
        dense_w = (w_gate_d[i // 2], w_up_d[i // 2], w_down_d[i // 2]) if i % 2 == 0 else ()
        res = _postmix(sgu, yf, at, wf, wo, h, mods, gpost, gfpre, router, tm=2 * tm, tiles_per_batch=tpb // 2,
                       side_cast=dense_w)
        if not last:
            yfc = _ctx_dft(frec, fimc, bsz, n_ctx, mctx)
            atc = _context_attention(qc, kc, vc, sk, bsz, n_ctx)
            resc = _postmix(sguc, yfc, atc, wf, wo, hc, mods, gpost, gfpre, router, tm=2 * tm, tiles_per_batch=None)

        j = i // 2
        if i % 2 == 0:
            wg, wu, wd = res[-3:]
            side = ()
            if not last:
                jn = (i + 1) // 2
                side = (w_gate_e[jn].reshape(N_EXPERTS * D_MODEL, D_FF), w_up_e[jn].reshape(N_EXPERTS * D_MODEL, D_FF),
                        w_down_e[jn].reshape(N_EXPERTS * D_FF, D_MODEL))
            h, expert_w = _ffn_dense(res[1], wg, wu, wd, res[0], mods, gfpost, tm=tm, tiles_per_batch=tpb,
                                     side_cast=side)
            if not last:
                hc, _ = _ffn_dense(resc[1], wg, wu, wd, resc[0], mods, gfpost, tm=tm, tiles_per_batch=None)
        else:
            wg = expert_w[0].reshape(N_EXPERTS, D_MODEL, D_FF)
            wu = expert_w[1].reshape(N_EXPERTS, D_MODEL, D_FF)
            wd = expert_w[2].reshape(N_EXPERTS, D_FF, D_MODEL)
            assert last and tm == MOE_TB, "the expert FFN is only built for the final layer's latent tokens"
            h = _ffn_moe(res[1], res[2], res[3], wg, wu, wd, res[0], mods, gfpost,
                         tiles_per_batch=n_lat // MOE_TB)
    return h.reshape(bsz, n_lat, D_MODEL)
```

```python
import functools

import numpy as np
import jax
import jax.numpy as jnp
from jax import lax
from jax.experimental import pallas as pl
from jax.experimental.pallas import tpu as pltpu

D_MODEL = 1024
DEPTH = 2
GRID_W = 64
HEAD_DIM = 64
EPS = 1e-6
A_HEADS = 4
A_WIDTH = 256
CHUNK = 128
B_WIDTH = 256
C_Q_HEADS = 8
C_WIDTH = 512
KV_WIDTH = 128
WINDOW = 128
BLOCK = 128
ROPE_BASE = 10000.0
OFF_B = 512
OFF_Q = 768
OFF_K = 1280
OFF_V = 1408
N_IN = 1536
D_FF = 3584
N_EXPERTS = 8

LANES = 128
VMEM_LIMIT = 60 * 1024 * 1024
NEG = -1e30
LOG2E = 1.4426950408889634
assert WINDOW == BLOCK

F32 = jnp.float32
BF16 = jnp.bfloat16


def _dot(a, b):
    return jnp.dot(a, b, preferred_element_type=F32)


def _cparams(sem):
    return pltpu.CompilerParams(dimension_semantics=sem, vmem_limit_bytes=VMEM_LIMIT)


def _adaln_kernel(c_ref, w_ref, b_ref, o_ref):
    c = c_ref[...]
    s = (c * jax.nn.sigmoid(c)).astype(BF16)
    o_ref[...] = _dot(s, w_ref[...].astype(BF16)) + b_ref[...]


def _adaln(cond8, w, b):
    tn = 1536
    nl = w.shape[0]
    return pl.pallas_call(
        _adaln_kernel,
        grid=(nl, 6 * D_MODEL // tn),
        in_specs=[pl.BlockSpec((8, D_MODEL), lambda l, j: (0, 0)),
                  pl.BlockSpec((None, D_MODEL, tn), lambda l, j: (l, 0, j)),
                  pl.BlockSpec((None, 1, tn), lambda l, j: (l, 0, j))],
        out_specs=pl.BlockSpec((None, 8, tn), lambda l, j: (l, 0, j)),
        out_shape=jax.ShapeDtypeStruct((nl, 8, 6 * D_MODEL), F32),
        compiler_params=_cparams(("arbitrary", "arbitrary")),
    )(cond8, w, b)


def _gelu_tanh(x):
    return 0.5 * x * (1.0 + jnp.tanh(0.7978845608028654 * (x + 0.044715 * x * x * x)))


PRE_SUB = 512


def _premix_kernel(*refs, rope):
    if rope:
        (h_ref, sh_ref, sc_ref, g_ref, win_ref, ws_ref, bs_ref, gv_ref, hm_ref, wdft_ref,
         cos_ref, s1_ref, s2_ref, sgu_ref, re_ref, im_ref, q_ref, k_ref, v_ref, fsc_ref) = refs
    else:
        (h_ref, sh_ref, sc_ref, g_ref, win_ref, ws_ref, bs_ref, gv_ref, hm_ref, wdft_ref,
         sgu_ref, re_ref, im_ref, q_ref, k_ref, v_ref) = refs
    tm = h_ref.shape[0]
    nl = 2 * B_WIDTH // LANES
    head = lax.broadcasted_iota(jnp.int32, (CHUNK, A_WIDTH), 1) // HEAD_DIM
    lo = lax.broadcasted_iota(jnp.int32, (PRE_SUB, LANES), 1) < HEAD_DIM
    for sb in range(tm // PRE_SUB):
        rs = slice(sb * PRE_SUB, (sb + 1) * PRE_SUB)
        x = h_ref[rs, :]
        ms = jnp.mean(x * x, axis=-1, keepdims=True)
        xn = x * lax.rsqrt(ms + EPS) * g_ref[...]
        xm = (xn * (1.0 + sc_ref[...]) + sh_ref[...]).astype(BF16)
        z = _dot(xm, win_ref[...])
        zcols = lambda c0, c1: z[:, c0:c1]

        a = _gelu_tanh(zcols(0, OFF_B))
        u = a[:, :A_WIDTH]
        v = a[:, A_WIDTH:]
        msv = _dot((v * v).astype(BF16), hm_ref[...])
        vn = (v * lax.rsqrt(msv + EPS) * gv_ref[...]).astype(BF16)
        for ck in range(PRE_SUB // CHUNK):
            rows = slice(ck * CHUNK, (ck + 1) * CHUNK)
            vc = vn[rows]
            vstack = jnp.concatenate([jnp.where(head == hh, vc, jnp.zeros_like(vc)) for hh in range(A_HEADS)], axis=0)
            sv = bs_ref[...] + _dot(ws_ref[...], vstack)
            sgu_ref[sb * PRE_SUB + ck * CHUNK:sb * PRE_SUB + (ck + 1) * CHUNK, :] = (u[rows] * sv).astype(BF16)

        f = _dot(zcols(OFF_B, OFF_Q).astype(BF16), wdft_ref[...])
        if rope:
            for j in range(nl):
                fsc_ref[j, rs, :] = f[:, j * LANES:(j + 1) * LANES]
        else:
            re_ref[rs, :] = f[:, :B_WIDTH].astype(BF16)
            im_ref[rs, :] = f[:, B_WIDTH:].astype(BF16)

        def rot(t):
            if not rope:
                return t
            return (t * cos_ref[rs, :] + pltpu.roll(t, LANES - 16, 1) * s1_ref[rs, :]
                    + pltpu.roll(t, 16, 1) * s2_ref[rs, :])

        zq = zcols(OFF_Q, OFF_K)
        for j in range(C_WIDTH // LANES):
            q_ref[rs, j * LANES:(j + 1) * LANES] = (
                rot(zq[:, j * LANES:(j + 1) * LANES]) * (HEAD_DIM ** -0.5 * LOG2E)).astype(BF16)

        zkv = zcols(OFF_K, N_IN)
        kk = rot(zkv[:, :KV_WIDTH])
        kr = pltpu.roll(kk, HEAD_DIM, 1)
        k_ref[rs, :LANES] = jnp.where(lo, kk, kr).astype(BF16)
        k_ref[rs, LANES:] = jnp.where(lo, kr, kk).astype(BF16)
        vv = zkv[:, KV_WIDTH:]
        vr = pltpu.roll(vv, HEAD_DIM, 1)
        v_ref[rs, :LANES] = jnp.where(lo, vv, vr).astype(BF16)
        v_ref[rs, LANES:] = jnp.where(lo, vr, vv).astype(BF16)

    if rope:
        nr = tm // GRID_W
        for cc in range(GRID_W):
            for j in range(nl):
                blk = fsc_ref[j, pl.ds(cc, nr, stride=GRID_W), :].astype(BF16)
                dst = re_ref if j < nl // 2 else im_ref
                c0 = cc * B_WIDTH + (j % (nl // 2)) * LANES
                dst[:, c0:c0 + LANES] = blk


def _premix(h, mods, g, win, ws, bsx, gv, hm, wdft, rope_tabs, *, tm, tiles_per_batch, n_pos):
    T = h.shape[0]
    rope = rope_tabs is not None
    if tiles_per_batch is None:
        mrow = lambda i: 4
    else:
        mrow = lambda i: i // tiles_per_batch
    const2 = lambda i: (0, 0)
    in_specs = [
        pl.BlockSpec((tm, D_MODEL), lambda i: (i, 0)),
        pl.BlockSpec((None, None, 1, D_MODEL), lambda i: (mrow(i), 0, 0, 0)),
        pl.BlockSpec((None, None, 1, D_MODEL), lambda i: (mrow(i), 1, 0, 0)),
        pl.BlockSpec((1, D_MODEL), const2),
        pl.BlockSpec((D_MODEL, N_IN), const2),
        pl.BlockSpec((CHUNK, A_HEADS * CHUNK), const2),
        pl.BlockSpec((CHUNK, A_WIDTH), const2),
        pl.BlockSpec((1, A_WIDTH), const2),
        pl.BlockSpec((A_WIDTH, A_WIDTH), const2),
        pl.BlockSpec((B_WIDTH, 2 * B_WIDTH), const2),
    ]
    args = [h, mods, mods, g, win, ws, bsx, gv, hm, wdft]
    if rope:
        nt = n_pos // tm
        for t in rope_tabs:
            in_specs.append(pl.BlockSpec((tm, LANES), lambda i: (i % nt, 0)))
            args.append(t)
    widths = (A_WIDTH, B_WIDTH, B_WIDTH, C_WIDTH, 2 * KV_WIDTH, 2 * KV_WIDTH)
    out_specs = [pl.BlockSpec((tm, w), lambda i: (i, 0)) for w in widths]
    out_shape = [jax.ShapeDtypeStruct((T, w), BF16) for w in widths]
    scratch = []
    if rope:
        for o in (1, 2):
            out_specs[o] = pl.BlockSpec((tm // GRID_W, GRID_W * B_WIDTH), lambda i: (i, 0))
            out_shape[o] = jax.ShapeDtypeStruct((T // GRID_W, GRID_W * B_WIDTH), BF16)
        scratch = [pltpu.VMEM((2 * B_WIDTH // LANES, tm, LANES), F32)]
    return pl.pallas_call(
        functools.partial(_premix_kernel, rope=rope),
        grid=(T // tm,),
        in_specs=in_specs, out_specs=out_specs, out_shape=out_shape, scratch_shapes=scratch,
        compiler_params=_cparams(("parallel",)),
    )(*args)


def _dft1_kernel(m1_ref, twc_ref, tws_ref, re_ref, im_ref, o_ref):
    x = jnp.concatenate([re_ref[0], im_ref[0]], axis=0)
    a = _dot(m1_ref[...], x)
    nr = a.shape[0] // 2
    are, aim = a[:nr], a[nr:]
    c, s = twc_ref[...], tws_ref[...]
    bre = (are * c + aim * s).astype(BF16)
    bim = (aim * c - are * s).astype(BF16)
    for cl in range(o_ref.shape[2]):
        o_ref[0, 0, cl] = bre[:, cl * B_WIDTH:(cl + 1) * B_WIDTH]
        o_ref[0, 1, cl] = bim[:, cl * B_WIDTH:(cl + 1) * B_WIDTH]


def _dft3_kernel(m3_ref, x_ref, o_ref, xs_ref, os_ref):
    kb = x_ref.shape[3]
    nl = B_WIDTH // LANES
    for p in range(2):
        for c in range(GRID_W):
            xc = x_ref[0, p, c].astype(F32)
            for j in range(nl):
                xs_ref[p * nl + j, c * kb:(c + 1) * kb, :] = xc[:, j * LANES:(j + 1) * LANES]
    for k in range(kb):
        rows = [jnp.concatenate([xs_ref[p * nl + j, pl.ds(k, GRID_W, stride=kb), :] for j in range(nl)], axis=1)
                for p in range(2)]
        res = _dot(m3_ref[...], jnp.concatenate(rows, axis=0).astype(BF16))
        for j in range(nl):
            os_ref[j, pl.ds(k, GRID_W, stride=kb), :] = res[:, j * LANES:(j + 1) * LANES]
    for k2 in range(GRID_W):
        o_ref[0, k2] = jnp.concatenate(
            [os_ref[j, k2 * kb:(k2 + 1) * kb, :] for j in range(nl)], axis=1).astype(BF16)


def _seq_dft(re, im, bsz, n_rows, m1, twc, tws, m3):
    ncol = GRID_W * B_WIDTH
    tn = 2048
    re3 = re.reshape(bsz, n_rows, ncol)
    im3 = im.reshape(bsz, n_rows, ncol)
    st1 = pl.pallas_call(
        _dft1_kernel,
        grid=(ncol // tn, bsz),
        in_specs=[pl.BlockSpec((2 * n_rows, 2 * n_rows), lambda j, b: (0, 0)),
                  pl.BlockSpec((n_rows, tn), lambda j, b: (0, j)),
                  pl.BlockSpec((n_rows, tn), lambda j, b: (0, j)),
                  pl.BlockSpec((1, n_rows, tn), lambda j, b: (b, 0, j)),
                  pl.BlockSpec((1, n_rows, tn), lambda j, b: (b, 0, j))],
        out_specs=pl.BlockSpec((1, 2, tn // B_WIDTH, n_rows, B_WIDTH), lambda j, b: (b, 0, j, 0, 0)),
        out_shape=jax.ShapeDtypeStruct((bsz, 2, GRID_W, n_rows, B_WIDTH), BF16),
        compiler_params=_cparams(("parallel", "parallel")),
    )(m1, twc, tws, re3, im3)
    kb = min(n_rows, 16)
    y = pl.pallas_call(
        _dft3_kernel,
        grid=(bsz, n_rows // kb),
        in_specs=[pl.BlockSpec((GRID_W, 2 * GRID_W), lambda b, j: (0, 0)),
                  pl.BlockSpec((1, 2, GRID_W, kb, B_WIDTH), lambda b, j: (b, 0, 0, j, 0))],
        out_specs=pl.BlockSpec((1, GRID_W, kb, B_WIDTH), lambda b, j: (b, 0, j, 0)),
        out_shape=jax.ShapeDtypeStruct((bsz, GRID_W, n_rows, B_WIDTH), BF16),
        scratch_shapes=[pltpu.VMEM((2 * B_WIDTH // LANES, GRID_W * kb, LANES), F32),
                        pltpu.VMEM((B_WIDTH // LANES, GRID_W * kb, LANES), F32)],
        compiler_params=_cparams(("parallel", "parallel")),
    )(m3, st1)
    return y.reshape(bsz * GRID_W * n_rows, B_WIDTH)


def _ctx_dft_kernel(m_ref, re_ref, im_ref, o_ref):
    x = jnp.concatenate([re_ref[0], im_ref[0]], axis=0)
    o_ref[0] = _dot(m_ref[...], x).astype(BF16)


def _ctx_dft(re, im, bsz, m, mat):
    re3 = re.reshape(bsz, m, B_WIDTH)
    im3 = im.reshape(bsz, m, B_WIDTH)
    y = pl.pallas_call(
        _ctx_dft_kernel,
        grid=(bsz,),
        in_specs=[pl.BlockSpec((m, 2 * m), lambda b: (0, 0)),
                  pl.BlockSpec((1, m, B_WIDTH), lambda b: (b, 0, 0)),
                  pl.BlockSpec((1, m, B_WIDTH), lambda b: (b, 0, 0))],
        out_specs=pl.BlockSpec((1, m, B_WIDTH), lambda b: (b, 0, 0)),
        out_shape=jax.ShapeDtypeStruct((bsz, m, B_WIDTH), BF16),
        compiler_params=_cparams(("parallel",)),
    )(mat, re3, im3)
    return y.reshape(bsz * m, B_WIDTH)


ATTN_SUBBLOCKS = 4


def _attn_kernel(sink_ref, *refs, local, nb):
    if local:
        q_ref, kp_ref, kc_ref, kn_ref, vp_ref, vc_ref, vn_ref, kx_ref, vx_ref, o_ref = refs
    else:
        q_ref, kx_ref, vx_ref, o_ref = refs
    tq = BLOCK if local else q_ref.shape[1]
    nsb = q_ref.shape[1] // tq
    lo = lax.broadcasted_iota(jnp.int32, (tq, LANES), 1) < HEAD_DIM
    zero = jnp.zeros((tq, LANES), BF16)
    lo4 = lax.broadcasted_iota(jnp.int32, (4 * tq, LANES), 1) < HEAD_DIM
    if local:
        j = pl.program_id(1)
        row = lax.broadcasted_iota(jnp.int32, (BLOCK, BLOCK), 0)
        col = lax.broadcasted_iota(jnp.int32, (BLOCK, BLOCK), 1)

        def key_blocks(p_ref, c_ref, n_ref, gl):
            return ([p_ref[0][:, gl]] + [c_ref[0][sb * BLOCK:(sb + 1) * BLOCK, gl] for sb in range(nsb)]
                    + [n_ref[0][:, gl]])
    jobs = []
    for g in range(2):
        gl = slice(g * LANES, (g + 1) * LANES)
        if local:
            kb = key_blocks(kp_ref, kc_ref, kn_ref, gl)
            vb = key_blocks(vp_ref, vc_ref, vn_ref, gl)
        for sb in range(nsb):
            if local:
                kg = jnp.concatenate(kb[sb:sb + 3] + [kx_ref[0][:, gl]], axis=0)
                vg = jnp.concatenate(vb[sb:sb + 3] + [vx_ref[0][:, gl]], axis=0)
            else:
                kg = kx_ref[0][:, gl]
                vg = vx_ref[0][:, gl]
            q = q_ref[0, sb * tq:(sb + 1) * tq, :]
            qs = []
            for p in range(2):
                qp = q[:, g * 2 * LANES + p * LANES: g * 2 * LANES + (p + 1) * LANES]
                qs.append(jnp.where(lo, qp, zero))
                qs.append(jnp.where(lo, zero, qp))
            q4 = jnp.concatenate(qs, axis=0)
            s4 = lax.dot_general(q4, kg, (((1,), (1,)), ((), ())), preferred_element_type=F32)
            jobs.append((g, sb, s4, vg))
    for g, sb, s4, vg in jobs:
        if local:
            valid_prev = (col >= row) & (j > 0) if sb == 0 else (col >= row)
            valid_next = (col <= row) & (j < nb // nsb - 1) if sb == nsb - 1 else (col <= row)
        ps, sinks = [], []
        for hl in range(4):
            s = s4[hl * tq:(hl + 1) * tq]
            if local:
                s = jnp.concatenate(
                    [jnp.where(valid_prev, s[:, :BLOCK], NEG), s[:, BLOCK:2 * BLOCK],
                     jnp.where(valid_next, s[:, 2 * BLOCK:3 * BLOCK], NEG), s[:, 3 * BLOCK:]], axis=1)
            sk = sink_ref[g * 4 + hl] * LOG2E
            m = jnp.maximum(jnp.max(s, axis=-1, keepdims=True), sk)
            ps.append(jnp.exp2(s - m).astype(BF16))
            sinks.append(jnp.exp2(sk - m))
        lov = lax.broadcasted_iota(jnp.int32, vg.shape, 1) < HEAD_DIM
        ve = jnp.where(lov, vg, jnp.ones_like(vg))
        o4 = _dot(jnp.concatenate(ps, axis=0), ve)
        o4 = o4 + jnp.where(lo4, 0.0, jnp.concatenate(sinks, axis=0))
        r4 = pltpu.roll(o4, HEAD_DIM, 1)
        for p in range(2):
            ev = slice((2 * p) * tq, (2 * p + 1) * tq)
            od = slice((2 * p + 1) * tq, (2 * p + 2) * tq)
            o_pair = jnp.where(lo, o4[ev] / r4[ev], r4[od] / o4[od])
            c0 = g * 2 * LANES + p * LANES
            o_ref[0, sb * tq:(sb + 1) * tq, c0:c0 + LANES] = o_pair.astype(BF16)


def _window_attention(q, k, v, kx, vx, sink, bsz, n, m):
    nb = n // BLOCK
    q3 = q.reshape(bsz, n, C_WIDTH)
    k3 = k.reshape(bsz, n, 2 * KV_WIDTH)
    v3 = v.reshape(bsz, n, 2 * KV_WIDTH)
    kx3 = kx.reshape(bsz, m, 2 * KV_WIDTH)
    vx3 = vx.reshape(bsz, m, 2 * KV_WIDTH)
    nsb = ATTN_SUBBLOCKS
    edge = lambda f: pl.BlockSpec((1, BLOCK, 2 * KV_WIDTH), f)
    own = pl.BlockSpec((1, nsb * BLOCK, 2 * KV_WIDTH), lambda b, i, s: (b, i, 0))
    prev = lambda b, i, s: (b, jnp.maximum(i * nsb - 1, 0), 0)
    cur = lambda b, i, s: (b, i, 0)
    nxt = lambda b, i, s: (b, jnp.minimum((i + 1) * nsb, nb - 1), 0)
    ctxs = pl.BlockSpec((1, m, 2 * KV_WIDTH), lambda b, i, s: (b, 0, 0))
    o = pl.pallas_call(
        functools.partial(_attn_kernel, local=True, nb=nb),
        grid_spec=pltpu.PrefetchScalarGridSpec(
            num_scalar_prefetch=1,
            grid=(bsz, nb // nsb),
            in_specs=[pl.BlockSpec((1, nsb * BLOCK, C_WIDTH), cur),
                      edge(prev), own, edge(nxt), edge(prev), own, edge(nxt), ctxs, ctxs],
            out_specs=pl.BlockSpec((1, nsb * BLOCK, C_WIDTH), cur)),
        out_shape=jax.ShapeDtypeStruct((bsz, n, C_WIDTH), BF16),
        compiler_params=_cparams(("parallel", "parallel")),
    )(sink, q3, k3, k3, k3, v3, v3, v3, kx3, vx3)
    return o.reshape(bsz * n, C_WIDTH)


def _context_attention(q, kx, vx, sink, bsz, m):
    q3 = q.reshape(bsz, m, C_WIDTH)
    kx3 = kx.reshape(bsz, m, 2 * KV_WIDTH)
    vx3 = vx.reshape(bsz, m, 2 * KV_WIDTH)
    ctxs = pl.BlockSpec((1, m, 2 * KV_WIDTH), lambda b, s: (b, 0, 0))
    o = pl.pallas_call(
        functools.partial(_attn_kernel, local=False, nb=1),
        grid_spec=pltpu.PrefetchScalarGridSpec(
            num_scalar_prefetch=1,
            grid=(bsz,),
            in_specs=[pl.BlockSpec((1, m, C_WIDTH), lambda b, s: (b, 0, 0)), ctxs, ctxs],
            out_specs=pl.BlockSpec((1, m, C_WIDTH), lambda b, s: (b, 0, 0))),
        out_shape=jax.ShapeDtypeStruct((bsz, m, C_WIDTH), BF16),
        compiler_params=_cparams(("parallel",)),
    )(sink, q3, kx3, vx3)
    return o.reshape(bsz * m, C_WIDTH)


def _postmix_kernel(*refs, route, ncast):
    if ncast:
        n_in = 11 + (2 if route else 0)
        for src, dst in zip(refs[n_in:n_in + ncast], refs[len(refs) - ncast:]):
            dst[...] = src[...].astype(BF16)
        refs = refs[:n_in] + refs[n_in + ncast:len(refs) - ncast]
    h_ref = refs[5]
    for sb in range(h_ref.shape[0] // MOE_TB):
        _postmix_rows(slice(sb * MOE_TB, (sb + 1) * MOE_TB), sb, refs, route)


def _postmix_rows(rows, sb, refs, route):
    (sgu_ref, yf_ref, at_ref, wf_ref, wo_ref, h_ref, gp_ref, gt_ref, gf_ref, sh_ref, sc_ref) = refs[:11]
    if route:
        wr_ref, br_ref, hn_ref, y_ref, cmb_ref, cnt_ref = refs[11:]
    else:
        hn_ref, y_ref = refs[11:]
    fm = _dot(yf_ref[rows, :], wf_ref[...]).astype(BF16)
    o = (_dot(sgu_ref[rows, :], wo_ref[:A_WIDTH, :])
         + _dot(fm, wo_ref[A_WIDTH:A_WIDTH + B_WIDTH, :])
         + _dot(at_ref[rows, :], wo_ref[A_WIDTH + B_WIDTH:, :]))
    ms = jnp.mean(o * o, axis=-1, keepdims=True)
    hn = h_ref[rows, :] + gt_ref[...] * (o * lax.rsqrt(ms + EPS) * gp_ref[...])
    hn_ref[rows, :] = hn
    ms2 = jnp.mean(hn * hn, axis=-1, keepdims=True)
    y = hn * lax.rsqrt(ms2 + EPS) * gf_ref[...] * (1.0 + sc_ref[...]) + sh_ref[...]
    y_ref[rows, :] = y.astype(BF16)
    if route:
        yh = y.astype(BF16)
        yl = (y - yh.astype(F32)).astype(BF16)
        w = wr_ref[...]
        wh = w.astype(BF16)
        wl = (w - wh.astype(F32)).astype(BF16)
        lg = _dot(yh, wh) + _dot(yl, wh) + _dot(yh, wl) + br_ref[...]
        lane = lax.broadcasted_iota(jnp.int32, lg.shape, 1)
        lg = jnp.where(lane < N_EXPERTS, lg, NEG)
        m1 = jnp.max(lg, axis=-1, keepdims=True)
        i1 = jnp.min(jnp.where(lg == m1, lane, LANES), axis=-1, keepdims=True)
        lg2 = jnp.where(lane == i1, NEG, lg)
        m2 = jnp.max(lg2, axis=-1, keepdims=True)
        i2 = jnp.min(jnp.where(lg2 == m2, lane, LANES), axis=-1, keepdims=True)
        e2 = jnp.exp(m2 - m1)
        g1 = 1.0 / (1.0 + e2)
        g2 = e2 * g1
        cmb = jnp.where(lane == i1, g1, 0.0) + jnp.where(lane == i2, g2, 0.0)
        cmb_ref[rows, :] = cmb
        cnt_ref[sb] = jnp.sum(jnp.where(cmb > 0.0, 1.0, 0.0), axis=0, keepdims=True)


def _postmix(sgu, yf, at, wf, wo, h, mods, gp, gf, router, *, tm, tiles_per_batch, side_cast=()):
    T = h.shape[0]
    if tiles_per_batch is None:
        mrow = lambda i: 4
    else:
        mrow = lambda i: i // tiles_per_batch
    const2 = lambda i: (0, 0)
    row = lambda w: pl.BlockSpec((tm, w), lambda i: (i, 0))
    mod = lambda j: pl.BlockSpec((None, None, 1, D_MODEL), lambda i: (mrow(i), j, 0, 0))
    in_specs = [row(A_WIDTH), row(B_WIDTH), row(C_WIDTH),
                pl.BlockSpec((B_WIDTH, B_WIDTH), const2),
                pl.BlockSpec((D_MODEL, D_MODEL), const2),
                row(D_MODEL),
                pl.BlockSpec((1, D_MODEL), const2),
                mod(2),
                pl.BlockSpec((1, D_MODEL), const2),
                mod(3), mod(4)]
    args = [sgu, yf, at, wf, wo, h, gp, mods, gf, mods, mods]
    out_specs = [row(D_MODEL), row(D_MODEL)]
    out_shape = [jax.ShapeDtypeStruct((T, D_MODEL), F32), jax.ShapeDtypeStruct((T, D_MODEL), BF16)]
    route = router is not None
    if route:
        in_specs += [pl.BlockSpec((D_MODEL, LANES), const2), pl.BlockSpec((1, LANES), const2)]
        args += list(router)
        out_specs += [row(LANES), pl.BlockSpec((tm // MOE_TB, 1, LANES), lambda i: (i, 0, 0))]
        out_shape += [jax.ShapeDtypeStruct((T, LANES), F32), jax.ShapeDtypeStruct((T // MOE_TB, 1, LANES), F32)]
    for a in side_cast:
        spec = pl.BlockSpec((a.shape[0] // (T // tm), a.shape[1]), lambda i: (i, 0))
        in_specs.append(spec)
        args.append(a)
        out_specs.append(spec)
        out_shape.append(jax.ShapeDtypeStruct(a.shape, BF16))
    return pl.pallas_call(
        functools.partial(_postmix_kernel, route=route, ncast=len(side_cast)),
        grid=(T // tm,),
        in_specs=in_specs, out_specs=out_specs, out_shape=out_shape,
        compiler_params=_cparams(("parallel",)),
    )(*args)


def _ffn_epilogue(f, h_ref, gp_ref, gt_ref, o_ref):
    ms = jnp.mean(f * f, axis=-1, keepdims=True)
    o_ref[...] = h_ref[...] + gt_ref[...] * (f * lax.rsqrt(ms + EPS) * gp_ref[...])


FF_CHUNK = 512


def _swiglu_tile(y, wg_ref, wu_ref, wd_ref, act_ref):
    for c in range(D_FF // FF_CHUNK):
        cols = slice(c * FF_CHUNK, (c + 1) * FF_CHUNK)
        gate = _dot(y, wg_ref[:, cols])
        up = _dot(y, wu_ref[:, cols])
        act_ref[:, cols] = (gate * jax.nn.sigmoid(gate) * up).astype(BF16)
    return _dot(act_ref[...], wd_ref[...])


def _ffn_dense_kernel(y_ref, wg_ref, wu_ref, wd_ref, h_ref, gp_ref, gt_ref, o_ref, act_ref):
    f = _swiglu_tile(y_ref[...], wg_ref, wu_ref, wd_ref, act_ref)
    _ffn_epilogue(f, h_ref, gp_ref, gt_ref, o_ref)


def _ffn_dense_cast_kernel(y_ref, wg_ref, wu_ref, wd_ref, h_ref, gp_ref, gt_ref, c0_ref, c1_ref, c2_ref,
                           o_ref, d0_ref, d1_ref, d2_ref, z_ref, act_ref):
    d0_ref[...] = c0_ref[...].astype(BF16)
    d1_ref[...] = c1_ref[...].astype(BF16)
    d2_ref[...] = c2_ref[...].astype(BF16)
    z_ref[...] = jnp.zeros_like(z_ref)
    f = _swiglu_tile(y_ref[...], wg_ref, wu_ref, wd_ref, act_ref)
    _ffn_epilogue(f, h_ref, gp_ref, gt_ref, o_ref)


def _ffn_dense(y, wg, wu, wd, h, mods, gp, *, tm, tiles_per_batch, side_cast=(), zero_rows=0):
    T = h.shape[0]
    nt = T // tm
    if tiles_per_batch is None:
        mrow = lambda i: 4
    else:
        mrow = lambda i: i // tiles_per_batch
    resident = pl.Buffered(1)
    in_specs = [pl.BlockSpec((tm, D_MODEL), lambda i: (i, 0)),
                pl.BlockSpec((D_MODEL, D_FF), lambda i: (0, 0), pipeline_mode=resident),
                pl.BlockSpec((D_MODEL, D_FF), lambda i: (0, 0), pipeline_mode=resident),
                pl.BlockSpec((D_FF, D_MODEL), lambda i: (0, 0), pipeline_mode=resident),
                pl.BlockSpec((tm, D_MODEL), lambda i: (i, 0)),
                pl.BlockSpec((1, D_MODEL), lambda i: (0, 0)),
                pl.BlockSpec((None, None, 1, D_MODEL), lambda i: (mrow(i), 5, 0, 0))]
    out_specs = [pl.BlockSpec((tm, D_MODEL), lambda i: (i, 0))]
    out_shape = [jax.ShapeDtypeStruct((T, D_MODEL), F32)]
    for a in side_cast:
        spec = pl.BlockSpec((a.shape[0] // nt, a.shape[1]), lambda i: (i, 0))
        in_specs.append(spec)
        out_specs.append(spec)
        out_shape.append(jax.ShapeDtypeStruct(a.shape, BF16))
    if side_cast:
        out_specs.append(pl.BlockSpec((zero_rows // nt, D_MODEL), lambda i: (i, 0)))
        out_shape.append(jax.ShapeDtypeStruct((zero_rows, D_MODEL), BF16))
    res = pl.pallas_call(
        _ffn_dense_cast_kernel if side_cast else _ffn_dense_kernel,
        grid=(nt,),
        in_specs=in_specs, out_specs=out_specs, out_shape=out_shape,
        scratch_shapes=[pltpu.VMEM((tm, D_FF), BF16)],
        compiler_params=_cparams(("parallel",)),
    )(y, wg, wu, wd, h, gp, mods, *side_cast)
    return res[0], tuple(res[1:])


MOE_TB = 512
MOE_TM = 512
SUB = 128
NSUB = MOE_TB // SUB
RUN_ALIGN = 16


def _moe_tile_bound(T):
    nt = T // MOE_TB
    rows = 2 * T + nt * N_EXPERTS * (RUN_ALIGN - 1) + N_EXPERTS * (SUB + MOE_TM - 1)
    return -(-rows // (2 * MOE_TM)) * 2


def _moe_plan(cnt, T):
    p = (cnt + RUN_ALIGN - 1) // RUN_ALIGN * RUN_ALIGN
    base = jnp.cumsum(p, axis=0) - p
    used = jnp.sum(p, axis=0)
    tiles = (used + SUB + MOE_TM - 1) // MOE_TM
    tend = jnp.cumsum(tiles)
    off = (tend - tiles) * MOE_TM
    rowbase = (off[None, :] + base).astype(jnp.int32)
    nsub = (cnt + SUB - 1) // SUB
    slotbase = (jnp.cumsum(nsub, axis=1) - nsub).astype(jnp.int32)
    nt_bound = _moe_tile_bound(T)
    j = jnp.arange(nt_bound, dtype=jnp.int32)
    tile_e = jnp.minimum(jnp.sum(j[:, None] >= tend[None, :], axis=1), N_EXPERTS - 1).astype(jnp.int32)
    nvalid = tend[-1].astype(jnp.int32)
    tile_blk = jnp.minimum(j, nvalid - 1)
    return rowbase, slotbase, tile_e, tile_blk, nvalid.reshape(1)


NSLOT = 2 * MOE_TB // SUB + N_EXPERTS
SLOT_ROWS = NSLOT * SUB
SLOT_CHUNK = 512


def _block_slots(sb_ref, cn_ref, b):
    last = b * N_EXPERTS + N_EXPERTS - 1
    return sb_ref[last] + (cn_ref[last] + SUB - 1) // SUB


def _for_each_run_slot(cn_ref, b, fn):
    for e in range(N_EXPERTS):
        for s in range(NSUB):
            @pl.when(cn_ref[b * N_EXPERTS + e] > SUB * s)
            def _(e=e, s=s):
                fn(e, s)


def _run_copy_out(stage, ys_ref, sems, rb_ref, sb_ref, b, par, e, s):
    slot = sb_ref[b * N_EXPERTS + e] + s
    r0 = pl.multiple_of(rb_ref[b * N_EXPERTS + e] + SUB * s, RUN_ALIGN)
    return pltpu.make_async_copy(stage.at[par, pl.ds(pl.multiple_of(slot * SUB, SUB), SUB), :],
                                 ys_ref.at[pl.ds(r0, SUB), :], sems.at[par, e, s])


def _dispatch_kernel(rb_ref, sb_ref, cn_ref, y_ref, cmb_ref, sbv_ref, ltri_ref, ysin_ref,
                     tok_ref, ys_ref, stage, sems):
    del ysin_ref
    b = pl.program_id(0)
    nb = pl.num_programs(0)
    par = b % 2
    cmb = cmb_ref[...]
    sel = cmb > 0.0
    rk = _dot(ltri_ref[...], jnp.where(sel, 1.0, 0.0).astype(BF16))
    srow = jnp.where(sel, rk + sbv_ref[...], -1.0)
    nsel = jnp.sum(jnp.where(sel, 1.0, 0.0), axis=-1, keepdims=True)
    sa = jnp.max(srow, axis=-1, keepdims=True)
    sb = jnp.where(nsel > 1.5, jnp.sum(jnp.where(sel, srow, 0.0), axis=-1, keepdims=True) - sa, -1.0)
    ga = jnp.sum(jnp.where(srow == sa, cmb, 0.0), axis=-1, keepdims=True)
    gb = jnp.sum(cmb, axis=-1, keepdims=True) - ga
    lane = lax.broadcasted_iota(jnp.int32, cmb.shape, 1)
    tok = jnp.where(lane == 0, sa, jnp.where(lane == 1, sb, jnp.where(lane == 2, ga, jnp.where(lane == 3, gb, 0.0))))
    tok_ref[...] = tok
    tok_t = tok.T
    nslots = _block_slots(sb_ref, cn_ref, b)
    rr0 = lax.broadcasted_iota(jnp.int32, (SLOT_CHUNK, cmb.shape[0]), 0).astype(F32)
    for c in range(SLOT_ROWS // SLOT_CHUNK):
        @pl.when(nslots * SUB > c * SLOT_CHUNK)
        def _(c=c):
            rr = rr0 + float(c * SLOT_CHUNK)
            g = jnp.where((rr == tok_t[0:1, :]) | (rr == tok_t[1:2, :]), 1.0, 0.0).astype(BF16)
            stage[par, c * SLOT_CHUNK:(c + 1) * SLOT_CHUNK, :] = _dot(g, y_ref[...]).astype(BF16)

    @pl.when(b > 0)
    def _():
        _for_each_run_slot(cn_ref, b - 1, lambda e, s: _run_copy_out(
            stage, ys_ref, sems, rb_ref, sb_ref, b - 1, 1 - par, e, s).wait())

    _for_each_run_slot(cn_ref, b, lambda e, s: _run_copy_out(
        stage, ys_ref, sems, rb_ref, sb_ref, b, par, e, s).start())

    @pl.when(b == nb - 1)
    def _():
        _for_each_run_slot(cn_ref, b, lambda e, s: _run_copy_out(
            stage, ys_ref, sems, rb_ref, sb_ref, b, par, e, s).wait())


def _moe_dispatch(y, cmb, rowbase, slotbase, cnt, ys0):
    T = y.shape[0]
    nt = T // MOE_TB
    nt_bound = ys0.shape[0] // MOE_TM
    sbv = jnp.zeros((nt, 1, LANES), F32).at[:, 0, :N_EXPERTS].set((slotbase * SUB).astype(F32))
    ltri = jnp.asarray(np.tril(np.ones((MOE_TB, MOE_TB), np.float32), -1)).astype(BF16)
    imap2 = lambda b, rb, sb, cn: (b, 0)
    return pl.pallas_call(
        _dispatch_kernel,
        grid_spec=pltpu.PrefetchScalarGridSpec(
            num_scalar_prefetch=3,
            grid=(nt,),
            in_specs=[pl.BlockSpec((MOE_TB, D_MODEL), imap2),
                      pl.BlockSpec((MOE_TB, LANES), imap2),
                      pl.BlockSpec((None, 1, LANES), lambda b, rb, sb, cn: (b, 0, 0)),
                      pl.BlockSpec((MOE_TB, MOE_TB), lambda b, rb, sb, cn: (0, 0)),
                      pl.BlockSpec(memory_space=pl.ANY)],
            out_specs=[pl.BlockSpec((MOE_TB, LANES), imap2),
                       pl.BlockSpec(memory_space=pl.ANY)],
            scratch_shapes=[pltpu.VMEM((2, SLOT_ROWS, D_MODEL), BF16),
                            pltpu.SemaphoreType.DMA((2, N_EXPERTS, NSUB))]),
        out_shape=[jax.ShapeDtypeStruct((T, LANES), F32),
                   jax.ShapeDtypeStruct((nt_bound * MOE_TM, D_MODEL), BF16)],
        input_output_aliases={7: 1},
        compiler_params=_cparams(("arbitrary",)),
    )(rowbase.reshape(-1), slotbase.reshape(-1), cnt.reshape(-1), y, cmb, sbv, ltri, ys0)


def _ffn_group_kernel(te_ref, tb_ref, nv_ref, y_ref, wg_ref, wu_ref, wd_ref, o_ref, act_ref):
    del te_ref, tb_ref
    j = pl.program_id(0)

    @pl.when(j < nv_ref[0])
    def _():
        o_ref[...] = _swiglu_tile(y_ref[...], wg_ref, wu_ref, wd_ref, act_ref).astype(BF16)

    @pl.when(j >= nv_ref[0])
    def _():
        o_ref[...] = jnp.zeros_like(o_ref)


def _ffn_group(ys, wg, wu, wd, tile_e, tile_blk, nvalid):
    rows = ys.shape[0]
    resident = pl.Buffered(2)
    return pl.pallas_call(
        _ffn_group_kernel,
        grid_spec=pltpu.PrefetchScalarGridSpec(
            num_scalar_prefetch=3,
            grid=(rows // MOE_TM,),
            in_specs=[pl.BlockSpec((MOE_TM, D_MODEL), lambda j, te, tb, nv: (tb[j], 0)),
                      pl.BlockSpec((None, D_MODEL, D_FF), lambda j, te, tb, nv: (te[j], 0, 0), pipeline_mode=resident),
                      pl.BlockSpec((None, D_MODEL, D_FF), lambda j, te, tb, nv: (te[j], 0, 0), pipeline_mode=resident),
                      pl.BlockSpec((None, D_FF, D_MODEL), lambda j, te, tb, nv: (te[j], 0, 0), pipeline_mode=resident)],
            out_specs=pl.BlockSpec((MOE_TM, D_MODEL), lambda j, te, tb, nv: (j, 0)),
            scratch_shapes=[pltpu.VMEM((MOE_TM, D_FF), BF16)]),
        out_shape=jax.ShapeDtypeStruct((rows, D_MODEL), BF16),
        compiler_params=_cparams(("arbitrary",)),
    )(tile_e, tile_blk, nvalid, ys, wg, wu, wd)


def _run_copy_in(fs_ref, fbuf, sems, rb_ref, sb_ref, b, par, e, s):
    slot = sb_ref[b * N_EXPERTS + e] + s
    r0 = pl.multiple_of(rb_ref[b * N_EXPERTS + e] + SUB * s, RUN_ALIGN)
    return pltpu.make_async_copy(fs_ref.at[pl.ds(r0, SUB), :],
                                 fbuf.at[par, pl.ds(pl.multiple_of(slot * SUB, SUB), SUB), :], sems.at[par, e, s])


def _combine_kernel(rb_ref, sb_ref, cn_ref, tok_ref, h_ref, gp_ref, gt_ref, fs_ref, o_ref, fbuf, sems):
    b = pl.program_id(0)
    nb = pl.num_programs(0)
    par = b % 2

    @pl.when(b == 0)
    def _():
        fbuf[...] = jnp.zeros_like(fbuf)
        _for_each_run_slot(cn_ref, b, lambda e, s: _run_copy_in(
            fs_ref, fbuf, sems, rb_ref, sb_ref, b, par, e, s).start())

    @pl.when(b + 1 < nb)
    def _():
        _for_each_run_slot(cn_ref, b + 1, lambda e, s: _run_copy_in(
            fs_ref, fbuf, sems, rb_ref, sb_ref, b + 1, 1 - par, e, s).start())

    tok = tok_ref[...]
    sa, sb, ga, gb = tok[:, 0:1], tok[:, 1:2], tok[:, 2:3], tok[:, 3:4]
    _for_each_run_slot(cn_ref, b, lambda e, s: _run_copy_in(
        fs_ref, fbuf, sems, rb_ref, sb_ref, b, par, e, s).wait())
    nslots = _block_slots(sb_ref, cn_ref, b)
    nck = SLOT_ROWS // SLOT_CHUNK
    for c in range(1, nck + 1):
        lo_rows, hi_rows = (c - 1) * SLOT_CHUNK, c * SLOT_CHUNK
        cond = (nslots * SUB > lo_rows) if c == nck else ((nslots * SUB > lo_rows) & (nslots * SUB <= hi_rows))
        if c == 1:
            cond = nslots * SUB <= hi_rows

        @pl.when(cond)
        def _(depth=hi_rows):
            cc = lax.broadcasted_iota(jnp.int32, (tok.shape[0], depth), 1).astype(F32)
            pick = jnp.where(cc == sa, ga, jnp.where(cc == sb, gb, 0.0)).astype(BF16)
            f = _dot(pick, fbuf[par, :depth, :])
            _ffn_epilogue(f, h_ref, gp_ref, gt_ref, o_ref)


def _moe_combine(fs, tok, h, mods, gp, rowbase, slotbase, cnt, *, tiles_per_batch):
    T = h.shape[0]
    nt = T // MOE_TB
    imap2 = lambda b, rb, sb, cn: (b, 0)
    return pl.pallas_call(
        _combine_kernel,
        grid_spec=pltpu.PrefetchScalarGridSpec(
            num_scalar_prefetch=3,
            grid=(nt,),
            in_specs=[pl.BlockSpec((MOE_TB, LANES), imap2),
                      pl.BlockSpec((MOE_TB, D_MODEL), imap2),
                      pl.BlockSpec((1, D_MODEL), lambda b, rb, sb, cn: (0, 0)),
                      pl.BlockSpec((None, None, 1, D_MODEL), lambda b, rb, sb, cn: (b // tiles_per_batch, 5, 0, 0)),
                      pl.BlockSpec(memory_space=pl.ANY)],
            out_specs=pl.BlockSpec((MOE_TB, D_MODEL), imap2),
            scratch_shapes=[pltpu.VMEM((2, SLOT_ROWS, D_MODEL), BF16),
                            pltpu.SemaphoreType.DMA((2, N_EXPERTS, NSUB))]),
        out_shape=jax.ShapeDtypeStruct((T, D_MODEL), F32),
        compiler_params=_cparams(("arbitrary",)),
    )(rowbase.reshape(-1), slotbase.reshape(-1), cnt.reshape(-1), tok, h, gp, mods, fs)


def _ffn_moe(y, cmb, cnt_tiles, wg, wu, wd, h, mods, gp, ys0, *, tiles_per_batch):
    T = h.shape[0]
    cnt = cnt_tiles.reshape(T // MOE_TB, LANES)[:, :N_EXPERTS].astype(jnp.int32)
    rowbase, slotbase, tile_e, tile_blk, nvalid = _moe_plan(cnt, T)
    tok, ys = _moe_dispatch(y, cmb, rowbase, slotbase, cnt, ys0)
    fs = _ffn_group(ys, wg, wu, wd, tile_e, tile_blk, nvalid)
    return _moe_combine(fs, tok, h, mods, gp, rowbase, slotbase, cnt, tiles_per_batch=tiles_per_batch)


def _blockdiag(m, reps):
    n = m.shape[0]
    out = np.zeros((n * reps, n * reps), np.float64)
    for r in range(reps):
        out[r * n:(r + 1) * n, r * n:(r + 1) * n] = m
    return out


def _dft_tables(n_rows):
    n = n_rows * GRID_W
    r = np.arange(n_rows)
    c = np.arange(GRID_W)
    a1 = 2 * np.pi * np.outer(r, r) / n_rows
    c1, s1 = np.cos(a1), np.sin(a1)
    m1 = np.block([[c1, s1], [-s1, c1]]) * 0.125
    at = 2 * np.pi * np.outer(r, c) / n
    twc = np.repeat(np.cos(at), B_WIDTH, axis=1)
    tws = np.repeat(np.sin(at), B_WIDTH, axis=1)
    a3 = 2 * np.pi * np.outer(c, c) / GRID_W
    m3 = np.concatenate([np.cos(a3), np.sin(a3)], axis=1) * (8.0 / np.sqrt(n))
    f32 = lambda t: jnp.asarray(t.astype(np.float32))
    return f32(m1).astype(BF16), f32(twc), f32(tws), f32(m3).astype(BF16)


def _channel_dft_table():
    d = np.arange(HEAD_DIM)
    a = 2 * np.pi * np.outer(d, d) / HEAD_DIM
    w = np.concatenate([_blockdiag(np.cos(a), 4), -_blockdiag(np.sin(a), 4)], axis=1) * 0.125
    return jnp.asarray(w.astype(np.float32)).astype(BF16)


def _ctx_dft_table(m):
    p = np.arange(m)
    a = 2 * np.pi * np.outer(p, p) / m
    w = np.concatenate([np.cos(a), np.sin(a)], axis=1) * (8.0 / np.sqrt(m * HEAD_DIM))
    return jnp.asarray(w.astype(np.float32)).astype(BF16)


def _rope_tables(n_tok):
    rows = n_tok // GRID_W
    row = jnp.broadcast_to(jnp.arange(rows)[:, None], (rows, GRID_W)).reshape(-1)
    col = jnp.broadcast_to(jnp.arange(GRID_W)[None, :], (rows, GRID_W)).reshape(-1)
    half = HEAD_DIM // 2
    inv = ROPE_BASE ** (-jnp.arange(0, half, 2, dtype=F32) / half)
    ang = jnp.stack([row.astype(F32)[:, None] * inv, col.astype(F32)[:, None] * inv], axis=1)
    cos, sin = jnp.cos(ang), jnp.sin(ang)
    zer = jnp.zeros_like(sin)
    lay = lambda a, b: jnp.tile(jnp.stack([a, b], axis=2).reshape(n_tok, HEAD_DIM), (1, LANES // HEAD_DIM))
    return lay(cos, cos), lay(-sin, zer), lay(zer, sin)


def kernel(x, c, ctx, c_ctx, w_ada, b_ada, g_mix_pre, g_mix_post, g_ffn_pre, g_ffn_post,
           w_in, w_s, b_s, g_v, w_f, sink, w_out, w_gate_d, w_up_d, w_down_d,
           w_router, b_router, w_gate_e, w_up_e, w_down_e):
    bsz, n_lat, _ = x.shape
    n_ctx = ctx.shape[1]
    T, Tc = bsz * n_lat, bsz * n_ctx
    tm = 512
    tpb = n_lat // tm

    rope_tabs = _rope_tables(n_lat)
    m1, twc, tws, m3 = _dft_tables(n_lat // GRID_W)
    wdft = _channel_dft_table()
    mctx = _ctx_dft_table(n_ctx)
    hm = jnp.asarray(_blockdiag(np.full((HEAD_DIM, HEAD_DIM), 1.0 / HEAD_DIM), A_HEADS).astype(np.float32)).astype(BF16)

    cond8 = jnp.zeros((8, D_MODEL), F32).at[:bsz].set(c).at[4].set(c_ctx)
    h = x.reshape(T, D_MODEL)
    hc = ctx.reshape(Tc, D_MODEL)

    mods_all = _adaln(cond8, w_ada, b_ada[:, None, :]).reshape(DEPTH, 8, 6, 1, D_MODEL)

    for i in range(DEPTH):
        last = i == DEPTH - 1
        mods = mods_all[i]
        win = w_in[i].astype(BF16)
        ws = jnp.transpose(w_s[i], (1, 0, 2)).reshape(CHUNK, A_HEADS * CHUNK).astype(BF16)
        bsx = jnp.repeat(b_s[i].T, HEAD_DIM, axis=1)
        gv = g_v[i].reshape(1, A_WIDTH)
        wf = jax.scipy.linalg.block_diag(*[w_f[i][g] for g in range(4)]).astype(BF16)
        wo = w_out[i].astype(BF16)
        gpre = g_mix_pre[i][None, :]
        gpost = g_mix_post[i][None, :]
        gfpre = g_ffn_pre[i][None, :]
        gfpost = g_ffn_post[i][None, :]
        sk = sink[i]

        sgu, fre, fim, q, k, v = _premix(h, mods, gpre, win, ws, bsx, gv, hm, wdft, rope_tabs,
                                         tm=2 * tm, tiles_per_batch=tpb // 2, n_pos=n_lat)
        sguc, frec, fimc, qc, kc, vc = _premix(hc, mods, gpre, win, ws, bsx, gv, hm, wdft, None,
                                               tm=tm, tiles_per_batch=None, n_pos=n_ctx)
        yf = _seq_dft(fre, fim, bsz, n_lat // GRID_W, m1, twc, tws, m3)
        at = _window_attention(q, k, v, kc, vc, sk, bsz, n_lat, n_ctx)

        if i % 2 == 0:
            router = None
        else:
            j = i // 2
            wr = jnp.zeros((D_MODEL, LANES), F32).at[:, :N_EXPERTS].set(w_router[j])
            br = jnp.zeros((1, LANES), F32).at[0, :N_EXPERTS].set(b_router[j])
            router = (wr, br)
        dense_w = (w_gate_d[i // 2], w_up_d[i // 2], w_down_d[i // 2]) if i % 2 == 0 else ()
        res = _postmix(sgu, yf, at, wf, wo, h, mods, gpost, gfpre, router, tm=2 * tm, tiles_per_batch=tpb // 2,
                       side_cast=dense_w)
        if not last:
            yfc = _ctx_dft(frec, fimc, bsz, n_ctx, mctx)
            atc = _context_attention(qc, kc, vc, sk, bsz, n_ctx)
            resc = _postmix(sguc, yfc, atc, wf, wo, hc, mods, gpost, gfpre, router, tm=2 * tm, tiles_per_batch=None)

        j = i // 2
        if i % 2 == 0:
            wg, wu, wd = res[-3:]
            side = ()
            if not last:
                jn = (i + 1) // 2
                side = (w_gate_e[jn].reshape(N_EXPERTS * D_MODEL, D_FF), w_up_e[jn].reshape(N_EXPERTS * D_MODEL, D_FF),
                        w_down_e[jn].reshape(N_EXPERTS * D_FF, D_MODEL))
            h, expert_w = _ffn_dense(res[1], wg, wu, wd, res[0], mods, gfpost, tm=tm, tiles_per_batch=tpb,
                                     side_cast=side, zero_rows=_moe_tile_bound(T) * MOE_TM)
            if not last:
                hc, _ = _ffn_dense(resc[1], wg, wu, wd, resc[0], mods, gfpost, tm=tm, tiles_per_batch=None)
        else:
            wg = expert_w[0].reshape(N_EXPERTS, D_MODEL, D_FF)
            wu = expert_w[1].reshape(N_EXPERTS, D_MODEL, D_FF)
            wd = expert_w[2].reshape(N_EXPERTS, D_FF, D_MODEL)
            assert last and tm == MOE_TB, "the expert FFN is only built for the final layer's latent tokens"
            h = _ffn_moe(res[1], res[2], res[3], wg, wu, wd, res[0], mods, gfpost, expert_w[3],
                         tiles_per_batch=n_lat // MOE_TB)
    return h.reshape(bsz, n_lat, D_MODEL)
```

```python
import functools

import numpy as np
import jax
import jax.numpy as jnp
from jax import lax
from jax.experimental import pallas as pl
from jax.experimental.pallas import tpu as pltpu

D_MODEL = 1024
DEPTH = 2
GRID_W = 64
HEAD_DIM = 64
EPS = 1e-6
A_HEADS = 4
A_WIDTH = 256
CHUNK = 128
B_WIDTH = 256
C_Q_HEADS = 8
C_WIDTH = 512
KV_WIDTH = 128
WINDOW = 128
BLOCK = 128
ROPE_BASE = 10000.0
OFF_B = 512
OFF_Q = 768
OFF_K = 1280
OFF_V = 1408
N_IN = 1536
D_FF = 3584
N_EXPERTS = 8

LANES = 128
VMEM_LIMIT = 60 * 1024 * 1024
NEG = -1e30
LOG2E = 1.4426950408889634
assert WINDOW == BLOCK

F32 = jnp.float32
BF16 = jnp.bfloat16


def _dot(a, b):
    return jnp.dot(a, b, preferred_element_type=F32)


def _cparams(sem):
    return pltpu.CompilerParams(dimension_semantics=sem, vmem_limit_bytes=VMEM_LIMIT)


def _adaln_kernel(c_ref, w_ref, b_ref, o_ref):
    c = c_ref[...]
    s = (c * jax.nn.sigmoid(c)).astype(BF16)
    o_ref[...] = _dot(s, w_ref[...].astype(BF16)) + b_ref[...]


def _adaln(cond8, w, b):
    tn = 1536
    nl = w.shape[0]
    return pl.pallas_call(
        _adaln_kernel,
        grid=(nl, 6 * D_MODEL // tn),
        in_specs=[pl.BlockSpec((8, D_MODEL), lambda l, j: (0, 0)),
                  pl.BlockSpec((None, D_MODEL, tn), lambda l, j: (l, 0, j)),
                  pl.BlockSpec((None, 1, tn), lambda l, j: (l, 0, j))],
        out_specs=pl.BlockSpec((None, 8, tn), lambda l, j: (l, 0, j)),
        out_shape=jax.ShapeDtypeStruct((nl, 8, 6 * D_MODEL), F32),
        compiler_params=_cparams(("arbitrary", "arbitrary")),
    )(cond8, w, b)


def _gelu_tanh(x):
    return 0.5 * x * (1.0 + jnp.tanh(0.7978845608028654 * (x + 0.044715 * x * x * x)))


PRE_SUB = 512


def _premix_kernel(*refs, rope):
    if rope:
        (h_ref, sh_ref, sc_ref, g_ref, win_ref, ws_ref, bs_ref, gv_ref, hm_ref, wdft_ref,
         cos_ref, s1_ref, s2_ref, sgu_ref, re_ref, im_ref, q_ref, k_ref, v_ref, fsc_ref) = refs
    else:
        (h_ref, sh_ref, sc_ref, g_ref, win_ref, ws_ref, bs_ref, gv_ref, hm_ref, wdft_ref,
         sgu_ref, re_ref, im_ref, q_ref, k_ref, v_ref) = refs
    tm = h_ref.shape[0]
    nl = 2 * B_WIDTH // LANES
    head = lax.broadcasted_iota(jnp.int32, (CHUNK, A_WIDTH), 1) // HEAD_DIM
    lo = lax.broadcasted_iota(jnp.int32, (PRE_SUB, LANES), 1) < HEAD_DIM
    for sb in range(tm // PRE_SUB):
        rs = slice(sb * PRE_SUB, (sb + 1) * PRE_SUB)
        x = h_ref[rs, :]
        ms = jnp.mean(x * x, axis=-1, keepdims=True)
        xn = x * lax.rsqrt(ms + EPS) * g_ref[...]
        xm = (xn * (1.0 + sc_ref[...]) + sh_ref[...]).astype(BF16)
        z = _dot(xm, win_ref[...])
        zcols = lambda c0, c1: z[:, c0:c1]

        a = _gelu_tanh(zcols(0, OFF_B))
        u = a[:, :A_WIDTH]
        v = a[:, A_WIDTH:]
        msv = _dot((v * v).astype(BF16), hm_ref[...])
        vn = (v * lax.rsqrt(msv + EPS) * gv_ref[...]).astype(BF16)
        for ck in range(PRE_SUB // CHUNK):
            rows = slice(ck * CHUNK, (ck + 1) * CHUNK)
            vc = vn[rows]
            vstack = jnp.concatenate([jnp.where(head == hh, vc, jnp.zeros_like(vc)) for hh in range(A_HEADS)], axis=0)
            sv = bs_ref[...] + _dot(ws_ref[...], vstack)
            sgu_ref[sb * PRE_SUB + ck * CHUNK:sb * PRE_SUB + (ck + 1) * CHUNK, :] = (u[rows] * sv).astype(BF16)

        f = _dot(zcols(OFF_B, OFF_Q).astype(BF16), wdft_ref[...])
        if rope:
            for j in range(nl):
                fsc_ref[j, rs, :] = f[:, j * LANES:(j + 1) * LANES]
        else:
            re_ref[rs, :] = f[:, :B_WIDTH].astype(BF16)
            im_ref[rs, :] = f[:, B_WIDTH:].astype(BF16)

        def rot(t):
            if not rope:
                return t
            return (t * cos_ref[rs, :] + pltpu.roll(t, LANES - 16, 1) * s1_ref[rs, :]
                    + pltpu.roll(t, 16, 1) * s2_ref[rs, :])

        zq = zcols(OFF_Q, OFF_K)
        for j in range(C_WIDTH // LANES):
            q_ref[rs, j * LANES:(j + 1) * LANES] = (
                rot(zq[:, j * LANES:(j + 1) * LANES]) * (HEAD_DIM ** -0.5 * LOG2E)).astype(BF16)

        zkv = zcols(OFF_K, N_IN)
        kk = rot(zkv[:, :KV_WIDTH])
        kr = pltpu.roll(kk, HEAD_DIM, 1)
        k_ref[rs, :LANES] = jnp.where(lo, kk, kr).astype(BF16)
        k_ref[rs, LANES:] = jnp.where(lo, kr, kk).astype(BF16)
        vv = zkv[:, KV_WIDTH:]
        vr = pltpu.roll(vv, HEAD_DIM, 1)
        v_ref[rs, :LANES] = jnp.where(lo, vv, vr).astype(BF16)
        v_ref[rs, LANES:] = jnp.where(lo, vr, vv).astype(BF16)

    if rope:
        nr = tm // GRID_W
        for cc in range(GRID_W):
            for j in range(nl):
                blk = fsc_ref[j, pl.ds(cc, nr, stride=GRID_W), :].astype(BF16)
                dst = re_ref if j < nl // 2 else im_ref
                c0 = cc * B_WIDTH + (j % (nl // 2)) * LANES
                dst[:, c0:c0 + LANES] = blk


def _premix(h, mods, g, win, ws, bsx, gv, hm, wdft, rope_tabs, *, tm, tiles_per_batch, n_pos):
    T = h.shape[0]
    rope = rope_tabs is not None
    if tiles_per_batch is None:
        mrow = lambda i: 4
    else:
        mrow = lambda i: i // tiles_per_batch
    const2 = lambda i: (0, 0)
    in_specs = [
        pl.BlockSpec((tm, D_MODEL), lambda i: (i, 0)),
        pl.BlockSpec((None, None, 1, D_MODEL), lambda i: (mrow(i), 0, 0, 0)),
        pl.BlockSpec((None, None, 1, D_MODEL), lambda i: (mrow(i), 1, 0, 0)),
        pl.BlockSpec((1, D_MODEL), const2),
        pl.BlockSpec((D_MODEL, N_IN), const2),
        pl.BlockSpec((CHUNK, A_HEADS * CHUNK), const2),
        pl.BlockSpec((CHUNK, A_WIDTH), const2),
        pl.BlockSpec((1, A_WIDTH), const2),
        pl.BlockSpec((A_WIDTH, A_WIDTH), const2),
        pl.BlockSpec((B_WIDTH, 2 * B_WIDTH), const2),
    ]
    args = [h, mods, mods, g, win, ws, bsx, gv, hm, wdft]
    if rope:
        nt = n_pos // tm
        for t in rope_tabs:
            in_specs.append(pl.BlockSpec((tm, LANES), lambda i: (i % nt, 0)))
            args.append(t)
    widths = (A_WIDTH, B_WIDTH, B_WIDTH, C_WIDTH, 2 * KV_WIDTH, 2 * KV_WIDTH)
    out_specs = [pl.BlockSpec((tm, w), lambda i: (i, 0)) for w in widths]
    out_shape = [jax.ShapeDtypeStruct((T, w), BF16) for w in widths]
    scratch = []
    if rope:
        for o in (1, 2):
            out_specs[o] = pl.BlockSpec((tm // GRID_W, GRID_W * B_WIDTH), lambda i: (i, 0))
            out_shape[o] = jax.ShapeDtypeStruct((T // GRID_W, GRID_W * B_WIDTH), BF16)
        scratch = [pltpu.VMEM((2 * B_WIDTH // LANES, tm, LANES), F32)]
    return pl.pallas_call(
        functools.partial(_premix_kernel, rope=rope),
        grid=(T // tm,),
        in_specs=in_specs, out_specs=out_specs, out_shape=out_shape, scratch_shapes=scratch,
        compiler_params=_cparams(("parallel",)),
    )(*args)


def _dft1_kernel(m1_ref, twc_ref, tws_ref, re_ref, im_ref, o_ref):
    x = jnp.concatenate([re_ref[0], im_ref[0]], axis=0)
    a = _dot(m1_ref[...], x)
    nr = a.shape[0] // 2
    are, aim = a[:nr], a[nr:]
    c, s = twc_ref[...], tws_ref[...]
    bre = (are * c + aim * s).astype(BF16)
    bim = (aim * c - are * s).astype(BF16)
    for cl in range(o_ref.shape[2]):
        o_ref[0, 0, cl] = bre[:, cl * B_WIDTH:(cl + 1) * B_WIDTH]
        o_ref[0, 1, cl] = bim[:, cl * B_WIDTH:(cl + 1) * B_WIDTH]


def _dft3_kernel(m3_ref, x_ref, o_ref, xs_ref, os_ref):
    kb = x_ref.shape[3]
    nl = B_WIDTH // LANES
    for p in range(2):
        for c in range(GRID_W):
            xc = x_ref[0, p, c].astype(F32)
            for j in range(nl):
                xs_ref[p * nl + j, c * kb:(c + 1) * kb, :] = xc[:, j * LANES:(j + 1) * LANES]
    for k in range(kb):
        rows = [jnp.concatenate([xs_ref[p * nl + j, pl.ds(k, GRID_W, stride=kb), :] for j in range(nl)], axis=1)
                for p in range(2)]
        res = _dot(m3_ref[...], jnp.concatenate(rows, axis=0).astype(BF16))
        for j in range(nl):
            os_ref[j, pl.ds(k, GRID_W, stride=kb), :] = res[:, j * LANES:(j + 1) * LANES]
    for k2 in range(GRID_W):
        o_ref[0, k2] = jnp.concatenate(
            [os_ref[j, k2 * kb:(k2 + 1) * kb, :] for j in range(nl)], axis=1).astype(BF16)


def _seq_dft(re, im, bsz, n_rows, m1, twc, tws, m3):
    ncol = GRID_W * B_WIDTH
    tn = 4096
    re3 = re.reshape(bsz, n_rows, ncol)
    im3 = im.reshape(bsz, n_rows, ncol)
    st1 = pl.pallas_call(
        _dft1_kernel,
        grid=(ncol // tn, bsz),
        in_specs=[pl.BlockSpec((2 * n_rows, 2 * n_rows), lambda j, b: (0, 0)),
                  pl.BlockSpec((n_rows, tn), lambda j, b: (0, j)),
                  pl.BlockSpec((n_rows, tn), lambda j, b: (0, j)),
                  pl.BlockSpec((1, n_rows, tn), lambda j, b: (b, 0, j)),
                  pl.BlockSpec((1, n_rows, tn), lambda j, b: (b, 0, j))],
        out_specs=pl.BlockSpec((1, 2, tn // B_WIDTH, n_rows, B_WIDTH), lambda j, b: (b, 0, j, 0, 0)),
        out_shape=jax.ShapeDtypeStruct((bsz, 2, GRID_W, n_rows, B_WIDTH), BF16),
        compiler_params=_cparams(("parallel", "parallel")),
    )(m1, twc, tws, re3, im3)
    kb = min(n_rows, 16)
    y = pl.pallas_call(
        _dft3_kernel,
        grid=(bsz, n_rows // kb),
        in_specs=[pl.BlockSpec((GRID_W, 2 * GRID_W), lambda b, j: (0, 0)),
                  pl.BlockSpec((1, 2, GRID_W, kb, B_WIDTH), lambda b, j: (b, 0, 0, j, 0))],
        out_specs=pl.BlockSpec((1, GRID_W, kb, B_WIDTH), lambda b, j: (b, 0, j, 0)),
        out_shape=jax.ShapeDtypeStruct((bsz, GRID_W, n_rows, B_WIDTH), BF16),
        scratch_shapes=[pltpu.VMEM((2 * B_WIDTH // LANES, GRID_W * kb, LANES), F32),
                        pltpu.VMEM((B_WIDTH // LANES, GRID_W * kb, LANES), F32)],
        compiler_params=_cparams(("parallel", "parallel")),
    )(m3, st1)
    return y.reshape(bsz * GRID_W * n_rows, B_WIDTH)


def _ctx_dft_kernel(m_ref, re_ref, im_ref, o_ref):
    x = jnp.concatenate([re_ref[0], im_ref[0]], axis=0)
    o_ref[0] = _dot(m_ref[...], x).astype(BF16)


def _ctx_dft(re, im, bsz, m, mat):
    re3 = re.reshape(bsz, m, B_WIDTH)
    im3 = im.reshape(bsz, m, B_WIDTH)
    y = pl.pallas_call(
        _ctx_dft_kernel,
        grid=(bsz,),
        in_specs=[pl.BlockSpec((m, 2 * m), lambda b: (0, 0)),
                  pl.BlockSpec((1, m, B_WIDTH), lambda b: (b, 0, 0)),
                  pl.BlockSpec((1, m, B_WIDTH), lambda b: (b, 0, 0))],
        out_specs=pl.BlockSpec((1, m, B_WIDTH), lambda b: (b, 0, 0)),
        out_shape=jax.ShapeDtypeStruct((bsz, m, B_WIDTH), BF16),
        compiler_params=_cparams(("parallel",)),
    )(mat, re3, im3)
    return y.reshape(bsz * m, B_WIDTH)


ATTN_SUBBLOCKS = 4


def _attn_kernel(sink_ref, *refs, local, nb):
    if local:
        q_ref, kp_ref, kc_ref, kn_ref, vp_ref, vc_ref, vn_ref, kx_ref, vx_ref, o_ref = refs
    else:
        q_ref, kx_ref, vx_ref, o_ref = refs
    tq = BLOCK if local else q_ref.shape[1]
    nsb = q_ref.shape[1] // tq
    lo = lax.broadcasted_iota(jnp.int32, (tq, LANES), 1) < HEAD_DIM
    zero = jnp.zeros((tq, LANES), BF16)
    lo4 = lax.broadcasted_iota(jnp.int32, (4 * tq, LANES), 1) < HEAD_DIM
    if local:
        j = pl.program_id(1)
        row = lax.broadcasted_iota(jnp.int32, (BLOCK, BLOCK), 0)
        col = lax.broadcasted_iota(jnp.int32, (BLOCK, BLOCK), 1)

        def key_blocks(p_ref, c_ref, n_ref, gl):
            return ([p_ref[0][:, gl]] + [c_ref[0][sb * BLOCK:(sb + 1) * BLOCK, gl] for sb in range(nsb)]
                    + [n_ref[0][:, gl]])
    jobs = []
    for g in range(2):
        gl = slice(g * LANES, (g + 1) * LANES)
        if local:
            kb = key_blocks(kp_ref, kc_ref, kn_ref, gl)
            vb = key_blocks(vp_ref, vc_ref, vn_ref, gl)
        for sb in range(nsb):
            if local:
                kg = jnp.concatenate(kb[sb:sb + 3] + [kx_ref[0][:, gl]], axis=0)
                vg = jnp.concatenate(vb[sb:sb + 3] + [vx_ref[0][:, gl]], axis=0)
            else:
                kg = kx_ref[0][:, gl]
                vg = vx_ref[0][:, gl]
            q = q_ref[0, sb * tq:(sb + 1) * tq, :]
            qs = []
            for p in range(2):
                qp = q[:, g * 2 * LANES + p * LANES: g * 2 * LANES + (p + 1) * LANES]
                qs.append(jnp.where(lo, qp, zero))
                qs.append(jnp.where(lo, zero, qp))
            q4 = jnp.concatenate(qs, axis=0)
            s4 = lax.dot_general(q4, kg, (((1,), (1,)), ((), ())), preferred_element_type=F32)
            jobs.append((g, sb, s4, vg))
    for g, sb, s4, vg in jobs:
        if local:
            valid_prev = (col >= row) & (j > 0) if sb == 0 else (col >= row)
            valid_next = (col <= row) & (j < nb // nsb - 1) if sb == nsb - 1 else (col <= row)
        ps, sinks = [], []
        for hl in range(4):
            s = s4[hl * tq:(hl + 1) * tq]
            if local:
                s = jnp.concatenate(
                    [jnp.where(valid_prev, s[:, :BLOCK], NEG), s[:, BLOCK:2 * BLOCK],
                     jnp.where(valid_next, s[:, 2 * BLOCK:3 * BLOCK], NEG), s[:, 3 * BLOCK:]], axis=1)
            sk = sink_ref[g * 4 + hl] * LOG2E
            m = jnp.maximum(jnp.max(s, axis=-1, keepdims=True), sk)
            ps.append(jnp.exp2(s - m).astype(BF16))
            sinks.append(jnp.exp2(sk - m))
        lov = lax.broadcasted_iota(jnp.int32, vg.shape, 1) < HEAD_DIM
        ve = jnp.where(lov, vg, jnp.ones_like(vg))
        o4 = _dot(jnp.concatenate(ps, axis=0), ve)
        o4 = o4 + jnp.where(lo4, 0.0, jnp.concatenate(sinks, axis=0))
        r4 = pltpu.roll(o4, HEAD_DIM, 1)
        for p in range(2):
            ev = slice((2 * p) * tq, (2 * p + 1) * tq)
            od = slice((2 * p + 1) * tq, (2 * p + 2) * tq)
            o_pair = jnp.where(lo, o4[ev] / r4[ev], r4[od] / o4[od])
            c0 = g * 2 * LANES + p * LANES
            o_ref[0, sb * tq:(sb + 1) * tq, c0:c0 + LANES] = o_pair.astype(BF16)


def _window_attention(q, k, v, kx, vx, sink, bsz, n, m):
    nb = n // BLOCK
    q3 = q.reshape(bsz, n, C_WIDTH)
    k3 = k.reshape(bsz, n, 2 * KV_WIDTH)
    v3 = v.reshape(bsz, n, 2 * KV_WIDTH)
    kx3 = kx.reshape(bsz, m, 2 * KV_WIDTH)
    vx3 = vx.reshape(bsz, m, 2 * KV_WIDTH)
    nsb = ATTN_SUBBLOCKS
    edge = lambda f: pl.BlockSpec((1, BLOCK, 2 * KV_WIDTH), f)
    own = pl.BlockSpec((1, nsb * BLOCK, 2 * KV_WIDTH), lambda b, i, s: (b, i, 0))
    prev = lambda b, i, s: (b, jnp.maximum(i * nsb - 1, 0), 0)
    cur = lambda b, i, s: (b, i, 0)
    nxt = lambda b, i, s: (b, jnp.minimum((i + 1) * nsb, nb - 1), 0)
    ctxs = pl.BlockSpec((1, m, 2 * KV_WIDTH), lambda b, i, s: (b, 0, 0))
    o = pl.pallas_call(
        functools.partial(_attn_kernel, local=True, nb=nb),
        grid_spec=pltpu.PrefetchScalarGridSpec(
            num_scalar_prefetch=1,
            grid=(bsz, nb // nsb),
            in_specs=[pl.BlockSpec((1, nsb * BLOCK, C_WIDTH), cur),
                      edge(prev), own, edge(nxt), edge(prev), own, edge(nxt), ctxs, ctxs],
            out_specs=pl.BlockSpec((1, nsb * BLOCK, C_WIDTH), cur)),
        out_shape=jax.ShapeDtypeStruct((bsz, n, C_WIDTH), BF16),
        compiler_params=_cparams(("parallel", "parallel")),
    )(sink, q3, k3, k3, k3, v3, v3, v3, kx3, vx3)
    return o.reshape(bsz * n, C_WIDTH)


def _context_attention(q, kx, vx, sink, bsz, m):
    q3 = q.reshape(bsz, m, C_WIDTH)
    kx3 = kx.reshape(bsz, m, 2 * KV_WIDTH)
    vx3 = vx.reshape(bsz, m, 2 * KV_WIDTH)
    ctxs = pl.BlockSpec((1, m, 2 * KV_WIDTH), lambda b, s: (b, 0, 0))
    o = pl.pallas_call(
        functools.partial(_attn_kernel, local=False, nb=1),
        grid_spec=pltpu.PrefetchScalarGridSpec(
            num_scalar_prefetch=1,
            grid=(bsz,),
            in_specs=[pl.BlockSpec((1, m, C_WIDTH), lambda b, s: (b, 0, 0)), ctxs, ctxs],
            out_specs=pl.BlockSpec((1, m, C_WIDTH), lambda b, s: (b, 0, 0))),
        out_shape=jax.ShapeDtypeStruct((bsz, m, C_WIDTH), BF16),
        compiler_params=_cparams(("parallel",)),
    )(sink, q3, kx3, vx3)
    return o.reshape(bsz * m, C_WIDTH)


def _postmix_kernel(*refs, route, ncast):
    if ncast:
        n_in = 11 + (2 if route else 0)
        for src, dst in zip(refs[n_in:n_in + ncast], refs[len(refs) - ncast:]):
            dst[...] = src[...].astype(BF16)
        refs = refs[:n_in] + refs[n_in + ncast:len(refs) - ncast]
    h_ref = refs[5]
    for sb in range(h_ref.shape[0] // MOE_TB):
        _postmix_rows(slice(sb * MOE_TB, (sb + 1) * MOE_TB), sb, refs, route)


def _postmix_rows(rows, sb, refs, route):
    (sgu_ref, yf_ref, at_ref, wf_ref, wo_ref, h_ref, gp_ref, gt_ref, gf_ref, sh_ref, sc_ref) = refs[:11]
    if route:
        wr_ref, br_ref, hn_ref, y_ref, cmb_ref, cnt_ref = refs[11:]
    else:
        hn_ref, y_ref = refs[11:]
    fm = _dot(yf_ref[rows, :], wf_ref[...]).astype(BF16)
    o = (_dot(sgu_ref[rows, :], wo_ref[:A_WIDTH, :])
         + _dot(fm, wo_ref[A_WIDTH:A_WIDTH + B_WIDTH, :])
         + _dot(at_ref[rows, :], wo_ref[A_WIDTH + B_WIDTH:, :]))
    ms = jnp.mean(o * o, axis=-1, keepdims=True)
    hn = h_ref[rows, :] + gt_ref[...] * (o * lax.rsqrt(ms + EPS) * gp_ref[...])
    hn_ref[rows, :] = hn
    ms2 = jnp.mean(hn * hn, axis=-1, keepdims=True)
    y = hn * lax.rsqrt(ms2 + EPS) * gf_ref[...] * (1.0 + sc_ref[...]) + sh_ref[...]
    y_ref[rows, :] = y.astype(BF16)
    if route:
        yh = y.astype(BF16)
        yl = (y - yh.astype(F32)).astype(BF16)
        w = wr_ref[...]
        wh = w.astype(BF16)
        wl = (w - wh.astype(F32)).astype(BF16)
        lg = _dot(yh, wh) + _dot(yl, wh) + _dot(yh, wl) + br_ref[...]
        lane = lax.broadcasted_iota(jnp.int32, lg.shape, 1)
        lg = jnp.where(lane < N_EXPERTS, lg, NEG)
        m1 = jnp.max(lg, axis=-1, keepdims=True)
        i1 = jnp.min(jnp.where(lg == m1, lane, LANES), axis=-1, keepdims=True)
        lg2 = jnp.where(lane == i1, NEG, lg)
        m2 = jnp.max(lg2, axis=-1, keepdims=True)
        i2 = jnp.min(jnp.where(lg2 == m2, lane, LANES), axis=-1, keepdims=True)
        e2 = jnp.exp(m2 - m1)
        g1 = 1.0 / (1.0 + e2)
        g2 = e2 * g1
        cmb = jnp.where(lane == i1, g1, 0.0) + jnp.where(lane == i2, g2, 0.0)
        cmb_ref[rows, :] = cmb
        cnt_ref[sb] = jnp.sum(jnp.where(cmb > 0.0, 1.0, 0.0), axis=0, keepdims=True)


def _postmix(sgu, yf, at, wf, wo, h, mods, gp, gf, router, *, tm, tiles_per_batch, side_cast=()):
    T = h.shape[0]
    if tiles_per_batch is None:
        mrow = lambda i: 4
    else:
        mrow = lambda i: i // tiles_per_batch
    const2 = lambda i: (0, 0)
    row = lambda w: pl.BlockSpec((tm, w), lambda i: (i, 0))
    mod = lambda j: pl.BlockSpec((None, None, 1, D_MODEL), lambda i: (mrow(i), j, 0, 0))
    in_specs = [row(A_WIDTH), row(B_WIDTH), row(C_WIDTH),
                pl.BlockSpec((B_WIDTH, B_WIDTH), const2),
                pl.BlockSpec((D_MODEL, D_MODEL), const2),
                row(D_MODEL),
                pl.BlockSpec((1, D_MODEL), const2),
                mod(2),
                pl.BlockSpec((1, D_MODEL), const2),
                mod(3), mod(4)]
    args = [sgu, yf, at, wf, wo, h, gp, mods, gf, mods, mods]
    out_specs = [row(D_MODEL), row(D_MODEL)]
    out_shape = [jax.ShapeDtypeStruct((T, D_MODEL), F32), jax.ShapeDtypeStruct((T, D_MODEL), BF16)]
    route = router is not None
    if route:
        in_specs += [pl.BlockSpec((D_MODEL, LANES), const2), pl.BlockSpec((1, LANES), const2)]
        args += list(router)
        out_specs += [row(LANES), pl.BlockSpec((tm // MOE_TB, 1, LANES), lambda i: (i, 0, 0))]
        out_shape += [jax.ShapeDtypeStruct((T, LANES), F32), jax.ShapeDtypeStruct((T // MOE_TB, 1, LANES), F32)]
    for a in side_cast:
        spec = pl.BlockSpec((a.shape[0] // (T // tm), a.shape[1]), lambda i: (i, 0))
        in_specs.append(spec)
        args.append(a)
        out_specs.append(spec)
        out_shape.append(jax.ShapeDtypeStruct(a.shape, BF16))
    return pl.pallas_call(
        functools.partial(_postmix_kernel, route=route, ncast=len(side_cast)),
        grid=(T // tm,),
        in_specs=in_specs, out_specs=out_specs, out_shape=out_shape,
        compiler_params=_cparams(("parallel",)),
    )(*args)


def _ffn_epilogue(f, h_ref, gp_ref, gt_ref, o_ref):
    ms = jnp.mean(f * f, axis=-1, keepdims=True)
    o_ref[...] = h_ref[...] + gt_ref[...] * (f * lax.rsqrt(ms + EPS) * gp_ref[...])


FF_CHUNK = 512


def _swiglu_tile(y, wg_ref, wu_ref, wd_ref, act_ref):
    for c in range(D_FF // FF_CHUNK):
        cols = slice(c * FF_CHUNK, (c + 1) * FF_CHUNK)
        gate = _dot(y, wg_ref[:, cols])
        up = _dot(y, wu_ref[:, cols])
        act_ref[:, cols] = (gate * jax.nn.sigmoid(gate) * up).astype(BF16)
    return _dot(act_ref[...], wd_ref[...])


def _ffn_dense_kernel(y_ref, wg_ref, wu_ref, wd_ref, h_ref, gp_ref, gt_ref, o_ref, act_ref):
    f = _swiglu_tile(y_ref[...], wg_ref, wu_ref, wd_ref, act_ref)
    _ffn_epilogue(f, h_ref, gp_ref, gt_ref, o_ref)


def _ffn_dense_cast_kernel(y_ref, wg_ref, wu_ref, wd_ref, h_ref, gp_ref, gt_ref, c0_ref, c1_ref, c2_ref,
                           o_ref, d0_ref, d1_ref, d2_ref, z_ref, act_ref):
    d0_ref[...] = c0_ref[...].astype(BF16)
    d1_ref[...] = c1_ref[...].astype(BF16)
    d2_ref[...] = c2_ref[...].astype(BF16)
    z_ref[...] = jnp.zeros_like(z_ref)
    f = _swiglu_tile(y_ref[...], wg_ref, wu_ref, wd_ref, act_ref)
    _ffn_epilogue(f, h_ref, gp_ref, gt_ref, o_ref)


def _ffn_dense(y, wg, wu, wd, h, mods, gp, *, tm, tiles_per_batch, side_cast=(), zero_rows=0):
    T = h.shape[0]
    nt = T // tm
    if tiles_per_batch is None:
        mrow = lambda i: 4
    else:
        mrow = lambda i: i // tiles_per_batch
    resident = pl.Buffered(1)
    in_specs = [pl.BlockSpec((tm, D_MODEL), lambda i: (i, 0)),
                pl.BlockSpec((D_MODEL, D_FF), lambda i: (0, 0), pipeline_mode=resident),
                pl.BlockSpec((D_MODEL, D_FF), lambda i: (0, 0), pipeline_mode=resident),
                pl.BlockSpec((D_FF, D_MODEL), lambda i: (0, 0), pipeline_mode=resident),
                pl.BlockSpec((tm, D_MODEL), lambda i: (i, 0)),
                pl.BlockSpec((1, D_MODEL), lambda i: (0, 0)),
                pl.BlockSpec((None, None, 1, D_MODEL), lambda i: (mrow(i), 5, 0, 0))]
    out_specs = [pl.BlockSpec((tm, D_MODEL), lambda i: (i, 0))]
    out_shape = [jax.ShapeDtypeStruct((T, D_MODEL), F32)]
    for a in side_cast:
        spec = pl.BlockSpec((a.shape[0] // nt, a.shape[1]), lambda i: (i, 0))
        in_specs.append(spec)
        out_specs.append(spec)
        out_shape.append(jax.ShapeDtypeStruct(a.shape, BF16))
    if side_cast:
        out_specs.append(pl.BlockSpec((zero_rows // nt, D_MODEL), lambda i: (i, 0)))
        out_shape.append(jax.ShapeDtypeStruct((zero_rows, D_MODEL), BF16))
    res = pl.pallas_call(
        _ffn_dense_cast_kernel if side_cast else _ffn_dense_kernel,
        grid=(nt,),
        in_specs=in_specs, out_specs=out_specs, out_shape=out_shape,
        scratch_shapes=[pltpu.VMEM((tm, D_FF), BF16)],
        compiler_params=_cparams(("parallel",)),
    )(y, wg, wu, wd, h, gp, mods, *side_cast)
    return res[0], tuple(res[1:])


MOE_TB = 512
MOE_TM = 512
SUB = 64
NSUB = MOE_TB // SUB
RUN_ALIGN = 16


def _moe_tile_bound(T):
    nt = T // MOE_TB
    rows = 2 * T + nt * N_EXPERTS * (RUN_ALIGN - 1) + N_EXPERTS * (SUB + MOE_TM - 1)
    return -(-rows // (2 * MOE_TM)) * 2


def _moe_plan(cnt, T):
    p = (cnt + RUN_ALIGN - 1) // RUN_ALIGN * RUN_ALIGN
    base = jnp.cumsum(p, axis=0) - p
    used = jnp.sum(p, axis=0)
    tiles = (used + SUB + MOE_TM - 1) // MOE_TM
    tend = jnp.cumsum(tiles)
    off = (tend - tiles) * MOE_TM
    rowbase = (off[None, :] + base).astype(jnp.int32)
    nsub = (cnt + SUB - 1) // SUB
    slotbase = (jnp.cumsum(nsub, axis=1) - nsub).astype(jnp.int32)
    nt_bound = _moe_tile_bound(T)
    j = jnp.arange(nt_bound, dtype=jnp.int32)
    tile_e = jnp.minimum(jnp.sum(j[:, None] >= tend[None, :], axis=1), N_EXPERTS - 1).astype(jnp.int32)
    nvalid = tend[-1].astype(jnp.int32)
    tile_blk = jnp.minimum(j, nvalid - 1)
    return rowbase, slotbase, tile_e, tile_blk, nvalid.reshape(1)


NSLOT = 2 * MOE_TB // SUB + N_EXPERTS
SLOT_ROWS = NSLOT * SUB
SLOT_CHUNK = 512


def _block_slots(sb_ref, cn_ref, b):
    last = b * N_EXPERTS + N_EXPERTS - 1
    return sb_ref[last] + (cn_ref[last] + SUB - 1) // SUB


def _for_each_run_slot(cn_ref, b, fn):
    for e in range(N_EXPERTS):
        for s in range(NSUB):
            @pl.when(cn_ref[b * N_EXPERTS + e] > SUB * s)
            def _(e=e, s=s):
                fn(e, s)


def _run_copy_out(stage, ys_ref, sems, rb_ref, sb_ref, b, par, e, s):
    slot = sb_ref[b * N_EXPERTS + e] + s
    r0 = pl.multiple_of(rb_ref[b * N_EXPERTS + e] + SUB * s, RUN_ALIGN)
    return pltpu.make_async_copy(stage.at[par, pl.ds(pl.multiple_of(slot * SUB, SUB), SUB), :],
                                 ys_ref.at[pl.ds(r0, SUB), :], sems.at[par, e, s])


def _dispatch_kernel(rb_ref, sb_ref, cn_ref, y_ref, cmb_ref, sbv_ref, ltri_ref, ysin_ref,
                     tok_ref, ys_ref, stage, sems):
    del ysin_ref
    b = pl.program_id(0)
    nb = pl.num_programs(0)
    par = b % 2
    cmb = cmb_ref[...]
    sel = cmb > 0.0
    rk = _dot(ltri_ref[...], jnp.where(sel, 1.0, 0.0).astype(BF16))
    srow = jnp.where(sel, rk + sbv_ref[...], -1.0)
    nsel = jnp.sum(jnp.where(sel, 1.0, 0.0), axis=-1, keepdims=True)
    sa = jnp.max(srow, axis=-1, keepdims=True)
    sb = jnp.where(nsel > 1.5, jnp.sum(jnp.where(sel, srow, 0.0), axis=-1, keepdims=True) - sa, -1.0)
    ga = jnp.sum(jnp.where(srow == sa, cmb, 0.0), axis=-1, keepdims=True)
    gb = jnp.sum(cmb, axis=-1, keepdims=True) - ga
    lane = lax.broadcasted_iota(jnp.int32, cmb.shape, 1)
    tok = jnp.where(lane == 0, sa, jnp.where(lane == 1, sb, jnp.where(lane == 2, ga, jnp.where(lane == 3, gb, 0.0))))
    tok_ref[...] = tok
    tok_t = tok.T
    nslots = _block_slots(sb_ref, cn_ref, b)
    rr0 = lax.broadcasted_iota(jnp.int32, (SLOT_CHUNK, cmb.shape[0]), 0).astype(F32)
    for c in range(SLOT_ROWS // SLOT_CHUNK):
        @pl.when(nslots * SUB > c * SLOT_CHUNK)
        def _(c=c):
            rr = rr0 + float(c * SLOT_CHUNK)
            g = jnp.where((rr == tok_t[0:1, :]) | (rr == tok_t[1:2, :]), 1.0, 0.0).astype(BF16)
            stage[par, c * SLOT_CHUNK:(c + 1) * SLOT_CHUNK, :] = _dot(g, y_ref[...]).astype(BF16)

    @pl.when(b > 0)
    def _():
        _for_each_run_slot(cn_ref, b - 1, lambda e, s: _run_copy_out(
            stage, ys_ref, sems, rb_ref, sb_ref, b - 1, 1 - par, e, s).wait())

    _for_each_run_slot(cn_ref, b, lambda e, s: _run_copy_out(
        stage, ys_ref, sems, rb_ref, sb_ref, b, par, e, s).start())

    @pl.when(b == nb - 1)
    def _():
        _for_each_run_slot(cn_ref, b, lambda e, s: _run_copy_out(
            stage, ys_ref, sems, rb_ref, sb_ref, b, par, e, s).wait())


def _moe_dispatch(y, cmb, rowbase, slotbase, cnt, ys0):
    T = y.shape[0]
    nt = T // MOE_TB
    nt_bound = ys0.shape[0] // MOE_TM
    sbv = jnp.zeros((nt, 1, LANES), F32).at[:, 0, :N_EXPERTS].set((slotbase * SUB).astype(F32))
    ltri = jnp.asarray(np.tril(np.ones((MOE_TB, MOE_TB), np.float32), -1)).astype(BF16)
    imap2 = lambda b, rb, sb, cn: (b, 0)
    return pl.pallas_call(
        _dispatch_kernel,
        grid_spec=pltpu.PrefetchScalarGridSpec(
            num_scalar_prefetch=3,
            grid=(nt,),
            in_specs=[pl.BlockSpec((MOE_TB, D_MODEL), imap2),
                      pl.BlockSpec((MOE_TB, LANES), imap2),
                      pl.BlockSpec((None, 1, LANES), lambda b, rb, sb, cn: (b, 0, 0)),
                      pl.BlockSpec((MOE_TB, MOE_TB), lambda b, rb, sb, cn: (0, 0)),
                      pl.BlockSpec(memory_space=pl.ANY)],
            out_specs=[pl.BlockSpec((MOE_TB, LANES), imap2),
                       pl.BlockSpec(memory_space=pl.ANY)],
            scratch_shapes=[pltpu.VMEM((2, SLOT_ROWS, D_MODEL), BF16),
                            pltpu.SemaphoreType.DMA((2, N_EXPERTS, NSUB))]),
        out_shape=[jax.ShapeDtypeStruct((T, LANES), F32),
                   jax.ShapeDtypeStruct((nt_bound * MOE_TM, D_MODEL), BF16)],
        input_output_aliases={7: 1},
        compiler_params=_cparams(("arbitrary",)),
    )(rowbase.reshape(-1), slotbase.reshape(-1), cnt.reshape(-1), y, cmb, sbv, ltri, ys0)


def _ffn_group_kernel(te_ref, tb_ref, nv_ref, y_ref, wg_ref, wu_ref, wd_ref, o_ref, act_ref):
    del te_ref, tb_ref
    j = pl.program_id(0)

    @pl.when(j < nv_ref[0])
    def _():
        o_ref[...] = _swiglu_tile(y_ref[...], wg_ref, wu_ref, wd_ref, act_ref).astype(BF16)

    @pl.when(j >= nv_ref[0])
    def _():
        o_ref[...] = jnp.zeros_like(o_ref)


def _ffn_group(ys, wg, wu, wd, tile_e, tile_blk, nvalid):
    rows = ys.shape[0]
    resident = pl.Buffered(2)
    return pl.pallas_call(
        _ffn_group_kernel,
        grid_spec=pltpu.PrefetchScalarGridSpec(
            num_scalar_prefetch=3,
            grid=(rows // MOE_TM,),
            in_specs=[pl.BlockSpec((MOE_TM, D_MODEL), lambda j, te, tb, nv: (tb[j], 0)),
                      pl.BlockSpec((None, D_MODEL, D_FF), lambda j, te, tb, nv: (te[j], 0, 0), pipeline_mode=resident),
                      pl.BlockSpec((None, D_MODEL, D_FF), lambda j, te, tb, nv: (te[j], 0, 0), pipeline_mode=resident),
                      pl.BlockSpec((None, D_FF, D_MODEL), lambda j, te, tb, nv: (te[j], 0, 0), pipeline_mode=resident)],
            out_specs=pl.BlockSpec((MOE_TM, D_MODEL), lambda j, te, tb, nv: (j, 0)),
            scratch_shapes=[pltpu.VMEM((MOE_TM, D_FF), BF16)]),
        out_shape=jax.ShapeDtypeStruct((rows, D_MODEL), BF16),
        compiler_params=_cparams(("arbitrary",)),
    )(tile_e, tile_blk, nvalid, ys, wg, wu, wd)


def _run_copy_in(fs_ref, fbuf, sems, rb_ref, sb_ref, b, par, e, s):
    slot = sb_ref[b * N_EXPERTS + e] + s
    r0 = pl.multiple_of(rb_ref[b * N_EXPERTS + e] + SUB * s, RUN_ALIGN)
    return pltpu.make_async_copy(fs_ref.at[pl.ds(r0, SUB), :],
                                 fbuf.at[par, pl.ds(pl.multiple_of(slot * SUB, SUB), SUB), :], sems.at[par, e, s])


def _combine_kernel(rb_ref, sb_ref, cn_ref, tok_ref, h_ref, gp_ref, gt_ref, fs_ref, o_ref, fbuf, sems):
    b = pl.program_id(0)
    nb = pl.num_programs(0)
    par = b % 2

    @pl.when(b == 0)
    def _():
        fbuf[...] = jnp.zeros_like(fbuf)
        _for_each_run_slot(cn_ref, b, lambda e, s: _run_copy_in(
            fs_ref, fbuf, sems, rb_ref, sb_ref, b, par, e, s).start())

    @pl.when(b + 1 < nb)
    def _():
        _for_each_run_slot(cn_ref, b + 1, lambda e, s: _run_copy_in(
            fs_ref, fbuf, sems, rb_ref, sb_ref, b + 1, 1 - par, e, s).start())

    tok = tok_ref[...]
    sa, sb, ga, gb = tok[:, 0:1], tok[:, 1:2], tok[:, 2:3], tok[:, 3:4]
    _for_each_run_slot(cn_ref, b, lambda e, s: _run_copy_in(
        fs_ref, fbuf, sems, rb_ref, sb_ref, b, par, e, s).wait())
    nslots = _block_slots(sb_ref, cn_ref, b)
    nck = SLOT_ROWS // SLOT_CHUNK
    for c in range(1, nck + 1):
        lo_rows, hi_rows = (c - 1) * SLOT_CHUNK, c * SLOT_CHUNK
        cond = (nslots * SUB > lo_rows) if c == nck else ((nslots * SUB > lo_rows) & (nslots * SUB <= hi_rows))
        if c == 1:
            cond = nslots * SUB <= hi_rows

        @pl.when(cond)
        def _(depth=hi_rows):
            cc = lax.broadcasted_iota(jnp.int32, (tok.shape[0], depth), 1).astype(F32)
            pick = jnp.where(cc == sa, ga, jnp.where(cc == sb, gb, 0.0)).astype(BF16)
            f = _dot(pick, fbuf[par, :depth, :])
            _ffn_epilogue(f, h_ref, gp_ref, gt_ref, o_ref)


def _moe_combine(fs, tok, h, mods, gp, rowbase, slotbase, cnt, *, tiles_per_batch):
    T = h.shape[0]
    nt = T // MOE_TB
    imap2 = lambda b, rb, sb, cn: (b, 0)
    return pl.pallas_call(
        _combine_kernel,
        grid_spec=pltpu.PrefetchScalarGridSpec(
            num_scalar_prefetch=3,
            grid=(nt,),
            in_specs=[pl.BlockSpec((MOE_TB, LANES), imap2),
                      pl.BlockSpec((MOE_TB, D_MODEL), imap2),
                      pl.BlockSpec((1, D_MODEL), lambda b, rb, sb, cn: (0, 0)),
                      pl.BlockSpec((None, None, 1, D_MODEL), lambda b, rb, sb, cn: (b // tiles_per_batch, 5, 0, 0)),
                      pl.BlockSpec(memory_space=pl.ANY)],
            out_specs=pl.BlockSpec((MOE_TB, D_MODEL), imap2),
            scratch_shapes=[pltpu.VMEM((2, SLOT_ROWS, D_MODEL), BF16),
                            pltpu.SemaphoreType.DMA((2, N_EXPERTS, NSUB))]),
        out_shape=jax.ShapeDtypeStruct((T, D_MODEL), F32),
        compiler_params=_cparams(("arbitrary",)),
    )(rowbase.reshape(-1), slotbase.reshape(-1), cnt.reshape(-1), tok, h, gp, mods, fs)


def _ffn_moe(y, cmb, cnt_tiles, wg, wu, wd, h, mods, gp, ys0, *, tiles_per_batch):
    T = h.shape[0]
    cnt = cnt_tiles.reshape(T // MOE_TB, LANES)[:, :N_EXPERTS].astype(jnp.int32)
    rowbase, slotbase, tile_e, tile_blk, nvalid = _moe_plan(cnt, T)
    tok, ys = _moe_dispatch(y, cmb, rowbase, slotbase, cnt, ys0)
    fs = _ffn_group(ys, wg, wu, wd, tile_e, tile_blk, nvalid)
    return _moe_combine(fs, tok, h, mods, gp, rowbase, slotbase, cnt, tiles_per_batch=tiles_per_batch)


def _blockdiag(m, reps):
    n = m.shape[0]
    out = np.zeros((n * reps, n * reps), np.float64)
    for r in range(reps):
        out[r * n:(r + 1) * n, r * n:(r + 1) * n] = m
    return out


def _dft_tables(n_rows):
    n = n_rows * GRID_W
    r = np.arange(n_rows)
    c = np.arange(GRID_W)
    a1 = 2 * np.pi * np.outer(r, r) / n_rows
    c1, s1 = np.cos(a1), np.sin(a1)
    m1 = np.block([[c1, s1], [-s1, c1]]) * 0.125
    at = 2 * np.pi * np.outer(r, c) / n
    twc = np.repeat(np.cos(at), B_WIDTH, axis=1)
    tws = np.repeat(np.sin(at), B_WIDTH, axis=1)
    a3 = 2 * np.pi * np.outer(c, c) / GRID_W
    m3 = np.concatenate([np.cos(a3), np.sin(a3)], axis=1) * (8.0 / np.sqrt(n))
    f32 = lambda t: jnp.asarray(t.astype(np.float32))
    return f32(m1).astype(BF16), f32(twc), f32(tws), f32(m3).astype(BF16)


def _channel_dft_table():
    d = np.arange(HEAD_DIM)
    a = 2 * np.pi * np.outer(d, d) / HEAD_DIM
    w = np.concatenate([_blockdiag(np.cos(a), 4), -_blockdiag(np.sin(a), 4)], axis=1) * 0.125
    return jnp.asarray(w.astype(np.float32)).astype(BF16)


def _ctx_dft_table(m):
    p = np.arange(m)
    a = 2 * np.pi * np.outer(p, p) / m
    w = np.concatenate([np.cos(a), np.sin(a)], axis=1) * (8.0 / np.sqrt(m * HEAD_DIM))
    return jnp.asarray(w.astype(np.float32)).astype(BF16)


def _rope_tables(n_tok):
    rows = n_tok // GRID_W
    row = jnp.broadcast_to(jnp.arange(rows)[:, None], (rows, GRID_W)).reshape(-1)
    col = jnp.broadcast_to(jnp.arange(GRID_W)[None, :], (rows, GRID_W)).reshape(-1)
    half = HEAD_DIM // 2
    inv = ROPE_BASE ** (-jnp.arange(0, half, 2, dtype=F32) / half)
    ang = jnp.stack([row.astype(F32)[:, None] * inv, col.astype(F32)[:, None] * inv], axis=1)
    cos, sin = jnp.cos(ang), jnp.sin(ang)
    zer = jnp.zeros_like(sin)
    lay = lambda a, b: jnp.tile(jnp.stack([a, b], axis=2).reshape(n_tok, HEAD_DIM), (1, LANES // HEAD_DIM))
    return lay(cos, cos), lay(-sin, zer), lay(zer, sin)


def kernel(x, c, ctx, c_ctx, w_ada, b_ada, g_mix_pre, g_mix_post, g_ffn_pre, g_ffn_post,
           w_in, w_s, b_s, g_v, w_f, sink, w_out, w_gate_d, w_up_d, w_down_d,
           w_router, b_router, w_gate_e, w_up_e, w_down_e):
    bsz, n_lat, _ = x.shape
    n_ctx = ctx.shape[1]
    T, Tc = bsz * n_lat, bsz * n_ctx
    tm = 512
    tpb = n_lat // tm

    rope_tabs = _rope_tables(n_lat)
    m1, twc, tws, m3 = _dft_tables(n_lat // GRID_W)
    wdft = _channel_dft_table()
    mctx = _ctx_dft_table(n_ctx)
    hm = jnp.asarray(_blockdiag(np.full((HEAD_DIM, HEAD_DIM), 1.0 / HEAD_DIM), A_HEADS).astype(np.float32)).astype(BF16)

    cond8 = jnp.zeros((8, D_MODEL), F32).at[:bsz].set(c).at[4].set(c_ctx)
    h = x.reshape(T, D_MODEL)
    hc = ctx.reshape(Tc, D_MODEL)

    mods_all = _adaln(cond8, w_ada, b_ada[:, None, :]).reshape(DEPTH, 8, 6, 1, D_MODEL)

    for i in range(DEPTH):
        last = i == DEPTH - 1
        mods = mods_all[i]
        win = w_in[i].astype(BF16)
        ws = jnp.transpose(w_s[i], (1, 0, 2)).reshape(CHUNK, A_HEADS * CHUNK).astype(BF16)
        bsx = jnp.repeat(b_s[i].T, HEAD_DIM, axis=1)
        gv = g_v[i].reshape(1, A_WIDTH)
        wf = jax.scipy.linalg.block_diag(*[w_f[i][g] for g in range(4)]).astype(BF16)
        wo = w_out[i].astype(BF16)
        gpre = g_mix_pre[i][None, :]
        gpost = g_mix_post[i][None, :]
        gfpre = g_ffn_pre[i][None, :]
        gfpost = g_ffn_post[i][None, :]
        sk = sink[i]

        sgu, fre, fim, q, k, v = _premix(h, mods, gpre, win, ws, bsx, gv, hm, wdft, rope_tabs,
                                         tm=2 * tm, tiles_per_batch=tpb // 2, n_pos=n_lat)
        sguc, frec, fimc, qc, kc, vc = _premix(hc, mods, gpre, win, ws, bsx, gv, hm, wdft, None,
                                               tm=tm, tiles_per_batch=None, n_pos=n_ctx)
        yf = _seq_dft(fre, fim, bsz, n_lat // GRID_W, m1, twc, tws, m3)
        at = _window_attention(q, k, v, kc, vc, sk, bsz, n_lat, n_ctx)

        if i % 2 == 0:
            router = None
        else:
            j = i // 2
            wr = jnp.zeros((D_MODEL, LANES), F32).at[:, :N_EXPERTS].set(w_router[j])
            br = jnp.zeros((1, LANES), F32).at[0, :N_EXPERTS].set(b_router[j])
            router = (wr, br)
        dense_w = (w_gate_d[i // 2], w_up_d[i // 2], w_down_d[i // 2]) if i % 2 == 0 else ()
        res = _postmix(sgu, yf, at, wf, wo, h, mods, gpost, gfpre, router, tm=2 * tm, tiles_per_batch=tpb // 2,
                       side_cast=dense_w)
        if not last:
            yfc = _ctx_dft(frec, fimc, bsz, n_ctx, mctx)
            atc = _context_attention(qc, kc, vc, sk, bsz, n_ctx)
            resc = _postmix(sguc, yfc, atc, wf, wo, hc, mods, gpost, gfpre, router, tm=2 * tm, tiles_per_batch=None)

        j = i // 2
        if i % 2 == 0:
            wg, wu, wd = res[-3:]
            side = ()
            if not last:
                jn = (i + 1) // 2
                side = (w_gate_e[jn].reshape(N_EXPERTS * D_MODEL, D_FF), w_up_e[jn].reshape(N_EXPERTS * D_MODEL, D_FF),
                        w_down_e[jn].reshape(N_EXPERTS * D_FF, D_MODEL))
            h, expert_w = _ffn_dense(res[1], wg, wu, wd, res[0], mods, gfpost, tm=tm, tiles_per_batch=tpb,
                                     side_cast=side, zero_rows=_moe_tile_bound(T) * MOE_TM)
            if not last:
                hc, _ = _ffn_dense(resc[1], wg, wu, wd, resc[0], mods, gfpost, tm=tm, tiles_per_batch=None)
        else:
            wg = expert_w[0].reshape(N_EXPERTS, D_MODEL, D_FF)
            wu = expert_w[1].reshape(N_EXPERTS, D_MODEL, D_FF)
            wd = expert_w[2].reshape(N_EXPERTS, D_FF, D_MODEL)
            assert last and tm == MOE_TB, "the expert FFN is only built for the final layer's latent tokens"
            h = _ffn_moe(res[1], res[2], res[3], wg, wu, wd, res[0], mods, gfpost, expert_w[3],
                         tiles_per_batch=n_lat // MOE_TB)
    return h.reshape(bsz, n_lat, D_MODEL)
```

```python
import functools

import numpy as np
import jax
import jax.numpy as jnp
from jax import lax
from jax.experimental import pallas as pl
from jax.experimental.pallas import tpu as pltpu

D_MODEL = 1024
DEPTH = 2
GRID_W = 64
HEAD_DIM = 64
EPS = 1e-6
A_HEADS = 4
A_WIDTH = 256
CHUNK = 128
B_WIDTH = 256
C_Q_HEADS = 8
C_WIDTH = 512
KV_WIDTH = 128
WINDOW = 128
BLOCK = 128
ROPE_BASE = 10000.0
OFF_B = 512
OFF_Q = 768
OFF_K = 1280
OFF_V = 1408
N_IN = 1536
D_FF = 3584
N_EXPERTS = 8

LANES = 128
VMEM_LIMIT = 60 * 1024 * 1024
NEG = -1e30
LOG2E = 1.4426950408889634
assert WINDOW == BLOCK

F32 = jnp.float32
BF16 = jnp.bfloat16


def _dot(a, b):
    return jnp.dot(a, b, preferred_element_type=F32)


def _cparams(sem):
    return pltpu.CompilerParams(dimension_semantics=sem, vmem_limit_bytes=VMEM_LIMIT)


def _adaln_kernel(c_ref, w_ref, b_ref, o_ref):
    c = c_ref[...]
    s = (c * jax.nn.sigmoid(c)).astype(BF16)
    o_ref[...] = _dot(s, w_ref[...].astype(BF16)) + b_ref[...]


def _adaln(cond8, w, b):
    tn = 1536
    nl = w.shape[0]
    return pl.pallas_call(
        _adaln_kernel,
        grid=(nl, 6 * D_MODEL // tn),
        in_specs=[pl.BlockSpec((8, D_MODEL), lambda l, j: (0, 0)),
                  pl.BlockSpec((None, D_MODEL, tn), lambda l, j: (l, 0, j)),
                  pl.BlockSpec((None, 1, tn), lambda l, j: (l, 0, j))],
        out_specs=pl.BlockSpec((None, 8, tn), lambda l, j: (l, 0, j)),
        out_shape=jax.ShapeDtypeStruct((nl, 8, 6 * D_MODEL), F32),
        compiler_params=_cparams(("arbitrary", "arbitrary")),
    )(cond8, w, b)


def _gelu_tanh(x):
    return 0.5 * x * (1.0 + jnp.tanh(0.7978845608028654 * (x + 0.044715 * x * x * x)))


PRE_SUB = 512


def _premix_kernel(*refs, rope):
    if rope:
        (h_ref, sh_ref, sc_ref, g_ref, win_ref, ws_ref, bs_ref, gv_ref, hm_ref, wdft_ref,
         cos_ref, s1_ref, s2_ref, sgu_ref, re_ref, im_ref, q_ref, k_ref, v_ref, fsc_ref) = refs
    else:
        (h_ref, sh_ref, sc_ref, g_ref, win_ref, ws_ref, bs_ref, gv_ref, hm_ref, wdft_ref,
         sgu_ref, re_ref, im_ref, q_ref, k_ref, v_ref) = refs
    tm = h_ref.shape[0]
    nl = 2 * B_WIDTH // LANES
    head = lax.broadcasted_iota(jnp.int32, (CHUNK, A_WIDTH), 1) // HEAD_DIM
    lo = lax.broadcasted_iota(jnp.int32, (PRE_SUB, LANES), 1) < HEAD_DIM
    for sb in range(tm // PRE_SUB):
        rs = slice(sb * PRE_SUB, (sb + 1) * PRE_SUB)
        x = h_ref[rs, :]
        ms = jnp.mean(x * x, axis=-1, keepdims=True)
        xn = x * lax.rsqrt(ms + EPS) * g_ref[...]
        xm = (xn * (1.0 + sc_ref[...]) + sh_ref[...]).astype(BF16)
        z = _dot(xm, win_ref[...])
        zcols = lambda c0, c1: z[:, c0:c1]

        a = _gelu_tanh(zcols(0, OFF_B))
        u = a[:, :A_WIDTH]
        v = a[:, A_WIDTH:]
        msv = _dot((v * v).astype(BF16), hm_ref[...])
        vn = (v * lax.rsqrt(msv + EPS) * gv_ref[...]).astype(BF16)
        for ck in range(PRE_SUB // CHUNK):
            rows = slice(ck * CHUNK, (ck + 1) * CHUNK)
            vc = vn[rows]
            vstack = jnp.concatenate([jnp.where(head == hh, vc, jnp.zeros_like(vc)) for hh in range(A_HEADS)], axis=0)
            sv = bs_ref[...] + _dot(ws_ref[...], vstack)
            sgu_ref[sb * PRE_SUB + ck * CHUNK:sb * PRE_SUB + (ck + 1) * CHUNK, :] = (u[rows] * sv).astype(BF16)

        f = _dot(zcols(OFF_B, OFF_Q).astype(BF16), wdft_ref[...])
        if rope:
            for j in range(nl):
                fsc_ref[j, rs, :] = f[:, j * LANES:(j + 1) * LANES]
        else:
            re_ref[rs, :] = f[:, :B_WIDTH].astype(BF16)
            im_ref[rs, :] = f[:, B_WIDTH:].astype(BF16)

        def rot(t):
            if not rope:
                return t
            return (t * cos_ref[rs, :] + pltpu.roll(t, LANES - 16, 1) * s1_ref[rs, :]
                    + pltpu.roll(t, 16, 1) * s2_ref[rs, :])

        zq = zcols(OFF_Q, OFF_K)
        for j in range(C_WIDTH // LANES):
            q_ref[rs, j * LANES:(j + 1) * LANES] = (
                rot(zq[:, j * LANES:(j + 1) * LANES]) * (HEAD_DIM ** -0.5 * LOG2E)).astype(BF16)

        zkv = zcols(OFF_K, N_IN)
        kk = rot(zkv[:, :KV_WIDTH])
        kr = pltpu.roll(kk, HEAD_DIM, 1)
        k_ref[rs, :LANES] = jnp.where(lo, kk, kr).astype(BF16)
        k_ref[rs, LANES:] = jnp.where(lo, kr, kk).astype(BF16)
        vv = zkv[:, KV_WIDTH:]
        vr = pltpu.roll(vv, HEAD_DIM, 1)
        v_ref[rs, :LANES] = jnp.where(lo, vv, vr).astype(BF16)
        v_ref[rs, LANES:] = jnp.where(lo, vr, vv).astype(BF16)

    if rope:
        nr = tm // GRID_W
        for cc in range(GRID_W):
            for j in range(nl):
                blk = fsc_ref[j, pl.ds(cc, nr, stride=GRID_W), :].astype(BF16)
                dst = re_ref if j < nl // 2 else im_ref
                c0 = cc * B_WIDTH + (j % (nl // 2)) * LANES
                dst[:, c0:c0 + LANES] = blk


def _premix(h, mods, g, win, ws, bsx, gv, hm, wdft, rope_tabs, *, tm, tiles_per_batch, n_pos):
    T = h.shape[0]
    rope = rope_tabs is not None
    if tiles_per_batch is None:
        mrow = lambda i: 4
    else:
        mrow = lambda i: i // tiles_per_batch
    const2 = lambda i: (0, 0)
    in_specs = [
        pl.BlockSpec((tm, D_MODEL), lambda i: (i, 0)),
        pl.BlockSpec((None, None, 1, D_MODEL), lambda i: (mrow(i), 0, 0, 0)),
        pl.BlockSpec((None, None, 1, D_MODEL), lambda i: (mrow(i), 1, 0, 0)),
        pl.BlockSpec((1, D_MODEL), const2),
        pl.BlockSpec((D_MODEL, N_IN), const2),
        pl.BlockSpec((CHUNK, A_HEADS * CHUNK), const2),
        pl.BlockSpec((CHUNK, A_WIDTH), const2),
        pl.BlockSpec((1, A_WIDTH), const2),
        pl.BlockSpec((A_WIDTH, A_WIDTH), const2),
        pl.BlockSpec((B_WIDTH, 2 * B_WIDTH), const2),
    ]
    args = [h, mods, mods, g, win, ws, bsx, gv, hm, wdft]
    if rope:
        nt = n_pos // tm
        for t in rope_tabs:
            in_specs.append(pl.BlockSpec((tm, LANES), lambda i: (i % nt, 0)))
            args.append(t)
    widths = (A_WIDTH, B_WIDTH, B_WIDTH, C_WIDTH, 2 * KV_WIDTH, 2 * KV_WIDTH)
    out_specs = [pl.BlockSpec((tm, w), lambda i: (i, 0)) for w in widths]
    out_shape = [jax.ShapeDtypeStruct((T, w), BF16) for w in widths]
    scratch = []
    if rope:
        for o in (1, 2):
            out_specs[o] = pl.BlockSpec((tm // GRID_W, GRID_W * B_WIDTH), lambda i: (i, 0))
            out_shape[o] = jax.ShapeDtypeStruct((T // GRID_W, GRID_W * B_WIDTH), BF16)
        scratch = [pltpu.VMEM((2 * B_WIDTH // LANES, tm, LANES), F32)]
    return pl.pallas_call(
        functools.partial(_premix_kernel, rope=rope),
        grid=(T // tm,),
        in_specs=in_specs, out_specs=out_specs, out_shape=out_shape, scratch_shapes=scratch,
        compiler_params=_cparams(("parallel",)),
    )(*args)


def _dft1_kernel(m1_ref, twc_ref, tws_ref, re_ref, im_ref, o_ref):
    x = jnp.concatenate([re_ref[0], im_ref[0]], axis=0)
    a = _dot(m1_ref[...], x)
    nr = a.shape[0] // 2
    are, aim = a[:nr], a[nr:]
    c, s = twc_ref[...], tws_ref[...]
    bre = (are * c + aim * s).astype(BF16)
    bim = (aim * c - are * s).astype(BF16)
    for cl in range(o_ref.shape[2]):
        o_ref[0, 0, cl] = bre[:, cl * B_WIDTH:(cl + 1) * B_WIDTH]
        o_ref[0, 1, cl] = bim[:, cl * B_WIDTH:(cl + 1) * B_WIDTH]


def _dft3_kernel(m3_ref, x_ref, o_ref, xs_ref, os_ref):
    kb = x_ref.shape[3]
    nl = B_WIDTH // LANES
    for p in range(2):
        for c in range(GRID_W):
            xc = x_ref[0, p, c].astype(F32)
            for j in range(nl):
                xs_ref[p * nl + j, c * kb:(c + 1) * kb, :] = xc[:, j * LANES:(j + 1) * LANES]
    for k in range(kb):
        rows = [jnp.concatenate([xs_ref[p * nl + j, pl.ds(k, GRID_W, stride=kb), :] for j in range(nl)], axis=1)
                for p in range(2)]
        res = _dot(m3_ref[...], jnp.concatenate(rows, axis=0).astype(BF16))
        for j in range(nl):
            os_ref[j, pl.ds(k, GRID_W, stride=kb), :] = res[:, j * LANES:(j + 1) * LANES]
    for k2 in range(GRID_W):
        o_ref[0, k2] = jnp.concatenate(
            [os_ref[j, k2 * kb:(k2 + 1) * kb, :] for j in range(nl)], axis=1).astype(BF16)


def _seq_dft(re, im, bsz, n_rows, m1, twc, tws, m3):
    ncol = GRID_W * B_WIDTH
    tn = 4096
    re3 = re.reshape(bsz, n_rows, ncol)
    im3 = im.reshape(bsz, n_rows, ncol)
    st1 = pl.pallas_call(
        _dft1_kernel,
        grid=(ncol // tn, bsz),
        in_specs=[pl.BlockSpec((2 * n_rows, 2 * n_rows), lambda j, b: (0, 0)),
                  pl.BlockSpec((n_rows, tn), lambda j, b: (0, j)),
                  pl.BlockSpec((n_rows, tn), lambda j, b: (0, j)),
                  pl.BlockSpec((1, n_rows, tn), lambda j, b: (b, 0, j)),
                  pl.BlockSpec((1, n_rows, tn), lambda j, b: (b, 0, j))],
        out_specs=pl.BlockSpec((1, 2, tn // B_WIDTH, n_rows, B_WIDTH), lambda j, b: (b, 0, j, 0, 0)),
        out_shape=jax.ShapeDtypeStruct((bsz, 2, GRID_W, n_rows, B_WIDTH), BF16),
        compiler_params=_cparams(("parallel", "parallel")),
    )(m1, twc, tws, re3, im3)
    kb = min(n_rows, 16)
    y = pl.pallas_call(
        _dft3_kernel,
        grid=(bsz, n_rows // kb),
        in_specs=[pl.BlockSpec((GRID_W, 2 * GRID_W), lambda b, j: (0, 0)),
                  pl.BlockSpec((1, 2, GRID_W, kb, B_WIDTH), lambda b, j: (b, 0, 0, j, 0))],
        out_specs=pl.BlockSpec((1, GRID_W, kb, B_WIDTH), lambda b, j: (b, 0, j, 0)),
        out_shape=jax.ShapeDtypeStruct((bsz, GRID_W, n_rows, B_WIDTH), BF16),
        scratch_shapes=[pltpu.VMEM((2 * B_WIDTH // LANES, GRID_W * kb, LANES), F32),
                        pltpu.VMEM((B_WIDTH // LANES, GRID_W * kb, LANES), F32)],
        compiler_params=_cparams(("parallel", "parallel")),
    )(m3, st1)
    return y.reshape(bsz * GRID_W * n_rows, B_WIDTH)


def _ctx_dft_kernel(m_ref, re_ref, im_ref, o_ref):
    x = jnp.concatenate([re_ref[0], im_ref[0]], axis=0)
    o_ref[0] = _dot(m_ref[...], x).astype(BF16)


def _ctx_dft(re, im, bsz, m, mat):
    re3 = re.reshape(bsz, m, B_WIDTH)
    im3 = im.reshape(bsz, m, B_WIDTH)
    y = pl.pallas_call(
        _ctx_dft_kernel,
        grid=(bsz,),
        in_specs=[pl.BlockSpec((m, 2 * m), lambda b: (0, 0)),
                  pl.BlockSpec((1, m, B_WIDTH), lambda b: (b, 0, 0)),
                  pl.BlockSpec((1, m, B_WIDTH), lambda b: (b, 0, 0))],
        out_specs=pl.BlockSpec((1, m, B_WIDTH), lambda b: (b, 0, 0)),
        out_shape=jax.ShapeDtypeStruct((bsz, m, B_WIDTH), BF16),
        compiler_params=_cparams(("parallel",)),
    )(mat, re3, im3)
    return y.reshape(bsz * m, B_WIDTH)


ATTN_SUBBLOCKS = 4


def _attn_kernel(sink_ref, *refs, local, nb):
    if local:
        q_ref, kp_ref, kc_ref, kn_ref, vp_ref, vc_ref, vn_ref, kx_ref, vx_ref, o_ref = refs
    else:
        q_ref, kx_ref, vx_ref, o_ref = refs
    tq = BLOCK if local else q_ref.shape[1]
    nsb = q_ref.shape[1] // tq
    lo = lax.broadcasted_iota(jnp.int32, (tq, LANES), 1) < HEAD_DIM
    zero = jnp.zeros((tq, LANES), BF16)
    lo4 = lax.broadcasted_iota(jnp.int32, (4 * tq, LANES), 1) < HEAD_DIM
    if local:
        j = pl.program_id(1)
        row = lax.broadcasted_iota(jnp.int32, (BLOCK, BLOCK), 0)
        col = lax.broadcasted_iota(jnp.int32, (BLOCK, BLOCK), 1)

        def key_blocks(p_ref, c_ref, n_ref, gl):
            return ([p_ref[0][:, gl]] + [c_ref[0][sb * BLOCK:(sb + 1) * BLOCK, gl] for sb in range(nsb)]
                    + [n_ref[0][:, gl]])
    jobs = []
    for g in range(2):
        gl = slice(g * LANES, (g + 1) * LANES)
        if local:
            kb = key_blocks(kp_ref, kc_ref, kn_ref, gl)
            vb = key_blocks(vp_ref, vc_ref, vn_ref, gl)
        for sb in range(nsb):
            if local:
                kg = jnp.concatenate(kb[sb:sb + 3] + [kx_ref[0][:, gl]], axis=0)
                vg = jnp.concatenate(vb[sb:sb + 3] + [vx_ref[0][:, gl]], axis=0)
            else:
                kg = kx_ref[0][:, gl]
                vg = vx_ref[0][:, gl]
            q = q_ref[0, sb * tq:(sb + 1) * tq, :]
            qs = []
            for p in range(2):
                qp = q[:, g * 2 * LANES + p * LANES: g * 2 * LANES + (p + 1) * LANES]
                qs.append(jnp.where(lo, qp, zero))
                qs.append(jnp.where(lo, zero, qp))
            q4 = jnp.concatenate(qs, axis=0)
            s4 = lax.dot_general(q4, kg, (((1,), (1,)), ((), ())), preferred_element_type=F32)
            jobs.append((g, sb, s4, vg))
    for g, sb, s4, vg in jobs:
        if local:
            valid_prev = (col >= row) & (j > 0) if sb == 0 else (col >= row)
            valid_next = (col <= row) & (j < nb // nsb - 1) if sb == nsb - 1 else (col <= row)
        ps, sinks = [], []
        for hl in range(4):
            s = s4[hl * tq:(hl + 1) * tq]
            if local:
                s = jnp.concatenate(
                    [jnp.where(valid_prev, s[:, :BLOCK], NEG), s[:, BLOCK:2 * BLOCK],
                     jnp.where(valid_next, s[:, 2 * BLOCK:3 * BLOCK], NEG), s[:, 3 * BLOCK:]], axis=1)
            sk = sink_ref[g * 4 + hl] * LOG2E
            m = jnp.maximum(jnp.max(s, axis=-1, keepdims=True), sk)
            ps.append(jnp.exp2(s - m).astype(BF16))
            sinks.append(jnp.exp2(sk - m))
        lov = lax.broadcasted_iota(jnp.int32, vg.shape, 1) < HEAD_DIM
        ve = jnp.where(lov, vg, jnp.ones_like(vg))
        o4 = _dot(jnp.concatenate(ps, axis=0), ve)
        o4 = o4 + jnp.where(lo4, 0.0, jnp.concatenate(sinks, axis=0))
        r4 = pltpu.roll(o4, HEAD_DIM, 1)
        for p in range(2):
            ev = slice((2 * p) * tq, (2 * p + 1) * tq)
            od = slice((2 * p + 1) * tq, (2 * p + 2) * tq)
            o_pair = jnp.where(lo, o4[ev] / r4[ev], r4[od] / o4[od])
            c0 = g * 2 * LANES + p * LANES
            o_ref[0, sb * tq:(sb + 1) * tq, c0:c0 + LANES] = o_pair.astype(BF16)


def _window_attention(q, k, v, kx, vx, sink, bsz, n, m):
    nb = n // BLOCK
    q3 = q.reshape(bsz, n, C_WIDTH)
    k3 = k.reshape(bsz, n, 2 * KV_WIDTH)
    v3 = v.reshape(bsz, n, 2 * KV_WIDTH)
    kx3 = kx.reshape(bsz, m, 2 * KV_WIDTH)
    vx3 = vx.reshape(bsz, m, 2 * KV_WIDTH)
    nsb = ATTN_SUBBLOCKS
    edge = lambda f: pl.BlockSpec((1, BLOCK, 2 * KV_WIDTH), f)
    own = pl.BlockSpec((1, nsb * BLOCK, 2 * KV_WIDTH), lambda b, i, s: (b, i, 0))
    prev = lambda b, i, s: (b, jnp.maximum(i * nsb - 1, 0), 0)
    cur = lambda b, i, s: (b, i, 0)
    nxt = lambda b, i, s: (b, jnp.minimum((i + 1) * nsb, nb - 1), 0)
    ctxs = pl.BlockSpec((1, m, 2 * KV_WIDTH), lambda b, i, s: (b, 0, 0))
    o = pl.pallas_call(
        functools.partial(_attn_kernel, local=True, nb=nb),
        grid_spec=pltpu.PrefetchScalarGridSpec(
            num_scalar_prefetch=1,
            grid=(bsz, nb // nsb),
            in_specs=[pl.BlockSpec((1, nsb * BLOCK, C_WIDTH), cur),
                      edge(prev), own, edge(nxt), edge(prev), own, edge(nxt), ctxs, ctxs],
            out_specs=pl.BlockSpec((1, nsb * BLOCK, C_WIDTH), cur)),
        out_shape=jax.ShapeDtypeStruct((bsz, n, C_WIDTH), BF16),
        compiler_params=_cparams(("parallel", "parallel")),
    )(sink, q3, k3, k3, k3, v3, v3, v3, kx3, vx3)
    return o.reshape(bsz * n, C_WIDTH)


def _context_attention(q, kx, vx, sink, bsz, m):
    q3 = q.reshape(bsz, m, C_WIDTH)
    kx3 = kx.reshape(bsz, m, 2 * KV_WIDTH)
    vx3 = vx.reshape(bsz, m, 2 * KV_WIDTH)
    ctxs = pl.BlockSpec((1, m, 2 * KV_WIDTH), lambda b, s: (b, 0, 0))
    o = pl.pallas_call(
        functools.partial(_attn_kernel, local=False, nb=1),
        grid_spec=pltpu.PrefetchScalarGridSpec(
            num_scalar_prefetch=1,
            grid=(bsz,),
            in_specs=[pl.BlockSpec((1, m, C_WIDTH), lambda b, s: (b, 0, 0)), ctxs, ctxs],
            out_specs=pl.BlockSpec((1, m, C_WIDTH), lambda b, s: (b, 0, 0))),
        out_shape=jax.ShapeDtypeStruct((bsz, m, C_WIDTH), BF16),
        compiler_params=_cparams(("parallel",)),
    )(sink, q3, kx3, vx3)
    return o.reshape(bsz * m, C_WIDTH)


def _postmix_kernel(*refs, route, ncast):
    if ncast:
        n_in = 11 + (2 if route else 0)
        for src, dst in zip(refs[n_in:n_in + ncast], refs[len(refs) - ncast:]):
            dst[...] = src[...].astype(BF16)
        refs = refs[:n_in] + refs[n_in + ncast:len(refs) - ncast]
    h_ref = refs[5]
    for sb in range(h_ref.shape[0] // MOE_TB):
        _postmix_rows(slice(sb * MOE_TB, (sb + 1) * MOE_TB), sb, refs, route)


def _postmix_rows(rows, sb, refs, route):
    (sgu_ref, yf_ref, at_ref, wf_ref, wo_ref, h_ref, gp_ref, gt_ref, gf_ref, sh_ref, sc_ref) = refs[:11]
    if route:
        wr_ref, br_ref, hn_ref, y_ref, cmb_ref, cnt_ref = refs[11:]
    else:
        hn_ref, y_ref = refs[11:]
    fm = _dot(yf_ref[rows, :], wf_ref[...]).astype(BF16)
    o = (_dot(sgu_ref[rows, :], wo_ref[:A_WIDTH, :])
         + _dot(fm, wo_ref[A_WIDTH:A_WIDTH + B_WIDTH, :])
         + _dot(at_ref[rows, :], wo_ref[A_WIDTH + B_WIDTH:, :]))
    ms = jnp.mean(o * o, axis=-1, keepdims=True)
    hn = h_ref[rows, :] + gt_ref[...] * (o * lax.rsqrt(ms + EPS) * gp_ref[...])
    hn_ref[rows, :] = hn
    ms2 = jnp.mean(hn * hn, axis=-1, keepdims=True)
    y = hn * lax.rsqrt(ms2 + EPS) * gf_ref[...] * (1.0 + sc_ref[...]) + sh_ref[...]
    y_ref[rows, :] = y.astype(BF16)
    if route:
        yh = y.astype(BF16)
        yl = (y - yh.astype(F32)).astype(BF16)
        w = wr_ref[...]
        wh = w.astype(BF16)
        wl = (w - wh.astype(F32)).astype(BF16)
        lg = _dot(yh, wh) + _dot(yl, wh) + _dot(yh, wl) + br_ref[...]
        lane = lax.broadcasted_iota(jnp.int32, lg.shape, 1)
        lg = jnp.where(lane < N_EXPERTS, lg, NEG)
        m1 = jnp.max(lg, axis=-1, keepdims=True)
        i1 = jnp.min(jnp.where(lg == m1, lane, LANES), axis=-1, keepdims=True)
        lg2 = jnp.where(lane == i1, NEG, lg)
        m2 = jnp.max(lg2, axis=-1, keepdims=True)
        i2 = jnp.min(jnp.where(lg2 == m2, lane, LANES), axis=-1, keepdims=True)
        e2 = jnp.exp(m2 - m1)
        g1 = 1.0 / (1.0 + e2)
        g2 = e2 * g1
        cmb = jnp.where(lane == i1, g1, 0.0) + jnp.where(lane == i2, g2, 0.0)
        cmb_ref[rows, :] = cmb
        cnt_ref[sb] = jnp.sum(jnp.where(cmb > 0.0, 1.0, 0.0), axis=0, keepdims=True)


def _postmix(sgu, yf, at, wf, wo, h, mods, gp, gf, router, *, tm, tiles_per_batch, side_cast=()):
    T = h.shape[0]
    if tiles_per_batch is None:
        mrow = lambda i: 4
    else:
        mrow = lambda i: i // tiles_per_batch
    const2 = lambda i: (0, 0)
    row = lambda w: pl.BlockSpec((tm, w), lambda i: (i, 0))
    mod = lambda j: pl.BlockSpec((None, None, 1, D_MODEL), lambda i: (mrow(i), j, 0, 0))
    in_specs = [row(A_WIDTH), row(B_WIDTH), row(C_WIDTH),
                pl.BlockSpec((B_WIDTH, B_WIDTH), const2),
                pl.BlockSpec((D_MODEL, D_MODEL), const2),
                row(D_MODEL),
                pl.BlockSpec((1, D_MODEL), const2),
                mod(2),
                pl.BlockSpec((1, D_MODEL), const2),
                mod(3), mod(4)]
    args = [sgu, yf, at, wf, wo, h, gp, mods, gf, mods, mods]
    out_specs = [row(D_MODEL), row(D_MODEL)]
    out_shape = [jax.ShapeDtypeStruct((T, D_MODEL), F32), jax.ShapeDtypeStruct((T, D_MODEL), BF16)]
    route = router is not None
    if route:
        in_specs += [pl.BlockSpec((D_MODEL, LANES), const2), pl.BlockSpec((1, LANES), const2)]
        args += list(router)
        out_specs += [row(LANES), pl.BlockSpec((tm // MOE_TB, 1, LANES), lambda i: (i, 0, 0))]
        out_shape += [jax.ShapeDtypeStruct((T, LANES), F32), jax.ShapeDtypeStruct((T // MOE_TB, 1, LANES), F32)]
    for a in side_cast:
        spec = pl.BlockSpec((a.shape[0] // (T // tm), a.shape[1]), lambda i: (i, 0))
        in_specs.append(spec)
        args.append(a)
        out_specs.append(spec)
        out_shape.append(jax.ShapeDtypeStruct(a.shape, BF16))
    return pl.pallas_call(
        functools.partial(_postmix_kernel, route=route, ncast=len(side_cast)),
        grid=(T // tm,),
        in_specs=in_specs, out_specs=out_specs, out_shape=out_shape,
        compiler_params=_cparams(("parallel",)),
    )(*args)


def _ffn_epilogue(f, h_ref, gp_ref, gt_ref, o_ref):
    ms = jnp.mean(f * f, axis=-1, keepdims=True)
    o_ref[...] = h_ref[...] + gt_ref[...] * (f * lax.rsqrt(ms + EPS) * gp_ref[...])


FF_CHUNK = 256


def _swiglu_tile(y, wg_ref, wu_ref, wd_ref, act_ref):
    for c in range(D_FF // FF_CHUNK):
        cols = slice(c * FF_CHUNK, (c + 1) * FF_CHUNK)
        gate = _dot(y, wg_ref[:, cols])
        up = _dot(y, wu_ref[:, cols])
        act_ref[:, cols] = (gate * jax.nn.sigmoid(gate) * up).astype(BF16)
    return _dot(act_ref[...], wd_ref[...])


def _ffn_dense_kernel(y_ref, wg_ref, wu_ref, wd_ref, h_ref, gp_ref, gt_ref, o_ref, act_ref):
    f = _swiglu_tile(y_ref[...], wg_ref, wu_ref, wd_ref, act_ref)
    _ffn_epilogue(f, h_ref, gp_ref, gt_ref, o_ref)


def _ffn_dense_cast_kernel(y_ref, wg_ref, wu_ref, wd_ref, h_ref, gp_ref, gt_ref, c0_ref, c1_ref, c2_ref,
                           o_ref, d0_ref, d1_ref, d2_ref, z_ref, act_ref):
    d0_ref[...] = c0_ref[...].astype(BF16)
    d1_ref[...] = c1_ref[...].astype(BF16)
    d2_ref[...] = c2_ref[...].astype(BF16)
    z_ref[...] = jnp.zeros_like(z_ref)
    f = _swiglu_tile(y_ref[...], wg_ref, wu_ref, wd_ref, act_ref)
    _ffn_epilogue(f, h_ref, gp_ref, gt_ref, o_ref)


def _ffn_dense(y, wg, wu, wd, h, mods, gp, *, tm, tiles_per_batch, side_cast=(), zero_rows=0):
    T = h.shape[0]
    nt = T // tm
    if tiles_per_batch is None:
        mrow = lambda i: 4
    else:
        mrow = lambda i: i // tiles_per_batch
    resident = pl.Buffered(1)
    in_specs = [pl.BlockSpec((tm, D_MODEL), lambda i: (i, 0)),
                pl.BlockSpec((D_MODEL, D_FF), lambda i: (0, 0), pipeline_mode=resident),
                pl.BlockSpec((D_MODEL, D_FF), lambda i: (0, 0), pipeline_mode=resident),
                pl.BlockSpec((D_FF, D_MODEL), lambda i: (0, 0), pipeline_mode=resident),
                pl.BlockSpec((tm, D_MODEL), lambda i: (i, 0)),
                pl.BlockSpec((1, D_MODEL), lambda i: (0, 0)),
                pl.BlockSpec((None, None, 1, D_MODEL), lambda i: (mrow(i), 5, 0, 0))]
    out_specs = [pl.BlockSpec((tm, D_MODEL), lambda i: (i, 0))]
    out_shape = [jax.ShapeDtypeStruct((T, D_MODEL), F32)]
    for a in side_cast:
        spec = pl.BlockSpec((a.shape[0] // nt, a.shape[1]), lambda i: (i, 0))
        in_specs.append(spec)
        out_specs.append(spec)
        out_shape.append(jax.ShapeDtypeStruct(a.shape, BF16))
    if side_cast:
        out_specs.append(pl.BlockSpec((zero_rows // nt, D_MODEL), lambda i: (i, 0)))
        out_shape.append(jax.ShapeDtypeStruct((zero_rows, D_MODEL), BF16))
    res = pl.pallas_call(
        _ffn_dense_cast_kernel if side_cast else _ffn_dense_kernel,
        grid=(nt,),
        in_specs=in_specs, out_specs=out_specs, out_shape=out_shape,
        scratch_shapes=[pltpu.VMEM((tm, D_FF), BF16)],
        compiler_params=_cparams(("parallel",)),
    )(y, wg, wu, wd, h, gp, mods, *side_cast)
    return res[0], tuple(res[1:])


MOE_TB = 512
MOE_TM = 512
SUB = 64
NSUB = MOE_TB // SUB
RUN_ALIGN = 16


def _moe_tile_bound(T):
    nt = T // MOE_TB
    rows = 2 * T + nt * N_EXPERTS * (RUN_ALIGN - 1) + N_EXPERTS * (SUB + MOE_TM - 1)
    return -(-rows // (2 * MOE_TM)) * 2


def _moe_plan(cnt, T):
    p = (cnt + RUN_ALIGN - 1) // RUN_ALIGN * RUN_ALIGN
    base = jnp.cumsum(p, axis=0) - p
    used = jnp.sum(p, axis=0)
    tiles = (used + SUB + MOE_TM - 1) // MOE_TM
    tend = jnp.cumsum(tiles)
    off = (tend - tiles) * MOE_TM
    rowbase = (off[None, :] + base).astype(jnp.int32)
    nsub = (cnt + SUB - 1) // SUB
    slotbase = (jnp.cumsum(nsub, axis=1) - nsub).astype(jnp.int32)
    nt_bound = _moe_tile_bound(T)
    j = jnp.arange(nt_bound, dtype=jnp.int32)
    tile_e = jnp.minimum(jnp.sum(j[:, None] >= tend[None, :], axis=1), N_EXPERTS - 1).astype(jnp.int32)
    nvalid = tend[-1].astype(jnp.int32)
    tile_blk = jnp.minimum(j, nvalid - 1)
    return rowbase, slotbase, tile_e, tile_blk, nvalid.reshape(1)


NSLOT = 2 * MOE_TB // SUB + N_EXPERTS
SLOT_ROWS = NSLOT * SUB
SLOT_CHUNK = 512


def _block_slots(sb_ref, cn_ref, b):
    last = b * N_EXPERTS + N_EXPERTS - 1
    return sb_ref[last] + (cn_ref[last] + SUB - 1) // SUB


def _for_each_run_slot(cn_ref, b, fn):
    for e in range(N_EXPERTS):
        for s in range(NSUB):
            @pl.when(cn_ref[b * N_EXPERTS + e] > SUB * s)
            def _(e=e, s=s):
                fn(e, s)


def _run_copy_out(stage, ys_ref, sems, rb_ref, sb_ref, b, par, e, s):
    slot = sb_ref[b * N_EXPERTS + e] + s
    r0 = pl.multiple_of(rb_ref[b * N_EXPERTS + e] + SUB * s, RUN_ALIGN)
    return pltpu.make_async_copy(stage.at[par, pl.ds(pl.multiple_of(slot * SUB, SUB), SUB), :],
                                 ys_ref.at[pl.ds(r0, SUB), :], sems.at[par, e, s])


def _dispatch_kernel(rb_ref, sb_ref, cn_ref, y_ref, cmb_ref, sbv_ref, ltri_ref, ysin_ref,
                     tok_ref, ys_ref, stage, sems):
    del ysin_ref
    b = pl.program_id(0)
    nb = pl.num_programs(0)
    par = b % 2
    cmb = cmb_ref[...]
    sel = cmb > 0.0
    rk = _dot(ltri_ref[...], jnp.where(sel, 1.0, 0.0).astype(BF16))
    srow = jnp.where(sel, rk + sbv_ref[...], -1.0)
    nsel = jnp.sum(jnp.where(sel, 1.0, 0.0), axis=-1, keepdims=True)
    sa = jnp.max(srow, axis=-1, keepdims=True)
    sb = jnp.where(nsel > 1.5, jnp.sum(jnp.where(sel, srow, 0.0), axis=-1, keepdims=True) - sa, -1.0)
    ga = jnp.sum(jnp.where(srow == sa, cmb, 0.0), axis=-1, keepdims=True)
    gb = jnp.sum(cmb, axis=-1, keepdims=True) - ga
    lane = lax.broadcasted_iota(jnp.int32, cmb.shape, 1)
    tok = jnp.where(lane == 0, sa, jnp.where(lane == 1, sb, jnp.where(lane == 2, ga, jnp.where(lane == 3, gb, 0.0))))
    tok_ref[...] = tok
    tok_t = tok.T
    nslots = _block_slots(sb_ref, cn_ref, b)
    rr0 = lax.broadcasted_iota(jnp.int32, (SLOT_CHUNK, cmb.shape[0]), 0).astype(F32)
    for c in range(SLOT_ROWS // SLOT_CHUNK):
        @pl.when(nslots * SUB > c * SLOT_CHUNK)
        def _(c=c):
            rr = rr0 + float(c * SLOT_CHUNK)
            g = jnp.where((rr == tok_t[0:1, :]) | (rr == tok_t[1:2, :]), 1.0, 0.0).astype(BF16)
            stage[par, c * SLOT_CHUNK:(c + 1) * SLOT_CHUNK, :] = _dot(g, y_ref[...]).astype(BF16)

    @pl.when(b > 0)
    def _():
        _for_each_run_slot(cn_ref, b - 1, lambda e, s: _run_copy_out(
            stage, ys_ref, sems, rb_ref, sb_ref, b - 1, 1 - par, e, s).wait())

    _for_each_run_slot(cn_ref, b, lambda e, s: _run_copy_out(
        stage, ys_ref, sems, rb_ref, sb_ref, b, par, e, s).start())

    @pl.when(b == nb - 1)
    def _():
        _for_each_run_slot(cn_ref, b, lambda e, s: _run_copy_out(
            stage, ys_ref, sems, rb_ref, sb_ref, b, par, e, s).wait())


def _moe_dispatch(y, cmb, rowbase, slotbase, cnt, ys0):
    T = y.shape[0]
    nt = T // MOE_TB
    nt_bound = ys0.shape[0] // MOE_TM
    sbv = jnp.zeros((nt, 1, LANES), F32).at[:, 0, :N_EXPERTS].set((slotbase * SUB).astype(F32))
    ltri = jnp.asarray(np.tril(np.ones((MOE_TB, MOE_TB), np.float32), -1)).astype(BF16)
    imap2 = lambda b, rb, sb, cn: (b, 0)
    return pl.pallas_call(
        _dispatch_kernel,
        grid_spec=pltpu.PrefetchScalarGridSpec(
            num_scalar_prefetch=3,
            grid=(nt,),
            in_specs=[pl.BlockSpec((MOE_TB, D_MODEL), imap2),
                      pl.BlockSpec((MOE_TB, LANES), imap2),
                      pl.BlockSpec((None, 1, LANES), lambda b, rb, sb, cn: (b, 0, 0)),
                      pl.BlockSpec((MOE_TB, MOE_TB), lambda b, rb, sb, cn: (0, 0)),
                      pl.BlockSpec(memory_space=pl.ANY)],
            out_specs=[pl.BlockSpec((MOE_TB, LANES), imap2),
                       pl.BlockSpec(memory_space=pl.ANY)],
            scratch_shapes=[pltpu.VMEM((2, SLOT_ROWS, D_MODEL), BF16),
                            pltpu.SemaphoreType.DMA((2, N_EXPERTS, NSUB))]),
        out_shape=[jax.ShapeDtypeStruct((T, LANES), F32),
                   jax.ShapeDtypeStruct((nt_bound * MOE_TM, D_MODEL), BF16)],
        input_output_aliases={7: 1},
        compiler_params=_cparams(("arbitrary",)),
    )(rowbase.reshape(-1), slotbase.reshape(-1), cnt.reshape(-1), y, cmb, sbv, ltri, ys0)


def _ffn_group_kernel(te_ref, tb_ref, nv_ref, y_ref, wg_ref, wu_ref, wd_ref, o_ref, act_ref):
    del te_ref, tb_ref
    j = pl.program_id(0)

    @pl.when(j < nv_ref[0])
    def _():
        o_ref[...] = _swiglu_tile(y_ref[...], wg_ref, wu_ref, wd_ref, act_ref).astype(BF16)

    @pl.when(j >= nv_ref[0])
    def _():
        o_ref[...] = jnp.zeros_like(o_ref)


def _ffn_group(ys, wg, wu, wd, tile_e, tile_blk, nvalid):
    rows = ys.shape[0]
    resident = pl.Buffered(2)
    return pl.pallas_call(
        _ffn_group_kernel,
        grid_spec=pltpu.PrefetchScalarGridSpec(
            num_scalar_prefetch=3,
            grid=(rows // MOE_TM,),
            in_specs=[pl.BlockSpec((MOE_TM, D_MODEL), lambda j, te, tb, nv: (tb[j], 0)),
                      pl.BlockSpec((None, D_MODEL, D_FF), lambda j, te, tb, nv: (te[j], 0, 0), pipeline_mode=resident),
                      pl.BlockSpec((None, D_MODEL, D_FF), lambda j, te, tb, nv: (te[j], 0, 0), pipeline_mode=resident),
                      pl.BlockSpec((None, D_FF, D_MODEL), lambda j, te, tb, nv: (te[j], 0, 0), pipeline_mode=resident)],
            out_specs=pl.BlockSpec((MOE_TM, D_MODEL), lambda j, te, tb, nv: (j, 0)),
            scratch_shapes=[pltpu.VMEM((MOE_TM, D_FF), BF16)]),
        out_shape=jax.ShapeDtypeStruct((rows, D_MODEL), BF16),
        compiler_params=_cparams(("arbitrary",)),
    )(tile_e, tile_blk, nvalid, ys, wg, wu, wd)


def _run_copy_in(fs_ref, fbuf, sems, rb_ref, sb_ref, b, par, e, s):
    slot = sb_ref[b * N_EXPERTS + e] + s
    r0 = pl.multiple_of(rb_ref[b * N_EXPERTS + e] + SUB * s, RUN_ALIGN)
    return pltpu.make_async_copy(fs_ref.at[pl.ds(r0, SUB), :],
                                 fbuf.at[par, pl.ds(pl.multiple_of(slot * SUB, SUB), SUB), :], sems.at[par, e, s])


def _combine_kernel(rb_ref, sb_ref, cn_ref, tok_ref, h_ref, gp_ref, gt_ref, fs_ref, o_ref, fbuf, sems):
    b = pl.program_id(0)
    nb = pl.num_programs(0)
    par = b % 2

    @pl.when(b == 0)
    def _():
        fbuf[...] = jnp.zeros_like(fbuf)
        _for_each_run_slot(cn_ref, b, lambda e, s: _run_copy_in(
            fs_ref, fbuf, sems, rb_ref, sb_ref, b, par, e, s).start())

    @pl.when(b + 1 < nb)
    def _():
        _for_each_run_slot(cn_ref, b + 1, lambda e, s: _run_copy_in(
            fs_ref, fbuf, sems, rb_ref, sb_ref, b + 1, 1 - par, e, s).start())

    tok = tok_ref[...]
    sa, sb, ga, gb = tok[:, 0:1], tok[:, 1:2], tok[:, 2:3], tok[:, 3:4]
    _for_each_run_slot(cn_ref, b, lambda e, s: _run_copy_in(
        fs_ref, fbuf, sems, rb_ref, sb_ref, b, par, e, s).wait())
    nslots = _block_slots(sb_ref, cn_ref, b)
    nck = SLOT_ROWS // SLOT_CHUNK
    for c in range(1, nck + 1):
        lo_rows, hi_rows = (c - 1) * SLOT_CHUNK, c * SLOT_CHUNK
        cond = (nslots * SUB > lo_rows) if c == nck else ((nslots * SUB > lo_rows) & (nslots * SUB <= hi_rows))
        if c == 1:
            cond = nslots * SUB <= hi_rows

        @pl.when(cond)
        def _(depth=hi_rows):
            cc = lax.broadcasted_iota(jnp.int32, (tok.shape[0], depth), 1).astype(F32)
            pick = jnp.where(cc == sa, ga, jnp.where(cc == sb, gb, 0.0)).astype(BF16)
            f = _dot(pick, fbuf[par, :depth, :])
            _ffn_epilogue(f, h_ref, gp_ref, gt_ref, o_ref)


def _moe_combine(fs, tok, h, mods, gp, rowbase, slotbase, cnt, *, tiles_per_batch):
    T = h.shape[0]
    nt = T // MOE_TB
    imap2 = lambda b, rb, sb, cn: (b, 0)
    return pl.pallas_call(
        _combine_kernel,
        grid_spec=pltpu.PrefetchScalarGridSpec(
            num_scalar_prefetch=3,
            grid=(nt,),
            in_specs=[pl.BlockSpec((MOE_TB, LANES), imap2),
                      pl.BlockSpec((MOE_TB, D_MODEL), imap2),
                      pl.BlockSpec((1, D_MODEL), lambda b, rb, sb, cn: (0, 0)),
                      pl.BlockSpec((None, None, 1, D_MODEL), lambda b, rb, sb, cn: (b // tiles_per_batch, 5, 0, 0)),
                      pl.BlockSpec(memory_space=pl.ANY)],
            out_specs=pl.BlockSpec((MOE_TB, D_MODEL), imap2),
            scratch_shapes=[pltpu.VMEM((2, SLOT_ROWS, D_MODEL), BF16),
                            pltpu.SemaphoreType.DMA((2, N_EXPERTS, NSUB))]),
        out_shape=jax.ShapeDtypeStruct((T, D_MODEL), F32),
        compiler_params=_cparams(("arbitrary",)),
    )(rowbase.reshape(-1), slotbase.reshape(-1), cnt.reshape(-1), tok, h, gp, mods, fs)


def _ffn_moe(y, cmb, cnt_tiles, wg, wu, wd, h, mods, gp, ys0, *, tiles_per_batch):
    T = h.shape[0]
    cnt = cnt_tiles.reshape(T // MOE_TB, LANES)[:, :N_EXPERTS].astype(jnp.int32)
    rowbase, slotbase, tile_e, tile_blk, nvalid = _moe_plan(cnt, T)
    tok, ys = _moe_dispatch(y, cmb, rowbase, slotbase, cnt, ys0)
    fs = _ffn_group(ys, wg, wu, wd, tile_e, tile_blk, nvalid)
    return _moe_combine(fs, tok, h, mods, gp, rowbase, slotbase, cnt, tiles_per_batch=tiles_per_batch)


def _blockdiag(m, reps):
    n = m.shape[0]
    out = np.zeros((n * reps, n * reps), np.float64)
    for r in range(reps):
        out[r * n:(r + 1) * n, r * n:(r + 1) * n] = m
    return out


def _dft_tables(n_rows):
    n = n_rows * GRID_W
    r = np.arange(n_rows)
    c = np.arange(GRID_W)
    a1 = 2 * np.pi * np.outer(r, r) / n_rows
    c1, s1 = np.cos(a1), np.sin(a1)
    m1 = np.block([[c1, s1], [-s1, c1]]) * 0.125
    at = 2 * np.pi * np.outer(r, c) / n
    twc = np.repeat(np.cos(at), B_WIDTH, axis=1)
    tws = np.repeat(np.sin(at), B_WIDTH, axis=1)
    a3 = 2 * np.pi * np.outer(c, c) / GRID_W
    m3 = np.concatenate([np.cos(a3), np.sin(a3)], axis=1) * (8.0 / np.sqrt(n))
    f32 = lambda t: jnp.asarray(t.astype(np.float32))
    return f32(m1).astype(BF16), f32(twc), f32(tws), f32(m3).astype(BF16)


def _channel_dft_table():
    d = np.arange(HEAD_DIM)
    a = 2 * np.pi * np.outer(d, d) / HEAD_DIM
    w = np.concatenate([_blockdiag(np.cos(a), 4), -_blockdiag(np.sin(a), 4)], axis=1) * 0.125
    return jnp.asarray(w.astype(np.float32)).astype(BF16)


def _ctx_dft_table(m):
    p = np.arange(m)
    a = 2 * np.pi * np.outer(p, p) / m
    w = np.concatenate([np.cos(a), np.sin(a)], axis=1) * (8.0 / np.sqrt(m * HEAD_DIM))
    return jnp.asarray(w.astype(np.float32)).astype(BF16)


def _rope_tables(n_tok):
    rows = n_tok // GRID_W
    row = jnp.broadcast_to(jnp.arange(rows)[:, None], (rows, GRID_W)).reshape(-1)
    col = jnp.broadcast_to(jnp.arange(GRID_W)[None, :], (rows, GRID_W)).reshape(-1)
    half = HEAD_DIM // 2
    inv = ROPE_BASE ** (-jnp.arange(0, half, 2, dtype=F32) / half)
    ang = jnp.stack([row.astype(F32)[:, None] * inv, col.astype(F32)[:, None] * inv], axis=1)
    cos, sin = jnp.cos(ang), jnp.sin(ang)
    zer = jnp.zeros_like(sin)
    lay = lambda a, b: jnp.tile(jnp.stack([a, b], axis=2).reshape(n_tok, HEAD_DIM), (1, LANES // HEAD_DIM))
    return lay(cos, cos), lay(-sin, zer), lay(zer, sin)


def kernel(x, c, ctx, c_ctx, w_ada, b_ada, g_mix_pre, g_mix_post, g_ffn_pre, g_ffn_post,
           w_in, w_s, b_s, g_v, w_f, sink, w_out, w_gate_d, w_up_d, w_down_d,
           w_router, b_router, w_gate_e, w_up_e, w_down_e):
    bsz, n_lat, _ = x.shape
    n_ctx = ctx.shape[1]
    T, Tc = bsz * n_lat, bsz * n_ctx
    tm = 512
    tpb = n_lat // tm

    rope_tabs = _rope_tables(n_lat)
    m1, twc, tws, m3 = _dft_tables(n_lat // GRID_W)
    wdft = _channel_dft_table()
    mctx = _ctx_dft_table(n_ctx)
    hm = jnp.asarray(_blockdiag(np.full((HEAD_DIM, HEAD_DIM), 1.0 / HEAD_DIM), A_HEADS).astype(np.float32)).astype(BF16)

    cond8 = jnp.zeros((8, D_MODEL), F32).at[:bsz].set(c).at[4].set(c_ctx)
    h = x.reshape(T, D_MODEL)
    hc = ctx.reshape(Tc, D_MODEL)

    mods_all = _adaln(cond8, w_ada, b_ada[:, None, :]).reshape(DEPTH, 8, 6, 1, D_MODEL)

    for i in range(DEPTH):
        last = i == DEPTH - 1
        mods = mods_all[i]
        win = w_in[i].astype(BF16)
        ws = jnp.transpose(w_s[i], (1, 0, 2)).reshape(CHUNK, A_HEADS * CHUNK).astype(BF16)
        bsx = jnp.repeat(b_s[i].T, HEAD_DIM, axis=1)
        gv = g_v[i].reshape(1, A_WIDTH)
        wf = jax.scipy.linalg.block_diag(*[w_f[i][g] for g in range(4)]).astype(BF16)
        wo = w_out[i].astype(BF16)
        gpre = g_mix_pre[i][None, :]
        gpost = g_mix_post[i][None, :]
        gfpre = g_ffn_pre[i][None, :]
        gfpost = g_ffn_post[i][None, :]
        sk = sink[i]

        sgu, fre, fim, q, k, v = _premix(h, mods, gpre, win, ws, bsx, gv, hm, wdft, rope_tabs,
                                         tm=2 * tm, tiles_per_batch=tpb // 2, n_pos=n_lat)
        sguc, frec, fimc, qc, kc, vc = _premix(hc, mods, gpre, win, ws, bsx, gv, hm, wdft, None,
                                               tm=tm, tiles_per_batch=None, n_pos=n_ctx)
        yf = _seq_dft(fre, fim, bsz, n_lat // GRID_W, m1, twc, tws, m3)
        at = _window_attention(q, k, v, kc, vc, sk, bsz, n_lat, n_ctx)

        if i % 2 == 0:
            router = None
        else:
            j = i // 2
            wr = jnp.zeros((D_MODEL, LANES), F32).at[:, :N_EXPERTS].set(w_router[j])
            br = jnp.zeros((1, LANES), F32).at[0, :N_EXPERTS].set(b_router[j])
            router = (wr, br)
        dense_w = (w_gate_d[i // 2], w_up_d[i // 2], w_down_d[i // 2]) if i % 2 == 0 else ()
        res = _postmix(sgu, yf, at, wf, wo, h, mods, gpost, gfpre, router, tm=2 * tm, tiles_per_batch=tpb // 2,
                       side_cast=dense_w)
        if not last:
            yfc = _ctx_dft(frec, fimc, bsz, n_ctx, mctx)
            atc = _context_attention(qc, kc, vc, sk, bsz, n_ctx)
            resc = _postmix(sguc, yfc, atc, wf, wo, hc, mods, gpost, gfpre, router, tm=2 * tm, tiles_per_batch=None)

        j = i // 2
        if i % 2 == 0:
            wg, wu, wd = res[-3:]
            side = ()
            if not last:
                jn = (i + 1) // 2
                side = (w_gate_e[jn].reshape(N_EXPERTS * D_MODEL, D_FF), w_up_e[jn].reshape(N_EXPERTS * D_MODEL, D_FF),
                        w_down_e[jn].reshape(N_EXPERTS * D_FF, D_MODEL))
            h, expert_w = _ffn_dense(res[1], wg, wu, wd, res[0], mods, gfpost, tm=tm, tiles_per_batch=tpb,
                                     side_cast=side, zero_rows=_moe_tile_bound(T) * MOE_TM)
            if not last:
                hc, _ = _ffn_dense(resc[1], wg, wu, wd, resc[0], mods, gfpost, tm=tm, tiles_per_batch=None)
        else:
            wg = expert_w[0].reshape(N_EXPERTS, D_MODEL, D_FF)
            wu = expert_w[1].reshape(N_EXPERTS, D_MODEL, D_FF)
            wd = expert_w[2].reshape(N_EXPERTS, D_FF, D_MODEL)
            assert last and tm == MOE_TB, "the expert FFN is only built for the final layer's latent tokens"
            h = _ffn_moe(res[1], res[2], res[3], wg, wu, wd, res[0], mods, gfpost, expert_w[3],
                         tiles_per_batch=n_lat // MOE_TB)
    return h.reshape(bsz, n_lat, D_MODEL)
```

```python
import functools

import numpy as np
import jax
import jax.numpy as jnp
from jax import lax
from jax.experimental import pallas as pl
from jax.experimental.pallas import tpu as pltpu

D_MODEL = 1024
DEPTH = 2
GRID_W = 64
HEAD_DIM = 64
EPS = 1e-6
A_HEADS = 4
A_WIDTH = 256
CHUNK = 128
B_WIDTH = 256
C_Q_HEADS = 8
C_WIDTH = 512
KV_WIDTH = 128
WINDOW = 128
BLOCK = 128
ROPE_BASE = 10000.0
OFF_B = 512
OFF_Q = 768
OFF_K = 1280
OFF_V = 1408
N_IN = 1536
D_FF = 3584
N_EXPERTS = 8

LANES = 128
VMEM_LIMIT = 60 * 1024 * 1024
NEG = -1e30
LOG2E = 1.4426950408889634
assert WINDOW == BLOCK

F32 = jnp.float32
BF16 = jnp.bfloat16


def _dot(a, b):
    return jnp.dot(a, b, preferred_element_type=F32)


def _cparams(sem):
    return pltpu.CompilerParams(dimension_semantics=sem, vmem_limit_bytes=VMEM_LIMIT)


def _adaln_kernel(c_ref, w_ref, b_ref, o_ref):
    c = c_ref[...]
    s = (c * jax.nn.sigmoid(c)).astype(BF16)
    o_ref[...] = _dot(s, w_ref[...].astype(BF16)) + b_ref[...]


def _adaln(cond8, w, b):
    tn = 1536
    nl = w.shape[0]
    return pl.pallas_call(
        _adaln_kernel,
        grid=(nl, 6 * D_MODEL // tn),
        in_specs=[pl.BlockSpec((8, D_MODEL), lambda l, j: (0, 0)),
                  pl.BlockSpec((None, D_MODEL, tn), lambda l, j: (l, 0, j)),
                  pl.BlockSpec((None, 1, tn), lambda l, j: (l, 0, j))],
        out_specs=pl.BlockSpec((None, 8, tn), lambda l, j: (l, 0, j)),
        out_shape=jax.ShapeDtypeStruct((nl, 8, 6 * D_MODEL), F32),
        compiler_params=_cparams(("arbitrary", "arbitrary")),
    )(cond8, w, b)


def _gelu_tanh(x):
    return 0.5 * x * (1.0 + jnp.tanh(0.7978845608028654 * (x + 0.044715 * x * x * x)))


PRE_SUB = 512


def _premix_kernel(*refs, rope):
    if rope:
        (h_ref, sh_ref, sc_ref, g_ref, win_ref, ws_ref, bs_ref, gv_ref, hm_ref, wdft_ref,
         cos_ref, s1_ref, s2_ref, sgu_ref, re_ref, im_ref, q_ref, k_ref, v_ref, fsc_ref) = refs
    else:
        (h_ref, sh_ref, sc_ref, g_ref, win_ref, ws_ref, bs_ref, gv_ref, hm_ref, wdft_ref,
         sgu_ref, re_ref, im_ref, q_ref, k_ref, v_ref) = refs
    tm = h_ref.shape[0]
    nl = 2 * B_WIDTH // LANES
    head = lax.broadcasted_iota(jnp.int32, (CHUNK, A_WIDTH), 1) // HEAD_DIM
    lo = lax.broadcasted_iota(jnp.int32, (PRE_SUB, LANES), 1) < HEAD_DIM
    for sb in range(tm // PRE_SUB):
        rs = slice(sb * PRE_SUB, (sb + 1) * PRE_SUB)
        x = h_ref[rs, :]
        ms = jnp.mean(x * x, axis=-1, keepdims=True)
        xn = x * lax.rsqrt(ms + EPS) * g_ref[...]
        xm = (xn * (1.0 + sc_ref[...]) + sh_ref[...]).astype(BF16)
        z = _dot(xm, win_ref[...])
        zcols = lambda c0, c1: z[:, c0:c1]

        a = _gelu_tanh(zcols(0, OFF_B))
        u = a[:, :A_WIDTH]
        v = a[:, A_WIDTH:]
        msv = _dot((v * v).astype(BF16), hm_ref[...])
        vn = (v * lax.rsqrt(msv + EPS) * gv_ref[...]).astype(BF16)
        for ck in range(PRE_SUB // CHUNK):
            rows = slice(ck * CHUNK, (ck + 1) * CHUNK)
            vc = vn[rows]
            vstack = jnp.concatenate([jnp.where(head == hh, vc, jnp.zeros_like(vc)) for hh in range(A_HEADS)], axis=0)
            sv = bs_ref[...] + _dot(ws_ref[...], vstack)
            sgu_ref[sb * PRE_SUB + ck * CHUNK:sb * PRE_SUB + (ck + 1) * CHUNK, :] = (u[rows] * sv).astype(BF16)

        f = _dot(zcols(OFF_B, OFF_Q).astype(BF16), wdft_ref[...])
        if rope:
            for j in range(nl):
                fsc_ref[j, rs, :] = f[:, j * LANES:(j + 1) * LANES]
        else:
            re_ref[rs, :] = f[:, :B_WIDTH].astype(BF16)
            im_ref[rs, :] = f[:, B_WIDTH:].astype(BF16)

        def rot(t):
            if not rope:
                return t
            return (t * cos_ref[rs, :] + pltpu.roll(t, LANES - 16, 1) * s1_ref[rs, :]
                    + pltpu.roll(t, 16, 1) * s2_ref[rs, :])

        zq = zcols(OFF_Q, OFF_K)
        for j in range(C_WIDTH // LANES):
            q_ref[rs, j * LANES:(j + 1) * LANES] = (
                rot(zq[:, j * LANES:(j + 1) * LANES]) * (HEAD_DIM ** -0.5 * LOG2E)).astype(BF16)

        zkv = zcols(OFF_K, N_IN)
        kk = rot(zkv[:, :KV_WIDTH])
        kr = pltpu.roll(kk, HEAD_DIM, 1)
        k_ref[rs, :LANES] = jnp.where(lo, kk, kr).astype(BF16)
        k_ref[rs, LANES:] = jnp.where(lo, kr, kk).astype(BF16)
        vv = zkv[:, KV_WIDTH:]
        vr = pltpu.roll(vv, HEAD_DIM, 1)
        v_ref[rs, :LANES] = jnp.where(lo, vv, vr).astype(BF16)
        v_ref[rs, LANES:] = jnp.where(lo, vr, vv).astype(BF16)

    if rope:
        nr = tm // GRID_W
        for cc in range(GRID_W):
            for j in range(nl):
                blk = fsc_ref[j, pl.ds(cc, nr, stride=GRID_W), :].astype(BF16)
                dst = re_ref if j < nl // 2 else im_ref
                c0 = cc * B_WIDTH + (j % (nl // 2)) * LANES
                dst[:, c0:c0 + LANES] = blk


def _premix(h, mods, g, win, ws, bsx, gv, hm, wdft, rope_tabs, *, tm, tiles_per_batch, n_pos):
    T = h.shape[0]
    rope = rope_tabs is not None
    if tiles_per_batch is None:
        mrow = lambda i: 4
    else:
        mrow = lambda i: i // tiles_per_batch
    const2 = lambda i: (0, 0)
    in_specs = [
        pl.BlockSpec((tm, D_MODEL), lambda i: (i, 0)),
        pl.BlockSpec((None, None, 1, D_MODEL), lambda i: (mrow(i), 0, 0, 0)),
        pl.BlockSpec((None, None, 1, D_MODEL), lambda i: (mrow(i), 1, 0, 0)),
        pl.BlockSpec((1, D_MODEL), const2),
        pl.BlockSpec((D_MODEL, N_IN), const2),
        pl.BlockSpec((CHUNK, A_HEADS * CHUNK), const2),
        pl.BlockSpec((CHUNK, A_WIDTH), const2),
        pl.BlockSpec((1, A_WIDTH), const2),
        pl.BlockSpec((A_WIDTH, A_WIDTH), const2),
        pl.BlockSpec((B_WIDTH, 2 * B_WIDTH), const2),
    ]
    args = [h, mods, mods, g, win, ws, bsx, gv, hm, wdft]
    if rope:
        nt = n_pos // tm
        for t in rope_tabs:
            in_specs.append(pl.BlockSpec((tm, LANES), lambda i: (i % nt, 0)))
            args.append(t)
    widths = (A_WIDTH, B_WIDTH, B_WIDTH, C_WIDTH, 2 * KV_WIDTH, 2 * KV_WIDTH)
    out_specs = [pl.BlockSpec((tm, w), lambda i: (i, 0)) for w in widths]
    out_shape = [jax.ShapeDtypeStruct((T, w), BF16) for w in widths]
    scratch = []
    if rope:
        for o in (1, 2):
            out_specs[o] = pl.BlockSpec((tm // GRID_W, GRID_W * B_WIDTH), lambda i: (i, 0))
            out_shape[o] = jax.ShapeDtypeStruct((T // GRID_W, GRID_W * B_WIDTH), BF16)
        scratch = [pltpu.VMEM((2 * B_WIDTH // LANES, tm, LANES), F32)]
    return pl.pallas_call(
        functools.partial(_premix_kernel, rope=rope),
        grid=(T // tm,),
        in_specs=in_specs, out_specs=out_specs, out_shape=out_shape, scratch_shapes=scratch,
        compiler_params=_cparams(("parallel",)),
    )(*args)


def _dft1_kernel(m1_ref, twc_ref, tws_ref, re_ref, im_ref, o_ref):
    x = jnp.concatenate([re_ref[0], im_ref[0]], axis=0)
    a = _dot(m1_ref[...], x)
    nr = a.shape[0] // 2
    are, aim = a[:nr], a[nr:]
    c, s = twc_ref[...], tws_ref[...]
    bre = (are * c + aim * s).astype(BF16)
    bim = (aim * c - are * s).astype(BF16)
    for cl in range(o_ref.shape[2]):
        o_ref[0, 0, cl] = bre[:, cl * B_WIDTH:(cl + 1) * B_WIDTH]
        o_ref[0, 1, cl] = bim[:, cl * B_WIDTH:(cl + 1) * B_WIDTH]


def _dft3_kernel(m3_ref, x_ref, o_ref, xs_ref, os_ref):
    kb = x_ref.shape[3]
    nl = B_WIDTH // LANES
    for p in range(2):
        for c in range(GRID_W):
            xc = x_ref[0, p, c].astype(F32)
            for j in range(nl):
                xs_ref[p * nl + j, c * kb:(c + 1) * kb, :] = xc[:, j * LANES:(j + 1) * LANES]
    for k in range(kb):
        rows = [jnp.concatenate([xs_ref[p * nl + j, pl.ds(k, GRID_W, stride=kb), :] for j in range(nl)], axis=1)
                for p in range(2)]
        res = _dot(m3_ref[...], jnp.concatenate(rows, axis=0).astype(BF16))
        for j in range(nl):
            os_ref[j, pl.ds(k, GRID_W, stride=kb), :] = res[:, j * LANES:(j + 1) * LANES]
    for k2 in range(GRID_W):
        o_ref[0, k2] = jnp.concatenate(
            [os_ref[j, k2 * kb:(k2 + 1) * kb, :] for j in range(nl)], axis=1).astype(BF16)


def _seq_dft(re, im, bsz, n_rows, m1, twc, tws, m3):
    ncol = GRID_W * B_WIDTH
    tn = 4096
    re3 = re.reshape(bsz, n_rows, ncol)
    im3 = im.reshape(bsz, n_rows, ncol)
    st1 = pl.pallas_call(
        _dft1_kernel,
        grid=(ncol // tn, bsz),
        in_specs=[pl.BlockSpec((2 * n_rows, 2 * n_rows), lambda j, b: (0, 0)),
                  pl.BlockSpec((n_rows, tn), lambda j, b: (0, j)),
                  pl.BlockSpec((n_rows, tn), lambda j, b: (0, j)),
                  pl.BlockSpec((1, n_rows, tn), lambda j, b: (b, 0, j)),
                  pl.BlockSpec((1, n_rows, tn), lambda j, b: (b, 0, j))],
        out_specs=pl.BlockSpec((1, 2, tn // B_WIDTH, n_rows, B_WIDTH), lambda j, b: (b, 0, j, 0, 0)),
        out_shape=jax.ShapeDtypeStruct((bsz, 2, GRID_W, n_rows, B_WIDTH), BF16),
        compiler_params=_cparams(("parallel", "parallel")),
    )(m1, twc, tws, re3, im3)
    kb = min(n_rows, 16)
    y = pl.pallas_call(
        _dft3_kernel,
        grid=(bsz, n_rows // kb),
        in_specs=[pl.BlockSpec((GRID_W, 2 * GRID_W), lambda b, j: (0, 0)),
                  pl.BlockSpec((1, 2, GRID_W, kb, B_WIDTH), lambda b, j: (b, 0, 0, j, 0))],
        out_specs=pl.BlockSpec((1, GRID_W, kb, B_WIDTH), lambda b, j: (b, 0, j, 0)),
        out_shape=jax.ShapeDtypeStruct((bsz, GRID_W, n_rows, B_WIDTH), BF16),
        scratch_shapes=[pltpu.VMEM((2 * B_WIDTH // LANES, GRID_W * kb, LANES), F32),
                        pltpu.VMEM((B_WIDTH // LANES, GRID_W * kb, LANES), F32)],
        compiler_params=_cparams(("parallel", "parallel")),
    )(m3, st1)
    return y.reshape(bsz * GRID_W * n_rows, B_WIDTH)


def _ctx_dft_kernel(m_ref, re_ref, im_ref, o_ref):
    x = jnp.concatenate([re_ref[0], im_ref[0]], axis=0)
    o_ref[0] = _dot(m_ref[...], x).astype(BF16)


def _ctx_dft(re, im, bsz, m, mat):
    re3 = re.reshape(bsz, m, B_WIDTH)
    im3 = im.reshape(bsz, m, B_WIDTH)
    y = pl.pallas_call(
        _ctx_dft_kernel,
        grid=(bsz,),
        in_specs=[pl.BlockSpec((m, 2 * m), lambda b: (0, 0)),
                  pl.BlockSpec((1, m, B_WIDTH), lambda b: (b, 0, 0)),
                  pl.BlockSpec((1, m, B_WIDTH), lambda b: (b, 0, 0))],
        out_specs=pl.BlockSpec((1, m, B_WIDTH), lambda b: (b, 0, 0)),
        out_shape=jax.ShapeDtypeStruct((bsz, m, B_WIDTH), BF16),
        compiler_params=_cparams(("parallel",)),
    )(mat, re3, im3)
    return y.reshape(bsz * m, B_WIDTH)


ATTN_SUBBLOCKS = 4


def _attn_kernel(sink_ref, *refs, local, nb):
    if local:
        q_ref, kp_ref, kc_ref, kn_ref, vp_ref, vc_ref, vn_ref, kx_ref, vx_ref, o_ref = refs
    else:
        q_ref, kx_ref, vx_ref, o_ref = refs
    tq = BLOCK if local else q_ref.shape[1]
    nsb = q_ref.shape[1] // tq
    lo = lax.broadcasted_iota(jnp.int32, (tq, LANES), 1) < HEAD_DIM
    zero = jnp.zeros((tq, LANES), BF16)
    lo4 = lax.broadcasted_iota(jnp.int32, (4 * tq, LANES), 1) < HEAD_DIM
    if local:
        j = pl.program_id(1)
        row = lax.broadcasted_iota(jnp.int32, (BLOCK, BLOCK), 0)
        col = lax.broadcasted_iota(jnp.int32, (BLOCK, BLOCK), 1)

        def key_blocks(p_ref, c_ref, n_ref, gl):
            return ([p_ref[0][:, gl]] + [c_ref[0][sb * BLOCK:(sb + 1) * BLOCK, gl] for sb in range(nsb)]
                    + [n_ref[0][:, gl]])
    jobs = []
    for g in range(2):
        gl = slice(g * LANES, (g + 1) * LANES)
        if local:
            kb = key_blocks(kp_ref, kc_ref, kn_ref, gl)
            vb = key_blocks(vp_ref, vc_ref, vn_ref, gl)
        for sb in range(nsb):
            if local:
                kg = jnp.concatenate(kb[sb:sb + 3] + [kx_ref[0][:, gl]], axis=0)
                vg = jnp.concatenate(vb[sb:sb + 3] + [vx_ref[0][:, gl]], axis=0)
            else:
                kg = kx_ref[0][:, gl]
                vg = vx_ref[0][:, gl]
            q = q_ref[0, sb * tq:(sb + 1) * tq, :]
            qs = []
            for p in range(2):
                qp = q[:, g * 2 * LANES + p * LANES: g * 2 * LANES + (p + 1) * LANES]
                qs.append(jnp.where(lo, qp, zero))
                qs.append(jnp.where(lo, zero, qp))
            q4 = jnp.concatenate(qs, axis=0)
            s4 = lax.dot_general(q4, kg, (((1,), (1,)), ((), ())), preferred_element_type=F32)
            jobs.append((g, sb, s4, vg))
    for g, sb, s4, vg in jobs:
        if local:
            valid_prev = (col >= row) & (j > 0) if sb == 0 else (col >= row)
            valid_next = (col <= row) & (j < nb // nsb - 1) if sb == nsb - 1 else (col <= row)
        ps, sinks = [], []
        for hl in range(4):
            s = s4[hl * tq:(hl + 1) * tq]
            if local:
                s = jnp.concatenate(
                    [jnp.where(valid_prev, s[:, :BLOCK], NEG), s[:, BLOCK:2 * BLOCK],
                     jnp.where(valid_next, s[:, 2 * BLOCK:3 * BLOCK], NEG), s[:, 3 * BLOCK:]], axis=1)
            sk = sink_ref[g * 4 + hl] * LOG2E
            m = jnp.maximum(jnp.max(s, axis=-1, keepdims=True), sk)
            ps.append(jnp.exp2(s - m).astype(BF16))
            sinks.append(jnp.exp2(sk - m))
        lov = lax.broadcasted_iota(jnp.int32, vg.shape, 1) < HEAD_DIM
        ve = jnp.where(lov, vg, jnp.ones_like(vg))
        o4 = _dot(jnp.concatenate(ps, axis=0), ve)
        o4 = o4 + jnp.where(lo4, 0.0, jnp.concatenate(sinks, axis=0))
        r4 = pltpu.roll(o4, HEAD_DIM, 1)
        for p in range(2):
            ev = slice((2 * p) * tq, (2 * p + 1) * tq)
            od = slice((2 * p + 1) * tq, (2 * p + 2) * tq)
            o_pair = jnp.where(lo, o4[ev] / r4[ev], r4[od] / o4[od])
            c0 = g * 2 * LANES + p * LANES
            o_ref[0, sb * tq:(sb + 1) * tq, c0:c0 + LANES] = o_pair.astype(BF16)


def _window_attention(q, k, v, kx, vx, sink, bsz, n, m):
    nb = n // BLOCK
    q3 = q.reshape(bsz, n, C_WIDTH)
    k3 = k.reshape(bsz, n, 2 * KV_WIDTH)
    v3 = v.reshape(bsz, n, 2 * KV_WIDTH)
    kx3 = kx.reshape(bsz, m, 2 * KV_WIDTH)
    vx3 = vx.reshape(bsz, m, 2 * KV_WIDTH)
    nsb = ATTN_SUBBLOCKS
    edge = lambda f: pl.BlockSpec((1, BLOCK, 2 * KV_WIDTH), f)
    own = pl.BlockSpec((1, nsb * BLOCK, 2 * KV_WIDTH), lambda b, i, s: (b, i, 0))
    prev = lambda b, i, s: (b, jnp.maximum(i * nsb - 1, 0), 0)
    cur = lambda b, i, s: (b, i, 0)
    nxt = lambda b, i, s: (b, jnp.minimum((i + 1) * nsb, nb - 1), 0)
    ctxs = pl.BlockSpec((1, m, 2 * KV_WIDTH), lambda b, i, s: (b, 0, 0))
    o = pl.pallas_call(
        functools.partial(_attn_kernel, local=True, nb=nb),
        grid_spec=pltpu.PrefetchScalarGridSpec(
            num_scalar_prefetch=1,
            grid=(bsz, nb // nsb),
            in_specs=[pl.BlockSpec((1, nsb * BLOCK, C_WIDTH), cur),
                      edge(prev), own, edge(nxt), edge(prev), own, edge(nxt), ctxs, ctxs],
            out_specs=pl.BlockSpec((1, nsb * BLOCK, C_WIDTH), cur)),
        out_shape=jax.ShapeDtypeStruct((bsz, n, C_WIDTH), BF16),
        compiler_params=_cparams(("parallel", "parallel")),
    )(sink, q3, k3, k3, k3, v3, v3, v3, kx3, vx3)
    return o.reshape(bsz * n, C_WIDTH)


def _context_attention(q, kx, vx, sink, bsz, m):
    q3 = q.reshape(bsz, m, C_WIDTH)
    kx3 = kx.reshape(bsz, m, 2 * KV_WIDTH)
    vx3 = vx.reshape(bsz, m, 2 * KV_WIDTH)
    ctxs = pl.BlockSpec((1, m, 2 * KV_WIDTH), lambda b, s: (b, 0, 0))
    o = pl.pallas_call(
        functools.partial(_attn_kernel, local=False, nb=1),
        grid_spec=pltpu.PrefetchScalarGridSpec(
            num_scalar_prefetch=1,
            grid=(bsz,),
            in_specs=[pl.BlockSpec((1, m, C_WIDTH), lambda b, s: (b, 0, 0)), ctxs, ctxs],
            out_specs=pl.BlockSpec((1, m, C_WIDTH), lambda b, s: (b, 0, 0))),
        out_shape=jax.ShapeDtypeStruct((bsz, m, C_WIDTH), BF16),
        compiler_params=_cparams(("parallel",)),
    )(sink, q3, kx3, vx3)
    return o.reshape(bsz * m, C_WIDTH)


def _postmix_kernel(*refs, route, ncast):
    if ncast:
        n_in = 11 + (2 if route else 0)
        for src, dst in zip(refs[n_in:n_in + ncast], refs[len(refs) - ncast:]):
            dst[...] = src[...].astype(BF16)
        refs = refs[:n_in] + refs[n_in + ncast:len(refs) - ncast]
    h_ref = refs[5]
    for sb in range(h_ref.shape[0] // MOE_TB):
        _postmix_rows(slice(sb * MOE_TB, (sb + 1) * MOE_TB), sb, refs, route)


def _postmix_rows(rows, sb, refs, route):
    (sgu_ref, yf_ref, at_ref, wf_ref, wo_ref, h_ref, gp_ref, gt_ref, gf_ref, sh_ref, sc_ref) = refs[:11]
    if route:
        wr_ref, br_ref, hn_ref, y_ref, cmb_ref, cnt_ref = refs[11:]
    else:
        hn_ref, y_ref = refs[11:]
    fm = _dot(yf_ref[rows, :], wf_ref[...]).astype(BF16)
    o = (_dot(sgu_ref[rows, :], wo_ref[:A_WIDTH, :])
         + _dot(fm, wo_ref[A_WIDTH:A_WIDTH + B_WIDTH, :])
         + _dot(at_ref[rows, :], wo_ref[A_WIDTH + B_WIDTH:, :]))
    ms = jnp.mean(o * o, axis=-1, keepdims=True)
    hn = h_ref[rows, :] + gt_ref[...] * (o * lax.rsqrt(ms + EPS) * gp_ref[...])
    hn_ref[rows, :] = hn
    ms2 = jnp.mean(hn * hn, axis=-1, keepdims=True)
    y = hn * lax.rsqrt(ms2 + EPS) * gf_ref[...] * (1.0 + sc_ref[...]) + sh_ref[...]
    y_ref[rows, :] = y.astype(BF16)
    if route:
        yh = y.astype(BF16)
        yl = (y - yh.astype(F32)).astype(BF16)
        w = wr_ref[...]
        wh = w.astype(BF16)
        wl = (w - wh.astype(F32)).astype(BF16)
        n = yh.shape[0]
        r = _dot(jnp.concatenate([yh, yl], axis=0), jnp.concatenate([wh, wl], axis=1))
        lg = (r[:n, :LANES] + r[n:, :LANES]) + (r[:n, LANES:] + r[n:, LANES:]) + br_ref[...]
        lane = lax.broadcasted_iota(jnp.int32, lg.shape, 1)
        lg = jnp.where(lane < N_EXPERTS, lg, NEG)
        m1 = jnp.max(lg, axis=-1, keepdims=True)
        i1 = jnp.min(jnp.where(lg == m1, lane, LANES), axis=-1, keepdims=True)
        lg2 = jnp.where(lane == i1, NEG, lg)
        m2 = jnp.max(lg2, axis=-1, keepdims=True)
        i2 = jnp.min(jnp.where(lg2 == m2, lane, LANES), axis=-1, keepdims=True)
        e2 = jnp.exp(m2 - m1)
        g1 = 1.0 / (1.0 + e2)
        g2 = e2 * g1
        cmb = jnp.where(lane == i1, g1, 0.0) + jnp.where(lane == i2, g2, 0.0)
        cmb_ref[rows, :] = cmb
        cnt_ref[sb] = jnp.sum(jnp.where(cmb > 0.0, 1.0, 0.0), axis=0, keepdims=True)


def _postmix(sgu, yf, at, wf, wo, h, mods, gp, gf, router, *, tm, tiles_per_batch, side_cast=()):
    T = h.shape[0]
    if tiles_per_batch is None:
        mrow = lambda i: 4
    else:
        mrow = lambda i: i // tiles_per_batch
    const2 = lambda i: (0, 0)
    row = lambda w: pl.BlockSpec((tm, w), lambda i: (i, 0))
    mod = lambda j: pl.BlockSpec((None, None, 1, D_MODEL), lambda i: (mrow(i), j, 0, 0))
    in_specs = [row(A_WIDTH), row(B_WIDTH), row(C_WIDTH),
                pl.BlockSpec((B_WIDTH, B_WIDTH), const2),
                pl.BlockSpec((D_MODEL, D_MODEL), const2),
                row(D_MODEL),
                pl.BlockSpec((1, D_MODEL), const2),
                mod(2),
                pl.BlockSpec((1, D_MODEL), const2),
                mod(3), mod(4)]
    args = [sgu, yf, at, wf, wo, h, gp, mods, gf, mods, mods]
    out_specs = [row(D_MODEL), row(D_MODEL)]
    out_shape = [jax.ShapeDtypeStruct((T, D_MODEL), F32), jax.ShapeDtypeStruct((T, D_MODEL), BF16)]
    route = router is not None
    if route:
        in_specs += [pl.BlockSpec((D_MODEL, LANES), const2), pl.BlockSpec((1, LANES), const2)]
        args += list(router)
        out_specs += [row(LANES), pl.BlockSpec((tm // MOE_TB, 1, LANES), lambda i: (i, 0, 0))]
        out_shape += [jax.ShapeDtypeStruct((T, LANES), F32), jax.ShapeDtypeStruct((T // MOE_TB, 1, LANES), F32)]
    for a in side_cast:
        spec = pl.BlockSpec((a.shape[0] // (T // tm), a.shape[1]), lambda i: (i, 0))
        in_specs.append(spec)
        args.append(a)
        out_specs.append(spec)
        out_shape.append(jax.ShapeDtypeStruct(a.shape, BF16))
    return pl.pallas_call(
        functools.partial(_postmix_kernel, route=route, ncast=len(side_cast)),
        grid=(T // tm,),
        in_specs=in_specs, out_specs=out_specs, out_shape=out_shape,
        compiler_params=_cparams(("parallel",)),
    )(*args)


def _ffn_epilogue(f, h_ref, gp_ref, gt_ref, o_ref):
    ms = jnp.mean(f * f, axis=-1, keepdims=True)
    o_ref[...] = h_ref[...] + gt_ref[...] * (f * lax.rsqrt(ms + EPS) * gp_ref[...])


FF_CHUNK = 256


def _swiglu_tile(y, wg_ref, wu_ref, wd_ref, act_ref):
    for c in range(D_FF // FF_CHUNK):
        cols = slice(c * FF_CHUNK, (c + 1) * FF_CHUNK)
        gate = _dot(y, wg_ref[:, cols])
        up = _dot(y, wu_ref[:, cols])
        act_ref[:, cols] = (gate * jax.nn.sigmoid(gate) * up).astype(BF16)
    return _dot(act_ref[...], wd_ref[...])


def _ffn_dense_kernel(y_ref, wg_ref, wu_ref, wd_ref, h_ref, gp_ref, gt_ref, o_ref, act_ref):
    f = _swiglu_tile(y_ref[...], wg_ref, wu_ref, wd_ref, act_ref)
    _ffn_epilogue(f, h_ref, gp_ref, gt_ref, o_ref)


def _ffn_dense_cast_kernel(y_ref, wg_ref, wu_ref, wd_ref, h_ref, gp_ref, gt_ref, c0_ref, c1_ref, c2_ref,
                           o_ref, d0_ref, d1_ref, d2_ref, z_ref, act_ref):
    d0_ref[...] = c0_ref[...].astype(BF16)
    d1_ref[...] = c1_ref[...].astype(BF16)
    d2_ref[...] = c2_ref[...].astype(BF16)
    z_ref[...] = jnp.zeros_like(z_ref)
    f = _swiglu_tile(y_ref[...], wg_ref, wu_ref, wd_ref, act_ref)
    _ffn_epilogue(f, h_ref, gp_ref, gt_ref, o_ref)


def _ffn_dense(y, wg, wu, wd, h, mods, gp, *, tm, tiles_per_batch, side_cast=(), zero_rows=0):
    T = h.shape[0]
    nt = T // tm
    if tiles_per_batch is None:
        mrow = lambda i: 4
    else:
        mrow = lambda i: i // tiles_per_batch
    resident = pl.Buffered(1)
    in_specs = [pl.BlockSpec((tm, D_MODEL), lambda i: (i, 0)),
                pl.BlockSpec((D_MODEL, D_FF), lambda i: (0, 0), pipeline_mode=resident),
                pl.BlockSpec((D_MODEL, D_FF), lambda i: (0, 0), pipeline_mode=resident),
                pl.BlockSpec((D_FF, D_MODEL), lambda i: (0, 0), pipeline_mode=resident),
                pl.BlockSpec((tm, D_MODEL), lambda i: (i, 0)),
                pl.BlockSpec((1, D_MODEL), lambda i: (0, 0)),
                pl.BlockSpec((None, None, 1, D_MODEL), lambda i: (mrow(i), 5, 0, 0))]
    out_specs = [pl.BlockSpec((tm, D_MODEL), lambda i: (i, 0))]
    out_shape = [jax.ShapeDtypeStruct((T, D_MODEL), F32)]
    for a in side_cast:
        spec = pl.BlockSpec((a.shape[0] // nt, a.shape[1]), lambda i: (i, 0))
        in_specs.append(spec)
        out_specs.append(spec)
        out_shape.append(jax.ShapeDtypeStruct(a.shape, BF16))
    if side_cast:
        out_specs.append(pl.BlockSpec((zero_rows // nt, D_MODEL), lambda i: (i, 0)))
        out_shape.append(jax.ShapeDtypeStruct((zero_rows, D_MODEL), BF16))
    res = pl.pallas_call(
        _ffn_dense_cast_kernel if side_cast else _ffn_dense_kernel,
        grid=(nt,),
        in_specs=in_specs, out_specs=out_specs, out_shape=out_shape,
        scratch_shapes=[pltpu.VMEM((tm, D_FF), BF16)],
        compiler_params=_cparams(("parallel",)),
    )(y, wg, wu, wd, h, gp, mods, *side_cast)
    return res[0], tuple(res[1:])


MOE_TB = 512
MOE_TM = 512
SUB = 64
NSUB = MOE_TB // SUB
RUN_ALIGN = 16


def _moe_tile_bound(T):
    nt = T // MOE_TB
    rows = 2 * T + nt * N_EXPERTS * (RUN_ALIGN - 1) + N_EXPERTS * (SUB + MOE_TM - 1)
    return -(-rows // (2 * MOE_TM)) * 2


def _moe_plan(cnt, T):
    p = (cnt + RUN_ALIGN - 1) // RUN_ALIGN * RUN_ALIGN
    base = jnp.cumsum(p, axis=0) - p
    used = jnp.sum(p, axis=0)
    tiles = (used + SUB + MOE_TM - 1) // MOE_TM
    tend = jnp.cumsum(tiles)
    off = (tend - tiles) * MOE_TM
    rowbase = (off[None, :] + base).astype(jnp.int32)
    nsub = (cnt + SUB - 1) // SUB
    slotbase = (jnp.cumsum(nsub, axis=1) - nsub).astype(jnp.int32)
    nt_bound = _moe_tile_bound(T)
    j = jnp.arange(nt_bound, dtype=jnp.int32)
    tile_e = jnp.minimum(jnp.sum(j[:, None] >= tend[None, :], axis=1), N_EXPERTS - 1).astype(jnp.int32)
    nvalid = tend[-1].astype(jnp.int32)
    tile_blk = jnp.minimum(j, nvalid - 1)
    return rowbase, slotbase, tile_e, tile_blk, nvalid.reshape(1)


NSLOT = 2 * MOE_TB // SUB + N_EXPERTS
SLOT_ROWS = NSLOT * SUB
SLOT_CHUNK = 512


def _block_slots(sb_ref, cn_ref, b):
    last = b * N_EXPERTS + N_EXPERTS - 1
    return sb_ref[last] + (cn_ref[last] + SUB - 1) // SUB


def _for_each_run_slot(cn_ref, b, fn):
    for e in range(N_EXPERTS):
        for s in range(NSUB):
            @pl.when(cn_ref[b * N_EXPERTS + e] > SUB * s)
            def _(e=e, s=s):
                fn(e, s)


def _run_copy_out(stage, ys_ref, sems, rb_ref, sb_ref, b, par, e, s):
    slot = sb_ref[b * N_EXPERTS + e] + s
    r0 = pl.multiple_of(rb_ref[b * N_EXPERTS + e] + SUB * s, RUN_ALIGN)
    return pltpu.make_async_copy(stage.at[par, pl.ds(pl.multiple_of(slot * SUB, SUB), SUB), :],
                                 ys_ref.at[pl.ds(r0, SUB), :], sems.at[par, e, s])


def _dispatch_kernel(rb_ref, sb_ref, cn_ref, y_ref, cmb_ref, sbv_ref, ltri_ref, ysin_ref,
                     tok_ref, ys_ref, stage, sems):
    del ysin_ref
    b = pl.program_id(0)
    nb = pl.num_programs(0)
    par = b % 2
    cmb = cmb_ref[...]
    sel = cmb > 0.0
    rk = _dot(ltri_ref[...], jnp.where(sel, 1.0, 0.0).astype(BF16))
    srow = jnp.where(sel, rk + sbv_ref[...], -1.0)
    nsel = jnp.sum(jnp.where(sel, 1.0, 0.0), axis=-1, keepdims=True)
    sa = jnp.max(srow, axis=-1, keepdims=True)
    sb = jnp.where(nsel > 1.5, jnp.sum(jnp.where(sel, srow, 0.0), axis=-1, keepdims=True) - sa, -1.0)
    ga = jnp.sum(jnp.where(srow == sa, cmb, 0.0), axis=-1, keepdims=True)
    gb = jnp.sum(cmb, axis=-1, keepdims=True) - ga
    lane = lax.broadcasted_iota(jnp.int32, cmb.shape, 1)
    tok = jnp.where(lane == 0, sa, jnp.where(lane == 1, sb, jnp.where(lane == 2, ga, jnp.where(lane == 3, gb, 0.0))))
    tok_ref[...] = tok
    tok_t = tok.T
    nslots = _block_slots(sb_ref, cn_ref, b)
    rr0 = lax.broadcasted_iota(jnp.int32, (SLOT_CHUNK, cmb.shape[0]), 0).astype(F32)
    for c in range(SLOT_ROWS // SLOT_CHUNK):
        @pl.when(nslots * SUB > c * SLOT_CHUNK)
        def _(c=c):
            rr = rr0 + float(c * SLOT_CHUNK)
            g = jnp.where((rr == tok_t[0:1, :]) | (rr == tok_t[1:2, :]), 1.0, 0.0).astype(BF16)
            stage[par, c * SLOT_CHUNK:(c + 1) * SLOT_CHUNK, :] = _dot(g, y_ref[...]).astype(BF16)

    @pl.when(b > 0)
    def _():
        _for_each_run_slot(cn_ref, b - 1, lambda e, s: _run_copy_out(
            stage, ys_ref, sems, rb_ref, sb_ref, b - 1, 1 - par, e, s).wait())

    _for_each_run_slot(cn_ref, b, lambda e, s: _run_copy_out(
        stage, ys_ref, sems, rb_ref, sb_ref, b, par, e, s).start())

    @pl.when(b == nb - 1)
    def _():
        _for_each_run_slot(cn_ref, b, lambda e, s: _run_copy_out(
            stage, ys_ref, sems, rb_ref, sb_ref, b, par, e, s).wait())


def _moe_dispatch(y, cmb, rowbase, slotbase, cnt, ys0):
    T = y.shape[0]
    nt = T // MOE_TB
    nt_bound = ys0.shape[0] // MOE_TM
    sbv = jnp.zeros((nt, 1, LANES), F32).at[:, 0, :N_EXPERTS].set((slotbase * SUB).astype(F32))
    ltri = jnp.asarray(np.tril(np.ones((MOE_TB, MOE_TB), np.float32), -1)).astype(BF16)
    imap2 = lambda b, rb, sb, cn: (b, 0)
    return pl.pallas_call(
        _dispatch_kernel,
        grid_spec=pltpu.PrefetchScalarGridSpec(
            num_scalar_prefetch=3,
            grid=(nt,),
            in_specs=[pl.BlockSpec((MOE_TB, D_MODEL), imap2),
                      pl.BlockSpec((MOE_TB, LANES), imap2),
                      pl.BlockSpec((None, 1, LANES), lambda b, rb, sb, cn: (b, 0, 0)),
                      pl.BlockSpec((MOE_TB, MOE_TB), lambda b, rb, sb, cn: (0, 0)),
                      pl.BlockSpec(memory_space=pl.ANY)],
            out_specs=[pl.BlockSpec((MOE_TB, LANES), imap2),
                       pl.BlockSpec(memory_space=pl.ANY)],
            scratch_shapes=[pltpu.VMEM((2, SLOT_ROWS, D_MODEL), BF16),
                            pltpu.SemaphoreType.DMA((2, N_EXPERTS, NSUB))]),
        out_shape=[jax.ShapeDtypeStruct((T, LANES), F32),
                   jax.ShapeDtypeStruct((nt_bound * MOE_TM, D_MODEL), BF16)],
        input_output_aliases={7: 1},
        compiler_params=_cparams(("arbitrary",)),
    )(rowbase.reshape(-1), slotbase.reshape(-1), cnt.reshape(-1), y, cmb, sbv, ltri, ys0)


def _ffn_group_kernel(te_ref, tb_ref, nv_ref, y_ref, wg_ref, wu_ref, wd_ref, o_ref, act_ref):
    del te_ref, tb_ref
    j = pl.program_id(0)

    @pl.when(j < nv_ref[0])
    def _():
        o_ref[...] = _swiglu_tile(y_ref[...], wg_ref, wu_ref, wd_ref, act_ref).astype(BF16)

    @pl.when(j >= nv_ref[0])
    def _():
        o_ref[...] = jnp.zeros_like(o_ref)


def _ffn_group(ys, wg, wu, wd, tile_e, tile_blk, nvalid):
    rows = ys.shape[0]
    resident = pl.Buffered(2)
    return pl.pallas_call(
        _ffn_group_kernel,
        grid_spec=pltpu.PrefetchScalarGridSpec(
            num_scalar_prefetch=3,
            grid=(rows // MOE_TM,),
            in_specs=[pl.BlockSpec((MOE_TM, D_MODEL), lambda j, te, tb, nv: (tb[j], 0)),
                      pl.BlockSpec((None, D_MODEL, D_FF), lambda j, te, tb, nv: (te[j], 0, 0), pipeline_mode=resident),
                      pl.BlockSpec((None, D_MODEL, D_FF), lambda j, te, tb, nv: (te[j], 0, 0), pipeline_mode=resident),
                      pl.BlockSpec((None, D_FF, D_MODEL), lambda j, te, tb, nv: (te[j], 0, 0), pipeline_mode=resident)],
            out_specs=pl.BlockSpec((MOE_TM, D_MODEL), lambda j, te, tb, nv: (j, 0)),
            scratch_shapes=[pltpu.VMEM((MOE_TM, D_FF), BF16)]),
        out_shape=jax.ShapeDtypeStruct((rows, D_MODEL), BF16),
        compiler_params=_cparams(("arbitrary",)),
    )(tile_e, tile_blk, nvalid, ys, wg, wu, wd)


def _run_copy_in(fs_ref, fbuf, sems, rb_ref, sb_ref, b, par, e, s):
    slot = sb_ref[b * N_EXPERTS + e] + s
    r0 = pl.multiple_of(rb_ref[b * N_EXPERTS + e] + SUB * s, RUN_ALIGN)
    return pltpu.make_async_copy(fs_ref.at[pl.ds(r0, SUB), :],
                                 fbuf.at[par, pl.ds(pl.multiple_of(slot * SUB, SUB), SUB), :], sems.at[par, e, s])


def _combine_kernel(rb_ref, sb_ref, cn_ref, tok_ref, h_ref, gp_ref, gt_ref, fs_ref, o_ref, fbuf, sems):
    b = pl.program_id(0)
    nb = pl.num_programs(0)
    par = b % 2

    @pl.when(b == 0)
    def _():
        fbuf[...] = jnp.zeros_like(fbuf)
        _for_each_run_slot(cn_ref, b, lambda e, s: _run_copy_in(
            fs_ref, fbuf, sems, rb_ref, sb_ref, b, par, e, s).start())

    @pl.when(b + 1 < nb)
    def _():
        _for_each_run_slot(cn_ref, b + 1, lambda e, s: _run_copy_in(
            fs_ref, fbuf, sems, rb_ref, sb_ref, b + 1, 1 - par, e, s).start())

    tok = tok_ref[...]
    sa, sb, ga, gb = tok[:, 0:1], tok[:, 1:2], tok[:, 2:3], tok[:, 3:4]
    _for_each_run_slot(cn_ref, b, lambda e, s: _run_copy_in(
        fs_ref, fbuf, sems, rb_ref, sb_ref, b, par, e, s).wait())
    nslots = _block_slots(sb_ref, cn_ref, b)
    nck = SLOT_ROWS // SLOT_CHUNK
    for c in range(1, nck + 1):
        lo_rows, hi_rows = (c - 1) * SLOT_CHUNK, c * SLOT_CHUNK
        cond = (nslots * SUB > lo_rows) if c == nck else ((nslots * SUB > lo_rows) & (nslots * SUB <= hi_rows))
        if c == 1:
            cond = nslots * SUB <= hi_rows

        @pl.when(cond)
        def _(depth=hi_rows):
            cc = lax.broadcasted_iota(jnp.int32, (tok.shape[0], depth), 1).astype(F32)
            pick = jnp.where(cc == sa, ga, jnp.where(cc == sb, gb, 0.0)).astype(BF16)
            f = _dot(pick, fbuf[par, :depth, :])
            _ffn_epilogue(f, h_ref, gp_ref, gt_ref, o_ref)


def _moe_combine(fs, tok, h, mods, gp, rowbase, slotbase, cnt, *, tiles_per_batch):
    T = h.shape[0]
    nt = T // MOE_TB
    imap2 = lambda b, rb, sb, cn: (b, 0)
    return pl.pallas_call(
        _combine_kernel,
        grid_spec=pltpu.PrefetchScalarGridSpec(
            num_scalar_prefetch=3,
            grid=(nt,),
            in_specs=[pl.BlockSpec((MOE_TB, LANES), imap2),
                      pl.BlockSpec((MOE_TB, D_MODEL), imap2),
                      pl.BlockSpec((1, D_MODEL), lambda b, rb, sb, cn: (0, 0)),
                      pl.BlockSpec((None, None, 1, D_MODEL), lambda b, rb, sb, cn: (b // tiles_per_batch, 5, 0, 0)),
                      pl.BlockSpec(memory_space=pl.ANY)],
            out_specs=pl.BlockSpec((MOE_TB, D_MODEL), imap2),
            scratch_shapes=[pltpu.VMEM((2, SLOT_ROWS, D_MODEL), BF16),
                            pltpu.SemaphoreType.DMA((2, N_EXPERTS, NSUB))]),
        out_shape=jax.ShapeDtypeStruct((T, D_MODEL), F32),
        compiler_params=_cparams(("arbitrary",)),
    )(rowbase.reshape(-1), slotbase.reshape(-1), cnt.reshape(-1), tok, h, gp, mods, fs)


def _ffn_moe(y, cmb, cnt_tiles, wg, wu, wd, h, mods, gp, ys0, *, tiles_per_batch):
    T = h.shape[0]
    cnt = cnt_tiles.reshape(T // MOE_TB, LANES)[:, :N_EXPERTS].astype(jnp.int32)
    rowbase, slotbase, tile_e, tile_blk, nvalid = _moe_plan(cnt, T)
    tok, ys = _moe_dispatch(y, cmb, rowbase, slotbase, cnt, ys0)
    fs = _ffn_group(ys, wg, wu, wd, tile_e, tile_blk, nvalid)
    return _moe_combine(fs, tok, h, mods, gp, rowbase, slotbase, cnt, tiles_per_batch=tiles_per_batch)


def _blockdiag(m, reps):
    n = m.shape[0]
    out = np.zeros((n * reps, n * reps), np.float64)
    for r in range(reps):
        out[r * n:(r + 1) * n, r * n:(r + 1) * n] = m
    return out


def _dft_tables(n_rows):
    n = n_rows * GRID_W
    r = np.arange(n_rows)
    c = np.arange(GRID_W)
    a1 = 2 * np.pi * np.outer(r, r) / n_rows
    c1, s1 = np.cos(a1), np.sin(a1)
    m1 = np.block([[c1, s1], [-s1, c1]]) * 0.125
    at = 2 * np.pi * np.outer(r, c) / n
    twc = np.repeat(np.cos(at), B_WIDTH, axis=1)
    tws = np.repeat(np.sin(at), B_WIDTH, axis=1)
    a3 = 2 * np.pi * np.outer(c, c) / GRID_W
    m3 = np.concatenate([np.cos(a3), np.sin(a3)], axis=1) * (8.0 / np.sqrt(n))
    f32 = lambda t: jnp.asarray(t.astype(np.float32))
    return f32(m1).astype(BF16), f32(twc), f32(tws), f32(m3).astype(BF16)


def _channel_dft_table():
    d = np.arange(HEAD_DIM)
    a = 2 * np.pi * np.outer(d, d) / HEAD_DIM
    w = np.concatenate([_blockdiag(np.cos(a), 4), -_blockdiag(np.sin(a), 4)], axis=1) * 0.125
    return jnp.asarray(w.astype(np.float32)).astype(BF16)


def _ctx_dft_table(m):
    p = np.arange(m)
    a = 2 * np.pi * np.outer(p, p) / m
    w = np.concatenate([np.cos(a), np.sin(a)], axis=1) * (8.0 / np.sqrt(m * HEAD_DIM))
    return jnp.asarray(w.astype(np.float32)).astype(BF16)


def _rope_tables(n_tok):
    rows = n_tok // GRID_W
    row = jnp.broadcast_to(jnp.arange(rows)[:, None], (rows, GRID_W)).reshape(-1)
    col = jnp.broadcast_to(jnp.arange(GRID_W)[None, :], (rows, GRID_W)).reshape(-1)
    half = HEAD_DIM // 2
    inv = ROPE_BASE ** (-jnp.arange(0, half, 2, dtype=F32) / half)
    ang = jnp.stack([row.astype(F32)[:, None] * inv, col.astype(F32)[:, None] * inv], axis=1)
    cos, sin = jnp.cos(ang), jnp.sin(ang)
    zer = jnp.zeros_like(sin)
    lay = lambda a, b: jnp.tile(jnp.stack([a, b], axis=2).reshape(n_tok, HEAD_DIM), (1, LANES // HEAD_DIM))
    return lay(cos, cos), lay(-sin, zer), lay(zer, sin)


def kernel(x, c, ctx, c_ctx, w_ada, b_ada, g_mix_pre, g_mix_post, g_ffn_pre, g_ffn_post,
           w_in, w_s, b_s, g_v, w_f, sink, w_out, w_gate_d, w_up_d, w_down_d,
           w_router, b_router, w_gate_e, w_up_e, w_down_e):
    bsz, n_lat, _ = x.shape
    n_ctx = ctx.shape[1]
    T, Tc = bsz * n_lat, bsz * n_ctx
    tm = 512
    tpb = n_lat // tm

    rope_tabs = _rope_tables(n_lat)
    m1, twc, tws, m3 = _dft_tables(n_lat // GRID_W)
    wdft = _channel_dft_table()
    mctx = _ctx_dft_table(n_ctx)
    hm = jnp.asarray(_blockdiag(np.full((HEAD_DIM, HEAD_DIM), 1.0 / HEAD_DIM), A_HEADS).astype(np.float32)).astype(BF16)

    cond8 = jnp.zeros((8, D_MODEL), F32).at[:bsz].set(c).at[4].set(c_ctx)
    h = x.reshape(T, D_MODEL)
    hc = ctx.reshape(Tc, D_MODEL)

    mods_all = _adaln(cond8, w_ada, b_ada[:, None, :]).reshape(DEPTH, 8, 6, 1, D_MODEL)

    for i in range(DEPTH):
        last = i == DEPTH - 1
        mods = mods_all[i]
        win = w_in[i].astype(BF16)
        ws = jnp.transpose(w_s[i], (1, 0, 2)).reshape(CHUNK, A_HEADS * CHUNK).astype(BF16)
        bsx = jnp.repeat(b_s[i].T, HEAD_DIM, axis=1)
        gv = g_v[i].reshape(1, A_WIDTH)
        wf = jax.scipy.linalg.block_diag(*[w_f[i][g] for g in range(4)]).astype(BF16)
        wo = w_out[i].astype(BF16)
        gpre = g_mix_pre[i][None, :]
        gpost = g_mix_post[i][None, :]
        gfpre = g_ffn_pre[i][None, :]
        gfpost = g_ffn_post[i][None, :]
        sk = sink[i]

        sgu, fre, fim, q, k, v = _premix(h, mods, gpre, win, ws, bsx, gv, hm, wdft, rope_tabs,
                                         tm=2 * tm, tiles_per_batch=tpb // 2, n_pos=n_lat)
        sguc, frec, fimc, qc, kc, vc = _premix(hc, mods, gpre, win, ws, bsx, gv, hm, wdft, None,
                                               tm=tm, tiles_per_batch=None, n_pos=n_ctx)
        yf = _seq_dft(fre, fim, bsz, n_lat // GRID_W, m1, twc, tws, m3)
        at = _window_attention(q, k, v, kc, vc, sk, bsz, n_lat, n_ctx)

        if i % 2 == 0:
            router = None
        else:
            j = i // 2
            wr = jnp.zeros((D_MODEL, LANES), F32).at[:, :N_EXPERTS].set(w_router[j])
            br = jnp.zeros((1, LANES), F32).at[0, :N_EXPERTS].set(b_router[j])
            router = (wr, br)
        dense_w = (w_gate_d[i // 2], w_up_d[i // 2], w_down_d[i // 2]) if i % 2 == 0 else ()
        res = _postmix(sgu, yf, at, wf, wo, h, mods, gpost, gfpre, router, tm=2 * tm, tiles_per_batch=tpb // 2,
                       side_cast=dense_w)
        if not last:
            yfc = _ctx_dft(frec, fimc, bsz, n_ctx, mctx)
            atc = _context_attention(qc, kc, vc, sk, bsz, n_ctx)
            resc = _postmix(sguc, yfc, atc, wf, wo, hc, mods, gpost, gfpre, router, tm=2 * tm, tiles_per_batch=None)

        j = i // 2
        if i % 2 == 0:
            wg, wu, wd = res[-3:]
            side = ()
            if not last:
                jn = (i + 1) // 2
                side = (w_gate_e[jn].reshape(N_EXPERTS * D_MODEL, D_FF), w_up_e[jn].reshape(N_EXPERTS * D_MODEL, D_FF),
                        w_down_e[jn].reshape(N_EXPERTS * D_FF, D_MODEL))
            h, expert_w = _ffn_dense(res[1], wg, wu, wd, res[0], mods, gfpost, tm=tm, tiles_per_batch=tpb,
                                     side_cast=side, zero_rows=_moe_tile_bound(T) * MOE_TM)
            if not last:
                hc, _ = _ffn_dense(resc[1], wg, wu, wd, resc[0], mods, gfpost, tm=tm, tiles_per_batch=None)
        else:
            wg = expert_w[0].reshape(N_EXPERTS, D_MODEL, D_FF)
            wu = expert_w[1].reshape(N_EXPERTS, D_MODEL, D_FF)
            wd = expert_w[2].reshape(N_EXPERTS, D_FF, D_MODEL)
            assert last and tm == MOE_TB, "the expert FFN is only built for the final layer's latent tokens"
            h = _ffn_moe(res[1], res[2], res[3], wg, wu, wd, res[0], mods, gfpost, expert_w[3],
                         tiles_per_batch=n_lat // MOE_TB)
    return h.reshape(bsz, n_lat, D_MODEL)
```

```python
import functools

import numpy as np
import jax
import jax.numpy as jnp
from jax import lax
from jax.experimental import pallas as pl
from jax.experimental.pallas import tpu as pltpu

D_MODEL = 1024
DEPTH = 2
GRID_W = 64
HEAD_DIM = 64
EPS = 1e-6
A_HEADS = 4
A_WIDTH = 256
CHUNK = 128
B_WIDTH = 256
C_Q_HEADS = 8
C_WIDTH = 512
KV_WIDTH = 128
WINDOW = 128
BLOCK = 128
ROPE_BASE = 10000.0
OFF_B = 512
OFF_Q = 768
OFF_K = 1280
OFF_V = 1408
N_IN = 1536
D_FF = 3584
N_EXPERTS = 8

LANES = 128
VMEM_LIMIT = 60 * 1024 * 1024
NEG = -1e30
LOG2E = 1.4426950408889634
assert WINDOW == BLOCK

F32 = jnp.float32
BF16 = jnp.bfloat16


def _dot(a, b):
    return jnp.dot(a, b, preferred_element_type=F32)


def _cparams(sem):
    return pltpu.CompilerParams(dimension_semantics=sem, vmem_limit_bytes=VMEM_LIMIT)


def _adaln_kernel(c_ref, w_ref, b_ref, o_ref):
    c = c_ref[...]
    s = (c * jax.nn.sigmoid(c)).astype(BF16)
    o_ref[...] = _dot(s, w_ref[...].astype(BF16)) + b_ref[...]


def _adaln(cond8, w, b):
    tn = 1536
    nl = w.shape[0]
    return pl.pallas_call(
        _adaln_kernel,
        grid=(nl, 6 * D_MODEL // tn),
        in_specs=[pl.BlockSpec((8, D_MODEL), lambda l, j: (0, 0)),
                  pl.BlockSpec((None, D_MODEL, tn), lambda l, j: (l, 0, j)),
                  pl.BlockSpec((None, 1, tn), lambda l, j: (l, 0, j))],
        out_specs=pl.BlockSpec((None, 8, tn), lambda l, j: (l, 0, j)),
        out_shape=jax.ShapeDtypeStruct((nl, 8, 6 * D_MODEL), F32),
        compiler_params=_cparams(("arbitrary", "arbitrary")),
    )(cond8, w, b)


def _gelu_tanh(x):
    return 0.5 * x * (1.0 + jnp.tanh(0.7978845608028654 * (x + 0.044715 * x * x * x)))


PRE_SUB = 512


def _premix_kernel(*refs, rope):
    if rope:
        (h_ref, sh_ref, sc_ref, g_ref, win_ref, ws_ref, bs_ref, gv_ref, hm_ref, wdft_ref,
         cos_ref, s1_ref, s2_ref, sgu_ref, re_ref, im_ref, q_ref, k_ref, v_ref, fsc_ref) = refs
    else:
        (h_ref, sh_ref, sc_ref, g_ref, win_ref, ws_ref, bs_ref, gv_ref, hm_ref, wdft_ref,
         sgu_ref, re_ref, im_ref, q_ref, k_ref, v_ref) = refs
    tm = h_ref.shape[0]
    nl = 2 * B_WIDTH // LANES
    head = lax.broadcasted_iota(jnp.int32, (CHUNK, A_WIDTH), 1) // HEAD_DIM
    lo = lax.broadcasted_iota(jnp.int32, (PRE_SUB, LANES), 1) < HEAD_DIM
    for sb in range(tm // PRE_SUB):
        rs = slice(sb * PRE_SUB, (sb + 1) * PRE_SUB)
        x = h_ref[rs, :]
        ms = jnp.mean(x * x, axis=-1, keepdims=True)
        xn = x * lax.rsqrt(ms + EPS) * g_ref[...]
        xm = (xn * (1.0 + sc_ref[...]) + sh_ref[...]).astype(BF16)
        z = _dot(xm, win_ref[...])
        zcols = lambda c0, c1: z[:, c0:c1]

        a = _gelu_tanh(zcols(0, OFF_B))
        u = a[:, :A_WIDTH]
        v = a[:, A_WIDTH:]
        msv = _dot((v * v).astype(BF16), hm_ref[...])
        vn = (v * lax.rsqrt(msv + EPS) * gv_ref[...]).astype(BF16)
        for ck in range(PRE_SUB // CHUNK):
            rows = slice(ck * CHUNK, (ck + 1) * CHUNK)
            vc = vn[rows]
            vstack = jnp.concatenate([jnp.where(head == hh, vc, jnp.zeros_like(vc)) for hh in range(A_HEADS)], axis=0)
            sv = bs_ref[...] + _dot(ws_ref[...], vstack)
            sgu_ref[sb * PRE_SUB + ck * CHUNK:sb * PRE_SUB + (ck + 1) * CHUNK, :] = (u[rows] * sv).astype(BF16)

        f = _dot(zcols(OFF_B, OFF_Q).astype(BF16), wdft_ref[...])
        if rope:
            for j in range(nl):
                fsc_ref[j, rs, :] = f[:, j * LANES:(j + 1) * LANES]
        else:
            re_ref[rs, :] = f[:, :B_WIDTH].astype(BF16)
            im_ref[rs, :] = f[:, B_WIDTH:].astype(BF16)

        def rot(t):
            if not rope:
                return t
            return (t * cos_ref[rs, :] + pltpu.roll(t, LANES - 16, 1) * s1_ref[rs, :]
                    + pltpu.roll(t, 16, 1) * s2_ref[rs, :])

        zq = zcols(OFF_Q, OFF_K)
        for j in range(C_WIDTH // LANES):
            q_ref[rs, j * LANES:(j + 1) * LANES] = (
                rot(zq[:, j * LANES:(j + 1) * LANES]) * (HEAD_DIM ** -0.5 * LOG2E)).astype(BF16)

        zkv = zcols(OFF_K, N_IN)
        kk = rot(zkv[:, :KV_WIDTH])
        kr = pltpu.roll(kk, HEAD_DIM, 1)
        k_ref[rs, :LANES] = jnp.where(lo, kk, kr).astype(BF16)
        k_ref[rs, LANES:] = jnp.where(lo, kr, kk).astype(BF16)
        vv = zkv[:, KV_WIDTH:]
        vr = pltpu.roll(vv, HEAD_DIM, 1)
        v_ref[rs, :LANES] = jnp.where(lo, vv, vr).astype(BF16)
        v_ref[rs, LANES:] = jnp.where(lo, vr, vv).astype(BF16)

    if rope:
        nr = tm // GRID_W
        for cc in range(GRID_W):
            for j in range(nl):
                blk = fsc_ref[j, pl.ds(cc, nr, stride=GRID_W), :].astype(BF16)
                dst = re_ref if j < nl // 2 else im_ref
                c0 = cc * B_WIDTH + (j % (nl // 2)) * LANES
                dst[:, c0:c0 + LANES] = blk


def _premix(h, mods, g, win, ws, bsx, gv, hm, wdft, rope_tabs, *, tm, tiles_per_batch, n_pos):
    T = h.shape[0]
    rope = rope_tabs is not None
    if tiles_per_batch is None:
        mrow = lambda i: 4
    else:
        mrow = lambda i: i // tiles_per_batch
    const2 = lambda i: (0, 0)
    in_specs = [
        pl.BlockSpec((tm, D_MODEL), lambda i: (i, 0)),
        pl.BlockSpec((None, None, 1, D_MODEL), lambda i: (mrow(i), 0, 0, 0)),
        pl.BlockSpec((None, None, 1, D_MODEL), lambda i: (mrow(i), 1, 0, 0)),
        pl.BlockSpec((1, D_MODEL), const2),
        pl.BlockSpec((D_MODEL, N_IN), const2),
        pl.BlockSpec((CHUNK, A_HEADS * CHUNK), const2),
        pl.BlockSpec((CHUNK, A_WIDTH), const2),
        pl.BlockSpec((1, A_WIDTH), const2),
        pl.BlockSpec((A_WIDTH, A_WIDTH), const2),
        pl.BlockSpec((B_WIDTH, 2 * B_WIDTH), const2),
    ]
    args = [h, mods, mods, g, win, ws, bsx, gv, hm, wdft]
    if rope:
        nt = n_pos // tm
        for t in rope_tabs:
            in_specs.append(pl.BlockSpec((tm, LANES), lambda i: (i % nt, 0)))
            args.append(t)
    widths = (A_WIDTH, B_WIDTH, B_WIDTH, C_WIDTH, 2 * KV_WIDTH, 2 * KV_WIDTH)
    out_specs = [pl.BlockSpec((tm, w), lambda i: (i, 0)) for w in widths]
    out_shape = [jax.ShapeDtypeStruct((T, w), BF16) for w in widths]
    scratch = []
    if rope:
        for o in (1, 2):
            out_specs[o] = pl.BlockSpec((tm // GRID_W, GRID_W * B_WIDTH), lambda i: (i, 0))
            out_shape[o] = jax.ShapeDtypeStruct((T // GRID_W, GRID_W * B_WIDTH), BF16)
        scratch = [pltpu.VMEM((2 * B_WIDTH // LANES, tm, LANES), F32)]
    return pl.pallas_call(
        functools.partial(_premix_kernel, rope=rope),
        grid=(T // tm,),
        in_specs=in_specs, out_specs=out_specs, out_shape=out_shape, scratch_shapes=scratch,
        compiler_params=_cparams(("parallel",)),
    )(*args)


def _dft1_kernel(m1_ref, twc_ref, tws_ref, re_ref, im_ref, o_ref):
    x = jnp.concatenate([re_ref[0], im_ref[0]], axis=0)
    a = _dot(m1_ref[...], x)
    nr = a.shape[0] // 2
    are, aim = a[:nr], a[nr:]
    c, s = twc_ref[...], tws_ref[...]
    bre = (are * c + aim * s).astype(BF16)
    bim = (aim * c - are * s).astype(BF16)
    for cl in range(o_ref.shape[2]):
        o_ref[0, 0, cl] = bre[:, cl * B_WIDTH:(cl + 1) * B_WIDTH]
        o_ref[0, 1, cl] = bim[:, cl * B_WIDTH:(cl + 1) * B_WIDTH]


def _dft3_kernel(m3_ref, x_ref, o_ref, xs_ref, os_ref):
    kb = x_ref.shape[3]
    nl = B_WIDTH // LANES
    for p in range(2):
        for c in range(GRID_W):
            xc = x_ref[0, p, c].astype(F32)
            for j in range(nl):
                xs_ref[p * nl + j, c * kb:(c + 1) * kb, :] = xc[:, j * LANES:(j + 1) * LANES]
    for k in range(kb):
        rows = [jnp.concatenate([xs_ref[p * nl + j, pl.ds(k, GRID_W, stride=kb), :] for j in range(nl)], axis=1)
                for p in range(2)]
        res = _dot(m3_ref[...], jnp.concatenate(rows, axis=0).astype(BF16))
        for j in range(nl):
            os_ref[j, pl.ds(k, GRID_W, stride=kb), :] = res[:, j * LANES:(j + 1) * LANES]
    for k2 in range(GRID_W):
        o_ref[0, k2] = jnp.concatenate(
            [os_ref[j, k2 * kb:(k2 + 1) * kb, :] for j in range(nl)], axis=1).astype(BF16)


def _seq_dft(re, im, bsz, n_rows, m1, twc, tws, m3):
    ncol = GRID_W * B_WIDTH
    tn = 4096
    re3 = re.reshape(bsz, n_rows, ncol)
    im3 = im.reshape(bsz, n_rows, ncol)
    st1 = pl.pallas_call(
        _dft1_kernel,
        grid=(ncol // tn, bsz),
        in_specs=[pl.BlockSpec((2 * n_rows, 2 * n_rows), lambda j, b: (0, 0)),
                  pl.BlockSpec((n_rows, tn), lambda j, b: (0, j)),
                  pl.BlockSpec((n_rows, tn), lambda j, b: (0, j)),
                  pl.BlockSpec((1, n_rows, tn), lambda j, b: (b, 0, j)),
                  pl.BlockSpec((1, n_rows, tn), lambda j, b: (b, 0, j))],
        out_specs=pl.BlockSpec((1, 2, tn // B_WIDTH, n_rows, B_WIDTH), lambda j, b: (b, 0, j, 0, 0)),
        out_shape=jax.ShapeDtypeStruct((bsz, 2, GRID_W, n_rows, B_WIDTH), BF16),
        compiler_params=_cparams(("parallel", "parallel")),
    )(m1, twc, tws, re3, im3)
    kb = min(n_rows, 16)
    y = pl.pallas_call(
        _dft3_kernel,
        grid=(bsz, n_rows // kb),
        in_specs=[pl.BlockSpec((GRID_W, 2 * GRID_W), lambda b, j: (0, 0)),
                  pl.BlockSpec((1, 2, GRID_W, kb, B_WIDTH), lambda b, j: (b, 0, 0, j, 0))],
        out_specs=pl.BlockSpec((1, GRID_W, kb, B_WIDTH), lambda b, j: (b, 0, j, 0)),
        out_shape=jax.ShapeDtypeStruct((bsz, GRID_W, n_rows, B_WIDTH), BF16),
        scratch_shapes=[pltpu.VMEM((2 * B_WIDTH // LANES, GRID_W * kb, LANES), F32),
                        pltpu.VMEM((B_WIDTH // LANES, GRID_W * kb, LANES), F32)],
        compiler_params=_cparams(("parallel", "parallel")),
    )(m3, st1)
    return y.reshape(bsz * GRID_W * n_rows, B_WIDTH)


def _ctx_dft_kernel(m_ref, re_ref, im_ref, o_ref):
    x = jnp.concatenate([re_ref[0], im_ref[0]], axis=0)
    o_ref[0] = _dot(m_ref[...], x).astype(BF16)


def _ctx_dft(re, im, bsz, m, mat):
    re3 = re.reshape(bsz, m, B_WIDTH)
    im3 = im.reshape(bsz, m, B_WIDTH)
    y = pl.pallas_call(
        _ctx_dft_kernel,
        grid=(bsz,),
        in_specs=[pl.BlockSpec((m, 2 * m), lambda b: (0, 0)),
                  pl.BlockSpec((1, m, B_WIDTH), lambda b: (b, 0, 0)),
                  pl.BlockSpec((1, m, B_WIDTH), lambda b: (b, 0, 0))],
        out_specs=pl.BlockSpec((1, m, B_WIDTH), lambda b: (b, 0, 0)),
        out_shape=jax.ShapeDtypeStruct((bsz, m, B_WIDTH), BF16),
        compiler_params=_cparams(("parallel",)),
    )(mat, re3, im3)
    return y.reshape(bsz * m, B_WIDTH)


ATTN_SUBBLOCKS = 4


def _attn_kernel(sink_ref, *refs, local, nb):
    if local:
        q_ref, kp_ref, kc_ref, kn_ref, vp_ref, vc_ref, vn_ref, kx_ref, vx_ref, o_ref = refs
    else:
        q_ref, kx_ref, vx_ref, o_ref = refs
    tq = BLOCK if local else q_ref.shape[1]
    nsb = q_ref.shape[1] // tq
    lo = lax.broadcasted_iota(jnp.int32, (tq, LANES), 1) < HEAD_DIM
    zero = jnp.zeros((tq, LANES), BF16)
    lo4 = lax.broadcasted_iota(jnp.int32, (4 * tq, LANES), 1) < HEAD_DIM
    if local:
        j = pl.program_id(1)
        row = lax.broadcasted_iota(jnp.int32, (BLOCK, BLOCK), 0)
        col = lax.broadcasted_iota(jnp.int32, (BLOCK, BLOCK), 1)

        def key_blocks(p_ref, c_ref, n_ref, gl):
            return ([p_ref[0][:, gl]] + [c_ref[0][sb * BLOCK:(sb + 1) * BLOCK, gl] for sb in range(nsb)]
                    + [n_ref[0][:, gl]])
    jobs = []
    for g in range(2):
        gl = slice(g * LANES, (g + 1) * LANES)
        if local:
            kb = key_blocks(kp_ref, kc_ref, kn_ref, gl)
            vb = key_blocks(vp_ref, vc_ref, vn_ref, gl)
        for sb in range(nsb):
            if local:
                kg = jnp.concatenate(kb[sb:sb + 3] + [kx_ref[0][:, gl]], axis=0)
                vg = jnp.concatenate(vb[sb:sb + 3] + [vx_ref[0][:, gl]], axis=0)
            else:
                kg = kx_ref[0][:, gl]
                vg = vx_ref[0][:, gl]
            q = q_ref[0, sb * tq:(sb + 1) * tq, :]
            qs = []
            for p in range(2):
                qp = q[:, g * 2 * LANES + p * LANES: g * 2 * LANES + (p + 1) * LANES]
                qs.append(jnp.where(lo, qp, zero))
                qs.append(jnp.where(lo, zero, qp))
            q4 = jnp.concatenate(qs, axis=0)
            s4 = lax.dot_general(q4, kg, (((1,), (1,)), ((), ())), preferred_element_type=F32)
            jobs.append((g, sb, s4, vg))
    for g, sb, s4, vg in jobs:
        if local:
            valid_prev = (col >= row) & (j > 0) if sb == 0 else (col >= row)
            valid_next = (col <= row) & (j < nb // nsb - 1) if sb == nsb - 1 else (col <= row)
        ps, sinks = [], []
        for hl in range(4):
            s = s4[hl * tq:(hl + 1) * tq]
            if local:
                s = jnp.concatenate(
                    [jnp.where(valid_prev, s[:, :BLOCK], NEG), s[:, BLOCK:2 * BLOCK],
                     jnp.where(valid_next, s[:, 2 * BLOCK:3 * BLOCK], NEG), s[:, 3 * BLOCK:]], axis=1)
            sk = sink_ref[g * 4 + hl] * LOG2E
            m = jnp.maximum(jnp.max(s, axis=-1, keepdims=True), sk)
            ps.append(jnp.exp2(s - m).astype(BF16))
            sinks.append(jnp.exp2(sk - m))
        lov = lax.broadcasted_iota(jnp.int32, vg.shape, 1) < HEAD_DIM
        ve = jnp.where(lov, vg, jnp.ones_like(vg))
        o4 = _dot(jnp.concatenate(ps, axis=0), ve)
        o4 = o4 + jnp.where(lo4, 0.0, jnp.concatenate(sinks, axis=0))
        r4 = pltpu.roll(o4, HEAD_DIM, 1)
        for p in range(2):
            ev = slice((2 * p) * tq, (2 * p + 1) * tq)
            od = slice((2 * p + 1) * tq, (2 * p + 2) * tq)
            o_pair = jnp.where(lo, o4[ev] / r4[ev], r4[od] / o4[od])
            c0 = g * 2 * LANES + p * LANES
            o_ref[0, sb * tq:(sb + 1) * tq, c0:c0 + LANES] = o_pair.astype(BF16)


def _window_attention(q, k, v, kx, vx, sink, bsz, n, m):
    nb = n // BLOCK
    q3 = q.reshape(bsz, n, C_WIDTH)
    k3 = k.reshape(bsz, n, 2 * KV_WIDTH)
    v3 = v.reshape(bsz, n, 2 * KV_WIDTH)
    kx3 = kx.reshape(bsz, m, 2 * KV_WIDTH)
    vx3 = vx.reshape(bsz, m, 2 * KV_WIDTH)
    nsb = ATTN_SUBBLOCKS
    edge = lambda f: pl.BlockSpec((1, BLOCK, 2 * KV_WIDTH), f)
    own = pl.BlockSpec((1, nsb * BLOCK, 2 * KV_WIDTH), lambda b, i, s: (b, i, 0))
    prev = lambda b, i, s: (b, jnp.maximum(i * nsb - 1, 0), 0)
    cur = lambda b, i, s: (b, i, 0)
    nxt = lambda b, i, s: (b, jnp.minimum((i + 1) * nsb, nb - 1), 0)
    ctxs = pl.BlockSpec((1, m, 2 * KV_WIDTH), lambda b, i, s: (b, 0, 0))
    o = pl.pallas_call(
        functools.partial(_attn_kernel, local=True, nb=nb),
        grid_spec=pltpu.PrefetchScalarGridSpec(
            num_scalar_prefetch=1,
            grid=(bsz, nb // nsb),
            in_specs=[pl.BlockSpec((1, nsb * BLOCK, C_WIDTH), cur),
                      edge(prev), own, edge(nxt), edge(prev), own, edge(nxt), ctxs, ctxs],
            out_specs=pl.BlockSpec((1, nsb * BLOCK, C_WIDTH), cur)),
        out_shape=jax.ShapeDtypeStruct((bsz, n, C_WIDTH), BF16),
        compiler_params=_cparams(("parallel", "parallel")),
    )(sink, q3, k3, k3, k3, v3, v3, v3, kx3, vx3)
    return o.reshape(bsz * n, C_WIDTH)


def _context_attention(q, kx, vx, sink, bsz, m):
    q3 = q.reshape(bsz, m, C_WIDTH)
    kx3 = kx.reshape(bsz, m, 2 * KV_WIDTH)
    vx3 = vx.reshape(bsz, m, 2 * KV_WIDTH)
    ctxs = pl.BlockSpec((1, m, 2 * KV_WIDTH), lambda b, s: (b, 0, 0))
    o = pl.pallas_call(
        functools.partial(_attn_kernel, local=False, nb=1),
        grid_spec=pltpu.PrefetchScalarGridSpec(
            num_scalar_prefetch=1,
            grid=(bsz,),
            in_specs=[pl.BlockSpec((1, m, C_WIDTH), lambda b, s: (b, 0, 0)), ctxs, ctxs],
            out_specs=pl.BlockSpec((1, m, C_WIDTH), lambda b, s: (b, 0, 0))),
        out_shape=jax.ShapeDtypeStruct((bsz, m, C_WIDTH), BF16),
        compiler_params=_cparams(("parallel",)),
    )(sink, q3, kx3, vx3)
    return o.reshape(bsz * m, C_WIDTH)


def _postmix_kernel(*refs, route, ncast):
    if ncast:
        n_in = 11 + (2 if route else 0)
        for src, dst in zip(refs[n_in:n_in + ncast], refs[len(refs) - ncast:]):
            dst[...] = src[...].astype(BF16)
        refs = refs[:n_in] + refs[n_in + ncast:len(refs) - ncast]
    h_ref = refs[5]
    for sb in range(h_ref.shape[0] // MOE_TB):
        _postmix_rows(slice(sb * MOE_TB, (sb + 1) * MOE_TB), sb, refs, route)


def _postmix_rows(rows, sb, refs, route):
    (sgu_ref, yf_ref, at_ref, wf_ref, wo_ref, h_ref, gp_ref, gt_ref, gf_ref, sh_ref, sc_ref) = refs[:11]
    if route:
        wr_ref, br_ref, hn_ref, y_ref, cmb_ref, cnt_ref = refs[11:]
    else:
        hn_ref, y_ref = refs[11:]
    fm = _dot(yf_ref[rows, :], wf_ref[...]).astype(BF16)
    o = (_dot(sgu_ref[rows, :], wo_ref[:A_WIDTH, :])
         + _dot(fm, wo_ref[A_WIDTH:A_WIDTH + B_WIDTH, :])
         + _dot(at_ref[rows, :], wo_ref[A_WIDTH + B_WIDTH:, :]))
    ms = jnp.mean(o * o, axis=-1, keepdims=True)
    hn = h_ref[rows, :] + gt_ref[...] * (o * lax.rsqrt(ms + EPS) * gp_ref[...])
    hn_ref[rows, :] = hn
    ms2 = jnp.mean(hn * hn, axis=-1, keepdims=True)
    y = hn * lax.rsqrt(ms2 + EPS) * gf_ref[...] * (1.0 + sc_ref[...]) + sh_ref[...]
    y_ref[rows, :] = y.astype(BF16)
    if route:
        yh = y.astype(BF16)
        yl = (y - yh.astype(F32)).astype(BF16)
        w = wr_ref[...]
        wh = w.astype(BF16)
        wl = (w - wh.astype(F32)).astype(BF16)
        n = yh.shape[0]
        r = _dot(jnp.concatenate([yh, yl], axis=0), jnp.concatenate([wh, wl], axis=1))
        lg = (r[:n, :LANES] + r[n:, :LANES]) + (r[:n, LANES:] + r[n:, LANES:]) + br_ref[...]
        lane = lax.broadcasted_iota(jnp.int32, lg.shape, 1)
        lg = jnp.where(lane < N_EXPERTS, lg, NEG)
        m1 = jnp.max(lg, axis=-1, keepdims=True)
        i1 = jnp.min(jnp.where(lg == m1, lane, LANES), axis=-1, keepdims=True)
        lg2 = jnp.where(lane == i1, NEG, lg)
        m2 = jnp.max(lg2, axis=-1, keepdims=True)
        i2 = jnp.min(jnp.where(lg2 == m2, lane, LANES), axis=-1, keepdims=True)
        e2 = jnp.exp(m2 - m1)
        g1 = 1.0 / (1.0 + e2)
        g2 = e2 * g1
        cmb = jnp.where(lane == i1, g1, 0.0) + jnp.where(lane == i2, g2, 0.0)
        cmb_ref[rows, :] = cmb
        cnt_ref[sb] = jnp.sum(jnp.where(cmb > 0.0, 1.0, 0.0), axis=0, keepdims=True)


def _postmix(sgu, yf, at, wf, wo, h, mods, gp, gf, router, *, tm, tiles_per_batch, side_cast=()):
    T = h.shape[0]
    if tiles_per_batch is None:
        mrow = lambda i: 4
    else:
        mrow = lambda i: i // tiles_per_batch
    const2 = lambda i: (0, 0)
    row = lambda w: pl.BlockSpec((tm, w), lambda i: (i, 0))
    mod = lambda j: pl.BlockSpec((None, None, 1, D_MODEL), lambda i: (mrow(i), j, 0, 0))
    in_specs = [row(A_WIDTH), row(B_WIDTH), row(C_WIDTH),
                pl.BlockSpec((B_WIDTH, B_WIDTH), const2),
                pl.BlockSpec((D_MODEL, D_MODEL), const2),
                row(D_MODEL),
                pl.BlockSpec((1, D_MODEL), const2),
                mod(2),
                pl.BlockSpec((1, D_MODEL), const2),
                mod(3), mod(4)]
    args = [sgu, yf, at, wf, wo, h, gp, mods, gf, mods, mods]
    out_specs = [row(D_MODEL), row(D_MODEL)]
    out_shape = [jax.ShapeDtypeStruct((T, D_MODEL), F32), jax.ShapeDtypeStruct((T, D_MODEL), BF16)]
    route = router is not None
    if route:
        in_specs += [pl.BlockSpec((D_MODEL, LANES), const2), pl.BlockSpec((1, LANES), const2)]
        args += list(router)
        out_specs += [row(LANES), pl.BlockSpec((tm // MOE_TB, 1, LANES), lambda i: (i, 0, 0))]
        out_shape += [jax.ShapeDtypeStruct((T, LANES), F32), jax.ShapeDtypeStruct((T // MOE_TB, 1, LANES), F32)]
    for a in side_cast:
        spec = pl.BlockSpec((a.shape[0] // (T // tm), a.shape[1]), lambda i: (i, 0))
        in_specs.append(spec)
        args.append(a)
        out_specs.append(spec)
        out_shape.append(jax.ShapeDtypeStruct(a.shape, BF16))
    return pl.pallas_call(
        functools.partial(_postmix_kernel, route=route, ncast=len(side_cast)),
        grid=(T // tm,),
        in_specs=in_specs, out_specs=out_specs, out_shape=out_shape,
        compiler_params=_cparams(("parallel",)),
    )(*args)


def _ffn_epilogue(f, h_ref, gp_ref, gt_ref, o_ref):
    ms = jnp.mean(f * f, axis=-1, keepdims=True)
    o_ref[...] = h_ref[...] + gt_ref[...] * (f * lax.rsqrt(ms + EPS) * gp_ref[...])


FF_CHUNK = 256


def _swiglu_tile(y, wg_ref, wu_ref, wd_ref, act_ref):
    for c in range(D_FF // FF_CHUNK):
        cols = slice(c * FF_CHUNK, (c + 1) * FF_CHUNK)
        gate = _dot(y, wg_ref[:, cols])
        up = _dot(y, wu_ref[:, cols])
        act_ref[:, cols] = (gate * jax.nn.sigmoid(gate) * up).astype(BF16)
    return _dot(act_ref[...], wd_ref[...])


def _ffn_dense_kernel(y_ref, wg_ref, wu_ref, wd_ref, h_ref, gp_ref, gt_ref, o_ref, act_ref):
    f = _swiglu_tile(y_ref[...], wg_ref, wu_ref, wd_ref, act_ref)
    _ffn_epilogue(f, h_ref, gp_ref, gt_ref, o_ref)


def _ffn_dense_cast_kernel(y_ref, wg_ref, wu_ref, wd_ref, h_ref, gp_ref, gt_ref, c0_ref, c1_ref, c2_ref,
                           o_ref, d0_ref, d1_ref, d2_ref, z_ref, act_ref):
    d0_ref[...] = c0_ref[...].astype(BF16)
    d1_ref[...] = c1_ref[...].astype(BF16)
    d2_ref[...] = c2_ref[...].astype(BF16)
    z_ref[...] = jnp.zeros_like(z_ref)
    f = _swiglu_tile(y_ref[...], wg_ref, wu_ref, wd_ref, act_ref)
    _ffn_epilogue(f, h_ref, gp_ref, gt_ref, o_ref)


def _ffn_dense(y, wg, wu, wd, h, mods, gp, *, tm, tiles_per_batch, side_cast=(), zero_rows=0):
    T = h.shape[0]
    nt = T // tm
    if tiles_per_batch is None:
        mrow = lambda i: 4
    else:
        mrow = lambda i: i // tiles_per_batch
    resident = pl.Buffered(1)
    in_specs = [pl.BlockSpec((tm, D_MODEL), lambda i: (i, 0)),
                pl.BlockSpec((D_MODEL, D_FF), lambda i: (0, 0), pipeline_mode=resident),
                pl.BlockSpec((D_MODEL, D_FF), lambda i: (0, 0), pipeline_mode=resident),
                pl.BlockSpec((D_FF, D_MODEL), lambda i: (0, 0), pipeline_mode=resident),
                pl.BlockSpec((tm, D_MODEL), lambda i: (i, 0)),
                pl.BlockSpec((1, D_MODEL), lambda i: (0, 0)),
                pl.BlockSpec((None, None, 1, D_MODEL), lambda i: (mrow(i), 5, 0, 0))]
    out_specs = [pl.BlockSpec((tm, D_MODEL), lambda i: (i, 0))]
    out_shape = [jax.ShapeDtypeStruct((T, D_MODEL), F32)]
    for a in side_cast:
        spec = pl.BlockSpec((a.shape[0] // nt, a.shape[1]), lambda i: (i, 0))
        in_specs.append(spec)
        out_specs.append(spec)
        out_shape.append(jax.ShapeDtypeStruct(a.shape, BF16))
    if side_cast:
        out_specs.append(pl.BlockSpec((zero_rows // nt, D_MODEL), lambda i: (i, 0)))
        out_shape.append(jax.ShapeDtypeStruct((zero_rows, D_MODEL), BF16))
    res = pl.pallas_call(
        _ffn_dense_cast_kernel if side_cast else _ffn_dense_kernel,
        grid=(nt,),
        in_specs=in_specs, out_specs=out_specs, out_shape=out_shape,
        scratch_shapes=[pltpu.VMEM((tm, D_FF), BF16)],
        compiler_params=_cparams(("parallel",)),
    )(y, wg, wu, wd, h, gp, mods, *side_cast)
    return res[0], tuple(res[1:])


MOE_TB = 512
MOE_TM = 512
SUB = 64
NSUB = MOE_TB // SUB
RUN_ALIGN = 16


def _moe_tile_bound(T):
    nt = T // MOE_TB
    rows = 2 * T + nt * N_EXPERTS * (RUN_ALIGN - 1) + N_EXPERTS * (SUB + MOE_TM - 1)
    return -(-rows // (2 * MOE_TM)) * 2


def _moe_plan(cnt, T):
    p = (cnt + RUN_ALIGN - 1) // RUN_ALIGN * RUN_ALIGN
    base = jnp.cumsum(p, axis=0) - p
    used = jnp.sum(p, axis=0)
    tiles = (used + SUB + MOE_TM - 1) // MOE_TM
    tend = jnp.cumsum(tiles)
    off = (tend - tiles) * MOE_TM
    rowbase = (off[None, :] + base).astype(jnp.int32)
    nsub = (cnt + SUB - 1) // SUB
    slotbase = (jnp.cumsum(nsub, axis=1) - nsub).astype(jnp.int32)
    nt_bound = _moe_tile_bound(T)
    j = jnp.arange(nt_bound, dtype=jnp.int32)
    tile_e = jnp.minimum(jnp.sum(j[:, None] >= tend[None, :], axis=1), N_EXPERTS - 1).astype(jnp.int32)
    nvalid = tend[-1].astype(jnp.int32)
    tile_blk = jnp.minimum(j, nvalid - 1)
    onehot = tile_e[:, None] == jnp.arange(N_EXPERTS, dtype=jnp.int32)[None, :]
    first = jnp.sum(jnp.where(onehot, (tend - tiles)[None, :], 0), axis=1)
    used_t = jnp.sum(jnp.where(onehot, used[None, :], 0), axis=1)
    tile_live = (((j - first) * MOE_TM < used_t) & (j < nvalid)).astype(jnp.int32)
    return rowbase, slotbase, tile_e, tile_blk, tile_live


NSLOT = 2 * MOE_TB // SUB + N_EXPERTS
SLOT_ROWS = NSLOT * SUB
SLOT_CHUNK = 512


def _block_slots(sb_ref, cn_ref, b):
    last = b * N_EXPERTS + N_EXPERTS - 1
    return sb_ref[last] + (cn_ref[last] + SUB - 1) // SUB


def _for_each_run_slot(cn_ref, b, fn):
    for e in range(N_EXPERTS):
        for s in range(NSUB):
            @pl.when(cn_ref[b * N_EXPERTS + e] > SUB * s)
            def _(e=e, s=s):
                fn(e, s)


def _run_copy_out(stage, ys_ref, sems, rb_ref, sb_ref, b, par, e, s):
    slot = sb_ref[b * N_EXPERTS + e] + s
    r0 = pl.multiple_of(rb_ref[b * N_EXPERTS + e] + SUB * s, RUN_ALIGN)
    return pltpu.make_async_copy(stage.at[par, pl.ds(pl.multiple_of(slot * SUB, SUB), SUB), :],
                                 ys_ref.at[pl.ds(r0, SUB), :], sems.at[par, e, s])


def _dispatch_kernel(rb_ref, sb_ref, cn_ref, y_ref, cmb_ref, sbv_ref, ltri_ref, ysin_ref,
                     tok_ref, ys_ref, stage, sems):
    del ysin_ref
    b = pl.program_id(0)
    nb = pl.num_programs(0)
    par = b % 2
    cmb = cmb_ref[...]
    sel = cmb > 0.0
    rk = _dot(ltri_ref[...], jnp.where(sel, 1.0, 0.0).astype(BF16))
    srow = jnp.where(sel, rk + sbv_ref[...], -1.0)
    nsel = jnp.sum(jnp.where(sel, 1.0, 0.0), axis=-1, keepdims=True)
    sa = jnp.max(srow, axis=-1, keepdims=True)
    sb = jnp.where(nsel > 1.5, jnp.sum(jnp.where(sel, srow, 0.0), axis=-1, keepdims=True) - sa, -1.0)
    ga = jnp.sum(jnp.where(srow == sa, cmb, 0.0), axis=-1, keepdims=True)
    gb = jnp.sum(cmb, axis=-1, keepdims=True) - ga
    lane = lax.broadcasted_iota(jnp.int32, cmb.shape, 1)
    tok = jnp.where(lane == 0, sa, jnp.where(lane == 1, sb, jnp.where(lane == 2, ga, jnp.where(lane == 3, gb, 0.0))))
    tok_ref[...] = tok
    tok_t = tok.T
    nslots = _block_slots(sb_ref, cn_ref, b)
    rr0 = lax.broadcasted_iota(jnp.int32, (SLOT_CHUNK, cmb.shape[0]), 0).astype(F32)
    for c in range(SLOT_ROWS // SLOT_CHUNK):
        @pl.when(nslots * SUB > c * SLOT_CHUNK)
        def _(c=c):
            rr = rr0 + float(c * SLOT_CHUNK)
            g = jnp.where((rr == tok_t[0:1, :]) | (rr == tok_t[1:2, :]), 1.0, 0.0).astype(BF16)
            stage[par, c * SLOT_CHUNK:(c + 1) * SLOT_CHUNK, :] = _dot(g, y_ref[...]).astype(BF16)

    @pl.when(b > 0)
    def _():
        _for_each_run_slot(cn_ref, b - 1, lambda e, s: _run_copy_out(
            stage, ys_ref, sems, rb_ref, sb_ref, b - 1, 1 - par, e, s).wait())

    _for_each_run_slot(cn_ref, b, lambda e, s: _run_copy_out(
        stage, ys_ref, sems, rb_ref, sb_ref, b, par, e, s).start())

    @pl.when(b == nb - 1)
    def _():
        _for_each_run_slot(cn_ref, b, lambda e, s: _run_copy_out(
            stage, ys_ref, sems, rb_ref, sb_ref, b, par, e, s).wait())


def _moe_dispatch(y, cmb, rowbase, slotbase, cnt, ys0):
    T = y.shape[0]
    nt = T // MOE_TB
    nt_bound = ys0.shape[0] // MOE_TM
    sbv = jnp.zeros((nt, 1, LANES), F32).at[:, 0, :N_EXPERTS].set((slotbase * SUB).astype(F32))
    ltri = jnp.asarray(np.tril(np.ones((MOE_TB, MOE_TB), np.float32), -1)).astype(BF16)
    imap2 = lambda b, rb, sb, cn: (b, 0)
    return pl.pallas_call(
        _dispatch_kernel,
        grid_spec=pltpu.PrefetchScalarGridSpec(
            num_scalar_prefetch=3,
            grid=(nt,),
            in_specs=[pl.BlockSpec((MOE_TB, D_MODEL), imap2),
                      pl.BlockSpec((MOE_TB, LANES), imap2),
                      pl.BlockSpec((None, 1, LANES), lambda b, rb, sb, cn: (b, 0, 0)),
                      pl.BlockSpec((MOE_TB, MOE_TB), lambda b, rb, sb, cn: (0, 0)),
                      pl.BlockSpec(memory_space=pl.ANY)],
            out_specs=[pl.BlockSpec((MOE_TB, LANES), imap2),
                       pl.BlockSpec(memory_space=pl.ANY)],
            scratch_shapes=[pltpu.VMEM((2, SLOT_ROWS, D_MODEL), BF16),
                            pltpu.SemaphoreType.DMA((2, N_EXPERTS, NSUB))]),
        out_shape=[jax.ShapeDtypeStruct((T, LANES), F32),
                   jax.ShapeDtypeStruct((nt_bound * MOE_TM, D_MODEL), BF16)],
        input_output_aliases={7: 1},
        compiler_params=_cparams(("arbitrary",)),
    )(rowbase.reshape(-1), slotbase.reshape(-1), cnt.reshape(-1), y, cmb, sbv, ltri, ys0)


def _ffn_group_kernel(te_ref, tb_ref, lv_ref, y_ref, wg_ref, wu_ref, wd_ref, o_ref, act_ref):
    del te_ref, tb_ref
    live = lv_ref[pl.program_id(0)] > 0

    @pl.when(live)
    def _():
        o_ref[...] = _swiglu_tile(y_ref[...], wg_ref, wu_ref, wd_ref, act_ref).astype(BF16)

    @pl.when(jnp.logical_not(live))
    def _():
        o_ref[...] = jnp.zeros_like(o_ref)


def _ffn_group(ys, wg, wu, wd, tile_e, tile_blk, tile_live):
    rows = ys.shape[0]
    resident = pl.Buffered(2)
    return pl.pallas_call(
        _ffn_group_kernel,
        grid_spec=pltpu.PrefetchScalarGridSpec(
            num_scalar_prefetch=3,
            grid=(rows // MOE_TM,),
            in_specs=[pl.BlockSpec((MOE_TM, D_MODEL), lambda j, te, tb, nv: (tb[j], 0)),
                      pl.BlockSpec((None, D_MODEL, D_FF), lambda j, te, tb, nv: (te[j], 0, 0), pipeline_mode=resident),
                      pl.BlockSpec((None, D_MODEL, D_FF), lambda j, te, tb, nv: (te[j], 0, 0), pipeline_mode=resident),
                      pl.BlockSpec((None, D_FF, D_MODEL), lambda j, te, tb, nv: (te[j], 0, 0), pipeline_mode=resident)],
            out_specs=pl.BlockSpec((MOE_TM, D_MODEL), lambda j, te, tb, nv: (j, 0)),
            scratch_shapes=[pltpu.VMEM((MOE_TM, D_FF), BF16)]),
        out_shape=jax.ShapeDtypeStruct((rows, D_MODEL), BF16),
        compiler_params=_cparams(("arbitrary",)),
    )(tile_e, tile_blk, tile_live, ys, wg, wu, wd)


def _run_copy_in(fs_ref, fbuf, sems, rb_ref, sb_ref, b, par, e, s):
    slot = sb_ref[b * N_EXPERTS + e] + s
    r0 = pl.multiple_of(rb_ref[b * N_EXPERTS + e] + SUB * s, RUN_ALIGN)
    return pltpu.make_async_copy(fs_ref.at[pl.ds(r0, SUB), :],
                                 fbuf.at[par, pl.ds(pl.multiple_of(slot * SUB, SUB), SUB), :], sems.at[par, e, s])


def _combine_kernel(rb_ref, sb_ref, cn_ref, tok_ref, h_ref, gp_ref, gt_ref, fs_ref, o_ref, fbuf, sems):
    b = pl.program_id(0)
    nb = pl.num_programs(0)
    par = b % 2

    @pl.when(b == 0)
    def _():
        fbuf[...] = jnp.zeros_like(fbuf)
        _for_each_run_slot(cn_ref, b, lambda e, s: _run_copy_in(
            fs_ref, fbuf, sems, rb_ref, sb_ref, b, par, e, s).start())

    @pl.when(b + 1 < nb)
    def _():
        _for_each_run_slot(cn_ref, b + 1, lambda e, s: _run_copy_in(
            fs_ref, fbuf, sems, rb_ref, sb_ref, b + 1, 1 - par, e, s).start())

    tok = tok_ref[...]
    sa, sb, ga, gb = tok[:, 0:1], tok[:, 1:2], tok[:, 2:3], tok[:, 3:4]
    _for_each_run_slot(cn_ref, b, lambda e, s: _run_copy_in(
        fs_ref, fbuf, sems, rb_ref, sb_ref, b, par, e, s).wait())
    nslots = _block_slots(sb_ref, cn_ref, b)
    nck = SLOT_ROWS // SLOT_CHUNK
    for c in range(1, nck + 1):
        lo_rows, hi_rows = (c - 1) * SLOT_CHUNK, c * SLOT_CHUNK
        cond = (nslots * SUB > lo_rows) if c == nck else ((nslots * SUB > lo_rows) & (nslots * SUB <= hi_rows))
        if c == 1:
            cond = nslots * SUB <= hi_rows

        @pl.when(cond)
        def _(depth=hi_rows):
            cc = lax.broadcasted_iota(jnp.int32, (tok.shape[0], depth), 1).astype(F32)
            pick = jnp.where(cc == sa, ga, jnp.where(cc == sb, gb, 0.0)).astype(BF16)
            f = _dot(pick, fbuf[par, :depth, :])
            _ffn_epilogue(f, h_ref, gp_ref, gt_ref, o_ref)


def _moe_combine(fs, tok, h, mods, gp, rowbase, slotbase, cnt, *, tiles_per_batch):
    T = h.shape[0]
    nt = T // MOE_TB
    imap2 = lambda b, rb, sb, cn: (b, 0)
    return pl.pallas_call(
        _combine_kernel,
        grid_spec=pltpu.PrefetchScalarGridSpec(
            num_scalar_prefetch=3,
            grid=(nt,),
            in_specs=[pl.BlockSpec((MOE_TB, LANES), imap2),
                      pl.BlockSpec((MOE_TB, D_MODEL), imap2),
                      pl.BlockSpec((1, D_MODEL), lambda b, rb, sb, cn: (0, 0)),
                      pl.BlockSpec((None, None, 1, D_MODEL), lambda b, rb, sb, cn: (b // tiles_per_batch, 5, 0, 0)),
                      pl.BlockSpec(memory_space=pl.ANY)],
            out_specs=pl.BlockSpec((MOE_TB, D_MODEL), imap2),
            scratch_shapes=[pltpu.VMEM((2, SLOT_ROWS, D_MODEL), BF16),
                            pltpu.SemaphoreType.DMA((2, N_EXPERTS, NSUB))]),
        out_shape=jax.ShapeDtypeStruct((T, D_MODEL), F32),
        compiler_params=_cparams(("arbitrary",)),
    )(rowbase.reshape(-1), slotbase.reshape(-1), cnt.reshape(-1), tok, h, gp, mods, fs)


def _ffn_moe(y, cmb, cnt_tiles, wg, wu, wd, h, mods, gp, ys0, *, tiles_per_batch):
    T = h.shape[0]
    cnt = cnt_tiles.reshape(T // MOE_TB, LANES)[:, :N_EXPERTS].astype(jnp.int32)
    rowbase, slotbase, tile_e, tile_blk, tile_live = _moe_plan(cnt, T)
    tok, ys = _moe_dispatch(y, cmb, rowbase, slotbase, cnt, ys0)
    fs = _ffn_group(ys, wg, wu, wd, tile_e, tile_blk, tile_live)
    return _moe_combine(fs, tok, h, mods, gp, rowbase, slotbase, cnt, tiles_per_batch=tiles_per_batch)


def _blockdiag(m, reps):
    n = m.shape[0]
    out = np.zeros((n * reps, n * reps), np.float64)
    for r in range(reps):
        out[r * n:(r + 1) * n, r * n:(r + 1) * n] = m
    return out


def _dft_tables(n_rows):
    n = n_rows * GRID_W
    r = np.arange(n_rows)
    c = np.arange(GRID_W)
    a1 = 2 * np.pi * np.outer(r, r) / n_rows
    c1, s1 = np.cos(a1), np.sin(a1)
    m1 = np.block([[c1, s1], [-s1, c1]]) * 0.125
    at = 2 * np.pi * np.outer(r, c) / n
    twc = np.repeat(np.cos(at), B_WIDTH, axis=1)
    tws = np.repeat(np.sin(at), B_WIDTH, axis=1)
    a3 = 2 * np.pi * np.outer(c, c) / GRID_W
    m3 = np.concatenate([np.cos(a3), np.sin(a3)], axis=1) * (8.0 / np.sqrt(n))
    f32 = lambda t: jnp.asarray(t.astype(np.float32))
    return f32(m1).astype(BF16), f32(twc), f32(tws), f32(m3).astype(BF16)


def _channel_dft_table():
    d = np.arange(HEAD_DIM)
    a = 2 * np.pi * np.outer(d, d) / HEAD_DIM
    w = np.concatenate([_blockdiag(np.cos(a), 4), -_blockdiag(np.sin(a), 4)], axis=1) * 0.125
    return jnp.asarray(w.astype(np.float32)).astype(BF16)


def _ctx_dft_table(m):
    p = np.arange(m)
    a = 2 * np.pi * np.outer(p, p) / m
    w = np.concatenate([np.cos(a), np.sin(a)], axis=1) * (8.0 / np.sqrt(m * HEAD_DIM))
    return jnp.asarray(w.astype(np.float32)).astype(BF16)


def _rope_tables(n_tok):
    rows = n_tok // GRID_W
    row = jnp.broadcast_to(jnp.arange(rows)[:, None], (rows, GRID_W)).reshape(-1)
    col = jnp.broadcast_to(jnp.arange(GRID_W)[None, :], (rows, GRID_W)).reshape(-1)
    half = HEAD_DIM // 2
    inv = ROPE_BASE ** (-jnp.arange(0, half, 2, dtype=F32) / half)
    ang = jnp.stack([row.astype(F32)[:, None] * inv, col.astype(F32)[:, None] * inv], axis=1)
    cos, sin = jnp.cos(ang), jnp.sin(ang)
    zer = jnp.zeros_like(sin)
    lay = lambda a, b: jnp.tile(jnp.stack([a, b], axis=2).reshape(n_tok, HEAD_DIM), (1, LANES // HEAD_DIM))
    return lay(cos, cos), lay(-sin, zer), lay(zer, sin)


def kernel(x, c, ctx, c_ctx, w_ada, b_ada, g_mix_pre, g_mix_post, g_ffn_pre, g_ffn_post,
           w_in, w_s, b_s, g_v, w_f, sink, w_out, w_gate_d, w_up_d, w_down_d,
           w_router, b_router, w_gate_e, w_up_e, w_down_e):
    bsz, n_lat, _ = x.shape
    n_ctx = ctx.shape[1]
    T, Tc = bsz * n_lat, bsz * n_ctx
    tm = 512
    tpb = n_lat // tm

    rope_tabs = _rope_tables(n_lat)
    m1, twc, tws, m3 = _dft_tables(n_lat // GRID_W)
    wdft = _channel_dft_table()
    mctx = _ctx_dft_table(n_ctx)
    hm = jnp.asarray(_blockdiag(np.full((HEAD_DIM, HEAD_DIM), 1.0 / HEAD_DIM), A_HEADS).astype(np.float32)).astype(BF16)

    cond8 = jnp.zeros((8, D_MODEL), F32).at[:bsz].set(c).at[4].set(c_ctx)
    h = x.reshape(T, D_MODEL)
    hc = ctx.reshape(Tc, D_MODEL)

    mods_all = _adaln(cond8, w_ada, b_ada[:, None, :]).reshape(DEPTH, 8, 6, 1, D_MODEL)

    for i in range(DEPTH):
        last = i == DEPTH - 1
        mods = mods_all[i]
        win = w_in[i].astype(BF16)
        ws = jnp.transpose(w_s[i], (1, 0, 2)).reshape(CHUNK, A_HEADS * CHUNK).astype(BF16)
        bsx = jnp.repeat(b_s[i].T, HEAD_DIM, axis=1)
        gv = g_v[i].reshape(1, A_WIDTH)
        wf = jax.scipy.linalg.block_diag(*[w_f[i][g] for g in range(4)]).astype(BF16)
        wo = w_out[i].astype(BF16)
        gpre = g_mix_pre[i][None, :]
        gpost = g_mix_post[i][None, :]
        gfpre = g_ffn_pre[i][None, :]
        gfpost = g_ffn_post[i][None, :]
        sk = sink[i]

        sgu, fre, fim, q, k, v = _premix(h, mods, gpre, win, ws, bsx, gv, hm, wdft, rope_tabs,
                                         tm=2 * tm, tiles_per_batch=tpb // 2, n_pos=n_lat)
        sguc, frec, fimc, qc, kc, vc = _premix(hc, mods, gpre, win, ws, bsx, gv, hm, wdft, None,
                                               tm=tm, tiles_per_batch=None, n_pos=n_ctx)
        yf = _seq_dft(fre, fim, bsz, n_lat // GRID_W, m1, twc, tws, m3)
        at = _window_attention(q, k, v, kc, vc, sk, bsz, n_lat, n_ctx)

        if i % 2 == 0:
            router = None
        else:
            j = i // 2
            wr = jnp.zeros((D_MODEL, LANES), F32).at[:, :N_EXPERTS].set(w_router[j])
            br = jnp.zeros((1, LANES), F32).at[0, :N_EXPERTS].set(b_router[j])
            router = (wr, br)
        dense_w = (w_gate_d[i // 2], w_up_d[i // 2], w_down_d[i // 2]) if i % 2 == 0 else ()
        res = _postmix(sgu, yf, at, wf, wo, h, mods, gpost, gfpre, router, tm=2 * tm, tiles_per_batch=tpb // 2,
                       side_cast=dense_w)
        if not last:
            yfc = _ctx_dft(frec, fimc, bsz, n_ctx, mctx)
            atc = _context_attention(qc, kc, vc, sk, bsz, n_ctx)
            resc = _postmix(sguc, yfc, atc, wf, wo, hc, mods, gpost, gfpre, router, tm=2 * tm, tiles_per_batch=None)

        j = i // 2
        if i % 2 == 0:
            wg, wu, wd = res[-3:]
            side = ()
            if not last:
                jn = (i + 1) // 2
                side = (w_gate_e[jn].reshape(N_EXPERTS * D_MODEL, D_FF), w_up_e[jn].reshape(N_EXPERTS * D_MODEL, D_FF),
                        w_down_e[jn].reshape(N_EXPERTS * D_FF, D_MODEL))
            h, expert_w = _ffn_dense(res[1], wg, wu, wd, res[0], mods, gfpost, tm=tm, tiles_per_batch=tpb,
                                     side_cast=side, zero_rows=_moe_tile_bound(T) * MOE_TM)
            if not last:
                hc, _ = _ffn_dense(resc[1], wg, wu, wd, resc[0], mods, gfpost, tm=tm, tiles_per_batch=None)
        else:
            wg = expert_w[0].reshape(N_EXPERTS, D_MODEL, D_FF)
            wu = expert_w[1].reshape(N_EXPERTS, D_MODEL, D_FF)
            wd = expert_w[2].reshape(N_EXPERTS, D_FF, D_MODEL)
            assert last and tm == MOE_TB, "the expert FFN is only built for the final layer's latent tokens"
            h = _ffn_moe(res[1], res[2], res[3], wg, wu, wd, res[0], mods, gfpost, expert_w[3],
                         tiles_per_batch=n_lat // MOE_TB)
    return h.reshape(bsz, n_lat, D_MODEL)
```

```python
import functools

import numpy as np
import jax
import jax.numpy as jnp
from jax import lax
from jax.experimental import pallas as pl
from jax.experimental.pallas import tpu as pltpu

D_MODEL = 1024
DEPTH = 2
GRID_W = 64
HEAD_DIM = 64
EPS = 1e-6
A_HEADS = 4
A_WIDTH = 256
CHUNK = 128
B_WIDTH = 256
C_Q_HEADS = 8
C_WIDTH = 512
KV_WIDTH = 128
WINDOW = 128
BLOCK = 128
ROPE_BASE = 10000.0
OFF_B = 512
OFF_Q = 768
OFF_K = 1280
OFF_V = 1408
N_IN = 1536
D_FF = 3584
N_EXPERTS = 8

LANES = 128
VMEM_LIMIT = 60 * 1024 * 1024
NEG = -1e30
LOG2E = 1.4426950408889634
assert WINDOW == BLOCK

F32 = jnp.float32
BF16 = jnp.bfloat16


def _dot(a, b):
    return jnp.dot(a, b, preferred_element_type=F32)


def _cparams(sem):
    return pltpu.CompilerParams(dimension_semantics=sem, vmem_limit_bytes=VMEM_LIMIT)


def _adaln_kernel(c_ref, w_ref, b_ref, o_ref):
    c = c_ref[...]
    s = (c * jax.nn.sigmoid(c)).astype(BF16)
    o_ref[...] = _dot(s, w_ref[...].astype(BF16)) + b_ref[...]


def _adaln(cond8, w, b):
    tn = 1536
    nl = w.shape[0]
    return pl.pallas_call(
        _adaln_kernel,
        grid=(nl, 6 * D_MODEL // tn),
        in_specs=[pl.BlockSpec((8, D_MODEL), lambda l, j: (0, 0)),
                  pl.BlockSpec((None, D_MODEL, tn), lambda l, j: (l, 0, j)),
                  pl.BlockSpec((None, 1, tn), lambda l, j: (l, 0, j))],
        out_specs=pl.BlockSpec((None, 8, tn), lambda l, j: (l, 0, j)),
        out_shape=jax.ShapeDtypeStruct((nl, 8, 6 * D_MODEL), F32),
        compiler_params=_cparams(("arbitrary", "arbitrary")),
    )(cond8, w, b)


def _gelu_tanh(x):
    return 0.5 * x * (1.0 + jnp.tanh(0.7978845608028654 * (x + 0.044715 * x * x * x)))


PRE_SUB = 512


def _premix_kernel(*refs, rope):
    if rope:
        (h_ref, sh_ref, sc_ref, g_ref, win_ref, ws_ref, bs_ref, gv_ref, hm_ref, wdft_ref,
         cos_ref, s1_ref, s2_ref, sgu_ref, re_ref, im_ref, q_ref, k_ref, v_ref, fsc_ref) = refs
    else:
        (h_ref, sh_ref, sc_ref, g_ref, win_ref, ws_ref, bs_ref, gv_ref, hm_ref, wdft_ref,
         sgu_ref, re_ref, im_ref, q_ref, k_ref, v_ref) = refs
    tm = h_ref.shape[0]
    nl = 2 * B_WIDTH // LANES
    head = lax.broadcasted_iota(jnp.int32, (CHUNK, A_WIDTH), 1) // HEAD_DIM
    lo = lax.broadcasted_iota(jnp.int32, (PRE_SUB, LANES), 1) < HEAD_DIM
    for sb in range(tm // PRE_SUB):
        rs = slice(sb * PRE_SUB, (sb + 1) * PRE_SUB)
        x = h_ref[rs, :]
        ms = jnp.mean(x * x, axis=-1, keepdims=True)
        xn = x * lax.rsqrt(ms + EPS) * g_ref[...]
        xm = (xn * (1.0 + sc_ref[...]) + sh_ref[...]).astype(BF16)
        z = _dot(xm, win_ref[...])
        zcols = lambda c0, c1: z[:, c0:c1]

        a = _gelu_tanh(zcols(0, OFF_B))
        u = a[:, :A_WIDTH]
        v = a[:, A_WIDTH:]
        msv = _dot((v * v).astype(BF16), hm_ref[...])
        vn = (v * lax.rsqrt(msv + EPS) * gv_ref[...]).astype(BF16)
        for ck in range(PRE_SUB // CHUNK):
            rows = slice(ck * CHUNK, (ck + 1) * CHUNK)
            vc = vn[rows]
            vstack = jnp.concatenate([jnp.where(head == hh, vc, jnp.zeros_like(vc)) for hh in range(A_HEADS)], axis=0)
            sv = bs_ref[...] + _dot(ws_ref[...], vstack)
            sgu_ref[sb * PRE_SUB + ck * CHUNK:sb * PRE_SUB + (ck + 1) * CHUNK, :] = (u[rows] * sv).astype(BF16)

        f = _dot(zcols(OFF_B, OFF_Q).astype(BF16), wdft_ref[...])
        if rope:
            for j in range(nl):
                fsc_ref[j, rs, :] = f[:, j * LANES:(j + 1) * LANES]
        else:
            re_ref[rs, :] = f[:, :B_WIDTH].astype(BF16)
            im_ref[rs, :] = f[:, B_WIDTH:].astype(BF16)

        def rot(t):
            if not rope:
                return t
            return (t * cos_ref[rs, :] + pltpu.roll(t, LANES - 16, 1) * s1_ref[rs, :]
                    + pltpu.roll(t, 16, 1) * s2_ref[rs, :])

        zq = zcols(OFF_Q, OFF_K)
        for j in range(C_WIDTH // LANES):
            q_ref[rs, j * LANES:(j + 1) * LANES] = (
                rot(zq[:, j * LANES:(j + 1) * LANES]) * (HEAD_DIM ** -0.5 * LOG2E)).astype(BF16)

        zkv = zcols(OFF_K, N_IN)
        kk = rot(zkv[:, :KV_WIDTH])
        kr = pltpu.roll(kk, HEAD_DIM, 1)
        k_ref[rs, :LANES] = jnp.where(lo, kk, kr).astype(BF16)
        k_ref[rs, LANES:] = jnp.where(lo, kr, kk).astype(BF16)
        vv = zkv[:, KV_WIDTH:]
        vr = pltpu.roll(vv, HEAD_DIM, 1)
        v_ref[rs, :LANES] = jnp.where(lo, vv, vr).astype(BF16)
        v_ref[rs, LANES:] = jnp.where(lo, vr, vv).astype(BF16)

    if rope:
        nr = tm // GRID_W
        for cc in range(GRID_W):
            for j in range(nl):
                blk = fsc_ref[j, pl.ds(cc, nr, stride=GRID_W), :].astype(BF16)
                dst = re_ref if j < nl // 2 else im_ref
                c0 = cc * B_WIDTH + (j % (nl // 2)) * LANES
                dst[:, c0:c0 + LANES] = blk


def _premix(h, mods, g, win, ws, bsx, gv, hm, wdft, rope_tabs, *, tm, tiles_per_batch, n_pos):
    T = h.shape[0]
    rope = rope_tabs is not None
    if tiles_per_batch is None:
        mrow = lambda i: 4
    else:
        mrow = lambda i: i // tiles_per_batch
    const2 = lambda i: (0, 0)
    in_specs = [
        pl.BlockSpec((tm, D_MODEL), lambda i: (i, 0)),
        pl.BlockSpec((None, None, 1, D_MODEL), lambda i: (mrow(i), 0, 0, 0)),
        pl.BlockSpec((None, None, 1, D_MODEL), lambda i: (mrow(i), 1, 0, 0)),
        pl.BlockSpec((1, D_MODEL), const2),
        pl.BlockSpec((D_MODEL, N_IN), const2),
        pl.BlockSpec((CHUNK, A_HEADS * CHUNK), const2),
        pl.BlockSpec((CHUNK, A_WIDTH), const2),
        pl.BlockSpec((1, A_WIDTH), const2),
        pl.BlockSpec((A_WIDTH, A_WIDTH), const2),
        pl.BlockSpec((B_WIDTH, 2 * B_WIDTH), const2),
    ]
    args = [h, mods, mods, g, win, ws, bsx, gv, hm, wdft]
    if rope:
        nt = n_pos // tm
        for t in rope_tabs:
            in_specs.append(pl.BlockSpec((tm, LANES), lambda i: (i % nt, 0)))
            args.append(t)
    widths = (A_WIDTH, B_WIDTH, B_WIDTH, C_WIDTH, 2 * KV_WIDTH, 2 * KV_WIDTH)
    out_specs = [pl.BlockSpec((tm, w), lambda i: (i, 0)) for w in widths]
    out_shape = [jax.ShapeDtypeStruct((T, w), BF16) for w in widths]
    scratch = []
    if rope:
        for o in (1, 2):
            out_specs[o] = pl.BlockSpec((tm // GRID_W, GRID_W * B_WIDTH), lambda i: (i, 0))
            out_shape[o] = jax.ShapeDtypeStruct((T // GRID_W, GRID_W * B_WIDTH), BF16)
        scratch = [pltpu.VMEM((2 * B_WIDTH // LANES, tm, LANES), F32)]
    return pl.pallas_call(
        functools.partial(_premix_kernel, rope=rope),
        grid=(T // tm,),
        in_specs=in_specs, out_specs=out_specs, out_shape=out_shape, scratch_shapes=scratch,
        compiler_params=_cparams(("parallel",)),
    )(*args)


def _dft1_kernel(m1_ref, twc_ref, tws_ref, re_ref, im_ref, o_ref):
    x = jnp.concatenate([re_ref[0], im_ref[0]], axis=0)
    a = _dot(m1_ref[...], x)
    nr = a.shape[0] // 2
    are, aim = a[:nr], a[nr:]
    c, s = twc_ref[...], tws_ref[...]
    bre = (are * c + aim * s).astype(BF16)
    bim = (aim * c - are * s).astype(BF16)
    for cl in range(o_ref.shape[2]):
        o_ref[0, 0, cl] = bre[:, cl * B_WIDTH:(cl + 1) * B_WIDTH]
        o_ref[0, 1, cl] = bim[:, cl * B_WIDTH:(cl + 1) * B_WIDTH]


def _dft3_kernel(m3_ref, x_ref, o_ref, xs_ref, os_ref):
    kb = x_ref.shape[3]
    nl = B_WIDTH // LANES
    for p in range(2):
        for c in range(GRID_W):
            xc = x_ref[0, p, c].astype(F32)
            for j in range(nl):
                xs_ref[p * nl + j, c * kb:(c + 1) * kb, :] = xc[:, j * LANES:(j + 1) * LANES]
    for k in range(kb):
        rows = [jnp.concatenate([xs_ref[p * nl + j, pl.ds(k, GRID_W, stride=kb), :] for j in range(nl)], axis=1)
                for p in range(2)]
        res = _dot(m3_ref[...], jnp.concatenate(rows, axis=0).astype(BF16))
        for j in range(nl):
            os_ref[j, pl.ds(k, GRID_W, stride=kb), :] = res[:, j * LANES:(j + 1) * LANES]
    for k2 in range(GRID_W):
        o_ref[0, k2] = jnp.concatenate(
            [os_ref[j, k2 * kb:(k2 + 1) * kb, :] for j in range(nl)], axis=1).astype(BF16)


def _seq_dft(re, im, bsz, n_rows, m1, twc, tws, m3):
    ncol = GRID_W * B_WIDTH
    tn = 4096
    re3 = re.reshape(bsz, n_rows, ncol)
    im3 = im.reshape(bsz, n_rows, ncol)
    st1 = pl.pallas_call(
        _dft1_kernel,
        grid=(ncol // tn, bsz),
        in_specs=[pl.BlockSpec((2 * n_rows, 2 * n_rows), lambda j, b: (0, 0)),
                  pl.BlockSpec((n_rows, tn), lambda j, b: (0, j)),
                  pl.BlockSpec((n_rows, tn), lambda j, b: (0, j)),
                  pl.BlockSpec((1, n_rows, tn), lambda j, b: (b, 0, j)),
                  pl.BlockSpec((1, n_rows, tn), lambda j, b: (b, 0, j))],
        out_specs=pl.BlockSpec((1, 2, tn // B_WIDTH, n_rows, B_WIDTH), lambda j, b: (b, 0, j, 0, 0)),
        out_shape=jax.ShapeDtypeStruct((bsz, 2, GRID_W, n_rows, B_WIDTH), BF16),
        compiler_params=_cparams(("parallel", "parallel")),
    )(m1, twc, tws, re3, im3)
    kb = min(n_rows, 16)
    y = pl.pallas_call(
        _dft3_kernel,
        grid=(bsz, n_rows // kb),
        in_specs=[pl.BlockSpec((GRID_W, 2 * GRID_W), lambda b, j: (0, 0)),
                  pl.BlockSpec((1, 2, GRID_W, kb, B_WIDTH), lambda b, j: (b, 0, 0, j, 0))],
        out_specs=pl.BlockSpec((1, GRID_W, kb, B_WIDTH), lambda b, j: (b, 0, j, 0)),
        out_shape=jax.ShapeDtypeStruct((bsz, GRID_W, n_rows, B_WIDTH), BF16),
        scratch_shapes=[pltpu.VMEM((2 * B_WIDTH // LANES, GRID_W * kb, LANES), F32),
                        pltpu.VMEM((B_WIDTH // LANES, GRID_W * kb, LANES), F32)],
        compiler_params=_cparams(("parallel", "parallel")),
    )(m3, st1)
    return y.reshape(bsz * GRID_W * n_rows, B_WIDTH)


def _ctx_dft_kernel(m_ref, re_ref, im_ref, o_ref):
    x = jnp.concatenate([re_ref[0], im_ref[0]], axis=0)
    o_ref[0] = _dot(m_ref[...], x).astype(BF16)


def _ctx_dft(re, im, bsz, m, mat):
    re3 = re.reshape(bsz, m, B_WIDTH)
    im3 = im.reshape(bsz, m, B_WIDTH)
    y = pl.pallas_call(
        _ctx_dft_kernel,
        grid=(bsz,),
        in_specs=[pl.BlockSpec((m, 2 * m), lambda b: (0, 0)),
                  pl.BlockSpec((1, m, B_WIDTH), lambda b: (b, 0, 0)),
                  pl.BlockSpec((1, m, B_WIDTH), lambda b: (b, 0, 0))],
        out_specs=pl.BlockSpec((1, m, B_WIDTH), lambda b: (b, 0, 0)),
        out_shape=jax.ShapeDtypeStruct((bsz, m, B_WIDTH), BF16),
        compiler_params=_cparams(("parallel",)),
    )(mat, re3, im3)
    return y.reshape(bsz * m, B_WIDTH)


ATTN_SUBBLOCKS = 8


def _attn_kernel(sink_ref, *refs, local, nb):
    if local:
        q_ref, kp_ref, kc_ref, kn_ref, vp_ref, vc_ref, vn_ref, kx_ref, vx_ref, o_ref = refs
    else:
        q_ref, kx_ref, vx_ref, o_ref = refs
    tq = BLOCK if local else q_ref.shape[1]
    nsb = q_ref.shape[1] // tq
    lo = lax.broadcasted_iota(jnp.int32, (tq, LANES), 1) < HEAD_DIM
    zero = jnp.zeros((tq, LANES), BF16)
    lo4 = lax.broadcasted_iota(jnp.int32, (4 * tq, LANES), 1) < HEAD_DIM
    if local:
        j = pl.program_id(1)
        row = lax.broadcasted_iota(jnp.int32, (BLOCK, BLOCK), 0)
        col = lax.broadcasted_iota(jnp.int32, (BLOCK, BLOCK), 1)

        def key_blocks(p_ref, c_ref, n_ref, gl):
            return ([p_ref[0][:, gl]] + [c_ref[0][sb * BLOCK:(sb + 1) * BLOCK, gl] for sb in range(nsb)]
                    + [n_ref[0][:, gl]])
    jobs = []
    for g in range(2):
        gl = slice(g * LANES, (g + 1) * LANES)
        if local:
            kb = key_blocks(kp_ref, kc_ref, kn_ref, gl)
            vb = key_blocks(vp_ref, vc_ref, vn_ref, gl)
        for sb in range(nsb):
            if local:
                kg = jnp.concatenate(kb[sb:sb + 3] + [kx_ref[0][:, gl]], axis=0)
                vg = jnp.concatenate(vb[sb:sb + 3] + [vx_ref[0][:, gl]], axis=0)
            else:
                kg = kx_ref[0][:, gl]
                vg = vx_ref[0][:, gl]
            q = q_ref[0, sb * tq:(sb + 1) * tq, :]
            qs = []
            for p in range(2):
                qp = q[:, g * 2 * LANES + p * LANES: g * 2 * LANES + (p + 1) * LANES]
                qs.append(jnp.where(lo, qp, zero))
                qs.append(jnp.where(lo, zero, qp))
            q4 = jnp.concatenate(qs, axis=0)
            s4 = lax.dot_general(q4, kg, (((1,), (1,)), ((), ())), preferred_element_type=F32)
            jobs.append((g, sb, s4, vg))
    for g, sb, s4, vg in jobs:
        if local:
            valid_prev = (col >= row) & (j > 0) if sb == 0 else (col >= row)
            valid_next = (col <= row) & (j < nb // nsb - 1) if sb == nsb - 1 else (col <= row)
        ps, sinks = [], []
        for hl in range(4):
            s = s4[hl * tq:(hl + 1) * tq]
            if local:
                s = jnp.concatenate(
                    [jnp.where(valid_prev, s[:, :BLOCK], NEG), s[:, BLOCK:2 * BLOCK],
                     jnp.where(valid_next, s[:, 2 * BLOCK:3 * BLOCK], NEG), s[:, 3 * BLOCK:]], axis=1)
            sk = sink_ref[g * 4 + hl] * LOG2E
            m = jnp.maximum(jnp.max(s, axis=-1, keepdims=True), sk)
            ps.append(jnp.exp2(s - m).astype(BF16))
            sinks.append(jnp.exp2(sk - m))
        lov = lax.broadcasted_iota(jnp.int32, vg.shape, 1) < HEAD_DIM
        ve = jnp.where(lov, vg, jnp.ones_like(vg))
        o4 = _dot(jnp.concatenate(ps, axis=0), ve)
        o4 = o4 + jnp.where(lo4, 0.0, jnp.concatenate(sinks, axis=0))
        r4 = pltpu.roll(o4, HEAD_DIM, 1)
        for p in range(2):
            ev = slice((2 * p) * tq, (2 * p + 1) * tq)
            od = slice((2 * p + 1) * tq, (2 * p + 2) * tq)
            o_pair = jnp.where(lo, o4[ev] / r4[ev], r4[od] / o4[od])
            c0 = g * 2 * LANES + p * LANES
            o_ref[0, sb * tq:(sb + 1) * tq, c0:c0 + LANES] = o_pair.astype(BF16)


def _window_attention(q, k, v, kx, vx, sink, bsz, n, m):
    nb = n // BLOCK
    q3 = q.reshape(bsz, n, C_WIDTH)
    k3 = k.reshape(bsz, n, 2 * KV_WIDTH)
    v3 = v.reshape(bsz, n, 2 * KV_WIDTH)
    kx3 = kx.reshape(bsz, m, 2 * KV_WIDTH)
    vx3 = vx.reshape(bsz, m, 2 * KV_WIDTH)
    nsb = ATTN_SUBBLOCKS
    edge = lambda f: pl.BlockSpec((1, BLOCK, 2 * KV_WIDTH), f)
    own = pl.BlockSpec((1, nsb * BLOCK, 2 * KV_WIDTH), lambda b, i, s: (b, i, 0))
    prev = lambda b, i, s: (b, jnp.maximum(i * nsb - 1, 0), 0)
    cur = lambda b, i, s: (b, i, 0)
    nxt = lambda b, i, s: (b, jnp.minimum((i + 1) * nsb, nb - 1), 0)
    ctxs = pl.BlockSpec((1, m, 2 * KV_WIDTH), lambda b, i, s: (b, 0, 0))
    o = pl.pallas_call(
        functools.partial(_attn_kernel, local=True, nb=nb),
        grid_spec=pltpu.PrefetchScalarGridSpec(
            num_scalar_prefetch=1,
            grid=(bsz, nb // nsb),
            in_specs=[pl.BlockSpec((1, nsb * BLOCK, C_WIDTH), cur),
                      edge(prev), own, edge(nxt), edge(prev), own, edge(nxt), ctxs, ctxs],
            out_specs=pl.BlockSpec((1, nsb * BLOCK, C_WIDTH), cur)),
        out_shape=jax.ShapeDtypeStruct((bsz, n, C_WIDTH), BF16),
        compiler_params=_cparams(("parallel", "parallel")),
    )(sink, q3, k3, k3, k3, v3, v3, v3, kx3, vx3)
    return o.reshape(bsz * n, C_WIDTH)


def _context_attention(q, kx, vx, sink, bsz, m):
    q3 = q.reshape(bsz, m, C_WIDTH)
    kx3 = kx.reshape(bsz, m, 2 * KV_WIDTH)
    vx3 = vx.reshape(bsz, m, 2 * KV_WIDTH)
    ctxs = pl.BlockSpec((1, m, 2 * KV_WIDTH), lambda b, s: (b, 0, 0))
    o = pl.pallas_call(
        functools.partial(_attn_kernel, local=False, nb=1),
        grid_spec=pltpu.PrefetchScalarGridSpec(
            num_scalar_prefetch=1,
            grid=(bsz,),
            in_specs=[pl.BlockSpec((1, m, C_WIDTH), lambda b, s: (b, 0, 0)), ctxs, ctxs],
            out_specs=pl.BlockSpec((1, m, C_WIDTH), lambda b, s: (b, 0, 0))),
        out_shape=jax.ShapeDtypeStruct((bsz, m, C_WIDTH), BF16),
        compiler_params=_cparams(("parallel",)),
    )(sink, q3, kx3, vx3)
    return o.reshape(bsz * m, C_WIDTH)


def _postmix_kernel(*refs, route, ncast):
    if ncast:
        n_in = 11 + (2 if route else 0)
        for src, dst in zip(refs[n_in:n_in + ncast], refs[len(refs) - ncast:]):
            dst[...] = src[...].astype(BF16)
        refs = refs[:n_in] + refs[n_in + ncast:len(refs) - ncast]
    h_ref = refs[5]
    for sb in range(h_ref.shape[0] // MOE_TB):
        _postmix_rows(slice(sb * MOE_TB, (sb + 1) * MOE_TB), sb, refs, route)


def _postmix_rows(rows, sb, refs, route):
    (sgu_ref, yf_ref, at_ref, wf_ref, wo_ref, h_ref, gp_ref, gt_ref, gf_ref, sh_ref, sc_ref) = refs[:11]
    if route:
        wr_ref, br_ref, hn_ref, y_ref, cmb_ref, cnt_ref = refs[11:]
    else:
        hn_ref, y_ref = refs[11:]
    fm = _dot(yf_ref[rows, :], wf_ref[...]).astype(BF16)
    o = (_dot(sgu_ref[rows, :], wo_ref[:A_WIDTH, :])
         + _dot(fm, wo_ref[A_WIDTH:A_WIDTH + B_WIDTH, :])
         + _dot(at_ref[rows, :], wo_ref[A_WIDTH + B_WIDTH:, :]))
    ms = jnp.mean(o * o, axis=-1, keepdims=True)
    hn = h_ref[rows, :] + gt_ref[...] * (o * lax.rsqrt(ms + EPS) * gp_ref[...])
    hn_ref[rows, :] = hn
    ms2 = jnp.mean(hn * hn, axis=-1, keepdims=True)
    y = hn * lax.rsqrt(ms2 + EPS) * gf_ref[...] * (1.0 + sc_ref[...]) + sh_ref[...]
    y_ref[rows, :] = y.astype(BF16)
    if route:
        yh = y.astype(BF16)
        yl = (y - yh.astype(F32)).astype(BF16)
        w = wr_ref[...]
        wh = w.astype(BF16)
        wl = (w - wh.astype(F32)).astype(BF16)
        n = yh.shape[0]
        r = _dot(jnp.concatenate([yh, yl], axis=0), jnp.concatenate([wh, wl], axis=1))
        lg = (r[:n, :LANES] + r[n:, :LANES]) + (r[:n, LANES:] + r[n:, LANES:]) + br_ref[...]
        lane = lax.broadcasted_iota(jnp.int32, lg.shape, 1)
        lg = jnp.where(lane < N_EXPERTS, lg, NEG)
        m1 = jnp.max(lg, axis=-1, keepdims=True)
        i1 = jnp.min(jnp.where(lg == m1, lane, LANES), axis=-1, keepdims=True)
        lg2 = jnp.where(lane == i1, NEG, lg)
        m2 = jnp.max(lg2, axis=-1, keepdims=True)
        i2 = jnp.min(jnp.where(lg2 == m2, lane, LANES), axis=-1, keepdims=True)
        e2 = jnp.exp(m2 - m1)
        g1 = 1.0 / (1.0 + e2)
        g2 = e2 * g1
        cmb = jnp.where(lane == i1, g1, 0.0) + jnp.where(lane == i2, g2, 0.0)
        cmb_ref[rows, :] = cmb
        cnt_ref[sb] = jnp.sum(jnp.where(cmb > 0.0, 1.0, 0.0), axis=0, keepdims=True)


def _postmix(sgu, yf, at, wf, wo, h, mods, gp, gf, router, *, tm, tiles_per_batch, side_cast=()):
    T = h.shape[0]
    if tiles_per_batch is None:
        mrow = lambda i: 4
    else:
        mrow = lambda i: i // tiles_per_batch
    const2 = lambda i: (0, 0)
    row = lambda w: pl.BlockSpec((tm, w), lambda i: (i, 0))
    mod = lambda j: pl.BlockSpec((None, None, 1, D_MODEL), lambda i: (mrow(i), j, 0, 0))
    in_specs = [row(A_WIDTH), row(B_WIDTH), row(C_WIDTH),
                pl.BlockSpec((B_WIDTH, B_WIDTH), const2),
                pl.BlockSpec((D_MODEL, D_MODEL), const2),
                row(D_MODEL),
                pl.BlockSpec((1, D_MODEL), const2),
                mod(2),
                pl.BlockSpec((1, D_MODEL), const2),
                mod(3), mod(4)]
    args = [sgu, yf, at, wf, wo, h, gp, mods, gf, mods, mods]
    out_specs = [row(D_MODEL), row(D_MODEL)]
    out_shape = [jax.ShapeDtypeStruct((T, D_MODEL), F32), jax.ShapeDtypeStruct((T, D_MODEL), BF16)]
    route = router is not None
    if route:
        in_specs += [pl.BlockSpec((D_MODEL, LANES), const2), pl.BlockSpec((1, LANES), const2)]
        args += list(router)
        out_specs += [row(LANES), pl.BlockSpec((tm // MOE_TB, 1, LANES), lambda i: (i, 0, 0))]
        out_shape += [jax.ShapeDtypeStruct((T, LANES), F32), jax.ShapeDtypeStruct((T // MOE_TB, 1, LANES), F32)]
    for a in side_cast:
        spec = pl.BlockSpec((a.shape[0] // (T // tm), a.shape[1]), lambda i: (i, 0))
        in_specs.append(spec)
        args.append(a)
        out_specs.append(spec)
        out_shape.append(jax.ShapeDtypeStruct(a.shape, BF16))
    return pl.pallas_call(
        functools.partial(_postmix_kernel, route=route, ncast=len(side_cast)),
        grid=(T // tm,),
        in_specs=in_specs, out_specs=out_specs, out_shape=out_shape,
        compiler_params=_cparams(("parallel",)),
    )(*args)


def _ffn_epilogue(f, h_ref, gp_ref, gt_ref, o_ref):
    ms = jnp.mean(f * f, axis=-1, keepdims=True)
    o_ref[...] = h_ref[...] + gt_ref[...] * (f * lax.rsqrt(ms + EPS) * gp_ref[...])


FF_CHUNK = 256


def _swiglu_tile(y, wg_ref, wu_ref, wd_ref, act_ref):
    for c in range(D_FF // FF_CHUNK):
        cols = slice(c * FF_CHUNK, (c + 1) * FF_CHUNK)
        gate = _dot(y, wg_ref[:, cols])
        up = _dot(y, wu_ref[:, cols])
        act_ref[:, cols] = (gate * jax.nn.sigmoid(gate) * up).astype(BF16)
    return _dot(act_ref[...], wd_ref[...])


def _ffn_dense_kernel(y_ref, wg_ref, wu_ref, wd_ref, h_ref, gp_ref, gt_ref, o_ref, act_ref):
    f = _swiglu_tile(y_ref[...], wg_ref, wu_ref, wd_ref, act_ref)
    _ffn_epilogue(f, h_ref, gp_ref, gt_ref, o_ref)


def _ffn_dense_cast_kernel(y_ref, wg_ref, wu_ref, wd_ref, h_ref, gp_ref, gt_ref, c0_ref, c1_ref, c2_ref,
                           o_ref, d0_ref, d1_ref, d2_ref, z_ref, act_ref):
    d0_ref[...] = c0_ref[...].astype(BF16)
    d1_ref[...] = c1_ref[...].astype(BF16)
    d2_ref[...] = c2_ref[...].astype(BF16)
    z_ref[...] = jnp.zeros_like(z_ref)
    f = _swiglu_tile(y_ref[...], wg_ref, wu_ref, wd_ref, act_ref)
    _ffn_epilogue(f, h_ref, gp_ref, gt_ref, o_ref)


def _ffn_dense(y, wg, wu, wd, h, mods, gp, *, tm, tiles_per_batch, side_cast=(), zero_rows=0):
    T = h.shape[0]
    nt = T // tm
    if tiles_per_batch is None:
        mrow = lambda i: 4
    else:
        mrow = lambda i: i // tiles_per_batch
    resident = pl.Buffered(1)
    in_specs = [pl.BlockSpec((tm, D_MODEL), lambda i: (i, 0)),
                pl.BlockSpec((D_MODEL, D_FF), lambda i: (0, 0), pipeline_mode=resident),
                pl.BlockSpec((D_MODEL, D_FF), lambda i: (0, 0), pipeline_mode=resident),
                pl.BlockSpec((D_FF, D_MODEL), lambda i: (0, 0), pipeline_mode=resident),
                pl.BlockSpec((tm, D_MODEL), lambda i: (i, 0)),
                pl.BlockSpec((1, D_MODEL), lambda i: (0, 0)),
                pl.BlockSpec((None, None, 1, D_MODEL), lambda i: (mrow(i), 5, 0, 0))]
    out_specs = [pl.BlockSpec((tm, D_MODEL), lambda i: (i, 0))]
    out_shape = [jax.ShapeDtypeStruct((T, D_MODEL), F32)]
    for a in side_cast:
        spec = pl.BlockSpec((a.shape[0] // nt, a.shape[1]), lambda i: (i, 0))
        in_specs.append(spec)
        out_specs.append(spec)
        out_shape.append(jax.ShapeDtypeStruct(a.shape, BF16))
    if side_cast:
        out_specs.append(pl.BlockSpec((zero_rows // nt, D_MODEL), lambda i: (i, 0)))
        out_shape.append(jax.ShapeDtypeStruct((zero_rows, D_MODEL), BF16))
    res = pl.pallas_call(
        _ffn_dense_cast_kernel if side_cast else _ffn_dense_kernel,
        grid=(nt,),
        in_specs=in_specs, out_specs=out_specs, out_shape=out_shape,
        scratch_shapes=[pltpu.VMEM((tm, D_FF), BF16)],
        compiler_params=_cparams(("parallel",)),
    )(y, wg, wu, wd, h, gp, mods, *side_cast)
    return res[0], tuple(res[1:])


MOE_TB = 512
MOE_TM = 512
SUB = 64
NSUB = MOE_TB // SUB
RUN_ALIGN = 16


def _moe_tile_bound(T):
    nt = T // MOE_TB
    rows = 2 * T + nt * N_EXPERTS * (RUN_ALIGN - 1) + N_EXPERTS * (SUB + MOE_TM - 1)
    return -(-rows // (2 * MOE_TM)) * 2


def _moe_plan(cnt, T):
    p = (cnt + RUN_ALIGN - 1) // RUN_ALIGN * RUN_ALIGN
    base = jnp.cumsum(p, axis=0) - p
    used = jnp.sum(p, axis=0)
    tiles = (used + SUB + MOE_TM - 1) // MOE_TM
    tend = jnp.cumsum(tiles)
    off = (tend - tiles) * MOE_TM
    rowbase = (off[None, :] + base).astype(jnp.int32)
    nsub = (cnt + SUB - 1) // SUB
    slotbase = (jnp.cumsum(nsub, axis=1) - nsub).astype(jnp.int32)
    nt_bound = _moe_tile_bound(T)
    j = jnp.arange(nt_bound, dtype=jnp.int32)
    tile_e = jnp.minimum(jnp.sum(j[:, None] >= tend[None, :], axis=1), N_EXPERTS - 1).astype(jnp.int32)
    nvalid = tend[-1].astype(jnp.int32)
    tile_blk = jnp.minimum(j, nvalid - 1)
    onehot = tile_e[:, None] == jnp.arange(N_EXPERTS, dtype=jnp.int32)[None, :]
    first = jnp.sum(jnp.where(onehot, (tend - tiles)[None, :], 0), axis=1)
    used_t = jnp.sum(jnp.where(onehot, used[None, :], 0), axis=1)
    tile_live = (((j - first) * MOE_TM < used_t) & (j < nvalid)).astype(jnp.int32)
    return rowbase, slotbase, tile_e, tile_blk, tile_live


NSLOT = 2 * MOE_TB // SUB + N_EXPERTS
SLOT_ROWS = NSLOT * SUB
SLOT_CHUNK = 512


def _block_slots(sb_ref, cn_ref, b):
    last = b * N_EXPERTS + N_EXPERTS - 1
    return sb_ref[last] + (cn_ref[last] + SUB - 1) // SUB


def _for_each_run_slot(cn_ref, b, fn):
    for e in range(N_EXPERTS):
        for s in range(NSUB):
            @pl.when(cn_ref[b * N_EXPERTS + e] > SUB * s)
            def _(e=e, s=s):
                fn(e, s)


def _run_copy_out(stage, ys_ref, sems, rb_ref, sb_ref, b, par, e, s):
    slot = sb_ref[b * N_EXPERTS + e] + s
    r0 = pl.multiple_of(rb_ref[b * N_EXPERTS + e] + SUB * s, RUN_ALIGN)
    return pltpu.make_async_copy(stage.at[par, pl.ds(pl.multiple_of(slot * SUB, SUB), SUB), :],
                                 ys_ref.at[pl.ds(r0, SUB), :], sems.at[par, e, s])


def _dispatch_kernel(rb_ref, sb_ref, cn_ref, y_ref, cmb_ref, sbv_ref, ltri_ref, ysin_ref,
                     tok_ref, ys_ref, stage, sems):
    del ysin_ref
    b = pl.program_id(0)
    nb = pl.num_programs(0)
    par = b % 2
    cmb = cmb_ref[...]
    sel = cmb > 0.0
    rk = _dot(ltri_ref[...], jnp.where(sel, 1.0, 0.0).astype(BF16))
    srow = jnp.where(sel, rk + sbv_ref[...], -1.0)
    nsel = jnp.sum(jnp.where(sel, 1.0, 0.0), axis=-1, keepdims=True)
    sa = jnp.max(srow, axis=-1, keepdims=True)
    sb = jnp.where(nsel > 1.5, jnp.sum(jnp.where(sel, srow, 0.0), axis=-1, keepdims=True) - sa, -1.0)
    ga = jnp.sum(jnp.where(srow == sa, cmb, 0.0), axis=-1, keepdims=True)
    gb = jnp.sum(cmb, axis=-1, keepdims=True) - ga
    lane = lax.broadcasted_iota(jnp.int32, cmb.shape, 1)
    tok = jnp.where(lane == 0, sa, jnp.where(lane == 1, sb, jnp.where(lane == 2, ga, jnp.where(lane == 3, gb, 0.0))))
    tok_ref[...] = tok
    tok_t = tok.T
    nslots = _block_slots(sb_ref, cn_ref, b)
    rr0 = lax.broadcasted_iota(jnp.int32, (SLOT_CHUNK, cmb.shape[0]), 0).astype(F32)
    for c in range(SLOT_ROWS // SLOT_CHUNK):
        @pl.when(nslots * SUB > c * SLOT_CHUNK)
        def _(c=c):
            rr = rr0 + float(c * SLOT_CHUNK)
            g = jnp.where((rr == tok_t[0:1, :]) | (rr == tok_t[1:2, :]), 1.0, 0.0).astype(BF16)
            stage[par, c * SLOT_CHUNK:(c + 1) * SLOT_CHUNK, :] = _dot(g, y_ref[...]).astype(BF16)

    @pl.when(b > 0)
    def _():
        _for_each_run_slot(cn_ref, b - 1, lambda e, s: _run_copy_out(
            stage, ys_ref, sems, rb_ref, sb_ref, b - 1, 1 - par, e, s).wait())

    _for_each_run_slot(cn_ref, b, lambda e, s: _run_copy_out(
        stage, ys_ref, sems, rb_ref, sb_ref, b, par, e, s).start())

    @pl.when(b == nb - 1)
    def _():
        _for_each_run_slot(cn_ref, b, lambda e, s: _run_copy_out(
            stage, ys_ref, sems, rb_ref, sb_ref, b, par, e, s).wait())


def _moe_dispatch(y, cmb, rowbase, slotbase, cnt, ys0):
    T = y.shape[0]
    nt = T // MOE_TB
    nt_bound = ys0.shape[0] // MOE_TM
    sbv = jnp.zeros((nt, 1, LANES), F32).at[:, 0, :N_EXPERTS].set((slotbase * SUB).astype(F32))
    ltri = jnp.asarray(np.tril(np.ones((MOE_TB, MOE_TB), np.float32), -1)).astype(BF16)
    imap2 = lambda b, rb, sb, cn: (b, 0)
    return pl.pallas_call(
        _dispatch_kernel,
        grid_spec=pltpu.PrefetchScalarGridSpec(
            num_scalar_prefetch=3,
            grid=(nt,),
            in_specs=[pl.BlockSpec((MOE_TB, D_MODEL), imap2),
                      pl.BlockSpec((MOE_TB, LANES), imap2),
                      pl.BlockSpec((None, 1, LANES), lambda b, rb, sb, cn: (b, 0, 0)),
                      pl.BlockSpec((MOE_TB, MOE_TB), lambda b, rb, sb, cn: (0, 0)),
                      pl.BlockSpec(memory_space=pl.ANY)],
            out_specs=[pl.BlockSpec((MOE_TB, LANES), imap2),
                       pl.BlockSpec(memory_space=pl.ANY)],
            scratch_shapes=[pltpu.VMEM((2, SLOT_ROWS, D_MODEL), BF16),
                            pltpu.SemaphoreType.DMA((2, N_EXPERTS, NSUB))]),
        out_shape=[jax.ShapeDtypeStruct((T, LANES), F32),
                   jax.ShapeDtypeStruct((nt_bound * MOE_TM, D_MODEL), BF16)],
        input_output_aliases={7: 1},
        compiler_params=_cparams(("arbitrary",)),
    )(rowbase.reshape(-1), slotbase.reshape(-1), cnt.reshape(-1), y, cmb, sbv, ltri, ys0)


def _ffn_group_kernel(te_ref, tb_ref, lv_ref, y_ref, wg_ref, wu_ref, wd_ref, o_ref, act_ref):
    del te_ref, tb_ref
    live = lv_ref[pl.program_id(0)] > 0

    @pl.when(live)
    def _():
        o_ref[...] = _swiglu_tile(y_ref[...], wg_ref, wu_ref, wd_ref, act_ref).astype(BF16)

    @pl.when(jnp.logical_not(live))
    def _():
        o_ref[...] = jnp.zeros_like(o_ref)


def _ffn_group(ys, wg, wu, wd, tile_e, tile_blk, tile_live):
    rows = ys.shape[0]
    resident = pl.Buffered(2)
    return pl.pallas_call(
        _ffn_group_kernel,
        grid_spec=pltpu.PrefetchScalarGridSpec(
            num_scalar_prefetch=3,
            grid=(rows // MOE_TM,),
            in_specs=[pl.BlockSpec((MOE_TM, D_MODEL), lambda j, te, tb, nv: (tb[j], 0)),
                      pl.BlockSpec((None, D_MODEL, D_FF), lambda j, te, tb, nv: (te[j], 0, 0), pipeline_mode=resident),
                      pl.BlockSpec((None, D_MODEL, D_FF), lambda j, te, tb, nv: (te[j], 0, 0), pipeline_mode=resident),
                      pl.BlockSpec((None, D_FF, D_MODEL), lambda j, te, tb, nv: (te[j], 0, 0), pipeline_mode=resident)],
            out_specs=pl.BlockSpec((MOE_TM, D_MODEL), lambda j, te, tb, nv: (j, 0)),
            scratch_shapes=[pltpu.VMEM((MOE_TM, D_FF), BF16)]),
        out_shape=jax.ShapeDtypeStruct((rows, D_MODEL), BF16),
        compiler_params=_cparams(("arbitrary",)),
    )(tile_e, tile_blk, tile_live, ys, wg, wu, wd)


def _run_copy_in(fs_ref, fbuf, sems, rb_ref, sb_ref, b, par, e, s):
    slot = sb_ref[b * N_EXPERTS + e] + s
    r0 = pl.multiple_of(rb_ref[b * N_EXPERTS + e] + SUB * s, RUN_ALIGN)
    return pltpu.make_async_copy(fs_ref.at[pl.ds(r0, SUB), :],
                                 fbuf.at[par, pl.ds(pl.multiple_of(slot * SUB, SUB), SUB), :], sems.at[par, e, s])


def _combine_kernel(rb_ref, sb_ref, cn_ref, tok_ref, h_ref, gp_ref, gt_ref, fs_ref, o_ref, fbuf, sems):
    b = pl.program_id(0)
    nb = pl.num_programs(0)
    par = b % 2

    @pl.when(b == 0)
    def _():
        fbuf[...] = jnp.zeros_like(fbuf)
        _for_each_run_slot(cn_ref, b, lambda e, s: _run_copy_in(
            fs_ref, fbuf, sems, rb_ref, sb_ref, b, par, e, s).start())

    @pl.when(b + 1 < nb)
    def _():
        _for_each_run_slot(cn_ref, b + 1, lambda e, s: _run_copy_in(
            fs_ref, fbuf, sems, rb_ref, sb_ref, b + 1, 1 - par, e, s).start())

    tok = tok_ref[...]
    sa, sb, ga, gb = tok[:, 0:1], tok[:, 1:2], tok[:, 2:3], tok[:, 3:4]
    _for_each_run_slot(cn_ref, b, lambda e, s: _run_copy_in(
        fs_ref, fbuf, sems, rb_ref, sb_ref, b, par, e, s).wait())
    nslots = _block_slots(sb_ref, cn_ref, b)
    nck = SLOT_ROWS // SLOT_CHUNK
    for c in range(1, nck + 1):
        lo_rows, hi_rows = (c - 1) * SLOT_CHUNK, c * SLOT_CHUNK
        cond = (nslots * SUB > lo_rows) if c == nck else ((nslots * SUB > lo_rows) & (nslots * SUB <= hi_rows))
        if c == 1:
            cond = nslots * SUB <= hi_rows

        @pl.when(cond)
        def _(depth=hi_rows):
            cc = lax.broadcasted_iota(jnp.int32, (tok.shape[0], depth), 1).astype(F32)
            pick = jnp.where(cc == sa, ga, jnp.where(cc == sb, gb, 0.0)).astype(BF16)
            f = _dot(pick, fbuf[par, :depth, :])
            _ffn_epilogue(f, h_ref, gp_ref, gt_ref, o_ref)


def _moe_combine(fs, tok, h, mods, gp, rowbase, slotbase, cnt, *, tiles_per_batch):
    T = h.shape[0]
    nt = T // MOE_TB
    imap2 = lambda b, rb, sb, cn: (b, 0)
    return pl.pallas_call(
        _combine_kernel,
        grid_spec=pltpu.PrefetchScalarGridSpec(
            num_scalar_prefetch=3,
            grid=(nt,),
            in_specs=[pl.BlockSpec((MOE_TB, LANES), imap2),
                      pl.BlockSpec((MOE_TB, D_MODEL), imap2),
                      pl.BlockSpec((1, D_MODEL), lambda b, rb, sb, cn: (0, 0)),
                      pl.BlockSpec((None, None, 1, D_MODEL), lambda b, rb, sb, cn: (b // tiles_per_batch, 5, 0, 0)),
                      pl.BlockSpec(memory_space=pl.ANY)],
            out_specs=pl.BlockSpec((MOE_TB, D_MODEL), imap2),
            scratch_shapes=[pltpu.VMEM((2, SLOT_ROWS, D_MODEL), BF16),
                            pltpu.SemaphoreType.DMA((2, N_EXPERTS, NSUB))]),
        out_shape=jax.ShapeDtypeStruct((T, D_MODEL), F32),
        compiler_params=_cparams(("arbitrary",)),
    )(rowbase.reshape(-1), slotbase.reshape(-1), cnt.reshape(-1), tok, h, gp, mods, fs)


def _ffn_moe(y, cmb, cnt_tiles, wg, wu, wd, h, mods, gp, ys0, *, tiles_per_batch):
    T = h.shape[0]
    cnt = cnt_tiles.reshape(T // MOE_TB, LANES)[:, :N_EXPERTS].astype(jnp.int32)
    rowbase, slotbase, tile_e, tile_blk, tile_live = _moe_plan(cnt, T)
    tok, ys = _moe_dispatch(y, cmb, rowbase, slotbase, cnt, ys0)
    fs = _ffn_group(ys, wg, wu, wd, tile_e, tile_blk, tile_live)
    return _moe_combine(fs, tok, h, mods, gp, rowbase, slotbase, cnt, tiles_per_batch=tiles_per_batch)


def _blockdiag(m, reps):
    n = m.shape[0]
    out = np.zeros((n * reps, n * reps), np.float64)
    for r in range(reps):
        out[r * n:(r + 1) * n, r * n:(r + 1) * n] = m
    return out


def _dft_tables(n_rows):
    n = n_rows * GRID_W
    r = np.arange(n_rows)
    c = np.arange(GRID_W)
    a1 = 2 * np.pi * np.outer(r, r) / n_rows
    c1, s1 = np.cos(a1), np.sin(a1)
    m1 = np.block([[c1, s1], [-s1, c1]]) * 0.125
    at = 2 * np.pi * np.outer(r, c) / n
    twc = np.repeat(np.cos(at), B_WIDTH, axis=1)
    tws = np.repeat(np.sin(at), B_WIDTH, axis=1)
    a3 = 2 * np.pi * np.outer(c, c) / GRID_W
    m3 = np.concatenate([np.cos(a3), np.sin(a3)], axis=1) * (8.0 / np.sqrt(n))
    f32 = lambda t: jnp.asarray(t.astype(np.float32))
    return f32(m1).astype(BF16), f32(twc), f32(tws), f32(m3).astype(BF16)


def _channel_dft_table():
    d = np.arange(HEAD_DIM)
    a = 2 * np.pi * np.outer(d, d) / HEAD_DIM
    w = np.concatenate([_blockdiag(np.cos(a), 4), -_blockdiag(np.sin(a), 4)], axis=1) * 0.125
    return jnp.asarray(w.astype(np.float32)).astype(BF16)


def _ctx_dft_table(m):
    p = np.arange(m)
    a = 2 * np.pi * np.outer(p, p) / m
    w = np.concatenate([np.cos(a), np.sin(a)], axis=1) * (8.0 / np.sqrt(m * HEAD_DIM))
    return jnp.asarray(w.astype(np.float32)).astype(BF16)


def _rope_tables(n_tok):
    rows = n_tok // GRID_W
    row = jnp.broadcast_to(jnp.arange(rows)[:, None], (rows, GRID_W)).reshape(-1)
    col = jnp.broadcast_to(jnp.arange(GRID_W)[None, :], (rows, GRID_W)).reshape(-1)
    half = HEAD_DIM // 2
    inv = ROPE_BASE ** (-jnp.arange(0, half, 2, dtype=F32) / half)
    ang = jnp.stack([row.astype(F32)[:, None] * inv, col.astype(F32)[:, None] * inv], axis=1)
    cos, sin = jnp.cos(ang), jnp.sin(ang)
    zer = jnp.zeros_like(sin)
    lay = lambda a, b: jnp.tile(jnp.stack([a, b], axis=2).reshape(n_tok, HEAD_DIM), (1, LANES // HEAD_DIM))
    return lay(cos, cos), lay(-sin, zer), lay(zer, sin)


def kernel(x, c, ctx, c_ctx, w_ada, b_ada, g_mix_pre, g_mix_post, g_ffn_pre, g_ffn_post,
           w_in, w_s, b_s, g_v, w_f, sink, w_out, w_gate_d, w_up_d, w_down_d,
           w_router, b_router, w_gate_e, w_up_e, w_down_e):
    bsz, n_lat, _ = x.shape
    n_ctx = ctx.shape[1]
    T, Tc = bsz * n_lat, bsz * n_ctx
    tm = 512
    tpb = n_lat // tm

    rope_tabs = _rope_tables(n_lat)
    m1, twc, tws, m3 = _dft_tables(n_lat // GRID_W)
    wdft = _channel_dft_table()
    mctx = _ctx_dft_table(n_ctx)
    hm = jnp.asarray(_blockdiag(np.full((HEAD_DIM, HEAD_DIM), 1.0 / HEAD_DIM), A_HEADS).astype(np.float32)).astype(BF16)

    cond8 = jnp.zeros((8, D_MODEL), F32).at[:bsz].set(c).at[4].set(c_ctx)
    h = x.reshape(T, D_MODEL)
    hc = ctx.reshape(Tc, D_MODEL)

    mods_all = _adaln(cond8, w_ada, b_ada[:, None, :]).reshape(DEPTH, 8, 6, 1, D_MODEL)

    for i in range(DEPTH):
        last = i == DEPTH - 1
        mods = mods_all[i]
        win = w_in[i].astype(BF16)
        ws = jnp.transpose(w_s[i], (1, 0, 2)).reshape(CHUNK, A_HEADS * CHUNK).astype(BF16)
        bsx = jnp.repeat(b_s[i].T, HEAD_DIM, axis=1)
        gv = g_v[i].reshape(1, A_WIDTH)
        wf = jax.scipy.linalg.block_diag(*[w_f[i][g] for g in range(4)]).astype(BF16)
        wo = w_out[i].astype(BF16)
        gpre = g_mix_pre[i][None, :]
        gpost = g_mix_post[i][None, :]
        gfpre = g_ffn_pre[i][None, :]
        gfpost = g_ffn_post[i][None, :]
        sk = sink[i]

        sgu, fre, fim, q, k, v = _premix(h, mods, gpre, win, ws, bsx, gv, hm, wdft, rope_tabs,
                                         tm=2 * tm, tiles_per_batch=tpb // 2, n_pos=n_lat)
        sguc, frec, fimc, qc, kc, vc = _premix(hc, mods, gpre, win, ws, bsx, gv, hm, wdft, None,
                                               tm=tm, tiles_per_batch=None, n_pos=n_ctx)
        yf = _seq_dft(fre, fim, bsz, n_lat // GRID_W, m1, twc, tws, m3)
        at = _window_attention(q, k, v, kc, vc, sk, bsz, n_lat, n_ctx)

        if i % 2 == 0:
            router = None
        else:
            j = i // 2
            wr = jnp.zeros((D_MODEL, LANES), F32).at[:, :N_EXPERTS].set(w_router[j])
            br = jnp.zeros((1, LANES), F32).at[0, :N_EXPERTS].set(b_router[j])
            router = (wr, br)
        dense_w = (w_gate_d[i // 2], w_up_d[i // 2], w_down_d[i // 2]) if i % 2 == 0 else ()
        res = _postmix(sgu, yf, at, wf, wo, h, mods, gpost, gfpre, router, tm=2 * tm, tiles_per_batch=tpb // 2,
                       side_cast=dense_w)
        if not last:
            yfc = _ctx_dft(frec, fimc, bsz, n_ctx, mctx)
            atc = _context_attention(qc, kc, vc, sk, bsz, n_ctx)
            resc = _postmix(sguc, yfc, atc, wf, wo, hc, mods, gpost, gfpre, router, tm=2 * tm, tiles_per_batch=None)

        j = i // 2
        if i % 2 == 0:
            wg, wu, wd = res[-3:]
            side = ()
            if not last:
                jn = (i + 1) // 2
                side = (w_gate_e[jn].reshape(N_EXPERTS * D_MODEL, D_FF), w_up_e[jn].reshape(N_EXPERTS * D_MODEL, D_FF),
                        w_down_e[jn].reshape(N_EXPERTS * D_FF, D_MODEL))
            h, expert_w = _ffn_dense(res[1], wg, wu, wd, res[0], mods, gfpost, tm=tm, tiles_per_batch=tpb,
                                     side_cast=side, zero_rows=_moe_tile_bound(T) * MOE_TM)
            if not last:
                hc, _ = _ffn_dense(resc[1], wg, wu, wd, resc[0], mods, gfpost, tm=tm, tiles_per_batch=None)
        else:
            wg = expert_w[0].reshape(N_EXPERTS, D_MODEL, D_FF)
            wu = expert_w[1].reshape(N_EXPERTS, D_MODEL, D_FF)
            wd = expert_w[2].reshape(N_EXPERTS, D_FF, D_MODEL)
            assert last and tm == MOE_TB, "the expert FFN is only built for the final layer's latent tokens"
            h = _ffn_moe(res[1], res[2], res[3], wg, wu, wd, res[0], mods, gfpost, expert_w[3],
                         tiles_per_batch=n_lat // MOE_TB)
    return h.reshape(bsz, n_lat, D_MODEL)
```
